```python
import numpy as np
import jax
import jax.numpy as jnp
from jax import lax


D_MODEL = 1024
BATCH = 8
SEQ = 4096
DEPTH = 4

HEAD_DIM = D_MODEL // 16
N_RET = 6
N_FOX = 6
N_NSA = 4
RET_W = N_RET * HEAD_DIM
FOX_W = N_FOX * HEAD_DIM
NSA_W = N_NSA * HEAD_DIM
D_MIX = RET_W + FOX_W + NSA_W
D_FF = 4 * D_MODEL
RMS_EPS = 1e-6
BLOCK_Q = 128
RET_CHUNK = 128
ROPE_BASE = 10000.0
NSA_CMP_LEN = 32
NSA_CMP_STRIDE = 16
NSA_CMP_HIDDEN = 4 * HEAD_DIM
NSA_SEL_LEN = 64
NSA_TOP_N = 16
NSA_WINDOW = 512
NSA_FORCED_SCORE = 1e4
NEG_INF = -1e30
IN_SPLIT_SIZES = (RET_W, RET_W, RET_W, RET_W, FOX_W, FOX_W, FOX_W, N_FOX, NSA_W, HEAD_DIM, HEAD_DIM, HEAD_DIM, HEAD_DIM, HEAD_DIM, HEAD_DIM, 3 * N_NSA)
IN_COLS = 4 * RET_W + 3 * FOX_W + N_FOX + NSA_W + 6 * HEAD_DIM + 3 * N_NSA

kernel_name = 'hybrid_retention_fox_nsa_trunk'


def _rms_norm(x, g):
    xf = x.astype(jnp.float32)
    y = xf * lax.rsqrt(jnp.mean(xf * xf, axis=-1, keepdims=True) + RMS_EPS)
    return (y * g.astype(jnp.float32)).astype(x.dtype)


def _head_rms_norm(y, g, n_heads):
    B, S, W = y.shape
    yh = y.reshape(B, S, n_heads, W // n_heads)
    yh = yh * lax.rsqrt(jnp.mean(yh * yh, axis=-1, keepdims=True) + RMS_EPS)
    return yh.reshape(B, S, W) * g.astype(jnp.float32)


def _head_layer_norm(y, g, n_heads):
    B, S, W = y.shape
    yh = y.reshape(B, S, n_heads, W // n_heads)
    yc = yh - jnp.mean(yh, axis=-1, keepdims=True)
    yh = yc * lax.rsqrt(jnp.mean(yc * yc, axis=-1, keepdims=True) + RMS_EPS)
    return yh.reshape(B, S, W) * g.astype(jnp.float32)


def _to_heads(t, n_heads):
    B, S, _ = t.shape
    return t.astype(jnp.float32).reshape(B, S, n_heads, HEAD_DIM).transpose(0, 2, 1, 3)


def _to_blocks(t, block):
    B, H, S = t.shape[:3]
    t = t.reshape((B, H, S // block, block) + t.shape[3:])
    return jnp.moveaxis(t, 2, 0)


def _from_blocks(o, width):
    NB, B, H, BQ, D = o.shape
    return o.transpose(1, 0, 3, 2, 4).reshape(B, NB * BQ, width)


def _rotary(t):
    S = t.shape[2]
    half = HEAD_DIM // 2
    inv = 1.0 / (ROPE_BASE ** (jnp.arange(half, dtype=jnp.float32) / half))
    ang = jnp.arange(S, dtype=jnp.float32)[:, None] * inv[None, :]
    cos, sin = jnp.cos(ang), jnp.sin(ang)
    t1, t2 = t[..., :half], t[..., half:]
    return jnp.concatenate([t1 * cos - t2 * sin, t1 * sin + t2 * cos], axis=-1)


def _masked_softmax(s, mask):
    p = jax.nn.softmax(jnp.where(mask, s, NEG_INF), axis=-1)
    return p * mask.astype(jnp.float32)


def _retention(q, k, v, g, gain):
    B, S, _ = q.shape
    q = _rotary(_to_heads(q, N_RET))
    k = _rotary(_to_heads(k, N_RET)) * (HEAD_DIM ** -0.5)
    v = _to_heads(v, N_RET)
    log_gamma = jnp.log(1.0 - 2.0 ** (-5.0 - jnp.arange(N_RET, dtype=jnp.float32)))
    C = RET_CHUNK
    idx = jnp.arange(C, dtype=jnp.float32)
    diff = idx[:, None] - idx[None, :]
    decay_in = jnp.where(diff >= 0, jnp.exp(log_gamma[:, None, None] * jnp.maximum(diff, 0.0)), 0.0)
    decay_q = jnp.exp(log_gamma[:, None] * (idx + 1.0))[..., None]
    decay_k = jnp.exp(log_gamma[:, None] * (C - 1.0 - idx))[..., None]
    decay_chunk = jnp.exp(log_gamma * C)[:, None, None]

    def step(state, qkv):
        qc, kc, vc = qkv
        inner = jnp.einsum('bhnd,bhmd->bhnm', qc, kc) * decay_in
        out = jnp.einsum('bhnm,bhme->bhne', inner, vc) + jnp.einsum('bhnd,bhde->bhne', qc, state) * decay_q
        state = state * decay_chunk + jnp.einsum('bhmd,bhme->bhde', kc * decay_k, vc)
        return state, out

    state0 = jnp.zeros((B, N_RET, HEAD_DIM, HEAD_DIM), jnp.float32)
    _, out = lax.scan(step, state0, (_to_blocks(q, C), _to_blocks(k, C), _to_blocks(v, C)))
    out = _head_layer_norm(_from_blocks(out, RET_W), gain, N_RET)
    return out * jax.nn.silu(g.astype(jnp.float32))


def _forgetting_attention(q, k, v, f_logit, f_bias, gain):
    B, S, _ = q.shape
    q = _to_heads(q, N_FOX) * (HEAD_DIM ** -0.5)
    k = _to_heads(k, N_FOX)
    v = _to_heads(v, N_FOX)
    log_f = jax.nn.log_sigmoid(f_logit.astype(jnp.float32) + f_bias.astype(jnp.float32))
    cum = jnp.cumsum(log_f, axis=1).transpose(0, 2, 1)
    NB = S // BLOCK_Q
    key_pos = jnp.arange(S)

    def block(args):
        qb, cb, start = args
        t = start + jnp.arange(BLOCK_Q)
        s = jnp.einsum('bhqd,bhkd->bhqk', qb, k) + (cb[..., None] - cum[:, :, None, :])
        s = jnp.where(key_pos[None, :] <= t[:, None], s, NEG_INF)
        p = jax.nn.softmax(s, axis=-1)
        return jnp.einsum('bhqk,bhkd->bhqd', p, v)

    out = lax.map(block, (_to_blocks(q, BLOCK_Q), _to_blocks(cum, BLOCK_Q), jnp.arange(NB) * BLOCK_Q))
    return _head_rms_norm(_from_blocks(out, FOX_W), gain, N_FOX)


def _nsa_compress(kv, pos_emb, w1, w2):
    B, S, D = kv.shape
    n_cmp = (S - NSA_CMP_LEN) // NSA_CMP_STRIDE + 1
    idx = (jnp.arange(n_cmp) * NSA_CMP_STRIDE)[:, None] + jnp.arange(NSA_CMP_LEN)[None, :]
    blocks = kv[:, idx] + pos_emb.astype(jnp.float32)
    hid = jax.nn.gelu(blocks.reshape(B, n_cmp, NSA_CMP_LEN * D) @ w1)
    return (hid @ w2).astype(jnp.float32)


def _nsa_attention(q, k_cmp, v_cmp, k_sel, v_sel, k_win, v_win, gate_logit, pos_k, pos_v, w1_k, w2_k, w1_v, w2_v, gain):
    B, S, _ = q.shape
    q = _to_heads(q, N_NSA) * (HEAD_DIM ** -0.5)
    f32 = jnp.float32
    kc = _nsa_compress(k_cmp.astype(f32), pos_k, w1_k, w2_k)
    vc = _nsa_compress(v_cmp.astype(f32), pos_v, w1_v, w2_v)
    n_cmp = kc.shape[1]
    cmp_end = jnp.arange(n_cmp) * NSA_CMP_STRIDE + NSA_CMP_LEN - 1
    n_sel = S // NSA_SEL_LEN
    top_n = min(NSA_TOP_N, n_sel)
    cs = np.arange(n_cmp) * NSA_CMP_STRIDE
    ss = np.arange(n_sel) * NSA_SEL_LEN
    overlap = np.clip(np.minimum(cs[:, None] + NSA_CMP_LEN, ss[None, :] + NSA_SEL_LEN) - np.maximum(cs[:, None], ss[None, :]), 0, None)
    cmp_to_sel = jnp.asarray(overlap / NSA_CMP_LEN, dtype=f32)
    ks_blocks = k_sel.astype(f32).reshape(B, n_sel, NSA_SEL_LEN, HEAD_DIM)
    vs_blocks = v_sel.astype(f32).reshape(B, n_sel, NSA_SEL_LEN, HEAD_DIM)
    win_len = NSA_WINDOW + BLOCK_Q
    kw_pad = jnp.pad(k_win.astype(f32), ((0, 0), (NSA_WINDOW, 0), (0, 0)))
    vw_pad = jnp.pad(v_win.astype(f32), ((0, 0), (NSA_WINDOW, 0), (0, 0)))
    gates = jax.nn.sigmoid(gate_logit.astype(f32)).reshape(B, S, N_NSA, 3).transpose(0, 2, 1, 3)
    sel_ids = jnp.arange(n_sel)
    NB = S // BLOCK_Q

    def block(args):
        qb, gb, start = args
        t = start + jnp.arange(BLOCK_Q)
        p_c = _masked_softmax(jnp.einsum('bhqd,bnd->bhqn', qb, kc), cmp_end[None, :] <= t[:, None])
        o_c = jnp.einsum('bhqn,bnd->bhqd', p_c, vc)
        imp = jnp.einsum('bhqn,ns->bqs', p_c, cmp_to_sel)
        cur = t // NSA_SEL_LEN
        visible = sel_ids[None, :] * NSA_SEL_LEN <= t[:, None]
        forced = (sel_ids[None, :] == 0) | (sel_ids[None, :] == cur[:, None]) | (sel_ids[None, :] == cur[:, None] - 1)
        score = jnp.where(forced[None], NSA_FORCED_SCORE, imp)
        score = jnp.where(visible[None], score, -1.0)
        _, sel = lax.top_k(score, top_n)
        gk = jax.vmap(lambda kb, ids: kb[ids])(ks_blocks, sel)
        gv = jax.vmap(lambda vb, ids: vb[ids])(vs_blocks, sel)
        tok = sel[..., None] * NSA_SEL_LEN + jnp.arange(NSA_SEL_LEN)
        mask_s = (tok <= t[None, :, None, None]).reshape(B, 1, BLOCK_Q, top_n * NSA_SEL_LEN)
        s_s = jnp.einsum('bhqd,bqnld->bhqnl', qb, gk).reshape(B, N_NSA, BLOCK_Q, top_n * NSA_SEL_LEN)
        p_s = _masked_softmax(s_s, mask_s)
        o_s = jnp.einsum('bhqm,bqmd->bhqd', p_s, gv.reshape(B, BLOCK_Q, top_n * NSA_SEL_LEN, HEAD_DIM))
        kw = lax.dynamic_slice_in_dim(kw_pad, start, win_len, axis=1)
        vw = lax.dynamic_slice_in_dim(vw_pad, start, win_len, axis=1)
        kpos = start - NSA_WINDOW + jnp.arange(win_len)
        mask_w = (kpos[None, :] >= 0) & (kpos[None, :] <= t[:, None]) & (kpos[None, :] > t[:, None] - NSA_WINDOW)
        p_w = _masked_softmax(jnp.einsum('bhqd,bkd->bhqk', qb, kw), mask_w)
        o_w = jnp.einsum('bhqk,bkd->bhqd', p_w, vw)
        return gb[..., 0:1] * o_c + gb[..., 1:2] * o_s + gb[..., 2:3] * o_w

    out = lax.map(block, (_to_blocks(q, BLOCK_Q), _to_blocks(gates, BLOCK_Q), jnp.arange(NB) * BLOCK_Q))
    return _head_rms_norm(_from_blocks(out, NSA_W), gain, N_NSA)


def setup_inputs(seed: int = 0) -> dict:
    key = jax.random.key(seed)
    ks = jax.random.split(key, 20)
    f32 = jnp.float32

    def nrm(k, shape, scale):
        return jax.random.normal(k, shape, f32) * scale

    def gain(k, shape):
        return 1.0 + 0.1 * jax.random.normal(k, shape, f32)

    return {
        'x': jax.random.normal(ks[0], (BATCH, SEQ, D_MODEL), f32),
        'norm_attn': gain(ks[1], (DEPTH, D_MODEL)),
        'w_in': nrm(ks[2], (DEPTH, D_MODEL, IN_COLS), D_MODEL ** -0.5),
        'fox_forget_bias': 1.0 + 2.0 * jax.random.uniform(ks[3], (DEPTH, N_FOX), f32),
        'ret_norm_gain': gain(ks[4], (DEPTH, RET_W)),
        'fox_norm_gain': gain(ks[5], (DEPTH, FOX_W)),
        'nsa_norm_gain': gain(ks[6], (DEPTH, NSA_W)),
        'nsa_cmp_pos_k': nrm(ks[7], (DEPTH, NSA_CMP_LEN, HEAD_DIM), 0.2),
        'nsa_cmp_pos_v': nrm(ks[8], (DEPTH, NSA_CMP_LEN, HEAD_DIM), 0.2),
        'nsa_cmp_w1_k': nrm(ks[9], (DEPTH, NSA_CMP_LEN * HEAD_DIM, NSA_CMP_HIDDEN), (NSA_CMP_LEN * HEAD_DIM) ** -0.5),
        'nsa_cmp_w2_k': nrm(ks[10], (DEPTH, NSA_CMP_HIDDEN, HEAD_DIM), NSA_CMP_HIDDEN ** -0.5),
        'nsa_cmp_w1_v': nrm(ks[11], (DEPTH, NSA_CMP_LEN * HEAD_DIM, NSA_CMP_HIDDEN), (NSA_CMP_LEN * HEAD_DIM) ** -0.5),
        'nsa_cmp_w2_v': nrm(ks[12], (DEPTH, NSA_CMP_HIDDEN, HEAD_DIM), NSA_CMP_HIDDEN ** -0.5),
        'w_out': nrm(ks[13], (DEPTH, D_MIX, D_MODEL), D_MIX ** -0.5),
        'norm_mlp': gain(ks[14], (DEPTH, D_MODEL)),
        'w_mlp_in': nrm(ks[15], (DEPTH, D_MODEL, D_FF), D_MODEL ** -0.5),
        'w_mlp_out': nrm(ks[16], (DEPTH, D_FF, D_MODEL), D_FF ** -0.5),
        'norm_final': gain(ks[17], (D_MODEL,)),
    }


def reference(x, norm_attn, w_in, fox_forget_bias, ret_norm_gain, fox_norm_gain, nsa_norm_gain, nsa_cmp_pos_k, nsa_cmp_pos_v, nsa_cmp_w1_k, nsa_cmp_w2_k, nsa_cmp_w1_v, nsa_cmp_w2_v, w_out, norm_mlp, w_mlp_in, w_mlp_out, norm_final):
    split_points = np.cumsum(IN_SPLIT_SIZES)[:-1].tolist()
    for l in range(DEPTH):
        h = _rms_norm(x, norm_attn[l])
        parts = jnp.split(h @ w_in[l], split_points, axis=-1)
        (rq, rk, rv, rg, fq, fk, fv, ff, nq, nkc, nvc, nks, nvs, nkw, nvw, ngate) = parts
        y_ret = _retention(rq, rk, rv, rg, ret_norm_gain[l])
        y_fox = _forgetting_attention(fq, fk, fv, ff, fox_forget_bias[l], fox_norm_gain[l])
        y_nsa = _nsa_attention(nq, nkc, nvc, nks, nvs, nkw, nvw, ngate, nsa_cmp_pos_k[l], nsa_cmp_pos_v[l], nsa_cmp_w1_k[l], nsa_cmp_w2_k[l], nsa_cmp_w1_v[l], nsa_cmp_w2_v[l], nsa_norm_gain[l])
        y = jnp.concatenate([y_ret, y_fox, y_nsa], axis=-1).astype(x.dtype)
        x = x + y @ w_out[l]
        h = _rms_norm(x, norm_mlp[l])
        x = x + jnp.square(jax.nn.relu(h @ w_mlp_in[l])) @ w_mlp_out[l]
    return _rms_norm(x, norm_final)
```

```python
import functools

import numpy as np
import jax
import jax.numpy as jnp
from jax import lax
from jax.experimental import pallas as pl
from jax.experimental.pallas import tpu as pltpu

F32 = jnp.float32
BF16 = jnp.bfloat16

D_MODEL = 1024
HEAD_DIM = 64
N_RET = 6
N_FOX = 6
N_NSA = 4
RET_W = N_RET * HEAD_DIM
FOX_W = N_FOX * HEAD_DIM
NSA_W = N_NSA * HEAD_DIM
D_FF = 4 * D_MODEL
RMS_EPS = 1e-6
RET_CHUNK = 128
ROPE_BASE = 10000.0
CMP_LEN = 32
CMP_STRIDE = 16
CMP_HIDDEN = 4 * HEAD_DIM
SEL_LEN = 64
TOP_N = 16
WINDOW = 512
FORCED_SCORE = 1e4
NEG = -1e30

LANES = 128
N_PAIR = N_RET // 2

CB_RQ, CB_RK, CB_RV, CB_RG = 0, 3, 6, 9
CB_NQ = 12
CB_FQ, CB_FK, CB_FV = 14, 17, 20
CB_CMP, CB_SEL, CB_WIN = 23, 24, 25
N_MAIN = 26 * LANES
SMALL_FF = 0
SMALL_GATE = 8

VMEM_LIMIT = 56 * 1024 * 1024


def _cparams(sem):
    return pltpu.CompilerParams(dimension_semantics=sem, vmem_limit_bytes=VMEM_LIMIT)


def _resident(shape, index_map):
    return pl.BlockSpec(shape, index_map, pipeline_mode=pl.Buffered(1))


def _in_proj_columns():
    sizes = (RET_W, RET_W, RET_W, RET_W, FOX_W, FOX_W, FOX_W, N_FOX, NSA_W) + (HEAD_DIM,) * 6 + (3 * N_NSA,)
    off = np.concatenate([[0], np.cumsum(sizes)])
    (o_rq, o_rk, o_rv, o_rg, o_fq, o_fk, o_fv, o_ff, o_nq, o_kc, o_vc, o_ks, o_vs, o_kw, o_vw, o_gt) = off[:-1]
    half = HEAD_DIM // 2
    inter = []
    for p in range(N_PAIR):
        a, b = 2 * p, 2 * p + 1
        for h, part in ((a, 0), (b, 0), (a, 1), (b, 1)):
            inter.extend(range(h * HEAD_DIM + part * half, h * HEAD_DIM + (part + 1) * half))
    inter = np.asarray(inter)
    nat = np.arange(RET_W)
    cols, scale = [], []

    def add(idx, s=1.0):
        cols.append(np.asarray(idx))
        scale.append(np.full(len(idx), s, np.float32))

    qk_scale = HEAD_DIM ** -0.5
    add(o_rq + inter)
    add(o_rk + inter, qk_scale)
    add(o_rv + nat)
    add(o_rg + nat)
    add(o_nq + np.arange(NSA_W), qk_scale)
    add(o_fq + nat, qk_scale)
    add(o_fk + nat)
    add(o_fv + nat)
    for o in (o_kc, o_vc, o_ks, o_vs, o_kw, o_vw):
        add(o + np.arange(HEAD_DIM))
    cols = np.concatenate(cols)
    scale = np.concatenate(scale)
    assert cols.shape[0] == N_MAIN
    small_cols = np.zeros(LANES, np.int64)
    small_mask = np.zeros(LANES, np.float32)
    small_cols[SMALL_FF:SMALL_FF + N_FOX] = o_ff + np.arange(N_FOX)
    small_mask[SMALL_FF:SMALL_FF + N_FOX] = 1.0
    small_cols[SMALL_GATE:SMALL_GATE + 3 * N_NSA] = o_gt + np.arange(3 * N_NSA)
    small_mask[SMALL_GATE:SMALL_GATE + 3 * N_NSA] = 1.0
    return cols, scale, small_cols, small_mask


def _retention_tables(seq):
    half = HEAD_DIM // 2
    inv = 1.0 / (ROPE_BASE ** (jnp.arange(half, dtype=F32) / half))
    ang = jnp.arange(seq, dtype=F32)[:, None] * inv[None, :]
    cos, sin = jnp.cos(ang), jnp.sin(ang)
    cos_t = jnp.concatenate([cos, cos, cos, cos], axis=-1)
    sin_t = jnp.concatenate([-sin, -sin, sin, sin], axis=-1)
    log_gamma = np.log(1.0 - 2.0 ** (-5.0 - np.arange(N_RET, dtype=np.float32))).astype(np.float32)
    C = RET_CHUNK
    idx = np.arange(C, dtype=np.float32)
    diff = idx[:, None] - idx[None, :]
    lane = np.arange(LANES)
    head_k = (lane % HEAD_DIM) // half
    head_v = lane // HEAD_DIM
    d_in = np.zeros((N_PAIR, 2, C, C), np.float32)
    d_k = np.zeros((N_PAIR, C, LANES), np.float32)
    d_q = np.zeros((N_PAIR, C, LANES), np.float32)
    d_c = np.zeros((N_PAIR, LANES, LANES), np.float32)
    for p in range(N_PAIR):
        lg = log_gamma[2 * p:2 * p + 2]
        for j in range(2):
            d_in[p, j] = np.where(diff >= 0, np.exp(lg[j] * np.maximum(diff, 0.0)), 0.0)
        d_k[p] = np.exp(lg[head_k][None, :] * (C - 1.0 - idx)[:, None])
        d_q[p] = np.exp(lg[head_v][None, :] * (idx + 1.0)[:, None])
        d_c[p] = np.broadcast_to(np.exp(lg[head_v] * C)[None, :], (LANES, LANES))
    s_mask = (head_k[:, None] == head_v[None, :]).astype(np.float32)
    return cos_t, sin_t, jnp.asarray(d_in), jnp.asarray(d_k), jnp.asarray(d_q), jnp.asarray(d_c), jnp.asarray(s_mask)


def _rms(x, g):
    return x * lax.rsqrt(jnp.mean(x * x, axis=-1, keepdims=True) + RMS_EPS) * g


def _inproj_kernel(x_ref, g_ref, wm_ref, ws_ref, om_ref, os_ref, *, n_chunk):
    h = _rms(x_ref[...], g_ref[...]).astype(BF16)
    for n0 in range(0, N_MAIN, n_chunk):
        n1 = min(n0 + n_chunk, N_MAIN)
        om_ref[:, n0:n1] = jnp.dot(h, wm_ref[:, n0:n1], preferred_element_type=F32).astype(BF16)
    os_ref[...] = jnp.dot(h, ws_ref[...], preferred_element_type=F32)


def _inproj(x2d, g, wm, ws, tm):
    T = x2d.shape[0]
    return pl.pallas_call(
        functools.partial(_inproj_kernel, n_chunk=512),
        grid=(T // tm,),
        in_specs=[
            pl.BlockSpec((tm, D_MODEL), lambda i: (i, 0)),
            _resident((1, D_MODEL), lambda i: (0, 0)),
            _resident((D_MODEL, N_MAIN), lambda i: (0, 0)),
            _resident((D_MODEL, LANES), lambda i: (0, 0)),
        ],
        out_specs=[
            pl.BlockSpec((tm, N_MAIN), lambda i: (i, 0)),
            pl.BlockSpec((tm, LANES), lambda i: (i, 0)),
        ],
        out_shape=[jax.ShapeDtypeStruct((T, N_MAIN), BF16), jax.ShapeDtypeStruct((T, LANES), F32)],
        compiler_params=_cparams(("parallel",)),
        name="inproj",
    )(x2d, g, wm, ws)


def _cum_kernel(z_ref, b_ref, tri_ref, col_ref, row_ref, *, n_chunks):
    C = RET_CHUNK

    def body(c, carry):
        r0 = pl.multiple_of(c * C, C)
        z = z_ref[0, pl.ds(r0, C), :] + b_ref[...]
        lf = jnp.minimum(z, 0.0) - jnp.log1p(jnp.exp(-jnp.abs(z)))
        cs = jnp.dot(tri_ref[...], lf, preferred_element_type=F32, precision=lax.Precision.HIGHEST) + carry
        col_ref[0, pl.ds(r0, C), :] = cs
        row_ref[0, :, pl.ds(r0, C)] = cs.T[0:8, :]
        return cs[C - 1:C, :]

    lax.fori_loop(0, n_chunks, body, jnp.zeros((1, LANES), F32))


def _forget_cumsum(small3, bias_row):
    B, S, _ = small3.shape
    tri = jnp.asarray(np.tril(np.ones((RET_CHUNK, RET_CHUNK), np.float32)))
    return pl.pallas_call(
        functools.partial(_cum_kernel, n_chunks=S // RET_CHUNK),
        grid=(B,),
        in_specs=[
            pl.BlockSpec((1, S, LANES), lambda b: (b, 0, 0)),
            pl.BlockSpec((1, LANES), lambda b: (0, 0)),
            pl.BlockSpec((RET_CHUNK, RET_CHUNK), lambda b: (0, 0)),
        ],
        out_specs=[
            pl.BlockSpec((1, S, LANES), lambda b: (b, 0, 0)),
            pl.BlockSpec((1, 8, S), lambda b: (b, 0, 0)),
        ],
        out_shape=[jax.ShapeDtypeStruct((B, S, LANES), F32), jax.ShapeDtypeStruct((B, 8, S), F32)],
        compiler_params=_cparams(("parallel",)),
        name="forget_cumsum",
    )(small3, bias_row, tri)


def _dot_nt(a, b):
    return lax.dot_general(a, b, (((1,), (1,)), ((), ())), preferred_element_type=F32)


def _ret_kernel(q_ref, k_ref, v_ref, g_ref, cos_ref, sin_ref, din_ref, dk_ref, dq_ref, dc_ref, sm_ref, gain_ref,
                o_ref, state_ref, *, n_chunks):
    C = RET_CHUNK
    half = HEAD_DIM // 2
    lane = lax.broadcasted_iota(jnp.int32, (C, LANES), 1)
    khead = (lane % HEAD_DIM) // half
    vhead0 = lane < HEAD_DIM
    state_ref[...] = jnp.zeros_like(state_ref)

    def body(c, _):
        r0 = pl.multiple_of(c * C, C)
        rows = pl.ds(r0, C)
        cs, sn = cos_ref[rows, :], sin_ref[rows, :]
        q = q_ref[0, rows, :].astype(F32)
        k = k_ref[0, rows, :].astype(F32)
        q = q * cs + pltpu.roll(q, HEAD_DIM, 1) * sn
        k = k * cs + pltpu.roll(k, HEAD_DIM, 1) * sn
        v = v_ref[0, rows, :]
        qb, kb = q.astype(BF16), k.astype(BF16)
        state = state_ref[...]
        cross = jnp.dot(qb, state.astype(BF16), preferred_element_type=F32) * dq_ref[0]
        outs = []
        for j in range(2):
            qm = jnp.where(khead == j, qb, jnp.zeros_like(qb))
            inner = _dot_nt(qm, kb) * din_ref[0, j]
            outs.append(jnp.dot(inner.astype(BF16), v, preferred_element_type=F32))
        out = jnp.where(vhead0, outs[0], outs[1]) + cross
        kd = (k * dk_ref[0]).T.astype(BF16)
        state_ref[...] = state * dc_ref[0] + jnp.dot(kd, v, preferred_element_type=F32) * sm_ref[...]
        inv = 1.0 / HEAD_DIM
        s0 = jnp.sum(jnp.where(vhead0, out, 0.0), axis=-1, keepdims=True)
        s1 = jnp.sum(jnp.where(vhead0, 0.0, out), axis=-1, keepdims=True)
        yc = out - jnp.where(vhead0, s0, s1) * inv
        yc2 = yc * yc
        v0 = jnp.sum(jnp.where(vhead0, yc2, 0.0), axis=-1, keepdims=True)
        v1 = jnp.sum(jnp.where(vhead0, 0.0, yc2), axis=-1, keepdims=True)
        y = yc * lax.rsqrt(jnp.where(vhead0, v0, v1) * inv + RMS_EPS) * gain_ref[...]
        g = g_ref[0, rows, :].astype(F32)
        o_ref[0, rows, :] = (y * (g * (1.0 / (1.0 + jnp.exp(-g))))).astype(BF16)
        return 0

    lax.fori_loop(0, n_chunks, body, 0)


def _retention(main3, tables, gain_row):
    B, S, _ = main3.shape
    cos_t, sin_t, d_in, d_k, d_q, d_c, s_mask = tables
    C = RET_CHUNK
    seq_spec = lambda cb: pl.BlockSpec((1, S, LANES), lambda b, p: (b, 0, cb + p))
    return pl.pallas_call(
        functools.partial(_ret_kernel, n_chunks=S // C),
        grid=(B, N_PAIR),
        in_specs=[
            seq_spec(CB_RQ), seq_spec(CB_RK), seq_spec(CB_RV), seq_spec(CB_RG),
            pl.BlockSpec((S, LANES), lambda b, p: (0, 0)),
            pl.BlockSpec((S, LANES), lambda b, p: (0, 0)),
            pl.BlockSpec((1, 2, C, C), lambda b, p: (p, 0, 0, 0)),
            pl.BlockSpec((1, C, LANES), lambda b, p: (p, 0, 0)),
            pl.BlockSpec((1, C, LANES), lambda b, p: (p, 0, 0)),
            pl.BlockSpec((1, LANES, LANES), lambda b, p: (p, 0, 0)),
            pl.BlockSpec((LANES, LANES), lambda b, p: (0, 0)),
            pl.BlockSpec((1, LANES), lambda b, p: (0, p)),
        ],
        out_specs=pl.BlockSpec((1, S, LANES), lambda b, p: (b, 0, p)),
        out_shape=jax.ShapeDtypeStruct((B, S, RET_W), BF16),
        scratch_shapes=[pltpu.VMEM((LANES, LANES), F32)],
        compiler_params=_cparams(("parallel", "parallel")),
        name="retention",
    )(main3, main3, main3, main3, cos_t, sin_t, d_in, d_k, d_q, d_c, s_mask, gain_row)


def _fox_kernel(q_ref, k_ref, v_ref, ccol_ref, crow_ref, gain_ref, o_ref, *, tq, tk):
    p = pl.program_id(1)
    qi = pl.program_id(2)
    q = q_ref[0]
    lane_q = lax.broadcasted_iota(jnp.int32, (tq, LANES), 1)
    head0 = lane_q < HEAD_DIM
    ccol = ccol_ref[0]
    qms, cts = [], []
    for j in range(2):
        qms.append(jnp.where(head0 if j == 0 else jnp.logical_not(head0), q, jnp.zeros_like(q)))
        cts.append(jnp.sum(jnp.where(lane_q == 2 * p + j, ccol, 0.0), axis=-1, keepdims=True))
    q_pos = qi * tq + lax.broadcasted_iota(jnp.int32, (tq, tk), 0)
    k_off = lax.broadcasted_iota(jnp.int32, (tq, tk), 1)

    def tile(t, carry, masked):
        c0 = pl.multiple_of(t * tk, tk)
        k = k_ref[0, pl.ds(c0, tk), :]
        v = v_ref[0, pl.ds(c0, tk), :]
        new = []
        for j in range(2):
            m, l, acc = carry[3 * j:3 * j + 3]
            crow = crow_ref[0, pl.ds(2 * p + j, 1), pl.ds(c0, tk)]
            s = _dot_nt(qms[j], k) + (cts[j] - crow)
            if masked:
                s = jnp.where(k_off + c0 <= q_pos, s, NEG)
            m_new = jnp.maximum(m, jnp.max(s, axis=-1, keepdims=True))
            pr = jnp.exp(s - m_new)
            alpha = jnp.exp(m - m_new)
            l = alpha * l + jnp.sum(pr, axis=-1, keepdims=True)
            acc = alpha * acc + jnp.dot(pr.astype(BF16), v, preferred_element_type=F32)
            new += [m_new, l, acc]
        return tuple(new)

    init = []
    for _ in range(2):
        init += [jnp.full((tq, 1), NEG, F32), jnp.zeros((tq, 1), F32), jnp.zeros((tq, LANES), F32)]
    n_full = (qi * tq) // tk
    carry = lax.fori_loop(0, n_full, lambda t, c: tile(t, c, False), tuple(init))
    for d in range(max(1, tq // tk)):
        carry = tile(n_full + d, carry, True)
    o0 = carry[2] / carry[1]
    o1 = carry[5] / carry[4]
    out = jnp.where(head0, o0, o1)
    out2 = out * out
    inv = 1.0 / HEAD_DIM
    v0 = jnp.sum(jnp.where(head0, out2, 0.0), axis=-1, keepdims=True)
    v1 = jnp.sum(jnp.where(head0, 0.0, out2), axis=-1, keepdims=True)
    o_ref[0] = (out * lax.rsqrt(jnp.where(head0, v0, v1) * inv + RMS_EPS) * gain_ref[...]).astype(BF16)


def _fox(main3, cum_col, cum_row, gain_row, tq, tk):
    B, S, _ = main3.shape
    return pl.pallas_call(
        functools.partial(_fox_kernel, tq=tq, tk=tk),
        grid=(B, N_PAIR, S // tq),
        in_specs=[
            pl.BlockSpec((1, tq, LANES), lambda b, p, i: (b, i, CB_FQ + p)),
            pl.BlockSpec((1, S, LANES), lambda b, p, i: (b, 0, CB_FK + p)),
            pl.BlockSpec((1, S, LANES), lambda b, p, i: (b, 0, CB_FV + p)),
            pl.BlockSpec((1, tq, LANES), lambda b, p, i: (b, i, 0)),
            pl.BlockSpec((1, 8, S), lambda b, p, i: (b, 0, 0)),
            pl.BlockSpec((1, LANES), lambda b, p, i: (0, p)),
        ],
        out_specs=pl.BlockSpec((1, tq, LANES), lambda b, p, i: (b, i, p)),
        out_shape=jax.ShapeDtypeStruct((B, S, FOX_W), BF16),
        compiler_params=_cparams(("parallel", "parallel", "arbitrary")),
        name="fox_attention",
    )(main3, main3, main3, cum_col, cum_row, gain_row)


def _gelu_tanh(x):
    return 0.5 * x * (1.0 + jnp.tanh(np.sqrt(2.0 / np.pi).astype(np.float32) * (x + 0.044715 * (x * x * x))))


def _cmp_kernel(r_ref, pos_ref, wt_ref, wb_ref, w2_ref, o_ref):
    r = r_ref[0]
    n = r.shape[0]
    top = jnp.dot(r, wt_ref[...], preferred_element_type=F32)
    bot = jnp.dot(r, wb_ref[...], preferred_element_type=F32)
    cpos = jnp.dot(pos_ref[0:8, :], wt_ref[...], preferred_element_type=F32)
    cpos = cpos + jnp.dot(pos_ref[8:16, :], wb_ref[...], preferred_element_type=F32)
    hid = _gelu_tanh(top + pltpu.roll(bot, n - 1, 0) + cpos[0:1, :])
    o_ref[0] = jnp.dot(hid.astype(BF16), w2_ref[...], preferred_element_type=F32)


def _nsa_compress(r, pos2, wt, wb, w2):
    B, n, _ = r.shape
    width = CMP_STRIDE * LANES
    return pl.pallas_call(
        _cmp_kernel,
        grid=(B,),
        in_specs=[
            pl.BlockSpec((1, n, width), lambda b: (b, 0, 0)),
            pl.BlockSpec((16, width), lambda b: (0, 0)),
            pl.BlockSpec((width, 2 * CMP_HIDDEN), lambda b: (0, 0)),
            pl.BlockSpec((width, 2 * CMP_HIDDEN), lambda b: (0, 0)),
            pl.BlockSpec((2 * CMP_HIDDEN, LANES), lambda b: (0, 0)),
        ],
        out_specs=pl.BlockSpec((1, n, LANES), lambda b: (b, 0, 0)),
        out_shape=jax.ShapeDtypeStruct((B, n, LANES), F32),
        compiler_params=_cparams(("parallel",)),
        name="nsa_compress",
    )(r, pos2, wt, wb, w2)


def _softmax_rows(s, mask):
    sm = jnp.where(mask, s, NEG)
    m = jnp.max(sm, axis=-1, keepdims=True)
    e = jnp.where(mask, jnp.exp(sm - m), 0.0)
    l = jnp.sum(e, axis=-1, keepdims=True)
    return e / jnp.where(l > 0.0, l, 1.0)


def _nsa_kernel(q_ref, cmp_ref, c2s_ref, sel_ref, ksa_ref, win_ref, gl_ref, gain_ref, o_ref, *, tq, tk, n_cmp, n_sel, top_n):
    H = N_NSA
    qi = pl.program_id(1)
    start = qi * tq
    q = q_ref[0]
    t_row = start + lax.broadcasted_iota(jnp.int32, (tq, 1), 0)
    t_all = jnp.concatenate([t_row] * H, axis=0)
    zeros64 = jnp.zeros((tq, HEAD_DIM), BF16)
    q_heads = [q[:, h * HEAD_DIM:(h + 1) * HEAD_DIM] for h in range(H)]
    q_pad = jnp.concatenate([jnp.concatenate([qh, zeros64], axis=1) for qh in q_heads], axis=0)

    cmp = cmp_ref[0]
    n_pad = cmp.shape[0]
    s_c = lax.dot_general(q_pad.astype(F32), cmp, (((1,), (1,)), ((), ())), preferred_element_type=F32,
                          precision=lax.Precision.HIGHEST)
    n_id = lax.broadcasted_iota(jnp.int32, (H * tq, n_pad), 1)
    vis_c = jnp.logical_and(n_id * CMP_STRIDE + (CMP_LEN - 1) <= t_all, n_id < n_cmp)
    p_c = _softmax_rows(s_c, vis_c)
    o_c = jnp.dot(p_c.astype(BF16), cmp.astype(BF16), preferred_element_type=F32)

    p_sum = p_c[0:tq]
    for h in range(1, H):
        p_sum = p_sum + p_c[h * tq:(h + 1) * tq]
    imp = jnp.dot(p_sum, c2s_ref[...], preferred_element_type=F32, precision=lax.Precision.HIGHEST)
    s_id = lax.broadcasted_iota(jnp.int32, (tq, n_sel), 1)
    cur = t_row >> 6
    forced = jnp.logical_or(s_id == 0, jnp.logical_or(s_id == cur, s_id == cur - 1))
    score = jnp.where(forced, FORCED_SCORE, imp)
    score = jnp.where(s_id * SEL_LEN <= t_row, score, -1.0)
    rank = jnp.zeros((tq, n_sel), jnp.int32)
    for c in range(n_sel):
        col = score[:, c:c + 1]
        before = jnp.logical_or(col > score, jnp.logical_and(col == score, s_id > c))
        rank = rank + before.astype(jnp.int32)
    not_sel = jnp.where(rank < top_n, 0.0, 1.0).astype(BF16)
    if n_sel < HEAD_DIM:
        not_sel = jnp.concatenate([not_sel, jnp.zeros((tq, HEAD_DIM - n_sel), BF16)], axis=1)
    q_aug = jnp.concatenate([jnp.concatenate([qh, not_sel], axis=1) for qh in q_heads], axis=0)

    k_off = lax.broadcasted_iota(jnp.int32, (H * tq, tk), 1)

    def sel_tile(t, carry, masked):
        m, l, acc = carry
        c0 = pl.multiple_of(t * tk, tk)
        s = _dot_nt(q_aug, ksa_ref[0, pl.ds(c0, tk), :])
        if masked:
            s = jnp.where(k_off + c0 <= t_all, s, NEG)
        m_new = jnp.maximum(m, jnp.max(s, axis=-1, keepdims=True))
        pr = jnp.exp(s - m_new)
        alpha = jnp.exp(m - m_new)
        l = alpha * l + jnp.sum(pr, axis=-1, keepdims=True)
        acc = alpha * acc + jnp.dot(pr.astype(BF16), sel_ref[0, pl.ds(c0, tk), :], preferred_element_type=F32)
        return m_new, l, acc

    init = (jnp.full((H * tq, 1), NEG, F32), jnp.zeros((H * tq, 1), F32), jnp.zeros((H * tq, LANES), F32))
    n_full = start // tk
    carry = lax.fori_loop(0, n_full, lambda t, c: sel_tile(t, c, False), init)
    for d in range(max(1, tq // tk)):
        carry = sel_tile(n_full + d, carry, True)
    o_s = carry[2] / carry[1]

    wlen = WINDOW + tq
    base = pl.multiple_of(jnp.maximum(start - WINDOW, 0), tq)
    win = win_ref[0, pl.ds(base, wlen), :]
    s_w = _dot_nt(q_pad, win)
    kpos = base + lax.broadcasted_iota(jnp.int32, (H * tq, wlen), 1)
    vis_w = jnp.logical_and(kpos <= t_all, kpos > t_all - WINDOW)
    p_w = _softmax_rows(s_w, vis_w)
    o_w = jnp.dot(p_w.astype(BF16), win, preferred_element_type=F32)

    gl = gl_ref[0]
    gates = 1.0 / (1.0 + jnp.exp(-gl))
    outs = []
    for h in range(H):
        rows = slice(h * tq, (h + 1) * tq)
        g0 = gates[:, SMALL_GATE + 3 * h + 0:SMALL_GATE + 3 * h + 1]
        g1 = gates[:, SMALL_GATE + 3 * h + 1:SMALL_GATE + 3 * h + 2]
        g2 = gates[:, SMALL_GATE + 3 * h + 2:SMALL_GATE + 3 * h + 3]
        mix = g0 * o_c[rows] + g1 * o_s[rows] + g2 * o_w[rows]
        val = mix[:, HEAD_DIM:]
        outs.append(val * lax.rsqrt(jnp.mean(val * val, axis=-1, keepdims=True) + RMS_EPS))
    o_ref[0] = (jnp.concatenate(outs, axis=1) * gain_ref[...]).astype(BF16)


def _nsa(main3, cmp, c2s, ks_aug, small3, gain_row, tq, tk, n_cmp):
    B, S, _ = main3.shape
    n_sel = S // SEL_LEN
    n_pad = cmp.shape[1]
    return pl.pallas_call(
        functools.partial(_nsa_kernel, tq=tq, tk=tk, n_cmp=n_cmp, n_sel=n_sel, top_n=min(TOP_N, n_sel)),
        grid=(B, S // tq),
        in_specs=[
            pl.BlockSpec((1, tq, NSA_W), lambda b, i: (b, i, CB_NQ // 2)),
            pl.BlockSpec((1, n_pad, LANES), lambda b, i: (b, 0, 0)),
            pl.BlockSpec((n_pad, n_sel), lambda b, i: (0, 0)),
            pl.BlockSpec((1, S, LANES), lambda b, i: (b, 0, CB_SEL)),
            pl.BlockSpec((1, S, LANES), lambda b, i: (b, 0, 0)),
            pl.BlockSpec((1, S, LANES), lambda b, i: (b, 0, CB_WIN)),
            pl.BlockSpec((1, tq, LANES), lambda b, i: (b, i, 0)),
            pl.BlockSpec((1, NSA_W), lambda b, i: (0, 0)),
        ],
        out_specs=pl.BlockSpec((1, tq, NSA_W), lambda b, i: (b, i, 0)),
        out_shape=jax.ShapeDtypeStruct((B, S, NSA_W), BF16),
        compiler_params=_cparams(("parallel", "arbitrary")),
        name="nsa_attention",
    )(main3, cmp, c2s, main3, ks_aug, main3, small3, gain_row)


def _post_kernel(x_ref, yr_ref, yf_ref, yn_ref, wo_ref, g_ref, w1_ref, w2_ref, gf_ref, o_ref, *, ff_chunk, final):
    x = x_ref[...]
    x = x + jnp.dot(yr_ref[...], wo_ref[0:RET_W, :], preferred_element_type=F32)
    x = x + jnp.dot(yf_ref[...], wo_ref[RET_W:RET_W + FOX_W, :], preferred_element_type=F32)
    x = x + jnp.dot(yn_ref[...], wo_ref[RET_W + FOX_W:, :], preferred_element_type=F32)
    h = _rms(x, g_ref[...]).astype(BF16)
    o_ref[...] = x
    for c0 in range(0, D_FF, ff_chunk):
        hid = jnp.maximum(jnp.dot(h, w1_ref[:, c0:c0 + ff_chunk], preferred_element_type=F32), 0.0)
        o_ref[...] += jnp.dot((hid * hid).astype(BF16), w2_ref[c0:c0 + ff_chunk, :], preferred_element_type=F32)
    if final:
        o_ref[...] = _rms(o_ref[...], gf_ref[...])


def _post(x2d, yr, yf, yn, wo, g, w1, w2, gf, tm, final):
    T = x2d.shape[0]
    row = lambda w: pl.BlockSpec((tm, w), lambda i: (i, 0))
    return pl.pallas_call(
        functools.partial(_post_kernel, ff_chunk=512, final=final),
        grid=(T // tm,),
        in_specs=[
            row(D_MODEL), row(RET_W), row(FOX_W), row(NSA_W),
            _resident((D_MODEL, D_MODEL), lambda i: (0, 0)),
            _resident((1, D_MODEL), lambda i: (0, 0)),
            _resident((D_MODEL, D_FF), lambda i: (0, 0)),
            _resident((D_FF, D_MODEL), lambda i: (0, 0)),
            _resident((1, D_MODEL), lambda i: (0, 0)),
        ],
        out_specs=row(D_MODEL),
        out_shape=jax.ShapeDtypeStruct((T, D_MODEL), F32),
        compiler_params=_cparams(("parallel",)),
        name="outproj_mlp",
    )(x2d, yr, yf, yn, wo, g, w1, w2, gf)


def _tile_sizes(seq):
    return dict(tm=512, fox_tq=256, fox_tk=256, nsa_tq=128, nsa_tk=128)


def kernel(x, norm_attn, w_in, fox_forget_bias, ret_norm_gain, fox_norm_gain, nsa_norm_gain, nsa_cmp_pos_k, nsa_cmp_pos_v, nsa_cmp_w1_k, nsa_cmp_w2_k, nsa_cmp_w1_v, nsa_cmp_w2_v, w_out, norm_mlp, w_mlp_in, w_mlp_out, norm_final):
    B, S, D = x.shape
    depth = w_in.shape[0]
    assert D == D_MODEL and S % 512 == 0 and S >= WINDOW + 256
    T = B * S
    ts = _tile_sizes(S)

    cols, scale, small_cols, small_mask = _in_proj_columns()
    wm = (w_in[:, :, cols] * scale).astype(BF16)
    ws = (w_in[:, :, small_cols] * small_mask).astype(BF16)
    wo = w_out.astype(BF16)
    w1 = w_mlp_in.astype(BF16)
    w2 = w_mlp_out.astype(BF16)

    def expand_w1(wk, wv):
        L = wk.shape[0]
        wk = wk.reshape(L, CMP_LEN, HEAD_DIM, CMP_HIDDEN)
        wv = wv.reshape(L, CMP_LEN, HEAD_DIM, CMP_HIDDEN)
        z = jnp.zeros_like(wk)
        full = jnp.concatenate([jnp.concatenate([wk, z], axis=-1), jnp.concatenate([z, wv], axis=-1)], axis=2)
        full = full.reshape(L, CMP_LEN * LANES, 2 * CMP_HIDDEN).astype(BF16)
        return full[:, :CMP_STRIDE * LANES], full[:, CMP_STRIDE * LANES:]

    wt_all, wb_all = expand_w1(nsa_cmp_w1_k, nsa_cmp_w1_v)
    zk = jnp.zeros_like(nsa_cmp_w2_k)
    w2c = jnp.concatenate([jnp.concatenate([nsa_cmp_w2_k, zk], axis=-1),
                           jnp.concatenate([zk, nsa_cmp_w2_v], axis=-1)], axis=1).astype(BF16)
    pos = jnp.concatenate([nsa_cmp_pos_k, nsa_cmp_pos_v], axis=-1)
    pos_tb = pos.reshape(depth, 2, 1, CMP_STRIDE * LANES)
    pos_tb = jnp.broadcast_to(pos_tb, (depth, 2, 8, CMP_STRIDE * LANES)).reshape(depth, 16, CMP_STRIDE * LANES).astype(BF16)

    n_cmp = (S - CMP_LEN) // CMP_STRIDE + 1
    n_sel = S // SEL_LEN
    n_rows = S // CMP_STRIDE
    cs = np.arange(n_rows) * CMP_STRIDE
    ss = np.arange(n_sel) * SEL_LEN
    overlap = np.clip(np.minimum(cs[:, None] + CMP_LEN, ss[None, :] + SEL_LEN) - np.maximum(cs[:, None], ss[None, :]), 0, None)
    overlap[n_cmp:] = 0
    c2s = jnp.asarray(overlap / CMP_LEN, dtype=F32)
    assert n_sel <= HEAD_DIM, "the selected-branch key augmentation has 64 lanes, one per selection block"
    neg_onehot = jnp.asarray(np.where((np.arange(S)[:, None] // SEL_LEN) == np.arange(HEAD_DIM)[None, :], NEG, 0.0), dtype=BF16)

    tables = _retention_tables(S)
    fbias = jnp.zeros((depth, 1, LANES), F32).at[:, 0, SMALL_FF:SMALL_FF + N_FOX].set(fox_forget_bias)

    x2d = x.reshape(T, D)
    for l in range(depth):
        main, small = _inproj(x2d, norm_attn[l][None, :], wm[l], ws[l], ts["tm"])
        main3 = main.reshape(B, S, N_MAIN)
        small3 = small.reshape(B, S, LANES)
        cum_col, cum_row = _forget_cumsum(small3, fbias[l])
        y_ret = _retention(main3, tables, ret_norm_gain[l][None, :])
        y_fox = _fox(main3, cum_col, cum_row, fox_norm_gain[l][None, :], ts["fox_tq"], ts["fox_tk"])
        r = main3[:, :, CB_CMP * LANES:(CB_CMP + 1) * LANES].reshape(B, n_rows, CMP_STRIDE * LANES)
        cmp = _nsa_compress(r, pos_tb[l], wt_all[l], wb_all[l], w2c[l])
        ks = main3[:, :, CB_SEL * LANES:CB_SEL * LANES + HEAD_DIM]
        ks_aug = jnp.concatenate([ks, jnp.broadcast_to(neg_onehot[None], (B, S, HEAD_DIM))], axis=-1)
        y_nsa = _nsa(main3, cmp, c2s, ks_aug, small3, nsa_norm_gain[l][None, :], ts["nsa_tq"], ts["nsa_tk"], n_cmp)
        x2d = _post(x2d, y_ret.reshape(T, RET_W), y_fox.reshape(T, FOX_W), y_nsa.reshape(T, NSA_W), wo[l],
                    norm_mlp[l][None, :], w1[l], w2[l], norm_final[None, :], ts["tm"], final=(l == depth - 1))
    return x2d.reshape(B, S, D)
```

```python
import functools

import numpy as np
import jax
import jax.numpy as jnp
from jax import lax
from jax.experimental import pallas as pl
from jax.experimental.pallas import tpu as pltpu

F32 = jnp.float32
BF16 = jnp.bfloat16

D_MODEL = 1024
HEAD_DIM = 64
N_RET = 6
N_FOX = 6
N_NSA = 4
RET_W = N_RET * HEAD_DIM
FOX_W = N_FOX * HEAD_DIM
NSA_W = N_NSA * HEAD_DIM
D_FF = 4 * D_MODEL
RMS_EPS = 1e-6
RET_CHUNK = 128
ROPE_BASE = 10000.0
CMP_LEN = 32
CMP_STRIDE = 16
CMP_HIDDEN = 4 * HEAD_DIM
SEL_LEN = 64
TOP_N = 16
WINDOW = 512
FORCED_SCORE = 1e4
NEG = -1e30

LANES = 128
N_PAIR = N_RET // 2

CB_RQ, CB_RK, CB_RV, CB_RG = 0, 3, 6, 9
CB_NQ = 12
CB_FQ, CB_FK, CB_FV = 14, 17, 20
CB_CMP, CB_SEL, CB_WIN = 23, 24, 25
N_MAIN = 26 * LANES
SMALL_FF = 0
SMALL_GATE = 8

VMEM_LIMIT = 56 * 1024 * 1024


def _cparams(sem):
    return pltpu.CompilerParams(dimension_semantics=sem, vmem_limit_bytes=VMEM_LIMIT)


def _resident(shape, index_map):
    return pl.BlockSpec(shape, index_map, pipeline_mode=pl.Buffered(1))


def _in_proj_columns():
    sizes = (RET_W, RET_W, RET_W, RET_W, FOX_W, FOX_W, FOX_W, N_FOX, NSA_W) + (HEAD_DIM,) * 6 + (3 * N_NSA,)
    off = np.concatenate([[0], np.cumsum(sizes)])
    (o_rq, o_rk, o_rv, o_rg, o_fq, o_fk, o_fv, o_ff, o_nq, o_kc, o_vc, o_ks, o_vs, o_kw, o_vw, o_gt) = off[:-1]
    half = HEAD_DIM // 2
    inter = []
    for p in range(N_PAIR):
        a, b = 2 * p, 2 * p + 1
        for h, part in ((a, 0), (b, 0), (a, 1), (b, 1)):
            inter.extend(range(h * HEAD_DIM + part * half, h * HEAD_DIM + (part + 1) * half))
    inter = np.asarray(inter)
    nat = np.arange(RET_W)
    cols, scale = [], []

    def add(idx, s=1.0):
        cols.append(np.asarray(idx))
        scale.append(np.full(len(idx), s, np.float32))

    qk_scale = HEAD_DIM ** -0.5
    add(o_rq + inter)
    add(o_rk + inter, qk_scale)
    add(o_rv + nat)
    add(o_rg + nat)
    add(o_nq + np.arange(NSA_W), qk_scale)
    add(o_fq + nat, qk_scale)
    add(o_fk + nat)
    add(o_fv + nat)
    for o in (o_kc, o_vc, o_ks, o_vs, o_kw, o_vw):
        add(o + np.arange(HEAD_DIM))
    cols = np.concatenate(cols)
    scale = np.concatenate(scale)
    assert cols.shape[0] == N_MAIN
    small_cols = np.zeros(LANES, np.int64)
    small_mask = np.zeros(LANES, np.float32)
    small_cols[SMALL_FF:SMALL_FF + N_FOX] = o_ff + np.arange(N_FOX)
    small_mask[SMALL_FF:SMALL_FF + N_FOX] = 1.0
    small_cols[SMALL_GATE:SMALL_GATE + 3 * N_NSA] = o_gt + np.arange(3 * N_NSA)
    small_mask[SMALL_GATE:SMALL_GATE + 3 * N_NSA] = 1.0
    return cols, scale, small_cols, small_mask


def _retention_tables(seq):
    half = HEAD_DIM // 2
    inv = 1.0 / (ROPE_BASE ** (jnp.arange(half, dtype=F32) / half))
    ang = jnp.arange(seq, dtype=F32)[:, None] * inv[None, :]
    cos, sin = jnp.cos(ang), jnp.sin(ang)
    cos_t = jnp.concatenate([cos, cos, cos, cos], axis=-1)
    sin_t = jnp.concatenate([-sin, -sin, sin, sin], axis=-1)
    log_gamma = np.log(1.0 - 2.0 ** (-5.0 - np.arange(N_RET, dtype=np.float32))).astype(np.float32)
    C = RET_CHUNK
    idx = np.arange(C, dtype=np.float32)
    diff = idx[:, None] - idx[None, :]
    lane = np.arange(LANES)
    head_k = (lane % HEAD_DIM) // half
    head_v = lane // HEAD_DIM
    d_in = np.zeros((N_PAIR, 2, C, C), np.float32)
    d_k = np.zeros((N_PAIR, C, LANES), np.float32)
    d_q = np.zeros((N_PAIR, C, LANES), np.float32)
    d_c = np.zeros((N_PAIR, LANES, LANES), np.float32)
    for p in range(N_PAIR):
        lg = log_gamma[2 * p:2 * p + 2]
        for j in range(2):
            d_in[p, j] = np.where(diff >= 0, np.exp(lg[j] * np.maximum(diff, 0.0)), 0.0)
        d_k[p] = np.exp(lg[head_k][None, :] * (C - 1.0 - idx)[:, None])
        d_q[p] = np.exp(lg[head_v][None, :] * (idx + 1.0)[:, None])
        d_c[p] = np.broadcast_to(np.exp(lg[head_v] * C)[None, :], (LANES, LANES))
    s_mask = (head_k[:, None] == head_v[None, :]).astype(np.float32)
    return cos_t, sin_t, jnp.asarray(d_in), jnp.asarray(d_k), jnp.asarray(d_q), jnp.asarray(d_c), jnp.asarray(s_mask)


def _rms(x, g):
    return x * lax.rsqrt(jnp.mean(x * x, axis=-1, keepdims=True) + RMS_EPS) * g


def _inproj_kernel(x_ref, g_ref, wm_ref, ws_ref, om_ref, os_ref, *, n_chunk):
    h = _rms(x_ref[...], g_ref[...]).astype(BF16)
    for n0 in range(0, N_MAIN, n_chunk):
        n1 = min(n0 + n_chunk, N_MAIN)
        om_ref[:, n0:n1] = jnp.dot(h, wm_ref[:, n0:n1], preferred_element_type=F32).astype(BF16)
    os_ref[...] = jnp.dot(h, ws_ref[...], preferred_element_type=F32)


def _inproj(x2d, g, wm, ws, tm):
    T = x2d.shape[0]
    return pl.pallas_call(
        functools.partial(_inproj_kernel, n_chunk=512),
        grid=(T // tm,),
        in_specs=[
            pl.BlockSpec((tm, D_MODEL), lambda i: (i, 0)),
            _resident((1, D_MODEL), lambda i: (0, 0)),
            _resident((D_MODEL, N_MAIN), lambda i: (0, 0)),
            _resident((D_MODEL, LANES), lambda i: (0, 0)),
        ],
        out_specs=[
            pl.BlockSpec((tm, N_MAIN), lambda i: (i, 0)),
            pl.BlockSpec((tm, LANES), lambda i: (i, 0)),
        ],
        out_shape=[jax.ShapeDtypeStruct((T, N_MAIN), BF16), jax.ShapeDtypeStruct((T, LANES), F32)],
        compiler_params=_cparams(("parallel",)),
        name="inproj",
    )(x2d, g, wm, ws)


def _cum_kernel(z_ref, b_ref, tri_ref, col_ref, row_ref, *, n_chunks):
    C = RET_CHUNK

    def body(c, carry):
        r0 = pl.multiple_of(c * C, C)
        z = z_ref[0, pl.ds(r0, C), :] + b_ref[...]
        lf = jnp.minimum(z, 0.0) - jnp.log1p(jnp.exp(-jnp.abs(z)))
        cs = jnp.dot(tri_ref[...], lf, preferred_element_type=F32, precision=lax.Precision.HIGHEST) + carry
        col_ref[0, pl.ds(r0, C), :] = cs
        row_ref[0, :, pl.ds(r0, C)] = cs.T[0:8, :]
        return cs[C - 1:C, :]

    lax.fori_loop(0, n_chunks, body, jnp.zeros((1, LANES), F32))


def _forget_cumsum(small3, bias_row):
    B, S, _ = small3.shape
    tri = jnp.asarray(np.tril(np.ones((RET_CHUNK, RET_CHUNK), np.float32)))
    return pl.pallas_call(
        functools.partial(_cum_kernel, n_chunks=S // RET_CHUNK),
        grid=(B,),
        in_specs=[
            pl.BlockSpec((1, S, LANES), lambda b: (b, 0, 0)),
            pl.BlockSpec((1, LANES), lambda b: (0, 0)),
            pl.BlockSpec((RET_CHUNK, RET_CHUNK), lambda b: (0, 0)),
        ],
        out_specs=[
            pl.BlockSpec((1, S, LANES), lambda b: (b, 0, 0)),
            pl.BlockSpec((1, 8, S), lambda b: (b, 0, 0)),
        ],
        out_shape=[jax.ShapeDtypeStruct((B, S, LANES), F32), jax.ShapeDtypeStruct((B, 8, S), F32)],
        compiler_params=_cparams(("parallel",)),
        name="forget_cumsum",
    )(small3, bias_row, tri)


def _dot_nt(a, b):
    return lax.dot_general(a, b, (((1,), (1,)), ((), ())), preferred_element_type=F32)


def _ret_kernel(q_ref, k_ref, v_ref, g_ref, cos_ref, sin_ref, din_ref, dk_ref, dq_ref, dc_ref, sm_ref, gain_ref,
                o_ref, state_ref, *, n_chunks):
    C = RET_CHUNK
    half = HEAD_DIM // 2
    lane = lax.broadcasted_iota(jnp.int32, (C, LANES), 1)
    khead = (lane % HEAD_DIM) // half
    vhead0 = lane < HEAD_DIM
    state_ref[...] = jnp.zeros_like(state_ref)

    def body(c, _):
        r0 = pl.multiple_of(c * C, C)
        rows = pl.ds(r0, C)
        cs, sn = cos_ref[rows, :], sin_ref[rows, :]
        q = q_ref[0, rows, :].astype(F32)
        k = k_ref[0, rows, :].astype(F32)
        q = q * cs + pltpu.roll(q, HEAD_DIM, 1) * sn
        k = k * cs + pltpu.roll(k, HEAD_DIM, 1) * sn
        v = v_ref[0, rows, :]
        qb, kb = q.astype(BF16), k.astype(BF16)
        state = state_ref[...]
        cross = jnp.dot(qb, state.astype(BF16), preferred_element_type=F32) * dq_ref[0]
        outs = []
        for j in range(2):
            qm = jnp.where(khead == j, qb, jnp.zeros_like(qb))
            inner = _dot_nt(qm, kb) * din_ref[0, j]
            outs.append(jnp.dot(inner.astype(BF16), v, preferred_element_type=F32))
        out = jnp.where(vhead0, outs[0], outs[1]) + cross
        kd = (k * dk_ref[0]).T.astype(BF16)
        state_ref[...] = state * dc_ref[0] + jnp.dot(kd, v, preferred_element_type=F32) * sm_ref[...]
        inv = 1.0 / HEAD_DIM
        s0 = jnp.sum(jnp.where(vhead0, out, 0.0), axis=-1, keepdims=True)
        s1 = jnp.sum(jnp.where(vhead0, 0.0, out), axis=-1, keepdims=True)
        yc = out - jnp.where(vhead0, s0, s1) * inv
        yc2 = yc * yc
        v0 = jnp.sum(jnp.where(vhead0, yc2, 0.0), axis=-1, keepdims=True)
        v1 = jnp.sum(jnp.where(vhead0, 0.0, yc2), axis=-1, keepdims=True)
        y = yc * lax.rsqrt(jnp.where(vhead0, v0, v1) * inv + RMS_EPS) * gain_ref[...]
        g = g_ref[0, rows, :].astype(F32)
        o_ref[0, rows, :] = (y * (g * (1.0 / (1.0 + jnp.exp(-g))))).astype(BF16)
        return 0

    lax.fori_loop(0, n_chunks, body, 0)


def _retention(main3, tables, gain_row):
    B, S, _ = main3.shape
    cos_t, sin_t, d_in, d_k, d_q, d_c, s_mask = tables
    C = RET_CHUNK
    seq_spec = lambda cb: pl.BlockSpec((1, S, LANES), lambda b, p: (b, 0, cb + p))
    return pl.pallas_call(
        functools.partial(_ret_kernel, n_chunks=S // C),
        grid=(B, N_PAIR),
        in_specs=[
            seq_spec(CB_RQ), seq_spec(CB_RK), seq_spec(CB_RV), seq_spec(CB_RG),
            pl.BlockSpec((S, LANES), lambda b, p: (0, 0)),
            pl.BlockSpec((S, LANES), lambda b, p: (0, 0)),
            pl.BlockSpec((1, 2, C, C), lambda b, p: (p, 0, 0, 0)),
            pl.BlockSpec((1, C, LANES), lambda b, p: (p, 0, 0)),
            pl.BlockSpec((1, C, LANES), lambda b, p: (p, 0, 0)),
            pl.BlockSpec((1, LANES, LANES), lambda b, p: (p, 0, 0)),
            pl.BlockSpec((LANES, LANES), lambda b, p: (0, 0)),
            pl.BlockSpec((1, LANES), lambda b, p: (0, p)),
        ],
        out_specs=pl.BlockSpec((1, S, LANES), lambda b, p: (b, 0, p)),
        out_shape=jax.ShapeDtypeStruct((B, S, RET_W), BF16),
        scratch_shapes=[pltpu.VMEM((LANES, LANES), F32)],
        compiler_params=_cparams(("parallel", "parallel")),
        name="retention",
    )(main3, main3, main3, main3, cos_t, sin_t, d_in, d_k, d_q, d_c, s_mask, gain_row)


def _fox_kernel(q_ref, k_ref, v_ref, ccol_ref, crow_ref, gain_ref, o_ref, *, tq, tk):
    p = pl.program_id(1)
    qi = pl.program_id(2)
    q = q_ref[0]
    lane_q = lax.broadcasted_iota(jnp.int32, (tq, LANES), 1)
    head0 = lane_q < HEAD_DIM
    ccol = ccol_ref[0]
    qms, cts = [], []
    for j in range(2):
        qms.append(jnp.where(head0 if j == 0 else jnp.logical_not(head0), q, jnp.zeros_like(q)))
        cts.append(jnp.sum(jnp.where(lane_q == 2 * p + j, ccol, 0.0), axis=-1, keepdims=True))
    q_pos = qi * tq + lax.broadcasted_iota(jnp.int32, (tq, tk), 0)
    k_off = lax.broadcasted_iota(jnp.int32, (tq, tk), 1)
    head0_k = lax.broadcasted_iota(jnp.int32, (tk, LANES), 1) < HEAD_DIM
    one = jnp.ones((tk, LANES), BF16)

    def tile(t, carry, masked):
        c0 = pl.multiple_of(t * tk, tk)
        k = k_ref[0, pl.ds(c0, tk), :]
        v = v_ref[0, pl.ds(c0, tk), :]
        new = []
        for j in range(2):
            m, acc = carry[2 * j:2 * j + 2]
            crow = crow_ref[0, pl.ds(2 * p + j, 1), pl.ds(c0, tk)]
            s = _dot_nt(qms[j], k) + (cts[j] - crow)
            if masked:
                s = jnp.where(k_off + c0 <= q_pos, s, NEG)
            m_new = jnp.maximum(m, jnp.max(s, axis=-1, keepdims=True))
            pr = jnp.exp(s - m_new)
            vj = jnp.where(head0_k if j == 0 else jnp.logical_not(head0_k), v, one)
            acc = jnp.exp(m - m_new) * acc + jnp.dot(pr.astype(BF16), vj, preferred_element_type=F32)
            new += [m_new, acc]
        return tuple(new)

    init = []
    for _ in range(2):
        init += [jnp.full((tq, 1), NEG, F32), jnp.zeros((tq, LANES), F32)]
    n_full = (qi * tq) // tk
    carry = lax.fori_loop(0, n_full, lambda t, c: tile(t, c, False), tuple(init))
    for d in range(max(1, tq // tk)):
        carry = tile(n_full + d, carry, True)
    o0 = carry[1] / carry[1][:, HEAD_DIM:HEAD_DIM + 1]
    o1 = carry[3] / carry[3][:, 0:1]
    out = jnp.where(head0, o0, o1)
    out2 = out * out
    inv = 1.0 / HEAD_DIM
    v0 = jnp.sum(jnp.where(head0, out2, 0.0), axis=-1, keepdims=True)
    v1 = jnp.sum(jnp.where(head0, 0.0, out2), axis=-1, keepdims=True)
    o_ref[0] = (out * lax.rsqrt(jnp.where(head0, v0, v1) * inv + RMS_EPS) * gain_ref[...]).astype(BF16)


def _fox(main3, cum_col, cum_row, gain_row, tq, tk):
    B, S, _ = main3.shape
    return pl.pallas_call(
        functools.partial(_fox_kernel, tq=tq, tk=tk),
        grid=(B, N_PAIR, S // tq),
        in_specs=[
            pl.BlockSpec((1, tq, LANES), lambda b, p, i: (b, i, CB_FQ + p)),
            pl.BlockSpec((1, S, LANES), lambda b, p, i: (b, 0, CB_FK + p)),
            pl.BlockSpec((1, S, LANES), lambda b, p, i: (b, 0, CB_FV + p)),
            pl.BlockSpec((1, tq, LANES), lambda b, p, i: (b, i, 0)),
            pl.BlockSpec((1, 8, S), lambda b, p, i: (b, 0, 0)),
            pl.BlockSpec((1, LANES), lambda b, p, i: (0, p)),
        ],
        out_specs=pl.BlockSpec((1, tq, LANES), lambda b, p, i: (b, i, p)),
        out_shape=jax.ShapeDtypeStruct((B, S, FOX_W), BF16),
        compiler_params=_cparams(("parallel", "parallel", "arbitrary")),
        name="fox_attention",
    )(main3, main3, main3, cum_col, cum_row, gain_row)


def _gelu_tanh(x):
    return 0.5 * x * (1.0 + jnp.tanh(np.sqrt(2.0 / np.pi).astype(np.float32) * (x + 0.044715 * (x * x * x))))


def _cmp_kernel(r_ref, pos_ref, wt_ref, wb_ref, w2_ref, o_ref):
    r = r_ref[0]
    n = r.shape[0]
    top = jnp.dot(r, wt_ref[...], preferred_element_type=F32)
    bot = jnp.dot(r, wb_ref[...], preferred_element_type=F32)
    cpos = jnp.dot(pos_ref[0:8, :], wt_ref[...], preferred_element_type=F32)
    cpos = cpos + jnp.dot(pos_ref[8:16, :], wb_ref[...], preferred_element_type=F32)
    hid = _gelu_tanh(top + pltpu.roll(bot, n - 1, 0) + cpos[0:1, :])
    o_ref[0] = jnp.dot(hid.astype(BF16), w2_ref[...], preferred_element_type=F32)


def _nsa_compress(r, pos2, wt, wb, w2):
    B, n, _ = r.shape
    width = CMP_STRIDE * LANES
    return pl.pallas_call(
        _cmp_kernel,
        grid=(B,),
        in_specs=[
            pl.BlockSpec((1, n, width), lambda b: (b, 0, 0)),
            pl.BlockSpec((16, width), lambda b: (0, 0)),
            pl.BlockSpec((width, 2 * CMP_HIDDEN), lambda b: (0, 0)),
            pl.BlockSpec((width, 2 * CMP_HIDDEN), lambda b: (0, 0)),
            pl.BlockSpec((2 * CMP_HIDDEN, LANES), lambda b: (0, 0)),
        ],
        out_specs=pl.BlockSpec((1, n, LANES), lambda b: (b, 0, 0)),
        out_shape=jax.ShapeDtypeStruct((B, n, LANES), F32),
        compiler_params=_cparams(("parallel",)),
        name="nsa_compress",
    )(r, pos2, wt, wb, w2)


def _softmax_rows(s, mask):
    sm = jnp.where(mask, s, NEG)
    m = jnp.max(sm, axis=-1, keepdims=True)
    e = jnp.where(mask, jnp.exp(sm - m), 0.0)
    l = jnp.sum(e, axis=-1, keepdims=True)
    return e / jnp.where(l > 0.0, l, 1.0)


def _nsa_kernel(q_ref, cmp_ref, c2s_ref, sel_ref, ksa_ref, win_ref, gl_ref, gain_ref, o_ref, *, tq, tk, n_cmp, n_sel, top_n):
    H = N_NSA
    qi = pl.program_id(1)
    start = qi * tq
    q = q_ref[0]
    t_row = start + lax.broadcasted_iota(jnp.int32, (tq, 1), 0)
    t_all = jnp.concatenate([t_row] * H, axis=0)
    zeros64 = jnp.zeros((tq, HEAD_DIM), BF16)
    q_heads = [q[:, h * HEAD_DIM:(h + 1) * HEAD_DIM] for h in range(H)]
    q_pad = jnp.concatenate([jnp.concatenate([qh, zeros64], axis=1) for qh in q_heads], axis=0)

    cmp = cmp_ref[0]
    n_pad = cmp.shape[0]
    cmp_hi = cmp.astype(BF16)
    cmp_lo = (cmp - cmp_hi.astype(F32)).astype(BF16)
    s_c = _dot_nt(q_pad, cmp_hi) + _dot_nt(q_pad, cmp_lo)
    n_id = lax.broadcasted_iota(jnp.int32, (H * tq, n_pad), 1)
    vis_c = jnp.logical_and(n_id * CMP_STRIDE + (CMP_LEN - 1) <= t_all, n_id < n_cmp)
    p_c = _softmax_rows(s_c, vis_c)
    o_c = jnp.dot(p_c.astype(BF16), cmp_hi, preferred_element_type=F32)

    p_sum = p_c[0:tq]
    for h in range(1, H):
        p_sum = p_sum + p_c[h * tq:(h + 1) * tq]
    p_hi = p_sum.astype(BF16)
    p_lo = (p_sum - p_hi.astype(F32)).astype(BF16)
    c2s_t = c2s_ref[...]
    imp = _dot_nt(c2s_t, p_hi) + _dot_nt(c2s_t, p_lo)
    t_lane = start + lax.broadcasted_iota(jnp.int32, (HEAD_DIM, tq), 1)
    s_id = lax.broadcasted_iota(jnp.int32, (HEAD_DIM, tq), 0)
    cur = t_lane >> 6
    forced = jnp.logical_or(s_id == 0, jnp.logical_or(s_id == cur, s_id == cur - 1))
    score = jnp.where(forced, FORCED_SCORE, imp)
    score = jnp.where(s_id * SEL_LEN <= t_lane, score, -1.0)
    SUB = 8
    groups = [score[g * SUB:(g + 1) * SUB, :] for g in range(HEAD_DIM // SUB)]
    ranks = [jnp.zeros((SUB, tq), jnp.int32) for _ in groups]
    sub_id = lax.broadcasted_iota(jnp.int32, (SUB, tq), 0)
    for c in range(n_sel):
        row = score[c:c + 1, :]
        for g in range(len(groups)):
            if g * SUB > c:
                before = row >= groups[g]
            elif g * SUB + SUB - 1 < c:
                before = row > groups[g]
            else:
                before = jnp.logical_or(row > groups[g], jnp.logical_and(row == groups[g], sub_id > c - g * SUB))
            ranks[g] = ranks[g] + before.astype(jnp.int32)
    not_sel = jnp.concatenate([jnp.where(r < top_n, 0.0, 1.0) for r in ranks], axis=0)
    ns = jnp.concatenate([jnp.zeros((HEAD_DIM, tq), F32), not_sel], axis=0).T.astype(BF16)
    q_aug = q_pad + jnp.concatenate([ns] * H, axis=0)

    def sel_tile(c0, size, carry, diagonal):
        m, acc = carry
        s = _dot_nt(q_aug, ksa_ref[0, pl.ds(c0, size), :])
        if diagonal:
            r_id = lax.broadcasted_iota(jnp.int32, (H * tq, size), 0) & (tq - 1)
            c_id = lax.broadcasted_iota(jnp.int32, (H * tq, size), 1)
            s = jnp.where(c_id <= r_id, s, NEG)
        m_new = jnp.maximum(m, jnp.max(s, axis=-1, keepdims=True))
        pr = jnp.exp(s - m_new)
        key_lanes = lax.broadcasted_iota(jnp.int32, (size, LANES), 1) < HEAD_DIM
        v1 = jnp.where(key_lanes, jnp.ones((size, LANES), BF16), sel_ref[0, pl.ds(c0, size), :])
        acc = jnp.exp(m - m_new) * acc + jnp.dot(pr.astype(BF16), v1, preferred_element_type=F32)
        return m_new, acc

    carry = (jnp.full((H * tq, 1), NEG, F32), jnp.zeros((H * tq, LANES), F32))
    n_full = start // tk
    carry = lax.fori_loop(0, n_full, lambda t, c: sel_tile(pl.multiple_of(t * tk, tk), tk, c, False), carry)
    n_tail = (start - n_full * tk) // tq
    for d in range(tk // tq - 1):
        c0 = pl.multiple_of(n_full * tk + d * tq, tq)
        carry = lax.cond(d < n_tail, lambda c, c0=c0: sel_tile(c0, tq, c, False), lambda c: c, carry)
    carry = sel_tile(pl.multiple_of(start, tq), tq, carry, True)
    o_s = carry[1] / carry[1][:, 0:1]

    wlen = WINDOW + tq
    base = pl.multiple_of(jnp.maximum(start - WINDOW, 0), tq)
    win = win_ref[0, pl.ds(base, wlen), :]
    s_w = _dot_nt(q_pad, win)
    kpos = base + lax.broadcasted_iota(jnp.int32, (H * tq, wlen), 1)
    vis_w = jnp.logical_and(kpos <= t_all, kpos > t_all - WINDOW)
    s_w = jnp.where(vis_w, s_w, NEG)
    e_w = jnp.exp(s_w - jnp.max(s_w, axis=-1, keepdims=True))
    win1 = jnp.where(lax.broadcasted_iota(jnp.int32, (wlen, LANES), 1) < HEAD_DIM, jnp.ones((wlen, LANES), BF16), win)
    o_w = jnp.dot(e_w.astype(BF16), win1, preferred_element_type=F32)
    o_w = o_w / o_w[:, 0:1]

    gl = gl_ref[0]
    gates = 1.0 / (1.0 + jnp.exp(-gl))
    outs = []
    for h in range(H):
        rows = slice(h * tq, (h + 1) * tq)
        g0 = gates[:, SMALL_GATE + 3 * h + 0:SMALL_GATE + 3 * h + 1]
        g1 = gates[:, SMALL_GATE + 3 * h + 1:SMALL_GATE + 3 * h + 2]
        g2 = gates[:, SMALL_GATE + 3 * h + 2:SMALL_GATE + 3 * h + 3]
        mix = g0 * o_c[rows] + g1 * o_s[rows] + g2 * o_w[rows]
        val = mix[:, HEAD_DIM:]
        outs.append(val * lax.rsqrt(jnp.mean(val * val, axis=-1, keepdims=True) + RMS_EPS))
    o_ref[0] = (jnp.concatenate(outs, axis=1) * gain_ref[...]).astype(BF16)


def _nsa(main3, cmp, c2s, ks_aug, small3, gain_row, tq, tk, n_cmp):
    B, S, _ = main3.shape
    n_sel = S // SEL_LEN
    n_pad = cmp.shape[1]
    return pl.pallas_call(
        functools.partial(_nsa_kernel, tq=tq, tk=tk, n_cmp=n_cmp, n_sel=n_sel, top_n=min(TOP_N, n_sel)),
        grid=(B, S // tq),
        in_specs=[
            pl.BlockSpec((1, tq, NSA_W), lambda b, i: (b, i, CB_NQ // 2)),
            pl.BlockSpec((1, n_pad, LANES), lambda b, i: (b, 0, 0)),
            pl.BlockSpec((HEAD_DIM, n_pad), lambda b, i: (0, 0)),
            pl.BlockSpec((1, S, LANES), lambda b, i: (b, 0, CB_SEL)),
            pl.BlockSpec((1, S, LANES), lambda b, i: (b, 0, 0)),
            pl.BlockSpec((1, S, LANES), lambda b, i: (b, 0, CB_WIN)),
            pl.BlockSpec((1, tq, LANES), lambda b, i: (b, i, 0)),
            pl.BlockSpec((1, NSA_W), lambda b, i: (0, 0)),
        ],
        out_specs=pl.BlockSpec((1, tq, NSA_W), lambda b, i: (b, i, 0)),
        out_shape=jax.ShapeDtypeStruct((B, S, NSA_W), BF16),
        compiler_params=_cparams(("parallel", "arbitrary")),
        name="nsa_attention",
    )(main3, cmp, c2s, main3, ks_aug, main3, small3, gain_row)


def _post_kernel(x_ref, yr_ref, yf_ref, yn_ref, wo_ref, g_ref, w1_ref, w2_ref, gf_ref, o_ref, *, ff_chunk, final):
    x = x_ref[...]
    x = x + jnp.dot(yr_ref[...], wo_ref[0:RET_W, :], preferred_element_type=F32)
    x = x + jnp.dot(yf_ref[...], wo_ref[RET_W:RET_W + FOX_W, :], preferred_element_type=F32)
    x = x + jnp.dot(yn_ref[...], wo_ref[RET_W + FOX_W:, :], preferred_element_type=F32)
    h = _rms(x, g_ref[...]).astype(BF16)
    o_ref[...] = x
    for c0 in range(0, D_FF, ff_chunk):
        hid = jnp.maximum(jnp.dot(h, w1_ref[:, c0:c0 + ff_chunk], preferred_element_type=F32), 0.0)
        o_ref[...] += jnp.dot((hid * hid).astype(BF16), w2_ref[c0:c0 + ff_chunk, :], preferred_element_type=F32)
    if final:
        o_ref[...] = _rms(o_ref[...], gf_ref[...])


def _post(x2d, yr, yf, yn, wo, g, w1, w2, gf, tm, final):
    T = x2d.shape[0]
    row = lambda w: pl.BlockSpec((tm, w), lambda i: (i, 0))
    return pl.pallas_call(
        functools.partial(_post_kernel, ff_chunk=512, final=final),
        grid=(T // tm,),
        in_specs=[
            row(D_MODEL), row(RET_W), row(FOX_W), row(NSA_W),
            _resident((D_MODEL, D_MODEL), lambda i: (0, 0)),
            _resident((1, D_MODEL), lambda i: (0, 0)),
            _resident((D_MODEL, D_FF), lambda i: (0, 0)),
            _resident((D_FF, D_MODEL), lambda i: (0, 0)),
            _resident((1, D_MODEL), lambda i: (0, 0)),
        ],
        out_specs=row(D_MODEL),
        out_shape=jax.ShapeDtypeStruct((T, D_MODEL), F32),
        compiler_params=_cparams(("parallel",)),
        name="outproj_mlp",
    )(x2d, yr, yf, yn, wo, g, w1, w2, gf)


def _tile_sizes(seq):
    return dict(tm=512, fox_tq=512, fox_tk=512, nsa_tq=256, nsa_tk=512)


def kernel(x, norm_attn, w_in, fox_forget_bias, ret_norm_gain, fox_norm_gain, nsa_norm_gain, nsa_cmp_pos_k, nsa_cmp_pos_v, nsa_cmp_w1_k, nsa_cmp_w2_k, nsa_cmp_w1_v, nsa_cmp_w2_v, w_out, norm_mlp, w_mlp_in, w_mlp_out, norm_final):
    B, S, D = x.shape
    depth = w_in.shape[0]
    assert D == D_MODEL and S % 512 == 0 and S >= WINDOW + 256
    T = B * S
    ts = _tile_sizes(S)

    cols, scale, small_cols, small_mask = _in_proj_columns()
    wm = (w_in[:, :, cols] * scale).astype(BF16)
    ws = (w_in[:, :, small_cols] * small_mask).astype(BF16)
    wo = w_out.astype(BF16)
    w1 = w_mlp_in.astype(BF16)
    w2 = w_mlp_out.astype(BF16)

    def expand_w1(wk, wv):
        L = wk.shape[0]
        wk = wk.reshape(L, CMP_LEN, HEAD_DIM, CMP_HIDDEN)
        wv = wv.reshape(L, CMP_LEN, HEAD_DIM, CMP_HIDDEN)
        z = jnp.zeros_like(wk)
        full = jnp.concatenate([jnp.concatenate([wk, z], axis=-1), jnp.concatenate([z, wv], axis=-1)], axis=2)
        full = full.reshape(L, CMP_LEN * LANES, 2 * CMP_HIDDEN).astype(BF16)
        return full[:, :CMP_STRIDE * LANES], full[:, CMP_STRIDE * LANES:]

    wt_all, wb_all = expand_w1(nsa_cmp_w1_k, nsa_cmp_w1_v)
    zk = jnp.zeros_like(nsa_cmp_w2_k)
    w2c = jnp.concatenate([jnp.concatenate([nsa_cmp_w2_k, zk], axis=-1),
                           jnp.concatenate([zk, nsa_cmp_w2_v], axis=-1)], axis=1).astype(BF16)
    pos = jnp.concatenate([nsa_cmp_pos_k, nsa_cmp_pos_v], axis=-1)
    pos_tb = pos.reshape(depth, 2, 1, CMP_STRIDE * LANES)
    pos_tb = jnp.broadcast_to(pos_tb, (depth, 2, 8, CMP_STRIDE * LANES)).reshape(depth, 16, CMP_STRIDE * LANES).astype(BF16)

    n_cmp = (S - CMP_LEN) // CMP_STRIDE + 1
    n_sel = S // SEL_LEN
    n_rows = S // CMP_STRIDE
    cs = np.arange(n_rows) * CMP_STRIDE
    ss = np.arange(n_sel) * SEL_LEN
    overlap = np.clip(np.minimum(cs[:, None] + CMP_LEN, ss[None, :] + SEL_LEN) - np.maximum(cs[:, None], ss[None, :]), 0, None)
    overlap[n_cmp:] = 0
    c2s_np = np.zeros((HEAD_DIM, n_rows), np.float32)
    c2s_np[:n_sel] = (overlap / CMP_LEN).T
    c2s = jnp.asarray(c2s_np, dtype=BF16)
    assert n_sel <= HEAD_DIM, "the selected-branch key augmentation has 64 lanes, one per selection block"
    neg_onehot = jnp.asarray(np.where((np.arange(S)[:, None] // SEL_LEN) == np.arange(HEAD_DIM)[None, :], NEG, 0.0), dtype=BF16)

    tables = _retention_tables(S)
    fbias = jnp.zeros((depth, 1, LANES), F32).at[:, 0, SMALL_FF:SMALL_FF + N_FOX].set(fox_forget_bias)

    x2d = x.reshape(T, D)
    for l in range(depth):
        main, small = _inproj(x2d, norm_attn[l][None, :], wm[l], ws[l], ts["tm"])
        main3 = main.reshape(B, S, N_MAIN)
        small3 = small.reshape(B, S, LANES)
        cum_col, cum_row = _forget_cumsum(small3, fbias[l])
        y_ret = _retention(main3, tables, ret_norm_gain[l][None, :])
        y_fox = _fox(main3, cum_col, cum_row, fox_norm_gain[l][None, :], ts["fox_tq"], ts["fox_tk"])
        r = main3[:, :, CB_CMP * LANES:(CB_CMP + 1) * LANES].reshape(B, n_rows, CMP_STRIDE * LANES)
        cmp = _nsa_compress(r, pos_tb[l], wt_all[l], wb_all[l], w2c[l])
        ks = main3[:, :, CB_SEL * LANES:CB_SEL * LANES + HEAD_DIM]
        ks_aug = jnp.concatenate([ks, jnp.broadcast_to(neg_onehot[None], (B, S, HEAD_DIM))], axis=-1)
        y_nsa = _nsa(main3, cmp, c2s, ks_aug, small3, nsa_norm_gain[l][None, :], ts["nsa_tq"], ts["nsa_tk"], n_cmp)
        x2d = _post(x2d, y_ret.reshape(T, RET_W), y_fox.reshape(T, FOX_W), y_nsa.reshape(T, NSA_W), wo[l],
                    norm_mlp[l][None, :], w1[l], w2[l], norm_final[None, :], ts["tm"], final=(l == depth - 1))
    return x2d.reshape(B, S, D)
```

```python
import functools

import numpy as np
import jax
import jax.numpy as jnp
from jax import lax
from jax.experimental import pallas as pl
from jax.experimental.pallas import tpu as pltpu

F32 = jnp.float32
BF16 = jnp.bfloat16

D_MODEL = 1024
HEAD_DIM = 64
N_RET = 6
N_FOX = 6
N_NSA = 4
RET_W = N_RET * HEAD_DIM
FOX_W = N_FOX * HEAD_DIM
NSA_W = N_NSA * HEAD_DIM
D_FF = 4 * D_MODEL
RMS_EPS = 1e-6
RET_CHUNK = 128
ROPE_BASE = 10000.0
CMP_LEN = 32
CMP_STRIDE = 16
CMP_HIDDEN = 4 * HEAD_DIM
SEL_LEN = 64
TOP_N = 16
WINDOW = 512
FORCED_SCORE = 1e4
NEG = -1e30

LANES = 128
N_PAIR = N_RET // 2

CB_RQ, CB_RK, CB_RV, CB_RG = 0, 3, 6, 9
CB_NQ = 12
CB_FQ, CB_FK, CB_FV = 14, 17, 20
CB_CMP, CB_SEL, CB_WIN = 23, 24, 25
N_MAIN = 26 * LANES
SMALL_FF = 0
SMALL_GATE = 8

VMEM_LIMIT = 56 * 1024 * 1024


def _cparams(sem):
    return pltpu.CompilerParams(dimension_semantics=sem, vmem_limit_bytes=VMEM_LIMIT)


def _resident(shape, index_map):
    return pl.BlockSpec(shape, index_map, pipeline_mode=pl.Buffered(1))


def _in_proj_columns():
    sizes = (RET_W, RET_W, RET_W, RET_W, FOX_W, FOX_W, FOX_W, N_FOX, NSA_W) + (HEAD_DIM,) * 6 + (3 * N_NSA,)
    off = np.concatenate([[0], np.cumsum(sizes)])
    (o_rq, o_rk, o_rv, o_rg, o_fq, o_fk, o_fv, o_ff, o_nq, o_kc, o_vc, o_ks, o_vs, o_kw, o_vw, o_gt) = off[:-1]
    half = HEAD_DIM // 2
    inter = []
    for p in range(N_PAIR):
        a, b = 2 * p, 2 * p + 1
        for h, part in ((a, 0), (b, 0), (a, 1), (b, 1)):
            inter.extend(range(h * HEAD_DIM + part * half, h * HEAD_DIM + (part + 1) * half))
    inter = np.asarray(inter)
    nat = np.arange(RET_W)
    cols, scale = [], []

    def add(idx, s=1.0):
        cols.append(np.asarray(idx))
        scale.append(np.full(len(idx), s, np.float32))

    qk_scale = HEAD_DIM ** -0.5
    add(o_rq + inter)
    add(o_rk + inter, qk_scale)
    add(o_rv + nat)
    add(o_rg + nat)
    add(o_nq + np.arange(NSA_W), qk_scale)
    add(o_fq + nat, qk_scale)
    add(o_fk + nat)
    add(o_fv + nat)
    for o in (o_kc, o_vc, o_ks, o_vs, o_kw, o_vw):
        add(o + np.arange(HEAD_DIM))
    cols = np.concatenate(cols)
    scale = np.concatenate(scale)
    assert cols.shape[0] == N_MAIN
    small_cols = np.zeros(LANES, np.int64)
    small_mask = np.zeros(LANES, np.float32)
    small_cols[SMALL_FF:SMALL_FF + N_FOX] = o_ff + np.arange(N_FOX)
    small_mask[SMALL_FF:SMALL_FF + N_FOX] = 1.0
    small_cols[SMALL_GATE:SMALL_GATE + 3 * N_NSA] = o_gt + np.arange(3 * N_NSA)
    small_mask[SMALL_GATE:SMALL_GATE + 3 * N_NSA] = 1.0
    return cols, scale, small_cols, small_mask


def _retention_tables(seq):
    half = HEAD_DIM // 2
    inv = 1.0 / (ROPE_BASE ** (jnp.arange(half, dtype=F32) / half))
    ang = jnp.arange(seq, dtype=F32)[:, None] * inv[None, :]
    cos, sin = jnp.cos(ang), jnp.sin(ang)
    cos_t = jnp.concatenate([cos, cos, cos, cos], axis=-1)
    sin_t = jnp.concatenate([-sin, -sin, sin, sin], axis=-1)
    log_gamma = np.log(1.0 - 2.0 ** (-5.0 - np.arange(N_RET, dtype=np.float32))).astype(np.float32)
    C = RET_CHUNK
    idx = np.arange(C, dtype=np.float32)
    diff = idx[:, None] - idx[None, :]
    lane = np.arange(LANES)
    head_k = (lane % HEAD_DIM) // half
    head_v = lane // HEAD_DIM
    d_in = np.zeros((N_PAIR, 2, C, C), np.float32)
    d_k = np.zeros((N_PAIR, C, LANES), np.float32)
    d_q = np.zeros((N_PAIR, C, LANES), np.float32)
    d_c = np.zeros((N_PAIR, LANES, LANES), np.float32)
    for p in range(N_PAIR):
        lg = log_gamma[2 * p:2 * p + 2]
        for j in range(2):
            d_in[p, j] = np.where(diff >= 0, np.exp(lg[j] * np.maximum(diff, 0.0)), 0.0)
        d_k[p] = np.exp(lg[head_k][None, :] * (C - 1.0 - idx)[:, None])
        d_q[p] = np.exp(lg[head_v][None, :] * (idx + 1.0)[:, None])
        d_c[p] = np.broadcast_to(np.exp(lg[head_v] * C)[None, :], (LANES, LANES))
    s_mask = (head_k[:, None] == head_v[None, :]).astype(np.float32)
    return cos_t, sin_t, jnp.asarray(d_in), jnp.asarray(d_k), jnp.asarray(d_q), jnp.asarray(d_c), jnp.asarray(s_mask)


def _rms(x, g):
    return x * lax.rsqrt(jnp.mean(x * x, axis=-1, keepdims=True) + RMS_EPS) * g


def _inproj_kernel(x_ref, g_ref, wm_ref, ws_ref, om_ref, os_ref, *, n_chunk):
    h = _rms(x_ref[...], g_ref[...]).astype(BF16)
    for n0 in range(0, N_MAIN, n_chunk):
        n1 = min(n0 + n_chunk, N_MAIN)
        om_ref[:, n0:n1] = jnp.dot(h, wm_ref[:, n0:n1], preferred_element_type=F32).astype(BF16)
    os_ref[...] = jnp.dot(h, ws_ref[...], preferred_element_type=F32)


def _inproj(x2d, g, wm, ws, tm):
    T = x2d.shape[0]
    return pl.pallas_call(
        functools.partial(_inproj_kernel, n_chunk=512),
        grid=(T // tm,),
        in_specs=[
            pl.BlockSpec((tm, D_MODEL), lambda i: (i, 0)),
            _resident((1, D_MODEL), lambda i: (0, 0)),
            _resident((D_MODEL, N_MAIN), lambda i: (0, 0)),
            _resident((D_MODEL, LANES), lambda i: (0, 0)),
        ],
        out_specs=[
            pl.BlockSpec((tm, N_MAIN), lambda i: (i, 0)),
            pl.BlockSpec((tm, LANES), lambda i: (i, 0)),
        ],
        out_shape=[jax.ShapeDtypeStruct((T, N_MAIN), BF16), jax.ShapeDtypeStruct((T, LANES), F32)],
        compiler_params=_cparams(("parallel",)),
        name="inproj",
    )(x2d, g, wm, ws)


N_SPLIT = 3
AUG_STRIDE = 2 * N_SPLIT


def _bias_placement():
    mq = np.zeros((N_SPLIT, LANES, N_PAIR * LANES), np.float32)
    mk = np.zeros((N_SPLIT, LANES, N_PAIR * LANES), np.float32)
    one_q = np.zeros((1, N_PAIR * LANES), np.float32)
    one_k = np.zeros((1, N_PAIR * LANES), np.float32)
    for h in range(N_FOX):
        base = (h // 2) * LANES + (h % 2) * AUG_STRIDE
        for i in range(N_SPLIT):
            mq[i, SMALL_FF + h, base + i] = 1.0
            mk[i, SMALL_FF + h, base + N_SPLIT + i] = -1.0
            one_q[0, base + N_SPLIT + i] = 1.0
            one_k[0, base + i] = 1.0
    return jnp.asarray(mq, BF16), jnp.asarray(mk, BF16), jnp.asarray(one_q), jnp.asarray(one_k)


def _cum_kernel(z_ref, b_ref, tri_ref, mq_ref, mk_ref, oq_ref, ok_ref, qa_ref, ka_ref, *, n_chunks):
    C = RET_CHUNK

    def body(c, carry):
        r0 = pl.multiple_of(c * C, C)
        z = z_ref[0, pl.ds(r0, C), :] + b_ref[...]
        lf = jnp.minimum(z, 0.0) - jnp.log1p(jnp.exp(-jnp.abs(z)))
        cs = jnp.dot(tri_ref[...], lf, preferred_element_type=F32, precision=lax.Precision.HIGHEST) + carry
        qa, ka, rest = oq_ref[...], ok_ref[...], cs
        for i in range(N_SPLIT):
            term = rest.astype(BF16)
            rest = rest - term.astype(F32)
            qa = qa + jnp.dot(term, mq_ref[i], preferred_element_type=F32)
            ka = ka + jnp.dot(term, mk_ref[i], preferred_element_type=F32)
        qa_ref[0, pl.ds(r0, C), :] = qa.astype(BF16)
        ka_ref[0, pl.ds(r0, C), :] = ka.astype(BF16)
        return cs[C - 1:C, :]

    lax.fori_loop(0, n_chunks, body, jnp.zeros((1, LANES), F32))


def _forget_cumsum(small3, bias_row):
    B, S, _ = small3.shape
    tri = jnp.asarray(np.tril(np.ones((RET_CHUNK, RET_CHUNK), np.float32)))
    mq, mk, one_q, one_k = _bias_placement()
    W = N_PAIR * LANES
    const = lambda shape: pl.BlockSpec(shape, lambda b: (0,) * len(shape))
    return pl.pallas_call(
        functools.partial(_cum_kernel, n_chunks=S // RET_CHUNK),
        grid=(B,),
        in_specs=[
            pl.BlockSpec((1, S, LANES), lambda b: (b, 0, 0)),
            const((1, LANES)), const((RET_CHUNK, RET_CHUNK)),
            const((N_SPLIT, LANES, W)), const((N_SPLIT, LANES, W)), const((1, W)), const((1, W)),
        ],
        out_specs=[
            pl.BlockSpec((1, S, W), lambda b: (b, 0, 0)),
            pl.BlockSpec((1, S, W), lambda b: (b, 0, 0)),
        ],
        out_shape=[jax.ShapeDtypeStruct((B, S, W), BF16), jax.ShapeDtypeStruct((B, S, W), BF16)],
        compiler_params=_cparams(("parallel",)),
        name="forget_cumsum",
    )(small3, bias_row, tri, mq, mk, one_q, one_k)


def _dot_nt(a, b):
    return lax.dot_general(a, b, (((1,), (1,)), ((), ())), preferred_element_type=F32)


def _ret_kernel(q_ref, k_ref, v_ref, g_ref, cos_ref, sin_ref, din_ref, dk_ref, dq_ref, dc_ref, sm_ref, gain_ref,
                o_ref, state_ref, *, n_chunks, unroll):
    C = RET_CHUNK
    half = HEAD_DIM // 2
    lane = lax.broadcasted_iota(jnp.int32, (C, LANES), 1)
    khead = (lane % HEAD_DIM) // half
    vhead0 = lane < HEAD_DIM
    state_ref[...] = jnp.zeros_like(state_ref)

    def pair_chunk(p, r0):
        rows = pl.ds(r0, C)
        lanes = slice(p * LANES, (p + 1) * LANES)
        cs, sn = cos_ref[rows, :], sin_ref[rows, :]
        q = q_ref[0, rows, lanes].astype(F32)
        k = k_ref[0, rows, lanes].astype(F32)
        q = q * cs + pltpu.roll(q, HEAD_DIM, 1) * sn
        k = k * cs + pltpu.roll(k, HEAD_DIM, 1) * sn
        v = v_ref[0, rows, lanes]
        qb, kb = q.astype(BF16), k.astype(BF16)
        state = state_ref[p]
        cross = jnp.dot(qb, state.astype(BF16), preferred_element_type=F32) * dq_ref[p]
        outs = []
        for j in range(2):
            qm = jnp.where(khead == j, qb, jnp.zeros_like(qb))
            inner = _dot_nt(qm, kb) * din_ref[p, j]
            outs.append(jnp.dot(inner.astype(BF16), v, preferred_element_type=F32))
        out = jnp.where(vhead0, outs[0], outs[1]) + cross
        kd = (k * dk_ref[p]).T.astype(BF16)
        state_ref[p] = state * dc_ref[p] + jnp.dot(kd, v, preferred_element_type=F32) * sm_ref[...]
        inv = 1.0 / HEAD_DIM
        s0 = jnp.sum(jnp.where(vhead0, out, 0.0), axis=-1, keepdims=True)
        s1 = jnp.sum(jnp.where(vhead0, 0.0, out), axis=-1, keepdims=True)
        yc = out - jnp.where(vhead0, s0, s1) * inv
        yc2 = yc * yc
        v0 = jnp.sum(jnp.where(vhead0, yc2, 0.0), axis=-1, keepdims=True)
        v1 = jnp.sum(jnp.where(vhead0, 0.0, yc2), axis=-1, keepdims=True)
        y = yc * lax.rsqrt(jnp.where(vhead0, v0, v1) * inv + RMS_EPS) * gain_ref[:, lanes]
        g = g_ref[0, rows, lanes].astype(F32)
        o_ref[0, rows, lanes] = (y * (g * (1.0 / (1.0 + jnp.exp(-g))))).astype(BF16)

    def body(c, _):
        for u in range(unroll):
            for p in range(N_PAIR):
                pair_chunk(p, pl.multiple_of((c * unroll + u) * C, C))
        return 0

    lax.fori_loop(0, n_chunks // unroll, body, 0)


def _retention(main3, tables, gain_row):
    B, S, _ = main3.shape
    cos_t, sin_t, d_in, d_k, d_q, d_c, s_mask = tables
    C = RET_CHUNK
    n_chunks = S // C
    seq_spec = lambda cb: pl.BlockSpec((1, S, RET_W), lambda b: (b, 0, cb // N_PAIR))
    const = lambda shape: pl.BlockSpec(shape, lambda b: (0,) * len(shape))
    return pl.pallas_call(
        functools.partial(_ret_kernel, n_chunks=n_chunks, unroll=2 if n_chunks % 2 == 0 else 1),
        grid=(B,),
        in_specs=[
            seq_spec(CB_RQ), seq_spec(CB_RK), seq_spec(CB_RV), seq_spec(CB_RG),
            const((S, LANES)), const((S, LANES)),
            const((N_PAIR, 2, C, C)), const((N_PAIR, C, LANES)), const((N_PAIR, C, LANES)),
            const((N_PAIR, LANES, LANES)), const((LANES, LANES)), const((1, RET_W)),
        ],
        out_specs=pl.BlockSpec((1, S, RET_W), lambda b: (b, 0, 0)),
        out_shape=jax.ShapeDtypeStruct((B, S, RET_W), BF16),
        scratch_shapes=[pltpu.VMEM((N_PAIR, LANES, LANES), F32)],
        compiler_params=_cparams(("parallel",)),
        name="retention",
    )(main3, main3, main3, main3, cos_t, sin_t, d_in, d_k, d_q, d_c, s_mask, gain_row)


def _fox_kernel(q_ref, qa_ref, k_ref, v_ref, gain_ref, o_ref, *, tq, tk):
    qi = pl.program_id(2)
    q = q_ref[0]
    qa = qa_ref[0]
    lane_q = lax.broadcasted_iota(jnp.int32, (tq, LANES), 1)
    head0 = lane_q < HEAD_DIM
    qops = []
    for j in range(2):
        qm = jnp.where(head0 if j == 0 else jnp.logical_not(head0), q, jnp.zeros_like(q))
        own = jnp.logical_and(lane_q >= j * AUG_STRIDE, lane_q < (j + 1) * AUG_STRIDE)
        qops.append(jnp.concatenate([qm, jnp.where(own, qa, jnp.zeros_like(qa))], axis=1))
    q_pos = qi * tq + lax.broadcasted_iota(jnp.int32, (tq, tk), 0)
    k_off = lax.broadcasted_iota(jnp.int32, (tq, tk), 1)
    head0_k = lax.broadcasted_iota(jnp.int32, (tk, LANES), 1) < HEAD_DIM
    one = jnp.ones((tk, LANES), BF16)

    def tile(t, carry, masked):
        c0 = pl.multiple_of(t * tk, tk)
        k = k_ref[0, pl.ds(c0, tk), :]
        v = v_ref[0, pl.ds(c0, tk), :]
        new = []
        for j in range(2):
            m, acc = carry[2 * j:2 * j + 2]
            s = _dot_nt(qops[j], k)
            if masked:
                s = jnp.where(k_off + c0 <= q_pos, s, NEG)
            m_new = jnp.maximum(m, jnp.max(s, axis=-1, keepdims=True))
            pr = jnp.exp(s - m_new)
            vj = jnp.where(head0_k if j == 0 else jnp.logical_not(head0_k), v, one)
            acc = jnp.exp(m - m_new) * acc + jnp.dot(pr.astype(BF16), vj, preferred_element_type=F32)
            new += [m_new, acc]
        return tuple(new)

    init = []
    for _ in range(2):
        init += [jnp.full((tq, 1), NEG, F32), jnp.zeros((tq, LANES), F32)]
    n_full = (qi * tq) // tk
    carry = lax.fori_loop(0, n_full, lambda t, c: tile(t, c, False), tuple(init))
    for d in range(max(1, tq // tk)):
        carry = tile(n_full + d, carry, True)
    o0 = carry[1] / carry[1][:, HEAD_DIM:HEAD_DIM + 1]
    o1 = carry[3] / carry[3][:, 0:1]
    out = jnp.where(head0, o0, o1)
    out2 = out * out
    inv = 1.0 / HEAD_DIM
    v0 = jnp.sum(jnp.where(head0, out2, 0.0), axis=-1, keepdims=True)
    v1 = jnp.sum(jnp.where(head0, 0.0, out2), axis=-1, keepdims=True)
    o_ref[0] = (out * lax.rsqrt(jnp.where(head0, v0, v1) * inv + RMS_EPS) * gain_ref[...]).astype(BF16)


def _fox(main3, q_aug, k_cat, gain_row, tq, tk):
    B, S, _ = main3.shape
    return pl.pallas_call(
        functools.partial(_fox_kernel, tq=tq, tk=tk),
        grid=(B, N_PAIR, S // tq),
        in_specs=[
            pl.BlockSpec((1, tq, LANES), lambda b, p, i: (b, i, CB_FQ + p)),
            pl.BlockSpec((1, tq, LANES), lambda b, p, i: (b, i, p)),
            pl.BlockSpec((1, S, 2 * LANES), lambda b, p, i: (b, 0, p)),
            pl.BlockSpec((1, S, LANES), lambda b, p, i: (b, 0, CB_FV + p)),
            pl.BlockSpec((1, LANES), lambda b, p, i: (0, p)),
        ],
        out_specs=pl.BlockSpec((1, tq, LANES), lambda b, p, i: (b, i, p)),
        out_shape=jax.ShapeDtypeStruct((B, S, FOX_W), BF16),
        compiler_params=_cparams(("parallel", "parallel", "arbitrary")),
        name="fox_attention",
    )(main3, q_aug, k_cat, main3, gain_row)


def _gelu_tanh(x):
    return 0.5 * x * (1.0 + jnp.tanh(np.sqrt(2.0 / np.pi).astype(np.float32) * (x + 0.044715 * (x * x * x))))


def _cmp_kernel(r_ref, pos_ref, wt_ref, wb_ref, w2_ref, o_ref):
    r = r_ref[0]
    n = r.shape[0]
    top = jnp.dot(r, wt_ref[...], preferred_element_type=F32)
    bot = jnp.dot(r, wb_ref[...], preferred_element_type=F32)
    cpos = jnp.dot(pos_ref[0:8, :], wt_ref[...], preferred_element_type=F32)
    cpos = cpos + jnp.dot(pos_ref[8:16, :], wb_ref[...], preferred_element_type=F32)
    hid = _gelu_tanh(top + pltpu.roll(bot, n - 1, 0) + cpos[0:1, :])
    o_ref[0] = jnp.dot(hid.astype(BF16), w2_ref[...], preferred_element_type=F32)


def _nsa_compress(r, pos2, wt, wb, w2):
    B, n, _ = r.shape
    width = CMP_STRIDE * LANES
    return pl.pallas_call(
        _cmp_kernel,
        grid=(B,),
        in_specs=[
            pl.BlockSpec((1, n, width), lambda b: (b, 0, 0)),
            pl.BlockSpec((16, width), lambda b: (0, 0)),
            pl.BlockSpec((width, 2 * CMP_HIDDEN), lambda b: (0, 0)),
            pl.BlockSpec((width, 2 * CMP_HIDDEN), lambda b: (0, 0)),
            pl.BlockSpec((2 * CMP_HIDDEN, LANES), lambda b: (0, 0)),
        ],
        out_specs=pl.BlockSpec((1, n, LANES), lambda b: (b, 0, 0)),
        out_shape=jax.ShapeDtypeStruct((B, n, LANES), F32),
        compiler_params=_cparams(("parallel",)),
        name="nsa_compress",
    )(r, pos2, wt, wb, w2)


def _softmax_rows(s, mask):
    sm = jnp.where(mask, s, NEG)
    m = jnp.max(sm, axis=-1, keepdims=True)
    e = jnp.where(mask, jnp.exp(sm - m), 0.0)
    l = jnp.sum(e, axis=-1, keepdims=True)
    return e / jnp.where(l > 0.0, l, 1.0)


def _nsa_kernel(q_ref, cmp_ref, c2s_ref, sel_ref, ksa_ref, win_ref, gl_ref, gain_ref, o_ref, *, tq, tk, n_cmp, n_sel, top_n):
    H = N_NSA
    qi = pl.program_id(1)
    start = qi * tq
    q = q_ref[0]
    t_row = start + lax.broadcasted_iota(jnp.int32, (tq, 1), 0)
    t_all = jnp.concatenate([t_row] * H, axis=0)
    zeros64 = jnp.zeros((tq, HEAD_DIM), BF16)
    q_heads = [q[:, h * HEAD_DIM:(h + 1) * HEAD_DIM] for h in range(H)]
    q_pad = jnp.concatenate([jnp.concatenate([qh, zeros64], axis=1) for qh in q_heads], axis=0)

    wlen = WINDOW + tq
    base = pl.multiple_of(jnp.maximum(start - WINDOW, 0), tq)
    win = win_ref[0, pl.ds(base, wlen), :]
    s_w = _dot_nt(q_pad, win)
    kpos = base + lax.broadcasted_iota(jnp.int32, (H * tq, wlen), 1)
    vis_w = jnp.logical_and(kpos <= t_all, kpos > t_all - WINDOW)
    s_w = jnp.where(vis_w, s_w, NEG)
    e_w = jnp.exp(s_w - jnp.max(s_w, axis=-1, keepdims=True))
    win1 = jnp.where(lax.broadcasted_iota(jnp.int32, (wlen, LANES), 1) < HEAD_DIM, jnp.ones((wlen, LANES), BF16), win)
    o_w = jnp.dot(e_w.astype(BF16), win1, preferred_element_type=F32)
    o_w = o_w / o_w[:, 0:1]

    cmp = cmp_ref[0]
    n_pad = cmp.shape[0]
    cmp_hi = cmp.astype(BF16)
    cmp_lo = (cmp - cmp_hi.astype(F32)).astype(BF16)
    s_c = _dot_nt(q_pad, cmp_hi) + _dot_nt(q_pad, cmp_lo)
    n_id = lax.broadcasted_iota(jnp.int32, (H * tq, n_pad), 1)
    vis_c = jnp.logical_and(n_id * CMP_STRIDE + (CMP_LEN - 1) <= t_all, n_id < n_cmp)
    p_c = _softmax_rows(s_c, vis_c)
    o_c = jnp.dot(p_c.astype(BF16), cmp_hi, preferred_element_type=F32)

    p_sum = p_c[0:tq]
    for h in range(1, H):
        p_sum = p_sum + p_c[h * tq:(h + 1) * tq]
    p_hi = p_sum.astype(BF16)
    p_lo = (p_sum - p_hi.astype(F32)).astype(BF16)
    c2s_t = c2s_ref[...]
    imp = _dot_nt(c2s_t, p_hi) + _dot_nt(c2s_t, p_lo)
    t_lane = start + lax.broadcasted_iota(jnp.int32, (HEAD_DIM, tq), 1)
    s_id = lax.broadcasted_iota(jnp.int32, (HEAD_DIM, tq), 0)
    cur = t_lane >> 6
    forced = jnp.logical_or(s_id == 0, jnp.logical_or(s_id == cur, s_id == cur - 1))
    score = jnp.where(forced, FORCED_SCORE, imp)
    score = jnp.where(s_id * SEL_LEN <= t_lane, score, -1.0)
    SUB = 8
    groups = [score[g * SUB:(g + 1) * SUB, :] for g in range(HEAD_DIM // SUB)]
    ranks = [jnp.zeros((SUB, tq), jnp.int32) for _ in groups]
    sub_id = lax.broadcasted_iota(jnp.int32, (SUB, tq), 0)
    for c in range(n_sel):
        row = score[c:c + 1, :]
        for g in range(len(groups)):
            if g * SUB > c:
                before = row >= groups[g]
            elif g * SUB + SUB - 1 < c:
                before = row > groups[g]
            else:
                before = jnp.logical_or(row > groups[g], jnp.logical_and(row == groups[g], sub_id > c - g * SUB))
            ranks[g] = ranks[g] + before.astype(jnp.int32)
    not_sel = jnp.concatenate([jnp.where(r < top_n, 0.0, 1.0) for r in ranks], axis=0)
    ns = jnp.concatenate([jnp.zeros((HEAD_DIM, tq), F32), not_sel], axis=0).T.astype(BF16)
    q_aug = q_pad + jnp.concatenate([ns] * H, axis=0)

    def sel_tile(c0, size, carry, diagonal):
        m, acc = carry
        s = _dot_nt(q_aug, ksa_ref[0, pl.ds(c0, size), :])
        if diagonal:
            r_id = lax.broadcasted_iota(jnp.int32, (H * tq, size), 0) & (tq - 1)
            c_id = lax.broadcasted_iota(jnp.int32, (H * tq, size), 1)
            s = jnp.where(c_id <= r_id, s, NEG)
        m_new = jnp.maximum(m, jnp.max(s, axis=-1, keepdims=True))
        pr = jnp.exp(s - m_new)
        key_lanes = lax.broadcasted_iota(jnp.int32, (size, LANES), 1) < HEAD_DIM
        v1 = jnp.where(key_lanes, jnp.ones((size, LANES), BF16), sel_ref[0, pl.ds(c0, size), :])
        acc = jnp.exp(m - m_new) * acc + jnp.dot(pr.astype(BF16), v1, preferred_element_type=F32)
        return m_new, acc

    carry = (jnp.full((H * tq, 1), NEG, F32), jnp.zeros((H * tq, LANES), F32))
    n_full = start // tk
    carry = lax.fori_loop(0, n_full, lambda t, c: sel_tile(pl.multiple_of(t * tk, tk), tk, c, False), carry)
    n_tail = (start - n_full * tk) // tq
    for d in range(tk // tq - 1):
        c0 = pl.multiple_of(n_full * tk + d * tq, tq)
        carry = lax.cond(d < n_tail, lambda c, c0=c0: sel_tile(c0, tq, c, False), lambda c: c, carry)
    carry = sel_tile(pl.multiple_of(start, tq), tq, carry, True)
    o_s = carry[1] / carry[1][:, 0:1]

    gl = gl_ref[0]
    gates = 1.0 / (1.0 + jnp.exp(-gl))
    outs = []
    for h in range(H):
        rows = slice(h * tq, (h + 1) * tq)
        g0 = gates[:, SMALL_GATE + 3 * h + 0:SMALL_GATE + 3 * h + 1]
        g1 = gates[:, SMALL_GATE + 3 * h + 1:SMALL_GATE + 3 * h + 2]
        g2 = gates[:, SMALL_GATE + 3 * h + 2:SMALL_GATE + 3 * h + 3]
        mix = g0 * o_c[rows] + g1 * o_s[rows] + g2 * o_w[rows]
        val = mix[:, HEAD_DIM:]
        outs.append(val * lax.rsqrt(jnp.mean(val * val, axis=-1, keepdims=True) + RMS_EPS))
    o_ref[0] = (jnp.concatenate(outs, axis=1) * gain_ref[...]).astype(BF16)


def _nsa(main3, cmp, c2s, ks_aug, small3, gain_row, tq, tk, n_cmp):
    B, S, _ = main3.shape
    n_sel = S // SEL_LEN
    n_pad = cmp.shape[1]
    return pl.pallas_call(
        functools.partial(_nsa_kernel, tq=tq, tk=tk, n_cmp=n_cmp, n_sel=n_sel, top_n=min(TOP_N, n_sel)),
        grid=(B, S // tq),
        in_specs=[
            pl.BlockSpec((1, tq, NSA_W), lambda b, i: (b, i, CB_NQ // 2)),
            pl.BlockSpec((1, n_pad, LANES), lambda b, i: (b, 0, 0)),
            pl.BlockSpec((HEAD_DIM, n_pad), lambda b, i: (0, 0)),
            pl.BlockSpec((1, S, LANES), lambda b, i: (b, 0, CB_SEL)),
            pl.BlockSpec((1, S, LANES), lambda b, i: (b, 0, 0)),
            pl.BlockSpec((1, S, LANES), lambda b, i: (b, 0, CB_WIN)),
            pl.BlockSpec((1, tq, LANES), lambda b, i: (b, i, 0)),
            pl.BlockSpec((1, NSA_W), lambda b, i: (0, 0)),
        ],
        out_specs=pl.BlockSpec((1, tq, NSA_W), lambda b, i: (b, i, 0)),
        out_shape=jax.ShapeDtypeStruct((B, S, NSA_W), BF16),
        compiler_params=_cparams(("parallel", "arbitrary")),
        name="nsa_attention",
    )(main3, cmp, c2s, main3, ks_aug, main3, small3, gain_row)


def _post_kernel(x_ref, yr_ref, yf_ref, yn_ref, wo_ref, g_ref, w1_ref, w2_ref, gf_ref, o_ref, *, ff_chunk, final):
    x = x_ref[...]
    x = x + jnp.dot(yr_ref[...], wo_ref[0:RET_W, :], preferred_element_type=F32)
    x = x + jnp.dot(yf_ref[...], wo_ref[RET_W:RET_W + FOX_W, :], preferred_element_type=F32)
    x = x + jnp.dot(yn_ref[...], wo_ref[RET_W + FOX_W:, :], preferred_element_type=F32)
    h = _rms(x, g_ref[...]).astype(BF16)
    o_ref[...] = x
    for c0 in range(0, D_FF, ff_chunk):
        hid = jnp.maximum(jnp.dot(h, w1_ref[:, c0:c0 + ff_chunk], preferred_element_type=F32), 0.0)
        o_ref[...] += jnp.dot((hid * hid).astype(BF16), w2_ref[c0:c0 + ff_chunk, :], preferred_element_type=F32)
    if final:
        o_ref[...] = _rms(o_ref[...], gf_ref[...])


def _post(x2d, yr, yf, yn, wo, g, w1, w2, gf, tm, final):
    T = x2d.shape[0]
    row = lambda w: pl.BlockSpec((tm, w), lambda i: (i, 0))
    return pl.pallas_call(
        functools.partial(_post_kernel, ff_chunk=512, final=final),
        grid=(T // tm,),
        in_specs=[
            row(D_MODEL), row(RET_W), row(FOX_W), row(NSA_W),
            _resident((D_MODEL, D_MODEL), lambda i: (0, 0)),
            _resident((1, D_MODEL), lambda i: (0, 0)),
            _resident((D_MODEL, D_FF), lambda i: (0, 0)),
            _resident((D_FF, D_MODEL), lambda i: (0, 0)),
            _resident((1, D_MODEL), lambda i: (0, 0)),
        ],
        out_specs=row(D_MODEL),
        out_shape=jax.ShapeDtypeStruct((T, D_MODEL), F32),
        compiler_params=_cparams(("parallel",)),
        name="outproj_mlp",
    )(x2d, yr, yf, yn, wo, g, w1, w2, gf)


def _tile_sizes(seq):
    return dict(tm=512, fox_tq=512, fox_tk=512, nsa_tq=256, nsa_tk=512)


def kernel(x, norm_attn, w_in, fox_forget_bias, ret_norm_gain, fox_norm_gain, nsa_norm_gain, nsa_cmp_pos_k, nsa_cmp_pos_v, nsa_cmp_w1_k, nsa_cmp_w2_k, nsa_cmp_w1_v, nsa_cmp_w2_v, w_out, norm_mlp, w_mlp_in, w_mlp_out, norm_final):
    B, S, D = x.shape
    depth = w_in.shape[0]
    assert D == D_MODEL and S % 512 == 0 and S >= WINDOW + 256
    T = B * S
    ts = _tile_sizes(S)

    cols, scale, small_cols, small_mask = _in_proj_columns()
    wm = (w_in[:, :, cols] * scale).astype(BF16)
    ws = (w_in[:, :, small_cols] * small_mask).astype(BF16)
    wo = w_out.astype(BF16)
    w1 = w_mlp_in.astype(BF16)
    w2 = w_mlp_out.astype(BF16)

    def expand_w1(wk, wv):
        L = wk.shape[0]
        wk = wk.reshape(L, CMP_LEN, HEAD_DIM, CMP_HIDDEN)
        wv = wv.reshape(L, CMP_LEN, HEAD_DIM, CMP_HIDDEN)
        z = jnp.zeros_like(wk)
        full = jnp.concatenate([jnp.concatenate([wk, z], axis=-1), jnp.concatenate([z, wv], axis=-1)], axis=2)
        full = full.reshape(L, CMP_LEN * LANES, 2 * CMP_HIDDEN).astype(BF16)
        return full[:, :CMP_STRIDE * LANES], full[:, CMP_STRIDE * LANES:]

    wt_all, wb_all = expand_w1(nsa_cmp_w1_k, nsa_cmp_w1_v)
    zk = jnp.zeros_like(nsa_cmp_w2_k)
    w2c = jnp.concatenate([jnp.concatenate([nsa_cmp_w2_k, zk], axis=-1),
                           jnp.concatenate([zk, nsa_cmp_w2_v], axis=-1)], axis=1).astype(BF16)
    pos = jnp.concatenate([nsa_cmp_pos_k, nsa_cmp_pos_v], axis=-1)
    pos_tb = pos.reshape(depth, 2, 1, CMP_STRIDE * LANES)
    pos_tb = jnp.broadcast_to(pos_tb, (depth, 2, 8, CMP_STRIDE * LANES)).reshape(depth, 16, CMP_STRIDE * LANES).astype(BF16)

    n_cmp = (S - CMP_LEN) // CMP_STRIDE + 1
    n_sel = S // SEL_LEN
    n_rows = S // CMP_STRIDE
    cs = np.arange(n_rows) * CMP_STRIDE
    ss = np.arange(n_sel) * SEL_LEN
    overlap = np.clip(np.minimum(cs[:, None] + CMP_LEN, ss[None, :] + SEL_LEN) - np.maximum(cs[:, None], ss[None, :]), 0, None)
    overlap[n_cmp:] = 0
    c2s_np = np.zeros((HEAD_DIM, n_rows), np.float32)
    c2s_np[:n_sel] = (overlap / CMP_LEN).T
    c2s = jnp.asarray(c2s_np, dtype=BF16)
    assert n_sel <= HEAD_DIM, "the selected-branch key augmentation has 64 lanes, one per selection block"
    neg_onehot = jnp.asarray(np.where((np.arange(S)[:, None] // SEL_LEN) == np.arange(HEAD_DIM)[None, :], NEG, 0.0), dtype=BF16)

    tables = _retention_tables(S)
    fbias = jnp.zeros((depth, 1, LANES), F32).at[:, 0, SMALL_FF:SMALL_FF + N_FOX].set(fox_forget_bias)

    x2d = x.reshape(T, D)
    for l in range(depth):
        main, small = _inproj(x2d, norm_attn[l][None, :], wm[l], ws[l], ts["tm"])
        main3 = main.reshape(B, S, N_MAIN)
        small3 = small.reshape(B, S, LANES)
        fox_qa, fox_ka = _forget_cumsum(small3, fbias[l])
        y_ret = _retention(main3, tables, ret_norm_gain[l][None, :])
        fk = main3[:, :, CB_FK * LANES:(CB_FK + N_PAIR) * LANES].reshape(B, S, N_PAIR, LANES)
        k_cat = jnp.concatenate([fk, fox_ka.reshape(B, S, N_PAIR, LANES)], axis=-1).reshape(B, S, 2 * FOX_W)
        y_fox = _fox(main3, fox_qa, k_cat, fox_norm_gain[l][None, :], ts["fox_tq"], ts["fox_tk"])
        r = main3[:, :, CB_CMP * LANES:(CB_CMP + 1) * LANES].reshape(B, n_rows, CMP_STRIDE * LANES)
        cmp = _nsa_compress(r, pos_tb[l], wt_all[l], wb_all[l], w2c[l])
        ks = main3[:, :, CB_SEL * LANES:CB_SEL * LANES + HEAD_DIM]
        ks_aug = jnp.concatenate([ks, jnp.broadcast_to(neg_onehot[None], (B, S, HEAD_DIM))], axis=-1)
        y_nsa = _nsa(main3, cmp, c2s, ks_aug, small3, nsa_norm_gain[l][None, :], ts["nsa_tq"], ts["nsa_tk"], n_cmp)
        x2d = _post(x2d, y_ret.reshape(T, RET_W), y_fox.reshape(T, FOX_W), y_nsa.reshape(T, NSA_W), wo[l],
                    norm_mlp[l][None, :], w1[l], w2[l], norm_final[None, :], ts["tm"], final=(l == depth - 1))
    return x2d.reshape(B, S, D)
```

```python
import functools

import numpy as np
import jax
import jax.numpy as jnp
from jax import lax
from jax.experimental import pallas as pl
from jax.experimental.pallas import tpu as pltpu

F32 = jnp.float32
BF16 = jnp.bfloat16

D_MODEL = 1024
HEAD_DIM = 64
N_RET = 6
N_FOX = 6
N_NSA = 4
RET_W = N_RET * HEAD_DIM
FOX_W = N_FOX * HEAD_DIM
NSA_W = N_NSA * HEAD_DIM
D_FF = 4 * D_MODEL
RMS_EPS = 1e-6
RET_CHUNK = 128
ROPE_BASE = 10000.0
CMP_LEN = 32
CMP_STRIDE = 16
CMP_HIDDEN = 4 * HEAD_DIM
SEL_LEN = 64
TOP_N = 16
WINDOW = 512
FORCED_SCORE = 1e4
NEG = -1e30

LANES = 128
N_PAIR = N_RET // 2

CB_RQ, CB_RK, CB_RV, CB_RG = 0, 3, 6, 9
CB_NQ = 12
CB_FQ, CB_FK, CB_FV = 14, 17, 20
CB_CMP, CB_SEL, CB_WIN = 23, 24, 25
N_MAIN = 26 * LANES
SMALL_FF = 0
SMALL_GATE = 8

VMEM_LIMIT = 56 * 1024 * 1024


def _cparams(sem):
    return pltpu.CompilerParams(dimension_semantics=sem, vmem_limit_bytes=VMEM_LIMIT)


def _resident(shape, index_map):
    return pl.BlockSpec(shape, index_map, pipeline_mode=pl.Buffered(1))


def _in_proj_columns():
    sizes = (RET_W, RET_W, RET_W, RET_W, FOX_W, FOX_W, FOX_W, N_FOX, NSA_W) + (HEAD_DIM,) * 6 + (3 * N_NSA,)
    off = np.concatenate([[0], np.cumsum(sizes)])
    (o_rq, o_rk, o_rv, o_rg, o_fq, o_fk, o_fv, o_ff, o_nq, o_kc, o_vc, o_ks, o_vs, o_kw, o_vw, o_gt) = off[:-1]
    half = HEAD_DIM // 2
    inter = []
    for p in range(N_PAIR):
        a, b = 2 * p, 2 * p + 1
        for h, part in ((a, 0), (b, 0), (a, 1), (b, 1)):
            inter.extend(range(h * HEAD_DIM + part * half, h * HEAD_DIM + (part + 1) * half))
    inter = np.asarray(inter)
    nat = np.arange(RET_W)
    cols, scale = [], []

    def add(idx, s=1.0):
        cols.append(np.asarray(idx))
        scale.append(np.full(len(idx), s, np.float32))

    qk_scale = HEAD_DIM ** -0.5
    add(o_rq + inter)
    add(o_rk + inter, qk_scale)
    add(o_rv + nat)
    add(o_rg + nat)
    add(o_nq + np.arange(NSA_W), qk_scale)
    add(o_fq + nat, qk_scale)
    add(o_fk + nat)
    add(o_fv + nat)
    for o in (o_kc, o_vc, o_ks, o_vs, o_kw, o_vw):
        add(o + np.arange(HEAD_DIM))
    cols = np.concatenate(cols)
    scale = np.concatenate(scale)
    assert cols.shape[0] == N_MAIN
    small_cols = np.zeros(LANES, np.int64)
    small_mask = np.zeros(LANES, np.float32)
    small_cols[SMALL_FF:SMALL_FF + N_FOX] = o_ff + np.arange(N_FOX)
    small_mask[SMALL_FF:SMALL_FF + N_FOX] = 1.0
    small_cols[SMALL_GATE:SMALL_GATE + 3 * N_NSA] = o_gt + np.arange(3 * N_NSA)
    small_mask[SMALL_GATE:SMALL_GATE + 3 * N_NSA] = 1.0
    return cols, scale, small_cols, small_mask


def _retention_tables(seq):
    half = HEAD_DIM // 2
    inv = 1.0 / (ROPE_BASE ** (jnp.arange(half, dtype=F32) / half))
    ang = jnp.arange(seq, dtype=F32)[:, None] * inv[None, :]
    cos, sin = jnp.cos(ang), jnp.sin(ang)
    cos_t = jnp.concatenate([cos, cos, cos, cos], axis=-1)
    sin_t = jnp.concatenate([-sin, -sin, sin, sin], axis=-1)
    log_gamma = np.log(1.0 - 2.0 ** (-5.0 - np.arange(N_RET, dtype=np.float32))).astype(np.float32)
    C = RET_CHUNK
    idx = np.arange(C, dtype=np.float32)
    diff = idx[:, None] - idx[None, :]
    lane = np.arange(LANES)
    head_k = (lane % HEAD_DIM) // half
    head_v = lane // HEAD_DIM
    d_in = np.zeros((N_PAIR, 2, C, C), np.float32)
    d_k = np.zeros((N_PAIR, C, LANES), np.float32)
    d_q = np.zeros((N_PAIR, C, LANES), np.float32)
    d_c = np.zeros((N_PAIR, LANES, LANES), np.float32)
    for p in range(N_PAIR):
        lg = log_gamma[2 * p:2 * p + 2]
        for j in range(2):
            d_in[p, j] = np.where(diff >= 0, np.exp(lg[j] * np.maximum(diff, 0.0)), 0.0)
        d_k[p] = np.exp(lg[head_k][None, :] * (C - 1.0 - idx)[:, None])
        d_q[p] = np.exp(lg[head_v][None, :] * (idx + 1.0)[:, None])
        d_c[p] = np.broadcast_to(np.exp(lg[head_v] * C)[None, :], (LANES, LANES))
    s_mask = (head_k[:, None] == head_v[None, :]).astype(np.float32)
    return cos_t, sin_t, jnp.asarray(d_in), jnp.asarray(d_k), jnp.asarray(d_q), jnp.asarray(d_c), jnp.asarray(s_mask)


def _rms(x, g):
    return x * lax.rsqrt(jnp.mean(x * x, axis=-1, keepdims=True) + RMS_EPS) * g


def _inproj_kernel(x_ref, g_ref, wm_ref, ws_ref, om_ref, os_ref, *, n_chunk):
    h = _rms(x_ref[...], g_ref[...]).astype(BF16)
    for n0 in range(0, N_MAIN, n_chunk):
        n1 = min(n0 + n_chunk, N_MAIN)
        om_ref[:, n0:n1] = jnp.dot(h, wm_ref[:, n0:n1], preferred_element_type=F32).astype(BF16)
    os_ref[...] = jnp.dot(h, ws_ref[...], preferred_element_type=F32)


def _inproj(x2d, g, wm, ws, tm):
    T = x2d.shape[0]
    return pl.pallas_call(
        functools.partial(_inproj_kernel, n_chunk=512),
        grid=(T // tm,),
        in_specs=[
            pl.BlockSpec((tm, D_MODEL), lambda i: (i, 0)),
            _resident((1, D_MODEL), lambda i: (0, 0)),
            _resident((D_MODEL, N_MAIN), lambda i: (0, 0)),
            _resident((D_MODEL, LANES), lambda i: (0, 0)),
        ],
        out_specs=[
            pl.BlockSpec((tm, N_MAIN), lambda i: (i, 0)),
            pl.BlockSpec((tm, LANES), lambda i: (i, 0)),
        ],
        out_shape=[jax.ShapeDtypeStruct((T, N_MAIN), BF16), jax.ShapeDtypeStruct((T, LANES), F32)],
        compiler_params=_cparams(("parallel",)),
        name="inproj",
    )(x2d, g, wm, ws)


N_SPLIT = 3
AUG_STRIDE = 2 * N_SPLIT


def _split_bf16(x):
    terms, rest = [], x
    for _ in range(N_SPLIT):
        t = rest.astype(BF16)
        rest = rest - t.astype(F32)
        terms.append(t)
    return terms


def _bias_placement():
    mq = np.zeros((N_SPLIT, LANES, LANES), np.float32)
    mk = np.zeros((N_SPLIT, LANES, LANES), np.float32)
    one_q = np.zeros((1, LANES), np.float32)
    one_k = np.zeros((1, LANES), np.float32)
    for h in range(N_FOX):
        base = h * AUG_STRIDE
        for i in range(N_SPLIT):
            mq[i, SMALL_FF + h, base + i] = 1.0
            mk[i, SMALL_FF + h, base + N_SPLIT + i] = -1.0
            one_q[0, base + N_SPLIT + i] = 1.0
            one_k[0, base + i] = 1.0
    return jnp.asarray(mq, BF16), jnp.asarray(mk, BF16), jnp.asarray(one_q), jnp.asarray(one_k)


def _cum_kernel(z_ref, b_ref, tri_ref, mq_ref, mk_ref, oq_ref, ok_ref, qa_ref, ka_ref, *, n_chunks):
    C = RET_CHUNK

    def body(c, carry):
        r0 = pl.multiple_of(c * C, C)
        z = z_ref[0, pl.ds(r0, C), :] + b_ref[...]
        lf = jnp.minimum(z, 0.0) - jnp.log1p(jnp.exp(-jnp.abs(z)))
        cs = carry
        for term in _split_bf16(lf):
            cs = cs + jnp.dot(tri_ref[...], term, preferred_element_type=F32)
        qa, ka = oq_ref[...], ok_ref[...]
        for i, term in enumerate(_split_bf16(cs)):
            qa = qa + jnp.dot(term, mq_ref[i], preferred_element_type=F32)
            ka = ka + jnp.dot(term, mk_ref[i], preferred_element_type=F32)
        qa_ref[0, pl.ds(r0, C), :] = qa.astype(BF16)
        ka_ref[0, pl.ds(r0, C), :] = ka.astype(BF16)
        return cs[C - 1:C, :]

    lax.fori_loop(0, n_chunks, body, jnp.zeros((1, LANES), F32))


def _forget_cumsum(small3, bias_row):
    B, S, _ = small3.shape
    tri = jnp.asarray(np.tril(np.ones((RET_CHUNK, RET_CHUNK), np.float32)), BF16)
    mq, mk, one_q, one_k = _bias_placement()
    const = lambda shape: pl.BlockSpec(shape, lambda b: (0,) * len(shape))
    return pl.pallas_call(
        functools.partial(_cum_kernel, n_chunks=S // RET_CHUNK),
        grid=(B,),
        in_specs=[
            pl.BlockSpec((1, S, LANES), lambda b: (b, 0, 0)),
            const((1, LANES)), const((RET_CHUNK, RET_CHUNK)),
            const((N_SPLIT, LANES, LANES)), const((N_SPLIT, LANES, LANES)), const((1, LANES)), const((1, LANES)),
        ],
        out_specs=[
            pl.BlockSpec((1, S, LANES), lambda b: (b, 0, 0)),
            pl.BlockSpec((1, S, LANES), lambda b: (b, 0, 0)),
        ],
        out_shape=[jax.ShapeDtypeStruct((B, S, LANES), BF16), jax.ShapeDtypeStruct((B, S, LANES), BF16)],
        compiler_params=_cparams(("parallel",)),
        name="forget_cumsum",
    )(small3, bias_row, tri, mq, mk, one_q, one_k)


def _dot_nt(a, b):
    return lax.dot_general(a, b, (((1,), (1,)), ((), ())), preferred_element_type=F32)


def _ret_kernel(q_ref, k_ref, v_ref, g_ref, cos_ref, sin_ref, din_ref, dk_ref, dq_ref, dc_ref, sm_ref, gain_ref,
                o_ref, state_ref, *, n_chunks, unroll):
    C = RET_CHUNK
    half = HEAD_DIM // 2
    lane = lax.broadcasted_iota(jnp.int32, (C, LANES), 1)
    khead = (lane % HEAD_DIM) // half
    vhead0 = lane < HEAD_DIM
    state_ref[...] = jnp.zeros_like(state_ref)

    def pair_chunk(p, r0):
        rows = pl.ds(r0, C)
        lanes = slice(p * LANES, (p + 1) * LANES)
        cs, sn = cos_ref[rows, :], sin_ref[rows, :]
        q = q_ref[0, rows, lanes].astype(F32)
        k = k_ref[0, rows, lanes].astype(F32)
        q = q * cs + pltpu.roll(q, HEAD_DIM, 1) * sn
        k = k * cs + pltpu.roll(k, HEAD_DIM, 1) * sn
        v = v_ref[0, rows, lanes]
        qb, kb = q.astype(BF16), k.astype(BF16)
        state = state_ref[p]
        cross = jnp.dot(qb, state.astype(BF16), preferred_element_type=F32) * dq_ref[p]
        outs = []
        for j in range(2):
            qm = jnp.where(khead == j, qb, jnp.zeros_like(qb))
            inner = _dot_nt(qm, kb) * din_ref[p, j]
            outs.append(jnp.dot(inner.astype(BF16), v, preferred_element_type=F32))
        out = jnp.where(vhead0, outs[0], outs[1]) + cross
        kd = (k * dk_ref[p]).T.astype(BF16)
        state_ref[p] = state * dc_ref[p] + jnp.dot(kd, v, preferred_element_type=F32) * sm_ref[...]
        inv = 1.0 / HEAD_DIM
        s0 = jnp.sum(jnp.where(vhead0, out, 0.0), axis=-1, keepdims=True)
        s1 = jnp.sum(jnp.where(vhead0, 0.0, out), axis=-1, keepdims=True)
        yc = out - jnp.where(vhead0, s0, s1) * inv
        yc2 = yc * yc
        v0 = jnp.sum(jnp.where(vhead0, yc2, 0.0), axis=-1, keepdims=True)
        v1 = jnp.sum(jnp.where(vhead0, 0.0, yc2), axis=-1, keepdims=True)
        y = yc * lax.rsqrt(jnp.where(vhead0, v0, v1) * inv + RMS_EPS) * gain_ref[:, lanes]
        g = g_ref[0, rows, lanes].astype(F32)
        o_ref[0, rows, lanes] = (y * (g * (1.0 / (1.0 + jnp.exp(-g))))).astype(BF16)

    def body(c, _):
        for u in range(unroll):
            for p in range(N_PAIR):
                pair_chunk(p, pl.multiple_of((c * unroll + u) * C, C))
        return 0

    lax.fori_loop(0, n_chunks // unroll, body, 0)


def _retention(main3, tables, gain_row):
    B, S, _ = main3.shape
    cos_t, sin_t, d_in, d_k, d_q, d_c, s_mask = tables
    C = RET_CHUNK
    n_chunks = S // C
    seq_spec = lambda cb: pl.BlockSpec((1, S, RET_W), lambda b: (b, 0, cb // N_PAIR))
    const = lambda shape: pl.BlockSpec(shape, lambda b: (0,) * len(shape))
    return pl.pallas_call(
        functools.partial(_ret_kernel, n_chunks=n_chunks, unroll=2 if n_chunks % 2 == 0 else 1),
        grid=(B,),
        in_specs=[
            seq_spec(CB_RQ), seq_spec(CB_RK), seq_spec(CB_RV), seq_spec(CB_RG),
            const((S, LANES)), const((S, LANES)),
            const((N_PAIR, 2, C, C)), const((N_PAIR, C, LANES)), const((N_PAIR, C, LANES)),
            const((N_PAIR, LANES, LANES)), const((LANES, LANES)), const((1, RET_W)),
        ],
        out_specs=pl.BlockSpec((1, S, RET_W), lambda b: (b, 0, 0)),
        out_shape=jax.ShapeDtypeStruct((B, S, RET_W), BF16),
        scratch_shapes=[pltpu.VMEM((N_PAIR, LANES, LANES), F32)],
        compiler_params=_cparams(("parallel",)),
        name="retention",
    )(main3, main3, main3, main3, cos_t, sin_t, d_in, d_k, d_q, d_c, s_mask, gain_row)


def _fox_kernel(q_ref, qa_ref, k_ref, v_ref, gain_ref, o_ref, *, tq, tk):
    p = pl.program_id(1)
    qi = pl.program_id(2)
    q = q_ref[0]
    qa = qa_ref[0]
    lane_q = lax.broadcasted_iota(jnp.int32, (tq, LANES), 1)
    head0 = lane_q < HEAD_DIM
    qops = []
    for j in range(2):
        qm = jnp.where(head0 if j == 0 else jnp.logical_not(head0), q, jnp.zeros_like(q))
        lo = (2 * p + j) * AUG_STRIDE
        own = jnp.logical_and(lane_q >= lo, lane_q < lo + AUG_STRIDE)
        qops.append(jnp.concatenate([qm, jnp.where(own, qa, jnp.zeros_like(qa))], axis=1))
    q_pos = qi * tq + lax.broadcasted_iota(jnp.int32, (tq, tk), 0)
    k_off = lax.broadcasted_iota(jnp.int32, (tq, tk), 1)
    head0_k = lax.broadcasted_iota(jnp.int32, (tk, LANES), 1) < HEAD_DIM
    one = jnp.ones((tk, LANES), BF16)

    def tile(t, carry, masked):
        c0 = pl.multiple_of(t * tk, tk)
        k = k_ref[0, pl.ds(c0, tk), :]
        v = v_ref[0, pl.ds(c0, tk), :]
        new = []
        for j in range(2):
            m, acc = carry[2 * j:2 * j + 2]
            s = _dot_nt(qops[j], k)
            if masked:
                s = jnp.where(k_off + c0 <= q_pos, s, NEG)
            m_new = jnp.maximum(m, jnp.max(s, axis=-1, keepdims=True))
            pr = jnp.exp(s - m_new)
            vj = jnp.where(head0_k if j == 0 else jnp.logical_not(head0_k), v, one)
            acc = jnp.exp(m - m_new) * acc + jnp.dot(pr.astype(BF16), vj, preferred_element_type=F32)
            new += [m_new, acc]
        return tuple(new)

    init = []
    for _ in range(2):
        init += [jnp.full((tq, 1), NEG, F32), jnp.zeros((tq, LANES), F32)]
    n_full = (qi * tq) // tk
    carry = lax.fori_loop(0, n_full, lambda t, c: tile(t, c, False), tuple(init))
    for d in range(max(1, tq // tk)):
        carry = tile(n_full + d, carry, True)
    o0 = carry[1] / carry[1][:, HEAD_DIM:HEAD_DIM + 1]
    o1 = carry[3] / carry[3][:, 0:1]
    out = jnp.where(head0, o0, o1)
    out2 = out * out
    inv = 1.0 / HEAD_DIM
    v0 = jnp.sum(jnp.where(head0, out2, 0.0), axis=-1, keepdims=True)
    v1 = jnp.sum(jnp.where(head0, 0.0, out2), axis=-1, keepdims=True)
    o_ref[0] = (out * lax.rsqrt(jnp.where(head0, v0, v1) * inv + RMS_EPS) * gain_ref[...]).astype(BF16)


def _fox(main3, q_aug, k_cat, gain_row, tq, tk):
    B, S, _ = main3.shape
    return pl.pallas_call(
        functools.partial(_fox_kernel, tq=tq, tk=tk),
        grid=(B, N_PAIR, S // tq),
        in_specs=[
            pl.BlockSpec((1, tq, LANES), lambda b, p, i: (b, i, CB_FQ + p)),
            pl.BlockSpec((1, tq, LANES), lambda b, p, i: (b, i, 0)),
            pl.BlockSpec((1, S, 2 * LANES), lambda b, p, i: (b, 0, p)),
            pl.BlockSpec((1, S, LANES), lambda b, p, i: (b, 0, CB_FV + p)),
            pl.BlockSpec((1, LANES), lambda b, p, i: (0, p)),
        ],
        out_specs=pl.BlockSpec((1, tq, LANES), lambda b, p, i: (b, i, p)),
        out_shape=jax.ShapeDtypeStruct((B, S, FOX_W), BF16),
        compiler_params=_cparams(("parallel", "parallel", "arbitrary")),
        name="fox_attention",
    )(main3, q_aug, k_cat, main3, gain_row)


def _gelu_tanh(x):
    return 0.5 * x * (1.0 + jnp.tanh(np.sqrt(2.0 / np.pi).astype(np.float32) * (x + 0.044715 * (x * x * x))))


def _cmp_kernel(r_ref, pos_ref, wt_ref, wb_ref, w2_ref, o_ref):
    r = r_ref[0]
    n = r.shape[0]
    top = jnp.dot(r, wt_ref[...], preferred_element_type=F32)
    bot = jnp.dot(r, wb_ref[...], preferred_element_type=F32)
    cpos = jnp.dot(pos_ref[0:8, :], wt_ref[...], preferred_element_type=F32)
    cpos = cpos + jnp.dot(pos_ref[8:16, :], wb_ref[...], preferred_element_type=F32)
    hid = _gelu_tanh(top + pltpu.roll(bot, n - 1, 0) + cpos[0:1, :])
    o_ref[0] = jnp.dot(hid.astype(BF16), w2_ref[...], preferred_element_type=F32)


def _nsa_compress(r, pos2, wt, wb, w2):
    B, n, _ = r.shape
    width = CMP_STRIDE * LANES
    return pl.pallas_call(
        _cmp_kernel,
        grid=(B,),
        in_specs=[
            pl.BlockSpec((1, n, width), lambda b: (b, 0, 0)),
            pl.BlockSpec((16, width), lambda b: (0, 0)),
            pl.BlockSpec((width, 2 * CMP_HIDDEN), lambda b: (0, 0)),
            pl.BlockSpec((width, 2 * CMP_HIDDEN), lambda b: (0, 0)),
            pl.BlockSpec((2 * CMP_HIDDEN, LANES), lambda b: (0, 0)),
        ],
        out_specs=pl.BlockSpec((1, n, LANES), lambda b: (b, 0, 0)),
        out_shape=jax.ShapeDtypeStruct((B, n, LANES), F32),
        compiler_params=_cparams(("parallel",)),
        name="nsa_compress",
    )(r, pos2, wt, wb, w2)


def _softmax_rows(s, mask):
    sm = jnp.where(mask, s, NEG)
    m = jnp.max(sm, axis=-1, keepdims=True)
    e = jnp.where(mask, jnp.exp(sm - m), 0.0)
    l = jnp.sum(e, axis=-1, keepdims=True)
    return e / jnp.where(l > 0.0, l, 1.0)


def _nsa_kernel(q_ref, cmp_ref, c2s_ref, sel_ref, ksa_ref, win_ref, gl_ref, gain_ref, o_ref, *, nb, tq, tk, n_cmp, n_sel, top_n):
    H = N_NSA
    qi = pl.program_id(1)
    start = qi * tq
    t_row = start + lax.broadcasted_iota(jnp.int32, (tq, 1), 0)
    t_all = jnp.concatenate([t_row] * H, axis=0)
    zeros64 = jnp.zeros((tq, HEAD_DIM), BF16)
    wlen = WINDOW + tq
    base = pl.multiple_of(jnp.maximum(start - WINDOW, 0), tq)
    n_pad = cmp_ref.shape[1]

    def prologue(bb):
        q = q_ref[bb]
        q_heads = [q[:, h * HEAD_DIM:(h + 1) * HEAD_DIM] for h in range(H)]
        q_pad = jnp.concatenate([jnp.concatenate([qh, zeros64], axis=1) for qh in q_heads], axis=0)

        win = win_ref[bb, pl.ds(base, wlen), :]
        s_w = _dot_nt(q_pad, win)
        kpos = base + lax.broadcasted_iota(jnp.int32, (H * tq, wlen), 1)
        vis_w = jnp.logical_and(kpos <= t_all, kpos > t_all - WINDOW)
        s_w = jnp.where(vis_w, s_w, NEG)
        e_w = jnp.exp(s_w - jnp.max(s_w, axis=-1, keepdims=True))
        win1 = jnp.where(lax.broadcasted_iota(jnp.int32, (wlen, LANES), 1) < HEAD_DIM, jnp.ones((wlen, LANES), BF16), win)
        o_w = jnp.dot(e_w.astype(BF16), win1, preferred_element_type=F32)
        o_w = o_w / o_w[:, 0:1]

        cmp = cmp_ref[bb]
        cmp_hi = cmp.astype(BF16)
        cmp_lo = (cmp - cmp_hi.astype(F32)).astype(BF16)
        s_c = _dot_nt(q_pad, cmp_hi) + _dot_nt(q_pad, cmp_lo)
        n_id = lax.broadcasted_iota(jnp.int32, (H * tq, n_pad), 1)
        vis_c = jnp.logical_and(n_id * CMP_STRIDE + (CMP_LEN - 1) <= t_all, n_id < n_cmp)
        p_c = _softmax_rows(s_c, vis_c)
        o_c = jnp.dot(p_c.astype(BF16), cmp_hi, preferred_element_type=F32)

        p_sum = p_c[0:tq]
        for h in range(1, H):
            p_sum = p_sum + p_c[h * tq:(h + 1) * tq]
        p_hi = p_sum.astype(BF16)
        p_lo = (p_sum - p_hi.astype(F32)).astype(BF16)
        c2s_t = c2s_ref[...]
        imp = _dot_nt(c2s_t, p_hi) + _dot_nt(c2s_t, p_lo)
        t_lane = start + lax.broadcasted_iota(jnp.int32, (HEAD_DIM, tq), 1)
        s_id = lax.broadcasted_iota(jnp.int32, (HEAD_DIM, tq), 0)
        cur = t_lane >> 6
        forced = jnp.logical_or(s_id == 0, jnp.logical_or(s_id == cur, s_id == cur - 1))
        score = jnp.where(forced, FORCED_SCORE, imp)
        score = jnp.where(s_id * SEL_LEN <= t_lane, score, -1.0)
        SUB = 8
        groups = [score[g * SUB:(g + 1) * SUB, :] for g in range(HEAD_DIM // SUB)]
        ranks = [jnp.zeros((SUB, tq), jnp.int32) for _ in groups]
        sub_id = lax.broadcasted_iota(jnp.int32, (SUB, tq), 0)
        for c in range(n_sel):
            row = score[c:c + 1, :]
            for g in range(len(groups)):
                if g * SUB > c:
                    before = row >= groups[g]
                elif g * SUB + SUB - 1 < c:
                    before = row > groups[g]
                else:
                    before = jnp.logical_or(row > groups[g], jnp.logical_and(row == groups[g], sub_id > c - g * SUB))
                ranks[g] = ranks[g] + before.astype(jnp.int32)
        not_sel = jnp.concatenate([jnp.where(r < top_n, 0.0, 1.0) for r in ranks], axis=0)
        ns = jnp.concatenate([jnp.zeros((HEAD_DIM, tq), F32), not_sel], axis=0).T.astype(BF16)
        q_aug = q_pad + jnp.concatenate([ns] * H, axis=0)
        return q_aug, o_c, o_w

    pro = [prologue(bb) for bb in range(nb)]

    def sel_tile(bb, c0, size, carry, diagonal):
        m, acc = carry
        s = _dot_nt(pro[bb][0], ksa_ref[bb, pl.ds(c0, size), :])
        if diagonal:
            r_id = lax.broadcasted_iota(jnp.int32, (H * tq, size), 0) & (tq - 1)
            c_id = lax.broadcasted_iota(jnp.int32, (H * tq, size), 1)
            s = jnp.where(c_id <= r_id, s, NEG)
        m_new = jnp.maximum(m, jnp.max(s, axis=-1, keepdims=True))
        pr = jnp.exp(s - m_new)
        key_lanes = lax.broadcasted_iota(jnp.int32, (size, LANES), 1) < HEAD_DIM
        v1 = jnp.where(key_lanes, jnp.ones((size, LANES), BF16), sel_ref[bb, pl.ds(c0, size), :])
        acc = jnp.exp(m - m_new) * acc + jnp.dot(pr.astype(BF16), v1, preferred_element_type=F32)
        return m_new, acc

    def all_rows(c0, size, carries, diagonal):
        return tuple(sel_tile(bb, c0, size, carries[bb], diagonal) for bb in range(nb))

    carries = tuple((jnp.full((H * tq, 1), NEG, F32), jnp.zeros((H * tq, LANES), F32)) for _ in range(nb))
    n_full = start // tk
    carries = lax.fori_loop(0, n_full, lambda t, c: all_rows(pl.multiple_of(t * tk, tk), tk, c, False), carries)
    n_tail = (start - n_full * tk) // tq
    for d in range(tk // tq - 1):
        c0 = pl.multiple_of(n_full * tk + d * tq, tq)
        carries = lax.cond(d < n_tail, lambda c, c0=c0: all_rows(c0, tq, c, False), lambda c: c, carries)
    carries = all_rows(pl.multiple_of(start, tq), tq, carries, True)

    for bb in range(nb):
        _, o_c, o_w = pro[bb]
        acc = carries[bb][1]
        o_s = acc / acc[:, 0:1]
        gates = 1.0 / (1.0 + jnp.exp(-gl_ref[bb]))
        outs = []
        for h in range(H):
            rows = slice(h * tq, (h + 1) * tq)
            g0 = gates[:, SMALL_GATE + 3 * h + 0:SMALL_GATE + 3 * h + 1]
            g1 = gates[:, SMALL_GATE + 3 * h + 1:SMALL_GATE + 3 * h + 2]
            g2 = gates[:, SMALL_GATE + 3 * h + 2:SMALL_GATE + 3 * h + 3]
            mix = g0 * o_c[rows] + g1 * o_s[rows] + g2 * o_w[rows]
            val = mix[:, HEAD_DIM:]
            outs.append(val * lax.rsqrt(jnp.mean(val * val, axis=-1, keepdims=True) + RMS_EPS))
        o_ref[bb] = (jnp.concatenate(outs, axis=1) * gain_ref[...]).astype(BF16)


def _nsa(main3, cmp, c2s, ks_aug, small3, gain_row, nb, tq, tk, n_cmp):
    B, S, _ = main3.shape
    n_sel = S // SEL_LEN
    n_pad = cmp.shape[1]
    return pl.pallas_call(
        functools.partial(_nsa_kernel, nb=nb, tq=tq, tk=tk, n_cmp=n_cmp, n_sel=n_sel, top_n=min(TOP_N, n_sel)),
        grid=(B // nb, S // tq),
        in_specs=[
            pl.BlockSpec((nb, tq, NSA_W), lambda b, i: (b, i, CB_NQ // 2)),
            pl.BlockSpec((nb, n_pad, LANES), lambda b, i: (b, 0, 0)),
            pl.BlockSpec((HEAD_DIM, n_pad), lambda b, i: (0, 0)),
            pl.BlockSpec((nb, S, LANES), lambda b, i: (b, 0, CB_SEL)),
            pl.BlockSpec((nb, S, LANES), lambda b, i: (b, 0, 0)),
            pl.BlockSpec((nb, S, LANES), lambda b, i: (b, 0, CB_WIN)),
            pl.BlockSpec((nb, tq, LANES), lambda b, i: (b, i, 0)),
            pl.BlockSpec((1, NSA_W), lambda b, i: (0, 0)),
        ],
        out_specs=pl.BlockSpec((nb, tq, NSA_W), lambda b, i: (b, i, 0)),
        out_shape=jax.ShapeDtypeStruct((B, S, NSA_W), BF16),
        compiler_params=_cparams(("parallel", "arbitrary")),
        name="nsa_attention",
    )(main3, cmp, c2s, main3, ks_aug, main3, small3, gain_row)


def _post_kernel(x_ref, yr_ref, yf_ref, yn_ref, wo_ref, g_ref, w1_ref, w2_ref, gf_ref, o_ref, *, ff_chunk, final):
    x = x_ref[...]
    x = x + jnp.dot(yr_ref[...], wo_ref[0:RET_W, :], preferred_element_type=F32)
    x = x + jnp.dot(yf_ref[...], wo_ref[RET_W:RET_W + FOX_W, :], preferred_element_type=F32)
    x = x + jnp.dot(yn_ref[...], wo_ref[RET_W + FOX_W:, :], preferred_element_type=F32)
    h = _rms(x, g_ref[...]).astype(BF16)
    o_ref[...] = x
    for c0 in range(0, D_FF, ff_chunk):
        hid = jnp.maximum(jnp.dot(h, w1_ref[:, c0:c0 + ff_chunk], preferred_element_type=F32), 0.0)
        o_ref[...] += jnp.dot((hid * hid).astype(BF16), w2_ref[c0:c0 + ff_chunk, :], preferred_element_type=F32)
    if final:
        o_ref[...] = _rms(o_ref[...], gf_ref[...])


def _post(x2d, yr, yf, yn, wo, g, w1, w2, gf, tm, final):
    T = x2d.shape[0]
    row = lambda w: pl.BlockSpec((tm, w), lambda i: (i, 0))
    return pl.pallas_call(
        functools.partial(_post_kernel, ff_chunk=512, final=final),
        grid=(T // tm,),
        in_specs=[
            row(D_MODEL), row(RET_W), row(FOX_W), row(NSA_W),
            _resident((D_MODEL, D_MODEL), lambda i: (0, 0)),
            _resident((1, D_MODEL), lambda i: (0, 0)),
            _resident((D_MODEL, D_FF), lambda i: (0, 0)),
            _resident((D_FF, D_MODEL), lambda i: (0, 0)),
            _resident((1, D_MODEL), lambda i: (0, 0)),
        ],
        out_specs=row(D_MODEL),
        out_shape=jax.ShapeDtypeStruct((T, D_MODEL), F32),
        compiler_params=_cparams(("parallel",)),
        name="outproj_mlp",
    )(x2d, yr, yf, yn, wo, g, w1, w2, gf)


def _tile_sizes(batch):
    return dict(tm=512, fox_tq=512, fox_tk=512, nsa_tq=256, nsa_tk=512, nsa_nb=2 if batch % 2 == 0 else 1)


def kernel(x, norm_attn, w_in, fox_forget_bias, ret_norm_gain, fox_norm_gain, nsa_norm_gain, nsa_cmp_pos_k, nsa_cmp_pos_v, nsa_cmp_w1_k, nsa_cmp_w2_k, nsa_cmp_w1_v, nsa_cmp_w2_v, w_out, norm_mlp, w_mlp_in, w_mlp_out, norm_final):
    B, S, D = x.shape
    depth = w_in.shape[0]
    assert D == D_MODEL and S % 512 == 0 and S >= WINDOW + 256
    T = B * S
    ts = _tile_sizes(B)

    cols, scale, small_cols, small_mask = _in_proj_columns()
    wm = (w_in[:, :, cols] * scale).astype(BF16)
    ws = (w_in[:, :, small_cols] * small_mask).astype(BF16)
    wo = w_out.astype(BF16)
    w1 = w_mlp_in.astype(BF16)
    w2 = w_mlp_out.astype(BF16)

    def expand_w1(wk, wv):
        L = wk.shape[0]
        wk = wk.reshape(L, CMP_LEN, HEAD_DIM, CMP_HIDDEN)
        wv = wv.reshape(L, CMP_LEN, HEAD_DIM, CMP_HIDDEN)
        z = jnp.zeros_like(wk)
        full = jnp.concatenate([jnp.concatenate([wk, z], axis=-1), jnp.concatenate([z, wv], axis=-1)], axis=2)
        full = full.reshape(L, CMP_LEN * LANES, 2 * CMP_HIDDEN).astype(BF16)
        return full[:, :CMP_STRIDE * LANES], full[:, CMP_STRIDE * LANES:]

    wt_all, wb_all = expand_w1(nsa_cmp_w1_k, nsa_cmp_w1_v)
    zk = jnp.zeros_like(nsa_cmp_w2_k)
    w2c = jnp.concatenate([jnp.concatenate([nsa_cmp_w2_k, zk], axis=-1),
                           jnp.concatenate([zk, nsa_cmp_w2_v], axis=-1)], axis=1).astype(BF16)
    pos = jnp.concatenate([nsa_cmp_pos_k, nsa_cmp_pos_v], axis=-1)
    pos_tb = pos.reshape(depth, 2, 1, CMP_STRIDE * LANES)
    pos_tb = jnp.broadcast_to(pos_tb, (depth, 2, 8, CMP_STRIDE * LANES)).reshape(depth, 16, CMP_STRIDE * LANES).astype(BF16)

    n_cmp = (S - CMP_LEN) // CMP_STRIDE + 1
    n_sel = S // SEL_LEN
    n_rows = S // CMP_STRIDE
    cs = np.arange(n_rows) * CMP_STRIDE
    ss = np.arange(n_sel) * SEL_LEN
    overlap = np.clip(np.minimum(cs[:, None] + CMP_LEN, ss[None, :] + SEL_LEN) - np.maximum(cs[:, None], ss[None, :]), 0, None)
    overlap[n_cmp:] = 0
    c2s_np = np.zeros((HEAD_DIM, n_rows), np.float32)
    c2s_np[:n_sel] = (overlap / CMP_LEN).T
    c2s = jnp.asarray(c2s_np, dtype=BF16)
    assert n_sel <= HEAD_DIM, "the selected-branch key augmentation has 64 lanes, one per selection block"
    neg_onehot = jnp.asarray(np.where((np.arange(S)[:, None] // SEL_LEN) == np.arange(HEAD_DIM)[None, :], NEG, 0.0), dtype=BF16)

    tables = _retention_tables(S)
    fbias = jnp.zeros((depth, 1, LANES), F32).at[:, 0, SMALL_FF:SMALL_FF + N_FOX].set(fox_forget_bias)

    x2d = x.reshape(T, D)
    for l in range(depth):
        main, small = _inproj(x2d, norm_attn[l][None, :], wm[l], ws[l], ts["tm"])
        main3 = main.reshape(B, S, N_MAIN)
        small3 = small.reshape(B, S, LANES)
        fox_qa, fox_ka = _forget_cumsum(small3, fbias[l])
        y_ret = _retention(main3, tables, ret_norm_gain[l][None, :])
        fk = main3[:, :, CB_FK * LANES:(CB_FK + N_PAIR) * LANES].reshape(B, S, N_PAIR, LANES)
        ka = jnp.broadcast_to(fox_ka[:, :, None, :], (B, S, N_PAIR, LANES))
        k_cat = jnp.concatenate([fk, ka], axis=-1).reshape(B, S, 2 * FOX_W)
        y_fox = _fox(main3, fox_qa, k_cat, fox_norm_gain[l][None, :], ts["fox_tq"], ts["fox_tk"])
        r = main3[:, :, CB_CMP * LANES:(CB_CMP + 1) * LANES].reshape(B, n_rows, CMP_STRIDE * LANES)
        cmp = _nsa_compress(r, pos_tb[l], wt_all[l], wb_all[l], w2c[l])
        ks = main3[:, :, CB_SEL * LANES:CB_SEL * LANES + HEAD_DIM]
        ks_aug = jnp.concatenate([ks, jnp.broadcast_to(neg_onehot[None], (B, S, HEAD_DIM))], axis=-1)
        y_nsa = _nsa(main3, cmp, c2s, ks_aug, small3, nsa_norm_gain[l][None, :], ts["nsa_nb"], ts["nsa_tq"], ts["nsa_tk"], n_cmp)
        x2d = _post(x2d, y_ret.reshape(T, RET_W), y_fox.reshape(T, FOX_W), y_nsa.reshape(T, NSA_W), wo[l],
                    norm_mlp[l][None, :], w1[l], w2[l], norm_final[None, :], ts["tm"], final=(l == depth - 1))
    return x2d.reshape(B, S, D)
```

```python
import functools

import numpy as np
import jax
import jax.numpy as jnp
from jax import lax
from jax.experimental import pallas as pl
from jax.experimental.pallas import tpu as pltpu

F32 = jnp.float32
BF16 = jnp.bfloat16

D_MODEL = 1024
HEAD_DIM = 64
N_RET = 6
N_FOX = 6
N_NSA = 4
RET_W = N_RET * HEAD_DIM
FOX_W = N_FOX * HEAD_DIM
NSA_W = N_NSA * HEAD_DIM
D_FF = 4 * D_MODEL
RMS_EPS = 1e-6
RET_CHUNK = 128
ROPE_BASE = 10000.0
CMP_LEN = 32
CMP_STRIDE = 16
CMP_HIDDEN = 4 * HEAD_DIM
SEL_LEN = 64
TOP_N = 16
WINDOW = 512
FORCED_SCORE = 1e4
NEG = -1e30

LANES = 128
N_PAIR = N_RET // 2

CB_RQ, CB_RK, CB_RV, CB_RG = 0, 3, 6, 9
CB_NQ = 12
CB_FQ, CB_FK, CB_FV = 14, 17, 20
CB_CMP, CB_SEL, CB_WIN = 23, 24, 25
N_MAIN = 26 * LANES
SMALL_FF = 0
SMALL_GATE = 8

VMEM_LIMIT = 56 * 1024 * 1024


def _cparams(sem):
    return pltpu.CompilerParams(dimension_semantics=sem, vmem_limit_bytes=VMEM_LIMIT)


def _resident(shape, index_map):
    return pl.BlockSpec(shape, index_map, pipeline_mode=pl.Buffered(1))


def _in_proj_columns():
    sizes = (RET_W, RET_W, RET_W, RET_W, FOX_W, FOX_W, FOX_W, N_FOX, NSA_W) + (HEAD_DIM,) * 6 + (3 * N_NSA,)
    off = np.concatenate([[0], np.cumsum(sizes)])
    (o_rq, o_rk, o_rv, o_rg, o_fq, o_fk, o_fv, o_ff, o_nq, o_kc, o_vc, o_ks, o_vs, o_kw, o_vw, o_gt) = off[:-1]
    half = HEAD_DIM // 2
    inter = []
    for p in range(N_PAIR):
        a, b = 2 * p, 2 * p + 1
        for h, part in ((a, 0), (b, 0), (a, 1), (b, 1)):
            inter.extend(range(h * HEAD_DIM + part * half, h * HEAD_DIM + (part + 1) * half))
    inter = np.asarray(inter)
    nat = np.arange(RET_W)
    cols, scale = [], []

    def add(idx, s=1.0):
        cols.append(np.asarray(idx))
        scale.append(np.full(len(idx), s, np.float32))

    qk_scale = HEAD_DIM ** -0.5
    add(o_rq + inter)
    add(o_rk + inter, qk_scale)
    add(o_rv + nat)
    add(o_rg + nat)
    add(o_nq + np.arange(NSA_W), qk_scale)
    add(o_fq + nat, qk_scale)
    add(o_fk + nat)
    add(o_fv + nat)
    for o in (o_kc, o_vc, o_ks, o_vs, o_kw, o_vw):
        add(o + np.arange(HEAD_DIM))
    cols = np.concatenate(cols)
    scale = np.concatenate(scale)
    assert cols.shape[0] == N_MAIN
    small_cols = np.zeros(LANES, np.int64)
    small_mask = np.zeros(LANES, np.float32)
    small_cols[SMALL_FF:SMALL_FF + N_FOX] = o_ff + np.arange(N_FOX)
    small_mask[SMALL_FF:SMALL_FF + N_FOX] = 1.0
    small_cols[SMALL_GATE:SMALL_GATE + 3 * N_NSA] = o_gt + np.arange(3 * N_NSA)
    small_mask[SMALL_GATE:SMALL_GATE + 3 * N_NSA] = 1.0
    return cols, scale, small_cols, small_mask


def _retention_tables(seq):
    half = HEAD_DIM // 2
    inv = 1.0 / (ROPE_BASE ** (jnp.arange(half, dtype=F32) / half))
    ang = jnp.arange(seq, dtype=F32)[:, None] * inv[None, :]
    cos, sin = jnp.cos(ang), jnp.sin(ang)
    cos_t = jnp.concatenate([cos, cos, cos, cos], axis=-1)
    sin_t = jnp.concatenate([-sin, -sin, sin, sin], axis=-1)
    log_gamma = np.log(1.0 - 2.0 ** (-5.0 - np.arange(N_RET, dtype=np.float32))).astype(np.float32)
    C = RET_CHUNK
    idx = np.arange(C, dtype=np.float32)
    diff = idx[:, None] - idx[None, :]
    lane = np.arange(LANES)
    head_k = (lane % HEAD_DIM) // half
    head_v = lane // HEAD_DIM
    d_in = np.zeros((N_PAIR, 2, C, C), np.float32)
    d_k = np.zeros((N_PAIR, C, LANES), np.float32)
    d_q = np.zeros((N_PAIR, C, LANES), np.float32)
    d_c = np.zeros((N_PAIR, LANES, LANES), np.float32)
    for p in range(N_PAIR):
        lg = log_gamma[2 * p:2 * p + 2]
        for j in range(2):
            d_in[p, j] = np.where(diff >= 0, np.exp(lg[j] * np.maximum(diff, 0.0)), 0.0)
        d_k[p] = np.exp(lg[head_k][None, :] * (C - 1.0 - idx)[:, None])
        d_q[p] = np.exp(lg[head_v][None, :] * (idx + 1.0)[:, None])
        d_c[p] = np.broadcast_to(np.exp(lg[head_v] * C)[None, :], (LANES, LANES))
    s_mask = (head_k[:, None] == head_v[None, :]).astype(np.float32)
    return cos_t, sin_t, jnp.asarray(d_in), jnp.asarray(d_k), jnp.asarray(d_q), jnp.asarray(d_c), jnp.asarray(s_mask)


def _rms(x, g):
    return x * lax.rsqrt(jnp.mean(x * x, axis=-1, keepdims=True) + RMS_EPS) * g


def _inproj_kernel(x_ref, g_ref, wm_ref, ws_ref, om_ref, os_ref, *, n_chunk):
    h = _rms(x_ref[...], g_ref[...]).astype(BF16)
    for n0 in range(0, N_MAIN, n_chunk):
        n1 = min(n0 + n_chunk, N_MAIN)
        om_ref[:, n0:n1] = jnp.dot(h, wm_ref[:, n0:n1], preferred_element_type=F32).astype(BF16)
    os_ref[...] = jnp.dot(h, ws_ref[...], preferred_element_type=F32)


def _inproj(x2d, g, wm, ws, tm):
    T = x2d.shape[0]
    return pl.pallas_call(
        functools.partial(_inproj_kernel, n_chunk=512),
        grid=(T // tm,),
        in_specs=[
            pl.BlockSpec((tm, D_MODEL), lambda i: (i, 0)),
            _resident((1, D_MODEL), lambda i: (0, 0)),
            _resident((D_MODEL, N_MAIN), lambda i: (0, 0)),
            _resident((D_MODEL, LANES), lambda i: (0, 0)),
        ],
        out_specs=[
            pl.BlockSpec((tm, N_MAIN), lambda i: (i, 0)),
            pl.BlockSpec((tm, LANES), lambda i: (i, 0)),
        ],
        out_shape=[jax.ShapeDtypeStruct((T, N_MAIN), BF16), jax.ShapeDtypeStruct((T, LANES), F32)],
        compiler_params=_cparams(("parallel",)),
        name="inproj",
    )(x2d, g, wm, ws)


N_SPLIT = 3
AUG_STRIDE = 2 * N_SPLIT


def _split_bf16(x):
    terms, rest = [], x
    for _ in range(N_SPLIT):
        t = rest.astype(BF16)
        rest = rest - t.astype(F32)
        terms.append(t)
    return terms


def _bias_placement():
    mq = np.zeros((N_SPLIT, LANES, LANES), np.float32)
    mk = np.zeros((N_SPLIT, LANES, LANES), np.float32)
    one_q = np.zeros((1, LANES), np.float32)
    one_k = np.zeros((1, LANES), np.float32)
    for h in range(N_FOX):
        base = h * AUG_STRIDE
        for i in range(N_SPLIT):
            mq[i, SMALL_FF + h, base + i] = 1.0
            mk[i, SMALL_FF + h, base + N_SPLIT + i] = -1.0
            one_q[0, base + N_SPLIT + i] = 1.0
            one_k[0, base + i] = 1.0
    return jnp.asarray(mq, BF16), jnp.asarray(mk, BF16), jnp.asarray(one_q), jnp.asarray(one_k)


def _cum_kernel(z_ref, b_ref, tri_ref, mq_ref, mk_ref, oq_ref, ok_ref, qa_ref, ka_ref, *, n_chunks):
    C = RET_CHUNK

    def body(c, carry):
        r0 = pl.multiple_of(c * C, C)
        z = z_ref[0, pl.ds(r0, C), :] + b_ref[...]
        lf = jnp.minimum(z, 0.0) - jnp.log1p(jnp.exp(-jnp.abs(z)))
        cs = carry
        for term in _split_bf16(lf):
            cs = cs + jnp.dot(tri_ref[...], term, preferred_element_type=F32)
        qa, ka = oq_ref[...], ok_ref[...]
        for i, term in enumerate(_split_bf16(cs)):
            qa = qa + jnp.dot(term, mq_ref[i], preferred_element_type=F32)
            ka = ka + jnp.dot(term, mk_ref[i], preferred_element_type=F32)
        qa_ref[0, pl.ds(r0, C), :] = qa.astype(BF16)
        ka_ref[0, pl.ds(r0, C), :] = ka.astype(BF16)
        return cs[C - 1:C, :]

    lax.fori_loop(0, n_chunks, body, jnp.zeros((1, LANES), F32))


def _forget_cumsum(small3, bias_row):
    B, S, _ = small3.shape
    tri = jnp.asarray(np.tril(np.ones((RET_CHUNK, RET_CHUNK), np.float32)), BF16)
    mq, mk, one_q, one_k = _bias_placement()
    const = lambda shape: pl.BlockSpec(shape, lambda b: (0,) * len(shape))
    return pl.pallas_call(
        functools.partial(_cum_kernel, n_chunks=S // RET_CHUNK),
        grid=(B,),
        in_specs=[
            pl.BlockSpec((1, S, LANES), lambda b: (b, 0, 0)),
            const((1, LANES)), const((RET_CHUNK, RET_CHUNK)),
            const((N_SPLIT, LANES, LANES)), const((N_SPLIT, LANES, LANES)), const((1, LANES)), const((1, LANES)),
        ],
        out_specs=[
            pl.BlockSpec((1, S, LANES), lambda b: (b, 0, 0)),
            pl.BlockSpec((1, S, LANES), lambda b: (b, 0, 0)),
        ],
        out_shape=[jax.ShapeDtypeStruct((B, S, LANES), BF16), jax.ShapeDtypeStruct((B, S, LANES), BF16)],
        compiler_params=_cparams(("parallel",)),
        name="forget_cumsum",
    )(small3, bias_row, tri, mq, mk, one_q, one_k)


def _dot_nt(a, b):
    return lax.dot_general(a, b, (((1,), (1,)), ((), ())), preferred_element_type=F32)


def _ret_kernel(q_ref, k_ref, v_ref, g_ref, cos_ref, sin_ref, din_ref, dk_ref, dq_ref, dc_ref, sm_ref, gain_ref,
                o_ref, state_ref, *, n_chunks, unroll):
    C = RET_CHUNK
    half = HEAD_DIM // 2
    lane = lax.broadcasted_iota(jnp.int32, (C, LANES), 1)
    khead = (lane % HEAD_DIM) // half
    vhead0 = lane < HEAD_DIM
    state_ref[...] = jnp.zeros_like(state_ref)

    def pair_chunk(p, r0):
        rows = pl.ds(r0, C)
        lanes = slice(p * LANES, (p + 1) * LANES)
        cs, sn = cos_ref[rows, :], sin_ref[rows, :]
        q = q_ref[0, rows, lanes].astype(F32)
        k = k_ref[0, rows, lanes].astype(F32)
        q = q * cs + pltpu.roll(q, HEAD_DIM, 1) * sn
        k = k * cs + pltpu.roll(k, HEAD_DIM, 1) * sn
        v = v_ref[0, rows, lanes]
        qb, kb = q.astype(BF16), k.astype(BF16)
        state = state_ref[p]
        cross = jnp.dot(qb, state.astype(BF16), preferred_element_type=F32) * dq_ref[p]
        outs = []
        for j in range(2):
            qm = jnp.where(khead == j, qb, jnp.zeros_like(qb))
            inner = _dot_nt(qm, kb) * din_ref[p, j]
            outs.append(jnp.dot(inner.astype(BF16), v, preferred_element_type=F32))
        out = jnp.where(vhead0, outs[0], outs[1]) + cross
        kd = (k * dk_ref[p]).T.astype(BF16)
        state_ref[p] = state * dc_ref[p] + jnp.dot(kd, v, preferred_element_type=F32) * sm_ref[...]
        inv = 1.0 / HEAD_DIM
        s0 = jnp.sum(jnp.where(vhead0, out, 0.0), axis=-1, keepdims=True)
        s1 = jnp.sum(jnp.where(vhead0, 0.0, out), axis=-1, keepdims=True)
        yc = out - jnp.where(vhead0, s0, s1) * inv
        yc2 = yc * yc
        v0 = jnp.sum(jnp.where(vhead0, yc2, 0.0), axis=-1, keepdims=True)
        v1 = jnp.sum(jnp.where(vhead0, 0.0, yc2), axis=-1, keepdims=True)
        y = yc * lax.rsqrt(jnp.where(vhead0, v0, v1) * inv + RMS_EPS) * gain_ref[:, lanes]
        g = g_ref[0, rows, lanes].astype(F32)
        o_ref[0, rows, lanes] = (y * (g * (1.0 / (1.0 + jnp.exp(-g))))).astype(BF16)

    def body(c, _):
        for u in range(unroll):
            for p in range(N_PAIR):
                pair_chunk(p, pl.multiple_of((c * unroll + u) * C, C))
        return 0

    lax.fori_loop(0, n_chunks // unroll, body, 0)


def _retention(main3, tables, gain_row):
    B, S, _ = main3.shape
    cos_t, sin_t, d_in, d_k, d_q, d_c, s_mask = tables
    C = RET_CHUNK
    n_chunks = S // C
    seq_spec = lambda cb: pl.BlockSpec((1, S, RET_W), lambda b: (b, 0, cb // N_PAIR))
    const = lambda shape: pl.BlockSpec(shape, lambda b: (0,) * len(shape))
    return pl.pallas_call(
        functools.partial(_ret_kernel, n_chunks=n_chunks, unroll=max(u for u in (1, 2, 4) if n_chunks % u == 0)),
        grid=(B,),
        in_specs=[
            seq_spec(CB_RQ), seq_spec(CB_RK), seq_spec(CB_RV), seq_spec(CB_RG),
            const((S, LANES)), const((S, LANES)),
            const((N_PAIR, 2, C, C)), const((N_PAIR, C, LANES)), const((N_PAIR, C, LANES)),
            const((N_PAIR, LANES, LANES)), const((LANES, LANES)), const((1, RET_W)),
        ],
        out_specs=pl.BlockSpec((1, S, RET_W), lambda b: (b, 0, 0)),
        out_shape=jax.ShapeDtypeStruct((B, S, RET_W), BF16),
        scratch_shapes=[pltpu.VMEM((N_PAIR, LANES, LANES), F32)],
        compiler_params=_cparams(("parallel",)),
        name="retention",
    )(main3, main3, main3, main3, cos_t, sin_t, d_in, d_k, d_q, d_c, s_mask, gain_row)


def _fox_kernel(q_ref, qa_ref, kh_ref, ka_ref, v_ref, gain_ref, o_ref, k_ref, *, tq, tk):
    p = pl.program_id(1)
    qi = pl.program_id(2)

    @pl.when(qi == 0)
    def _():
        k_ref[0, :, 0:LANES] = kh_ref[0]
        k_ref[0, :, LANES:2 * LANES] = ka_ref[0]

    q = q_ref[0]
    qa = qa_ref[0]
    lane_q = lax.broadcasted_iota(jnp.int32, (tq, LANES), 1)
    head0 = lane_q < HEAD_DIM
    qops = []
    for j in range(2):
        qm = jnp.where(head0 if j == 0 else jnp.logical_not(head0), q, jnp.zeros_like(q))
        lo = (2 * p + j) * AUG_STRIDE
        own = jnp.logical_and(lane_q >= lo, lane_q < lo + AUG_STRIDE)
        qops.append(jnp.concatenate([qm, jnp.where(own, qa, jnp.zeros_like(qa))], axis=1))
    q_pos = qi * tq + lax.broadcasted_iota(jnp.int32, (tq, tk), 0)
    k_off = lax.broadcasted_iota(jnp.int32, (tq, tk), 1)
    head0_k = lax.broadcasted_iota(jnp.int32, (tk, LANES), 1) < HEAD_DIM
    one = jnp.ones((tk, LANES), BF16)

    def tile(t, carry, masked):
        c0 = pl.multiple_of(t * tk, tk)
        k = k_ref[0, pl.ds(c0, tk), :]
        v = v_ref[0, pl.ds(c0, tk), :]
        new = []
        for j in range(2):
            m, acc = carry[2 * j:2 * j + 2]
            s = _dot_nt(qops[j], k)
            if masked:
                s = jnp.where(k_off + c0 <= q_pos, s, NEG)
            m_new = jnp.maximum(m, jnp.max(s, axis=-1, keepdims=True))
            pr = jnp.exp(s - m_new)
            vj = jnp.where(head0_k if j == 0 else jnp.logical_not(head0_k), v, one)
            acc = jnp.exp(m - m_new) * acc + jnp.dot(pr.astype(BF16), vj, preferred_element_type=F32)
            new += [m_new, acc]
        return tuple(new)

    init = []
    for _ in range(2):
        init += [jnp.full((tq, 1), NEG, F32), jnp.zeros((tq, LANES), F32)]
    n_full = qi
    carry = lax.fori_loop(0, n_full, lambda t, c: tile(t, c, False), tuple(init))
    carry = tile(n_full, carry, True)
    o0 = carry[1] / carry[1][:, HEAD_DIM:HEAD_DIM + 1]
    o1 = carry[3] / carry[3][:, 0:1]
    out = jnp.where(head0, o0, o1)
    out2 = out * out
    inv = 1.0 / HEAD_DIM
    v0 = jnp.sum(jnp.where(head0, out2, 0.0), axis=-1, keepdims=True)
    v1 = jnp.sum(jnp.where(head0, 0.0, out2), axis=-1, keepdims=True)
    o_ref[0] = (out * lax.rsqrt(jnp.where(head0, v0, v1) * inv + RMS_EPS) * gain_ref[...]).astype(BF16)


def _fox(main3, q_aug, k_aug, gain_row, tq, tk):
    B, S, _ = main3.shape
    assert tq == tk, "the kernel handles exactly one diagonal tile per query block"
    return pl.pallas_call(
        functools.partial(_fox_kernel, tq=tq, tk=tk),
        grid=(B, N_PAIR, S // tq),
        in_specs=[
            pl.BlockSpec((1, tq, LANES), lambda b, p, i: (b, i, CB_FQ + p)),
            pl.BlockSpec((1, tq, LANES), lambda b, p, i: (b, i, 0)),
            pl.BlockSpec((1, S, LANES), lambda b, p, i: (b, 0, CB_FK + p)),
            pl.BlockSpec((1, S, LANES), lambda b, p, i: (b, 0, 0)),
            pl.BlockSpec((1, S, LANES), lambda b, p, i: (b, 0, CB_FV + p)),
            pl.BlockSpec((1, LANES), lambda b, p, i: (0, p)),
        ],
        out_specs=pl.BlockSpec((1, tq, LANES), lambda b, p, i: (b, i, p)),
        out_shape=jax.ShapeDtypeStruct((B, S, FOX_W), BF16),
        scratch_shapes=[pltpu.VMEM((1, S, 2 * LANES), BF16)],
        compiler_params=_cparams(("parallel", "parallel", "arbitrary")),
        name="fox_attention",
    )(main3, q_aug, main3, k_aug, main3, gain_row)


def _gelu_tanh(x):
    return 0.5 * x * (1.0 + jnp.tanh(np.sqrt(2.0 / np.pi).astype(np.float32) * (x + 0.044715 * (x * x * x))))


def _cmp_kernel(r_ref, pos_ref, wt_ref, wb_ref, w2_ref, o_ref):
    r = r_ref[0]
    n = r.shape[0]
    top = jnp.dot(r, wt_ref[...], preferred_element_type=F32)
    bot = jnp.dot(r, wb_ref[...], preferred_element_type=F32)
    cpos = jnp.dot(pos_ref[0:8, :], wt_ref[...], preferred_element_type=F32)
    cpos = cpos + jnp.dot(pos_ref[8:16, :], wb_ref[...], preferred_element_type=F32)
    hid = _gelu_tanh(top + pltpu.roll(bot, n - 1, 0) + cpos[0:1, :])
    o_ref[0] = jnp.dot(hid.astype(BF16), w2_ref[...], preferred_element_type=F32)


def _nsa_compress(r, pos2, wt, wb, w2):
    B, n, _ = r.shape
    width = CMP_STRIDE * LANES
    return pl.pallas_call(
        _cmp_kernel,
        grid=(B,),
        in_specs=[
            pl.BlockSpec((1, n, width), lambda b: (b, 0, 0)),
            pl.BlockSpec((16, width), lambda b: (0, 0)),
            pl.BlockSpec((width, 2 * CMP_HIDDEN), lambda b: (0, 0)),
            pl.BlockSpec((width, 2 * CMP_HIDDEN), lambda b: (0, 0)),
            pl.BlockSpec((2 * CMP_HIDDEN, LANES), lambda b: (0, 0)),
        ],
        out_specs=pl.BlockSpec((1, n, LANES), lambda b: (b, 0, 0)),
        out_shape=jax.ShapeDtypeStruct((B, n, LANES), F32),
        compiler_params=_cparams(("parallel",)),
        name="nsa_compress",
    )(r, pos2, wt, wb, w2)


def _softmax_rows(s, mask):
    sm = jnp.where(mask, s, NEG)
    m = jnp.max(sm, axis=-1, keepdims=True)
    e = jnp.where(mask, jnp.exp(sm - m), 0.0)
    l = jnp.sum(e, axis=-1, keepdims=True)
    return e / jnp.where(l > 0.0, l, 1.0)


def _nsa_kernel(q_ref, cmp_ref, c2s_ref, sel_ref, noh_ref, win_ref, gl_ref, gain_ref, o_ref, ksa_ref, *, nb, tq, tk, n_cmp, n_sel, top_n):
    H = N_NSA
    qi = pl.program_id(1)
    start = qi * tq

    @pl.when(qi == 0)
    def _():
        key_lanes = lax.broadcasted_iota(jnp.int32, noh_ref.shape, 1) < HEAD_DIM
        for bb in range(nb):
            ksa_ref[bb] = jnp.where(key_lanes, sel_ref[bb], noh_ref[...])

    t_row = start + lax.broadcasted_iota(jnp.int32, (tq, 1), 0)
    t_all = jnp.concatenate([t_row] * H, axis=0)
    zeros64 = jnp.zeros((tq, HEAD_DIM), BF16)
    wlen = WINDOW + tq
    base = pl.multiple_of(jnp.maximum(start - WINDOW, 0), tq)
    n_pad = cmp_ref.shape[1]

    def prologue(bb):
        q = q_ref[bb]
        q_heads = [q[:, h * HEAD_DIM:(h + 1) * HEAD_DIM] for h in range(H)]
        q_pad = jnp.concatenate([jnp.concatenate([qh, zeros64], axis=1) for qh in q_heads], axis=0)

        win = win_ref[bb, pl.ds(base, wlen), :]
        s_w = _dot_nt(q_pad, win)
        kpos = base + lax.broadcasted_iota(jnp.int32, (H * tq, wlen), 1)
        vis_w = jnp.logical_and(kpos <= t_all, kpos > t_all - WINDOW)
        s_w = jnp.where(vis_w, s_w, NEG)
        e_w = jnp.exp(s_w - jnp.max(s_w, axis=-1, keepdims=True))
        win1 = jnp.where(lax.broadcasted_iota(jnp.int32, (wlen, LANES), 1) < HEAD_DIM, jnp.ones((wlen, LANES), BF16), win)
        o_w = jnp.dot(e_w.astype(BF16), win1, preferred_element_type=F32)
        o_w = o_w / o_w[:, 0:1]

        cmp = cmp_ref[bb]
        cmp_hi = cmp.astype(BF16)
        cmp_lo = (cmp - cmp_hi.astype(F32)).astype(BF16)
        s_c = _dot_nt(q_pad, cmp_hi) + _dot_nt(q_pad, cmp_lo)
        n_id = lax.broadcasted_iota(jnp.int32, (H * tq, n_pad), 1)
        vis_c = jnp.logical_and(n_id * CMP_STRIDE + (CMP_LEN - 1) <= t_all, n_id < n_cmp)
        p_c = _softmax_rows(s_c, vis_c)
        o_c = jnp.dot(p_c.astype(BF16), cmp_hi, preferred_element_type=F32)

        p_sum = p_c[0:tq]
        for h in range(1, H):
            p_sum = p_sum + p_c[h * tq:(h + 1) * tq]
        p_hi = p_sum.astype(BF16)
        p_lo = (p_sum - p_hi.astype(F32)).astype(BF16)
        c2s_t = c2s_ref[...]
        imp = _dot_nt(c2s_t, p_hi) + _dot_nt(c2s_t, p_lo)
        t_lane = start + lax.broadcasted_iota(jnp.int32, (HEAD_DIM, tq), 1)
        s_id = lax.broadcasted_iota(jnp.int32, (HEAD_DIM, tq), 0)
        cur = t_lane >> 6
        forced = jnp.logical_or(s_id == 0, jnp.logical_or(s_id == cur, s_id == cur - 1))
        score = jnp.where(forced, FORCED_SCORE, imp)
        score = jnp.where(s_id * SEL_LEN <= t_lane, score, -1.0)
        SUB = 8
        groups = [score[g * SUB:(g + 1) * SUB, :] for g in range(HEAD_DIM // SUB)]
        ranks = [jnp.zeros((SUB, tq), jnp.int32) for _ in groups]
        sub_id = lax.broadcasted_iota(jnp.int32, (SUB, tq), 0)
        for c in range(n_sel):
            row = score[c:c + 1, :]
            for g in range(len(groups)):
                if g * SUB > c:
                    before = row >= groups[g]
                elif g * SUB + SUB - 1 < c:
                    before = row > groups[g]
                else:
                    before = jnp.logical_or(row > groups[g], jnp.logical_and(row == groups[g], sub_id > c - g * SUB))
                ranks[g] = ranks[g] + before.astype(jnp.int32)
        not_sel = jnp.concatenate([jnp.where(r < top_n, 0.0, 1.0) for r in ranks], axis=0)
        ns = jnp.concatenate([jnp.zeros((HEAD_DIM, tq), F32), not_sel], axis=0).T.astype(BF16)
        q_aug = q_pad + jnp.concatenate([ns] * H, axis=0)
        return q_aug, o_c, o_w

    pro = [prologue(bb) for bb in range(nb)]

    def sel_tile(bb, c0, size, carry, diagonal):
        m, acc = carry
        s = _dot_nt(pro[bb][0], ksa_ref[bb, pl.ds(c0, size), :])
        if diagonal:
            r_id = lax.broadcasted_iota(jnp.int32, (H * tq, size), 0) & (tq - 1)
            c_id = lax.broadcasted_iota(jnp.int32, (H * tq, size), 1)
            s = jnp.where(c_id <= r_id, s, NEG)
        m_new = jnp.maximum(m, jnp.max(s, axis=-1, keepdims=True))
        pr = jnp.exp(s - m_new)
        key_lanes = lax.broadcasted_iota(jnp.int32, (size, LANES), 1) < HEAD_DIM
        v1 = jnp.where(key_lanes, jnp.ones((size, LANES), BF16), sel_ref[bb, pl.ds(c0, size), :])
        acc = jnp.exp(m - m_new) * acc + jnp.dot(pr.astype(BF16), v1, preferred_element_type=F32)
        return m_new, acc

    def all_rows(c0, size, carries, diagonal):
        return tuple(sel_tile(bb, c0, size, carries[bb], diagonal) for bb in range(nb))

    carries = tuple((jnp.full((H * tq, 1), NEG, F32), jnp.zeros((H * tq, LANES), F32)) for _ in range(nb))
    n_full = start // tk
    carries = lax.fori_loop(0, n_full, lambda t, c: all_rows(pl.multiple_of(t * tk, tk), tk, c, False), carries)
    n_tail = (start - n_full * tk) // tq
    for d in range(tk // tq - 1):
        c0 = pl.multiple_of(n_full * tk + d * tq, tq)
        carries = lax.cond(d < n_tail, lambda c, c0=c0: all_rows(c0, tq, c, False), lambda c: c, carries)
    carries = all_rows(pl.multiple_of(start, tq), tq, carries, True)

    for bb in range(nb):
        _, o_c, o_w = pro[bb]
        acc = carries[bb][1]
        o_s = acc / acc[:, 0:1]
        gates = 1.0 / (1.0 + jnp.exp(-gl_ref[bb]))
        outs = []
        for h in range(H):
            rows = slice(h * tq, (h + 1) * tq)
            g0 = gates[:, SMALL_GATE + 3 * h + 0:SMALL_GATE + 3 * h + 1]
            g1 = gates[:, SMALL_GATE + 3 * h + 1:SMALL_GATE + 3 * h + 2]
            g2 = gates[:, SMALL_GATE + 3 * h + 2:SMALL_GATE + 3 * h + 3]
            mix = g0 * o_c[rows] + g1 * o_s[rows] + g2 * o_w[rows]
            val = mix[:, HEAD_DIM:]
            outs.append(val * lax.rsqrt(jnp.mean(val * val, axis=-1, keepdims=True) + RMS_EPS))
        o_ref[bb] = (jnp.concatenate(outs, axis=1) * gain_ref[...]).astype(BF16)


def _nsa(main3, cmp, c2s, neg_onehot, small3, gain_row, nb, tq, tk, n_cmp):
    B, S, _ = main3.shape
    n_sel = S // SEL_LEN
    n_pad = cmp.shape[1]
    return pl.pallas_call(
        functools.partial(_nsa_kernel, nb=nb, tq=tq, tk=tk, n_cmp=n_cmp, n_sel=n_sel, top_n=min(TOP_N, n_sel)),
        grid=(B // nb, S // tq),
        in_specs=[
            pl.BlockSpec((nb, tq, NSA_W), lambda b, i: (b, i, CB_NQ // 2)),
            pl.BlockSpec((nb, n_pad, LANES), lambda b, i: (b, 0, 0)),
            pl.BlockSpec((HEAD_DIM, n_pad), lambda b, i: (0, 0)),
            pl.BlockSpec((nb, S, LANES), lambda b, i: (b, 0, CB_SEL)),
            pl.BlockSpec((S, LANES), lambda b, i: (0, 0)),
            pl.BlockSpec((nb, S, LANES), lambda b, i: (b, 0, CB_WIN)),
            pl.BlockSpec((nb, tq, LANES), lambda b, i: (b, i, 0)),
            pl.BlockSpec((1, NSA_W), lambda b, i: (0, 0)),
        ],
        out_specs=pl.BlockSpec((nb, tq, NSA_W), lambda b, i: (b, i, 0)),
        out_shape=jax.ShapeDtypeStruct((B, S, NSA_W), BF16),
        scratch_shapes=[pltpu.VMEM((nb, S, LANES), BF16)],
        compiler_params=_cparams(("parallel", "arbitrary")),
        name="nsa_attention",
    )(main3, cmp, c2s, main3, neg_onehot, main3, small3, gain_row)


def _post_kernel(x_ref, yr_ref, yf_ref, yn_ref, wo_ref, g_ref, w1_ref, w2_ref, gf_ref, o_ref, *, ff_chunk, final):
    x = x_ref[...]
    x = x + jnp.dot(yr_ref[...], wo_ref[0:RET_W, :], preferred_element_type=F32)
    x = x + jnp.dot(yf_ref[...], wo_ref[RET_W:RET_W + FOX_W, :], preferred_element_type=F32)
    x = x + jnp.dot(yn_ref[...], wo_ref[RET_W + FOX_W:, :], preferred_element_type=F32)
    h = _rms(x, g_ref[...]).astype(BF16)
    o_ref[...] = x
    for c0 in range(0, D_FF, ff_chunk):
        hid = jnp.maximum(jnp.dot(h, w1_ref[:, c0:c0 + ff_chunk], preferred_element_type=F32), 0.0)
        o_ref[...] += jnp.dot((hid * hid).astype(BF16), w2_ref[c0:c0 + ff_chunk, :], preferred_element_type=F32)
    if final:
        o_ref[...] = _rms(o_ref[...], gf_ref[...])


def _post(x2d, yr, yf, yn, wo, g, w1, w2, gf, tm, final):
    T = x2d.shape[0]
    row = lambda w: pl.BlockSpec((tm, w), lambda i: (i, 0))
    return pl.pallas_call(
        functools.partial(_post_kernel, ff_chunk=512, final=final),
        grid=(T // tm,),
        in_specs=[
            row(D_MODEL), row(RET_W), row(FOX_W), row(NSA_W),
            _resident((D_MODEL, D_MODEL), lambda i: (0, 0)),
            _resident((1, D_MODEL), lambda i: (0, 0)),
            _resident((D_MODEL, D_FF), lambda i: (0, 0)),
            _resident((D_FF, D_MODEL), lambda i: (0, 0)),
            _resident((1, D_MODEL), lambda i: (0, 0)),
        ],
        out_specs=row(D_MODEL),
        out_shape=jax.ShapeDtypeStruct((T, D_MODEL), F32),
        compiler_params=_cparams(("parallel",)),
        name="outproj_mlp",
    )(x2d, yr, yf, yn, wo, g, w1, w2, gf)


def _tile_sizes(batch):
    return dict(tm=512, fox_tq=512, fox_tk=512, nsa_tq=256, nsa_tk=512, nsa_nb=2 if batch % 2 == 0 else 1)


def kernel(x, norm_attn, w_in, fox_forget_bias, ret_norm_gain, fox_norm_gain, nsa_norm_gain, nsa_cmp_pos_k, nsa_cmp_pos_v, nsa_cmp_w1_k, nsa_cmp_w2_k, nsa_cmp_w1_v, nsa_cmp_w2_v, w_out, norm_mlp, w_mlp_in, w_mlp_out, norm_final):
    B, S, D = x.shape
    depth = w_in.shape[0]
    assert D == D_MODEL and S % 512 == 0 and S >= WINDOW + 256
    T = B * S
    ts = _tile_sizes(B)

    cols, scale, small_cols, small_mask = _in_proj_columns()
    wm = (w_in[:, :, cols] * scale).astype(BF16)
    ws = (w_in[:, :, small_cols] * small_mask).astype(BF16)
    wo = w_out.astype(BF16)
    w1 = w_mlp_in.astype(BF16)
    w2 = w_mlp_out.astype(BF16)

    def expand_w1(wk, wv):
        L = wk.shape[0]
        wk = wk.reshape(L, CMP_LEN, HEAD_DIM, CMP_HIDDEN)
        wv = wv.reshape(L, CMP_LEN, HEAD_DIM, CMP_HIDDEN)
        z = jnp.zeros_like(wk)
        full = jnp.concatenate([jnp.concatenate([wk, z], axis=-1), jnp.concatenate([z, wv], axis=-1)], axis=2)
        full = full.reshape(L, CMP_LEN * LANES, 2 * CMP_HIDDEN).astype(BF16)
        return full[:, :CMP_STRIDE * LANES], full[:, CMP_STRIDE * LANES:]

    wt_all, wb_all = expand_w1(nsa_cmp_w1_k, nsa_cmp_w1_v)
    zk = jnp.zeros_like(nsa_cmp_w2_k)
    w2c = jnp.concatenate([jnp.concatenate([nsa_cmp_w2_k, zk], axis=-1),
                           jnp.concatenate([zk, nsa_cmp_w2_v], axis=-1)], axis=1).astype(BF16)
    pos = jnp.concatenate([nsa_cmp_pos_k, nsa_cmp_pos_v], axis=-1)
    pos_tb = pos.reshape(depth, 2, 1, CMP_STRIDE * LANES)
    pos_tb = jnp.broadcast_to(pos_tb, (depth, 2, 8, CMP_STRIDE * LANES)).reshape(depth, 16, CMP_STRIDE * LANES).astype(BF16)

    n_cmp = (S - CMP_LEN) // CMP_STRIDE + 1
    n_sel = S // SEL_LEN
    n_rows = S // CMP_STRIDE
    cs = np.arange(n_rows) * CMP_STRIDE
    ss = np.arange(n_sel) * SEL_LEN
    overlap = np.clip(np.minimum(cs[:, None] + CMP_LEN, ss[None, :] + SEL_LEN) - np.maximum(cs[:, None], ss[None, :]), 0, None)
    overlap[n_cmp:] = 0
    c2s_np = np.zeros((HEAD_DIM, n_rows), np.float32)
    c2s_np[:n_sel] = (overlap / CMP_LEN).T
    c2s = jnp.asarray(c2s_np, dtype=BF16)
    assert n_sel <= HEAD_DIM, "the selected-branch key augmentation has 64 lanes, one per selection block"
    neg_onehot = jnp.asarray(np.where((np.arange(S)[:, None] // SEL_LEN) == np.arange(LANES)[None, :] - HEAD_DIM, NEG, 0.0), dtype=BF16)

    tables = _retention_tables(S)
    fbias = jnp.zeros((depth, 1, LANES), F32).at[:, 0, SMALL_FF:SMALL_FF + N_FOX].set(fox_forget_bias)

    x2d = x.reshape(T, D)
    for l in range(depth):
        main, small = _inproj(x2d, norm_attn[l][None, :], wm[l], ws[l], ts["tm"])
        main3 = main.reshape(B, S, N_MAIN)
        small3 = small.reshape(B, S, LANES)
        fox_qa, fox_ka = _forget_cumsum(small3, fbias[l])
        y_ret = _retention(main3, tables, ret_norm_gain[l][None, :])
        y_fox = _fox(main3, fox_qa, fox_ka, fox_norm_gain[l][None, :], ts["fox_tq"], ts["fox_tk"])
        r = main3[:, :, CB_CMP * LANES:(CB_CMP + 1) * LANES].reshape(B, n_rows, CMP_STRIDE * LANES)
        cmp = _nsa_compress(r, pos_tb[l], wt_all[l], wb_all[l], w2c[l])
        y_nsa = _nsa(main3, cmp, c2s, neg_onehot, small3, nsa_norm_gain[l][None, :], ts["nsa_nb"], ts["nsa_tq"], ts["nsa_tk"], n_cmp)
        x2d = _post(x2d, y_ret.reshape(T, RET_W), y_fox.reshape(T, FOX_W), y_nsa.reshape(T, NSA_W), wo[l],
                    norm_mlp[l][None, :], w1[l], w2[l], norm_final[None, :], ts["tm"], final=(l == depth - 1))
    return x2d.reshape(B, S, D)
```

```python
import functools

import numpy as np
import jax
import jax.numpy as jnp
from jax import lax
from jax.experimental import pallas as pl
from jax.experimental.pallas import tpu as pltpu

F32 = jnp.float32
BF16 = jnp.bfloat16

D_MODEL = 1024
HEAD_DIM = 64
N_RET = 6
N_FOX = 6
N_NSA = 4
RET_W = N_RET * HEAD_DIM
FOX_W = N_FOX * HEAD_DIM
NSA_W = N_NSA * HEAD_DIM
D_FF = 4 * D_MODEL
RMS_EPS = 1e-6
RET_CHUNK = 128
RET_BLOCK = 256
ROPE_BASE = 10000.0
CMP_LEN = 32
CMP_STRIDE = 16
CMP_HIDDEN = 4 * HEAD_DIM
SEL_LEN = 64
TOP_N = 16
WINDOW = 512
FORCED_SCORE = 1e4
NEG = -1e30

LANES = 128
N_PAIR = N_RET // 2

CB_RQ, CB_RK, CB_RV, CB_RG = 0, 3, 6, 9
CB_NQ = 12
CB_FQ, CB_FK, CB_FV = 14, 17, 20
CB_CMP, CB_SEL, CB_WIN = 23, 24, 25
N_MAIN = 26 * LANES
SMALL_FF = 0
SMALL_GATE = 8

VMEM_LIMIT = 56 * 1024 * 1024


def _cparams(sem):
    return pltpu.CompilerParams(dimension_semantics=sem, vmem_limit_bytes=VMEM_LIMIT)


def _resident(shape, index_map):
    return pl.BlockSpec(shape, index_map, pipeline_mode=pl.Buffered(1))


def _in_proj_columns():
    sizes = (RET_W, RET_W, RET_W, RET_W, FOX_W, FOX_W, FOX_W, N_FOX, NSA_W) + (HEAD_DIM,) * 6 + (3 * N_NSA,)
    off = np.concatenate([[0], np.cumsum(sizes)])
    (o_rq, o_rk, o_rv, o_rg, o_fq, o_fk, o_fv, o_ff, o_nq, o_kc, o_vc, o_ks, o_vs, o_kw, o_vw, o_gt) = off[:-1]
    half = HEAD_DIM // 2
    inter = []
    for p in range(N_PAIR):
        a, b = 2 * p, 2 * p + 1
        for h, part in ((a, 0), (b, 0), (a, 1), (b, 1)):
            inter.extend(range(h * HEAD_DIM + part * half, h * HEAD_DIM + (part + 1) * half))
    inter = np.asarray(inter)
    nat = np.arange(RET_W)
    cols, scale = [], []

    def add(idx, s=1.0):
        cols.append(np.asarray(idx))
        scale.append(np.full(len(idx), s, np.float32))

    qk_scale = HEAD_DIM ** -0.5
    add(o_rq + inter)
    add(o_rk + inter, qk_scale)
    add(o_rv + nat)
    add(o_rg + nat)
    add(o_nq + np.arange(NSA_W), qk_scale)
    add(o_fq + nat, qk_scale)
    add(o_fk + nat)
    add(o_fv + nat)
    for o in (o_kc, o_vc, o_ks, o_vs, o_kw, o_vw):
        add(o + np.arange(HEAD_DIM))
    cols = np.concatenate(cols)
    scale = np.concatenate(scale)
    assert cols.shape[0] == N_MAIN
    small_cols = np.zeros(LANES, np.int64)
    small_mask = np.zeros(LANES, np.float32)
    small_cols[SMALL_FF:SMALL_FF + N_FOX] = o_ff + np.arange(N_FOX)
    small_mask[SMALL_FF:SMALL_FF + N_FOX] = 1.0
    small_cols[SMALL_GATE:SMALL_GATE + 3 * N_NSA] = o_gt + np.arange(3 * N_NSA)
    small_mask[SMALL_GATE:SMALL_GATE + 3 * N_NSA] = 1.0
    return cols, scale, small_cols, small_mask


def _retention_tables(seq):
    half = HEAD_DIM // 2
    inv = 1.0 / (ROPE_BASE ** (jnp.arange(half, dtype=F32) / half))
    ang = jnp.arange(seq, dtype=F32)[:, None] * inv[None, :]
    cos, sin = jnp.cos(ang), jnp.sin(ang)
    cos_t = jnp.concatenate([cos, cos, cos, cos], axis=-1)
    sin_t = jnp.concatenate([-sin, -sin, sin, sin], axis=-1)
    log_gamma = np.log(1.0 - 2.0 ** (-5.0 - np.arange(N_RET, dtype=np.float32))).astype(np.float32)
    C = RET_BLOCK
    idx = np.arange(C, dtype=np.float32)
    diff = idx[:, None] - idx[None, :]
    lane = np.arange(LANES)
    head_k = (lane % HEAD_DIM) // half
    head_v = lane // HEAD_DIM
    d_in = np.zeros((N_PAIR, 2, C, C), np.float32)
    d_k = np.zeros((N_PAIR, C, LANES), np.float32)
    d_q = np.zeros((N_PAIR, C, LANES), np.float32)
    d_c = np.zeros((N_PAIR, LANES, LANES), np.float32)
    for p in range(N_PAIR):
        lg = log_gamma[2 * p:2 * p + 2]
        for j in range(2):
            d_in[p, j] = np.where(diff >= 0, np.exp(lg[j] * np.maximum(diff, 0.0)), 0.0)
        d_k[p] = np.exp(lg[head_k][None, :] * (C - 1.0 - idx)[:, None])
        d_q[p] = np.exp(lg[head_v][None, :] * (idx + 1.0)[:, None])
        d_c[p] = np.broadcast_to(np.exp(lg[head_v] * C)[None, :], (LANES, LANES))
    s_mask = (head_k[:, None] == head_v[None, :]).astype(np.float32)
    return cos_t, sin_t, jnp.asarray(d_in), jnp.asarray(d_k), jnp.asarray(d_q), jnp.asarray(d_c), jnp.asarray(s_mask)


def _rms(x, g):
    return x * lax.rsqrt(jnp.mean(x * x, axis=-1, keepdims=True) + RMS_EPS) * g


def _inproj_kernel(x_ref, g_ref, wm_ref, ws_ref, om_ref, os_ref, *, n_chunk):
    h = _rms(x_ref[...], g_ref[...]).astype(BF16)
    for n0 in range(0, N_MAIN, n_chunk):
        n1 = min(n0 + n_chunk, N_MAIN)
        om_ref[:, n0:n1] = jnp.dot(h, wm_ref[:, n0:n1], preferred_element_type=F32).astype(BF16)
    os_ref[...] = jnp.dot(h, ws_ref[...], preferred_element_type=F32)


def _inproj(x2d, g, wm, ws, tm):
    T = x2d.shape[0]
    return pl.pallas_call(
        functools.partial(_inproj_kernel, n_chunk=512),
        grid=(T // tm,),
        in_specs=[
            pl.BlockSpec((tm, D_MODEL), lambda i: (i, 0)),
            _resident((1, D_MODEL), lambda i: (0, 0)),
            _resident((D_MODEL, N_MAIN), lambda i: (0, 0)),
            _resident((D_MODEL, LANES), lambda i: (0, 0)),
        ],
        out_specs=[
            pl.BlockSpec((tm, N_MAIN), lambda i: (i, 0)),
            pl.BlockSpec((tm, LANES), lambda i: (i, 0)),
        ],
        out_shape=[jax.ShapeDtypeStruct((T, N_MAIN), BF16), jax.ShapeDtypeStruct((T, LANES), F32)],
        compiler_params=_cparams(("parallel",)),
        name="inproj",
    )(x2d, g, wm, ws)


N_SPLIT = 3
AUG_STRIDE = 2 * N_SPLIT


def _split_bf16(x):
    terms, rest = [], x
    for _ in range(N_SPLIT):
        t = rest.astype(BF16)
        rest = rest - t.astype(F32)
        terms.append(t)
    return terms


def _bias_placement():
    mq = np.zeros((N_SPLIT, LANES, LANES), np.float32)
    mk = np.zeros((N_SPLIT, LANES, LANES), np.float32)
    one_q = np.zeros((1, LANES), np.float32)
    one_k = np.zeros((1, LANES), np.float32)
    for h in range(N_FOX):
        base = h * AUG_STRIDE
        for i in range(N_SPLIT):
            mq[i, SMALL_FF + h, base + i] = 1.0
            mk[i, SMALL_FF + h, base + N_SPLIT + i] = -1.0
            one_q[0, base + N_SPLIT + i] = 1.0
            one_k[0, base + i] = 1.0
    return jnp.asarray(mq, BF16), jnp.asarray(mk, BF16), jnp.asarray(one_q), jnp.asarray(one_k)


def _cum_kernel(z_ref, b_ref, tri_ref, mq_ref, mk_ref, oq_ref, ok_ref, qa_ref, ka_ref, *, n_chunks):
    C = RET_CHUNK

    def body(c, carry):
        r0 = pl.multiple_of(c * C, C)
        z = z_ref[0, pl.ds(r0, C), :] + b_ref[...]
        lf = jnp.minimum(z, 0.0) - jnp.log1p(jnp.exp(-jnp.abs(z)))
        cs = carry
        for term in _split_bf16(lf):
            cs = cs + jnp.dot(tri_ref[...], term, preferred_element_type=F32)
        qa, ka = oq_ref[...], ok_ref[...]
        for i, term in enumerate(_split_bf16(cs)):
            qa = qa + jnp.dot(term, mq_ref[i], preferred_element_type=F32)
            ka = ka + jnp.dot(term, mk_ref[i], preferred_element_type=F32)
        qa_ref[0, pl.ds(r0, C), :] = qa.astype(BF16)
        ka_ref[0, pl.ds(r0, C), :] = ka.astype(BF16)
        return cs[C - 1:C, :]

    lax.fori_loop(0, n_chunks, body, jnp.zeros((1, LANES), F32))


def _forget_cumsum(small3, bias_row):
    B, S, _ = small3.shape
    tri = jnp.asarray(np.tril(np.ones((RET_CHUNK, RET_CHUNK), np.float32)), BF16)
    mq, mk, one_q, one_k = _bias_placement()
    const = lambda shape: pl.BlockSpec(shape, lambda b: (0,) * len(shape))
    return pl.pallas_call(
        functools.partial(_cum_kernel, n_chunks=S // RET_CHUNK),
        grid=(B,),
        in_specs=[
            pl.BlockSpec((1, S, LANES), lambda b: (b, 0, 0)),
            const((1, LANES)), const((RET_CHUNK, RET_CHUNK)),
            const((N_SPLIT, LANES, LANES)), const((N_SPLIT, LANES, LANES)), const((1, LANES)), const((1, LANES)),
        ],
        out_specs=[
            pl.BlockSpec((1, S, LANES), lambda b: (b, 0, 0)),
            pl.BlockSpec((1, S, LANES), lambda b: (b, 0, 0)),
        ],
        out_shape=[jax.ShapeDtypeStruct((B, S, LANES), BF16), jax.ShapeDtypeStruct((B, S, LANES), BF16)],
        compiler_params=_cparams(("parallel",)),
        name="forget_cumsum",
    )(small3, bias_row, tri, mq, mk, one_q, one_k)


def _dot_nt(a, b):
    return lax.dot_general(a, b, (((1,), (1,)), ((), ())), preferred_element_type=F32)


def _ret_kernel(q_ref, k_ref, v_ref, g_ref, cos_ref, sin_ref, din_ref, dk_ref, dq_ref, dc_ref, sm_ref, gain_ref,
                o_ref, state_ref, *, n_chunks, unroll):
    C = RET_BLOCK
    half = HEAD_DIM // 2
    lane = lax.broadcasted_iota(jnp.int32, (C, LANES), 1)
    khead = (lane % HEAD_DIM) // half
    vhead0 = lane < HEAD_DIM
    state_ref[...] = jnp.zeros_like(state_ref)

    def pair_chunk(p, r0):
        rows = pl.ds(r0, C)
        lanes = slice(p * LANES, (p + 1) * LANES)
        cs, sn = cos_ref[rows, :], sin_ref[rows, :]
        q = q_ref[0, rows, lanes].astype(F32)
        k = k_ref[0, rows, lanes].astype(F32)
        q = q * cs + pltpu.roll(q, HEAD_DIM, 1) * sn
        k = k * cs + pltpu.roll(k, HEAD_DIM, 1) * sn
        v = v_ref[0, rows, lanes]
        qb, kb = q.astype(BF16), k.astype(BF16)
        state = state_ref[p]
        cross = jnp.dot(qb, state.astype(BF16), preferred_element_type=F32) * dq_ref[p]
        outs = []
        for j in range(2):
            qm = jnp.where(khead == j, qb, jnp.zeros_like(qb))
            inner = _dot_nt(qm, kb) * din_ref[p, j]
            outs.append(jnp.dot(inner.astype(BF16), v, preferred_element_type=F32))
        out = jnp.where(vhead0, outs[0], outs[1]) + cross
        kd = (k * dk_ref[p]).T.astype(BF16)
        state_ref[p] = state * dc_ref[p] + jnp.dot(kd, v, preferred_element_type=F32) * sm_ref[...]
        inv = 1.0 / HEAD_DIM
        s0 = jnp.sum(jnp.where(vhead0, out, 0.0), axis=-1, keepdims=True)
        s1 = jnp.sum(jnp.where(vhead0, 0.0, out), axis=-1, keepdims=True)
        yc = out - jnp.where(vhead0, s0, s1) * inv
        yc2 = yc * yc
        v0 = jnp.sum(jnp.where(vhead0, yc2, 0.0), axis=-1, keepdims=True)
        v1 = jnp.sum(jnp.where(vhead0, 0.0, yc2), axis=-1, keepdims=True)
        y = yc * lax.rsqrt(jnp.where(vhead0, v0, v1) * inv + RMS_EPS) * gain_ref[:, lanes]
        g = g_ref[0, rows, lanes].astype(F32)
        o_ref[0, rows, lanes] = (y * (g * (1.0 / (1.0 + jnp.exp(-g))))).astype(BF16)

    def body(c, _):
        for u in range(unroll):
            for p in range(N_PAIR):
                pair_chunk(p, pl.multiple_of((c * unroll + u) * C, C))
        return 0

    lax.fori_loop(0, n_chunks // unroll, body, 0)


def _retention(main3, tables, gain_row):
    B, S, _ = main3.shape
    cos_t, sin_t, d_in, d_k, d_q, d_c, s_mask = tables
    C = RET_BLOCK
    n_chunks = S // C
    seq_spec = lambda cb: pl.BlockSpec((1, S, RET_W), lambda b: (b, 0, cb // N_PAIR))
    const = lambda shape: pl.BlockSpec(shape, lambda b: (0,) * len(shape))
    return pl.pallas_call(
        functools.partial(_ret_kernel, n_chunks=n_chunks, unroll=2 if n_chunks % 2 == 0 else 1),
        grid=(B,),
        in_specs=[
            seq_spec(CB_RQ), seq_spec(CB_RK), seq_spec(CB_RV), seq_spec(CB_RG),
            const((S, LANES)), const((S, LANES)),
            const((N_PAIR, 2, C, C)), const((N_PAIR, C, LANES)), const((N_PAIR, C, LANES)),
            const((N_PAIR, LANES, LANES)), const((LANES, LANES)), const((1, RET_W)),
        ],
        out_specs=pl.BlockSpec((1, S, RET_W), lambda b: (b, 0, 0)),
        out_shape=jax.ShapeDtypeStruct((B, S, RET_W), BF16),
        scratch_shapes=[pltpu.VMEM((N_PAIR, LANES, LANES), F32)],
        compiler_params=_cparams(("parallel",)),
        name="retention",
    )(main3, main3, main3, main3, cos_t, sin_t, d_in, d_k, d_q, d_c, s_mask, gain_row)


def _fox_kernel(q_ref, qa_ref, kh_ref, ka_ref, v_ref, gain_ref, havg_ref, o_ref, k_ref, *, tq, tk):
    p = pl.program_id(1)
    qi = pl.program_id(2)

    @pl.when(qi == 0)
    def _():
        k_ref[0, :, 0:LANES] = kh_ref[0]
        k_ref[0, :, LANES:2 * LANES] = ka_ref[0]

    q = q_ref[0]
    qa = qa_ref[0]
    lane_q = lax.broadcasted_iota(jnp.int32, (tq, LANES), 1)
    head0 = lane_q < HEAD_DIM
    qops = []
    for j in range(2):
        qm = jnp.where(head0 if j == 0 else jnp.logical_not(head0), q, jnp.zeros_like(q))
        lo = (2 * p + j) * AUG_STRIDE
        own = jnp.logical_and(lane_q >= lo, lane_q < lo + AUG_STRIDE)
        qops.append(jnp.concatenate([qm, jnp.where(own, qa, jnp.zeros_like(qa))], axis=1))
    q_pos = qi * tq + lax.broadcasted_iota(jnp.int32, (tq, tk), 0)
    k_off = lax.broadcasted_iota(jnp.int32, (tq, tk), 1)
    head0_k = lax.broadcasted_iota(jnp.int32, (tk, LANES), 1) < HEAD_DIM
    one = jnp.ones((tk, LANES), BF16)

    def tile(t, carry, masked):
        c0 = pl.multiple_of(t * tk, tk)
        k = k_ref[0, pl.ds(c0, tk), :]
        v = v_ref[0, pl.ds(c0, tk), :]
        new = []
        for j in range(2):
            m, acc = carry[2 * j:2 * j + 2]
            s = _dot_nt(qops[j], k)
            if masked:
                s = jnp.where(k_off + c0 <= q_pos, s, NEG)
            m_new = jnp.maximum(m, jnp.max(s, axis=-1, keepdims=True))
            pr = jnp.exp(s - m_new)
            vj = jnp.where(head0_k if j == 0 else jnp.logical_not(head0_k), v, one)
            acc = jnp.exp(m - m_new) * acc + jnp.dot(pr.astype(BF16), vj, preferred_element_type=F32)
            new += [m_new, acc]
        return tuple(new)

    init = []
    for _ in range(2):
        init += [jnp.full((tq, 1), NEG, F32), jnp.zeros((tq, LANES), F32)]
    n_full = qi
    carry = lax.fori_loop(0, n_full, lambda t, c: tile(t, c, False), tuple(init))
    carry = tile(n_full, carry, True)
    o0 = carry[1] * (1.0 / pltpu.roll(carry[1], HEAD_DIM, 1))
    o1 = carry[3] * (1.0 / pltpu.roll(carry[3], HEAD_DIM, 1))
    out = jnp.where(head0, o0, o1)
    sq = out * out
    sq_hi = sq.astype(BF16)
    sq_lo = (sq - sq_hi.astype(F32)).astype(BF16)
    ms = jnp.dot(sq_hi, havg_ref[...], preferred_element_type=F32) + jnp.dot(sq_lo, havg_ref[...], preferred_element_type=F32)
    o_ref[0] = (out * lax.rsqrt(ms + RMS_EPS) * gain_ref[...]).astype(BF16)


def _fox(main3, q_aug, k_aug, gain_row, tq, tk):
    B, S, _ = main3.shape
    assert tq == tk, "the kernel handles exactly one diagonal tile per query block"
    head_of = np.arange(LANES) // HEAD_DIM
    havg = (head_of[:, None] == head_of[None, :]).astype(np.float32) / HEAD_DIM
    return pl.pallas_call(
        functools.partial(_fox_kernel, tq=tq, tk=tk),
        grid=(B, N_PAIR, S // tq),
        in_specs=[
            pl.BlockSpec((1, tq, LANES), lambda b, p, i: (b, i, CB_FQ + p)),
            pl.BlockSpec((1, tq, LANES), lambda b, p, i: (b, i, 0)),
            pl.BlockSpec((1, S, LANES), lambda b, p, i: (b, 0, CB_FK + p)),
            pl.BlockSpec((1, S, LANES), lambda b, p, i: (b, 0, 0)),
            pl.BlockSpec((1, S, LANES), lambda b, p, i: (b, 0, CB_FV + p)),
            pl.BlockSpec((1, LANES), lambda b, p, i: (0, p)),
            pl.BlockSpec((LANES, LANES), lambda b, p, i: (0, 0)),
        ],
        out_specs=pl.BlockSpec((1, tq, LANES), lambda b, p, i: (b, i, p)),
        out_shape=jax.ShapeDtypeStruct((B, S, FOX_W), BF16),
        scratch_shapes=[pltpu.VMEM((1, S, 2 * LANES), BF16)],
        compiler_params=_cparams(("parallel", "parallel", "arbitrary")),
        name="fox_attention",
    )(main3, q_aug, main3, k_aug, main3, gain_row, jnp.asarray(havg, BF16))


def _gelu_tanh(x):
    return 0.5 * x * (1.0 + jnp.tanh(np.sqrt(2.0 / np.pi).astype(np.float32) * (x + 0.044715 * (x * x * x))))


def _cmp_kernel(r_ref, pos_ref, wt_ref, wb_ref, w2_ref, o_ref):
    r = r_ref[0]
    n = r.shape[0]
    top = jnp.dot(r, wt_ref[...], preferred_element_type=F32)
    bot = jnp.dot(r, wb_ref[...], preferred_element_type=F32)
    cpos = jnp.dot(pos_ref[0:8, :], wt_ref[...], preferred_element_type=F32)
    cpos = cpos + jnp.dot(pos_ref[8:16, :], wb_ref[...], preferred_element_type=F32)
    hid = _gelu_tanh(top + pltpu.roll(bot, n - 1, 0) + cpos[0:1, :])
    o_ref[0] = jnp.dot(hid.astype(BF16), w2_ref[...], preferred_element_type=F32)


def _nsa_compress(r, pos2, wt, wb, w2):
    B, n, _ = r.shape
    width = CMP_STRIDE * LANES
    return pl.pallas_call(
        _cmp_kernel,
        grid=(B,),
        in_specs=[
            pl.BlockSpec((1, n, width), lambda b: (b, 0, 0)),
            pl.BlockSpec((16, width), lambda b: (0, 0)),
            pl.BlockSpec((width, 2 * CMP_HIDDEN), lambda b: (0, 0)),
            pl.BlockSpec((width, 2 * CMP_HIDDEN), lambda b: (0, 0)),
            pl.BlockSpec((2 * CMP_HIDDEN, LANES), lambda b: (0, 0)),
        ],
        out_specs=pl.BlockSpec((1, n, LANES), lambda b: (b, 0, 0)),
        out_shape=jax.ShapeDtypeStruct((B, n, LANES), F32),
        compiler_params=_cparams(("parallel",)),
        name="nsa_compress",
    )(r, pos2, wt, wb, w2)


def _softmax_rows(s, mask):
    sm = jnp.where(mask, s, NEG)
    m = jnp.max(sm, axis=-1, keepdims=True)
    e = jnp.where(mask, jnp.exp(sm - m), 0.0)
    l = jnp.sum(e, axis=-1, keepdims=True)
    return e / jnp.where(l > 0.0, l, 1.0)


def _nsa_kernel(q_ref, cmp_ref, c2s_ref, sel_ref, noh_ref, win_ref, gl_ref, gain_ref, gsel_ref, havg_ref, o_ref, ksa_ref, *, nb, tq, tk, n_cmp,
                n_sel, top_n):
    H = N_NSA
    qi = pl.program_id(1)
    start = qi * tq

    @pl.when(qi == 0)
    def _():
        key_lanes = lax.broadcasted_iota(jnp.int32, noh_ref.shape, 1) < HEAD_DIM
        for bb in range(nb):
            ksa_ref[bb] = jnp.where(key_lanes, sel_ref[bb], noh_ref[...])

    t_row = start + lax.broadcasted_iota(jnp.int32, (tq, 1), 0)
    t_all = jnp.concatenate([t_row] * H, axis=0)
    zeros64 = jnp.zeros((tq, HEAD_DIM), BF16)
    wlen = WINDOW + tq
    base = pl.multiple_of(jnp.maximum(start - WINDOW, 0), tq)
    n_pad = cmp_ref.shape[1]

    def prologue(bb):
        q = q_ref[bb]
        q_heads = [q[:, h * HEAD_DIM:(h + 1) * HEAD_DIM] for h in range(H)]
        q_pad = jnp.concatenate([jnp.concatenate([qh, zeros64], axis=1) for qh in q_heads], axis=0)

        win = win_ref[bb, pl.ds(base, wlen), :]
        s_w = _dot_nt(q_pad, win)
        kpos = base + lax.broadcasted_iota(jnp.int32, (H * tq, wlen), 1)
        vis_w = jnp.logical_and(kpos <= t_all, kpos > t_all - WINDOW)
        s_w = jnp.where(vis_w, s_w, NEG)
        e_w = jnp.exp(s_w - jnp.max(s_w, axis=-1, keepdims=True))
        win1 = jnp.where(lax.broadcasted_iota(jnp.int32, (wlen, LANES), 1) < HEAD_DIM, jnp.ones((wlen, LANES), BF16), win)
        o_w = jnp.dot(e_w.astype(BF16), win1, preferred_element_type=F32)

        cmp = cmp_ref[bb]
        cmp_hi = cmp.astype(BF16)
        cmp_lo = (cmp - cmp_hi.astype(F32)).astype(BF16)
        s_c = _dot_nt(q_pad, cmp_hi) + _dot_nt(q_pad, cmp_lo)
        n_id = lax.broadcasted_iota(jnp.int32, (H * tq, n_pad), 1)
        vis_c = jnp.logical_and(n_id * CMP_STRIDE + (CMP_LEN - 1) <= t_all, n_id < n_cmp)
        p_c = _softmax_rows(s_c, vis_c)
        o_c = jnp.dot(p_c.astype(BF16), cmp_hi, preferred_element_type=F32)

        p_sum = p_c[0:tq]
        for h in range(1, H):
            p_sum = p_sum + p_c[h * tq:(h + 1) * tq]
        p_hi = p_sum.astype(BF16)
        p_lo = (p_sum - p_hi.astype(F32)).astype(BF16)
        c2s_t = c2s_ref[...]
        imp = _dot_nt(c2s_t, p_hi) + _dot_nt(c2s_t, p_lo)
        t_lane = start + lax.broadcasted_iota(jnp.int32, (HEAD_DIM, tq), 1)
        s_id = lax.broadcasted_iota(jnp.int32, (HEAD_DIM, tq), 0)
        cur = t_lane >> 6
        forced = jnp.logical_or(s_id == 0, jnp.logical_or(s_id == cur, s_id == cur - 1))
        score = jnp.where(forced, FORCED_SCORE, imp)
        score = jnp.where(s_id * SEL_LEN <= t_lane, score, -1.0)
        SUB = 8
        groups = [score[g * SUB:(g + 1) * SUB, :] for g in range(HEAD_DIM // SUB)]
        ranks = [jnp.zeros((SUB, tq), jnp.int32) for _ in groups]
        sub_id = lax.broadcasted_iota(jnp.int32, (SUB, tq), 0)
        for c in range(n_sel):
            row = score[c:c + 1, :]
            for g in range(len(groups)):
                if g * SUB > c:
                    before = row >= groups[g]
                elif g * SUB + SUB - 1 < c:
                    before = row > groups[g]
                else:
                    before = jnp.logical_or(row > groups[g], jnp.logical_and(row == groups[g], sub_id > c - g * SUB))
                ranks[g] = ranks[g] + before.astype(jnp.int32)
        not_sel = jnp.concatenate([jnp.where(r < top_n, 0.0, 1.0) for r in ranks], axis=0)
        ns = jnp.concatenate([jnp.zeros((HEAD_DIM, tq), F32), not_sel], axis=0).T.astype(BF16)
        q_aug = q_pad + jnp.concatenate([ns] * H, axis=0)
        return q_aug, o_c, o_w

    pro = [prologue(bb) for bb in range(nb)]

    def sel_tile(bb, c0, size, carry, diagonal):
        m, acc = carry
        s = _dot_nt(pro[bb][0], ksa_ref[bb, pl.ds(c0, size), :])
        if diagonal:
            r_id = lax.broadcasted_iota(jnp.int32, (H * tq, size), 0) & (tq - 1)
            c_id = lax.broadcasted_iota(jnp.int32, (H * tq, size), 1)
            s = jnp.where(c_id <= r_id, s, NEG)
        m_new = jnp.maximum(m, jnp.max(s, axis=-1, keepdims=True))
        pr = jnp.exp(s - m_new)
        key_lanes = lax.broadcasted_iota(jnp.int32, (size, LANES), 1) < HEAD_DIM
        v1 = jnp.where(key_lanes, jnp.ones((size, LANES), BF16), sel_ref[bb, pl.ds(c0, size), :])
        acc = jnp.exp(m - m_new) * acc + jnp.dot(pr.astype(BF16), v1, preferred_element_type=F32)
        return m_new, acc

    def all_rows(c0, size, carries, diagonal):
        return tuple(sel_tile(bb, c0, size, carries[bb], diagonal) for bb in range(nb))

    carries = tuple((jnp.full((H * tq, 1), NEG, F32), jnp.zeros((H * tq, LANES), F32)) for _ in range(nb))
    n_full = start // tk
    carries = lax.fori_loop(0, n_full, lambda t, c: all_rows(pl.multiple_of(t * tk, tk), tk, c, False), carries)
    n_tail = (start - n_full * tk) // tq
    for d in range(tk // tq - 1):
        c0 = pl.multiple_of(n_full * tk + d * tq, tq)
        carries = lax.cond(d < n_tail, lambda c, c0=c0: all_rows(c0, tq, c, False), lambda c: c, carries)
    carries = all_rows(pl.multiple_of(start, tq), tq, carries, True)

    first_half = lax.broadcasted_iota(jnp.int32, (tq, LANES), 1) < HEAD_DIM

    def place(acc, normalise):
        cols = []
        for h in range(0, H, 2):
            a0, a1 = acc[h * tq:(h + 1) * tq], acc[(h + 1) * tq:(h + 2) * tq]
            r0 = pltpu.roll(a0, HEAD_DIM, 1)
            if normalise:
                v0 = r0 * (1.0 / a0)
                v1 = a1 * (1.0 / pltpu.roll(a1, HEAD_DIM, 1))
            else:
                v0, v1 = r0, a1
            cols.append(jnp.where(first_half, v0, v1))
        return jnp.concatenate(cols, axis=1)

    for bb in range(nb):
        _, o_c, o_w = pro[bb]
        gates = 1.0 / (1.0 + jnp.exp(-gl_ref[bb]))
        g_hi = gates.astype(BF16)
        g_lo = (gates - g_hi.astype(F32)).astype(BF16)
        mix = None
        for c, y in enumerate((place(o_c, False), place(carries[bb][1], True), place(o_w, True))):
            g = jnp.dot(g_hi, gsel_ref[c], preferred_element_type=F32) + jnp.dot(g_lo, gsel_ref[c], preferred_element_type=F32)
            mix = g * y if mix is None else mix + g * y
        sq = mix * mix
        sq_hi = sq.astype(BF16)
        sq_lo = (sq - sq_hi.astype(F32)).astype(BF16)
        ms = jnp.dot(sq_hi, havg_ref[...], preferred_element_type=F32) + jnp.dot(sq_lo, havg_ref[...], preferred_element_type=F32)
        o_ref[bb] = (mix * lax.rsqrt(ms + RMS_EPS) * gain_ref[...]).astype(BF16)


def _nsa(main3, cmp, c2s, neg_onehot, small3, gain_row, nb, tq, tk, n_cmp):
    B, S, _ = main3.shape
    n_sel = S // SEL_LEN
    n_pad = cmp.shape[1]
    gsel = np.zeros((3, LANES, NSA_W), np.float32)
    for h in range(N_NSA):
        for c in range(3):
            gsel[c, SMALL_GATE + 3 * h + c, h * HEAD_DIM:(h + 1) * HEAD_DIM] = 1.0
    head_of = np.arange(NSA_W) // HEAD_DIM
    havg = (head_of[:, None] == head_of[None, :]).astype(np.float32) / HEAD_DIM
    return pl.pallas_call(
        functools.partial(_nsa_kernel, nb=nb, tq=tq, tk=tk, n_cmp=n_cmp, n_sel=n_sel, top_n=min(TOP_N, n_sel)),
        grid=(B // nb, S // tq),
        in_specs=[
            pl.BlockSpec((nb, tq, NSA_W), lambda b, i: (b, i, CB_NQ // 2)),
            pl.BlockSpec((nb, n_pad, LANES), lambda b, i: (b, 0, 0)),
            pl.BlockSpec((HEAD_DIM, n_pad), lambda b, i: (0, 0)),
            pl.BlockSpec((nb, S, LANES), lambda b, i: (b, 0, CB_SEL)),
            pl.BlockSpec((S, LANES), lambda b, i: (0, 0)),
            pl.BlockSpec((nb, S, LANES), lambda b, i: (b, 0, CB_WIN)),
            pl.BlockSpec((nb, tq, LANES), lambda b, i: (b, i, 0)),
            pl.BlockSpec((1, NSA_W), lambda b, i: (0, 0)),
            pl.BlockSpec((3, LANES, NSA_W), lambda b, i: (0, 0, 0)),
            pl.BlockSpec((NSA_W, NSA_W), lambda b, i: (0, 0)),
        ],
        out_specs=pl.BlockSpec((nb, tq, NSA_W), lambda b, i: (b, i, 0)),
        out_shape=jax.ShapeDtypeStruct((B, S, NSA_W), BF16),
        scratch_shapes=[pltpu.VMEM((nb, S, LANES), BF16)],
        compiler_params=_cparams(("parallel", "arbitrary")),
        name="nsa_attention",
    )(main3, cmp, c2s, main3, neg_onehot, main3, small3, gain_row, jnp.asarray(gsel, BF16), jnp.asarray(havg, BF16))


def _post_kernel(x_ref, yr_ref, yf_ref, yn_ref, wo_ref, g_ref, w1_ref, w2_ref, gf_ref, o_ref, *, ff_chunk, final):
    x = x_ref[...]
    x = x + jnp.dot(yr_ref[...], wo_ref[0:RET_W, :], preferred_element_type=F32)
    x = x + jnp.dot(yf_ref[...], wo_ref[RET_W:RET_W + FOX_W, :], preferred_element_type=F32)
    x = x + jnp.dot(yn_ref[...], wo_ref[RET_W + FOX_W:, :], preferred_element_type=F32)
    h = _rms(x, g_ref[...]).astype(BF16)
    o_ref[...] = x
    for c0 in range(0, D_FF, ff_chunk):
        hid = jnp.maximum(jnp.dot(h, w1_ref[:, c0:c0 + ff_chunk], preferred_element_type=F32), 0.0)
        o_ref[...] += jnp.dot((hid * hid).astype(BF16), w2_ref[c0:c0 + ff_chunk, :], preferred_element_type=F32)
    if final:
        o_ref[...] = _rms(o_ref[...], gf_ref[...])


def _post(x2d, yr, yf, yn, wo, g, w1, w2, gf, tm, final):
    T = x2d.shape[0]
    row = lambda w: pl.BlockSpec((tm, w), lambda i: (i, 0))
    return pl.pallas_call(
        functools.partial(_post_kernel, ff_chunk=512, final=final),
        grid=(T // tm,),
        in_specs=[
            row(D_MODEL), row(RET_W), row(FOX_W), row(NSA_W),
            _resident((D_MODEL, D_MODEL), lambda i: (0, 0)),
            _resident((1, D_MODEL), lambda i: (0, 0)),
            _resident((D_MODEL, D_FF), lambda i: (0, 0)),
            _resident((D_FF, D_MODEL), lambda i: (0, 0)),
            _resident((1, D_MODEL), lambda i: (0, 0)),
        ],
        out_specs=row(D_MODEL),
        out_shape=jax.ShapeDtypeStruct((T, D_MODEL), F32),
        compiler_params=_cparams(("parallel",)),
        name="outproj_mlp",
    )(x2d, yr, yf, yn, wo, g, w1, w2, gf)


def _tile_sizes(batch):
    return dict(tm=512, fox_tq=512, fox_tk=512, nsa_tq=256, nsa_tk=512, nsa_nb=2 if batch % 2 == 0 else 1)


def kernel(x, norm_attn, w_in, fox_forget_bias, ret_norm_gain, fox_norm_gain, nsa_norm_gain, nsa_cmp_pos_k, nsa_cmp_pos_v, nsa_cmp_w1_k, nsa_cmp_w2_k, nsa_cmp_w1_v, nsa_cmp_w2_v, w_out, norm_mlp, w_mlp_in, w_mlp_out, norm_final):
    B, S, D = x.shape
    depth = w_in.shape[0]
    assert D == D_MODEL and S % 512 == 0 and S >= WINDOW + 256
    T = B * S
    ts = _tile_sizes(B)

    cols, scale, small_cols, small_mask = _in_proj_columns()
    wm = (w_in[:, :, cols] * scale).astype(BF16)
    ws = (w_in[:, :, small_cols] * small_mask).astype(BF16)
    wo = w_out.astype(BF16)
    w1 = w_mlp_in.astype(BF16)
    w2 = w_mlp_out.astype(BF16)

    def expand_w1(wk, wv):
        L = wk.shape[0]
        wk = wk.reshape(L, CMP_LEN, HEAD_DIM, CMP_HIDDEN)
        wv = wv.reshape(L, CMP_LEN, HEAD_DIM, CMP_HIDDEN)
        z = jnp.zeros_like(wk)
        full = jnp.concatenate([jnp.concatenate([wk, z], axis=-1), jnp.concatenate([z, wv], axis=-1)], axis=2)
        full = full.reshape(L, CMP_LEN * LANES, 2 * CMP_HIDDEN).astype(BF16)
        return full[:, :CMP_STRIDE * LANES], full[:, CMP_STRIDE * LANES:]

    wt_all, wb_all = expand_w1(nsa_cmp_w1_k, nsa_cmp_w1_v)
    zk = jnp.zeros_like(nsa_cmp_w2_k)
    w2c = jnp.concatenate([jnp.concatenate([nsa_cmp_w2_k, zk], axis=-1),
                           jnp.concatenate([zk, nsa_cmp_w2_v], axis=-1)], axis=1).astype(BF16)
    pos = jnp.concatenate([nsa_cmp_pos_k, nsa_cmp_pos_v], axis=-1)
    pos_tb = pos.reshape(depth, 2, 1, CMP_STRIDE * LANES)
    pos_tb = jnp.broadcast_to(pos_tb, (depth, 2, 8, CMP_STRIDE * LANES)).reshape(depth, 16, CMP_STRIDE * LANES).astype(BF16)

    n_cmp = (S - CMP_LEN) // CMP_STRIDE + 1
    n_sel = S // SEL_LEN
    n_rows = S // CMP_STRIDE
    cs = np.arange(n_rows) * CMP_STRIDE
    ss = np.arange(n_sel) * SEL_LEN
    overlap = np.clip(np.minimum(cs[:, None] + CMP_LEN, ss[None, :] + SEL_LEN) - np.maximum(cs[:, None], ss[None, :]), 0, None)
    overlap[n_cmp:] = 0
    c2s_np = np.zeros((HEAD_DIM, n_rows), np.float32)
    c2s_np[:n_sel] = (overlap / CMP_LEN).T
    c2s = jnp.asarray(c2s_np, dtype=BF16)
    assert n_sel <= HEAD_DIM, "the selected-branch key augmentation has 64 lanes, one per selection block"
    neg_onehot = jnp.asarray(np.where((np.arange(S)[:, None] // SEL_LEN) == np.arange(LANES)[None, :] - HEAD_DIM, NEG, 0.0), dtype=BF16)

    tables = _retention_tables(S)
    fbias = jnp.zeros((depth, 1, LANES), F32).at[:, 0, SMALL_FF:SMALL_FF + N_FOX].set(fox_forget_bias)

    x2d = x.reshape(T, D)
    for l in range(depth):
        main, small = _inproj(x2d, norm_attn[l][None, :], wm[l], ws[l], ts["tm"])
        main3 = main.reshape(B, S, N_MAIN)
        small3 = small.reshape(B, S, LANES)
        fox_qa, fox_ka = _forget_cumsum(small3, fbias[l])
        y_ret = _retention(main3, tables, ret_norm_gain[l][None, :])
        y_fox = _fox(main3, fox_qa, fox_ka, fox_norm_gain[l][None, :], ts["fox_tq"], ts["fox_tk"])
        r = main3[:, :, CB_CMP * LANES:(CB_CMP + 1) * LANES].reshape(B, n_rows, CMP_STRIDE * LANES)
        cmp = _nsa_compress(r, pos_tb[l], wt_all[l], wb_all[l], w2c[l])
        y_nsa = _nsa(main3, cmp, c2s, neg_onehot, small3, nsa_norm_gain[l][None, :], ts["nsa_nb"], ts["nsa_tq"], ts["nsa_tk"], n_cmp)
        x2d = _post(x2d, y_ret.reshape(T, RET_W), y_fox.reshape(T, FOX_W), y_nsa.reshape(T, NSA_W), wo[l],
                    norm_mlp[l][None, :], w1[l], w2[l], norm_final[None, :], ts["tm"], final=(l == depth - 1))
    return x2d.reshape(B, S, D)
```

```python
import functools

import numpy as np
import jax
import jax.numpy as jnp
from jax import lax
from jax.experimental import pallas as pl
from jax.experimental.pallas import tpu as pltpu

F32 = jnp.float32
BF16 = jnp.bfloat16

D_MODEL = 1024
HEAD_DIM = 64
N_RET = 6
N_FOX = 6
N_NSA = 4
RET_W = N_RET * HEAD_DIM
FOX_W = N_FOX * HEAD_DIM
NSA_W = N_NSA * HEAD_DIM
D_FF = 4 * D_MODEL
RMS_EPS = 1e-6
RET_CHUNK = 128
RET_BLOCK = 256
ROPE_BASE = 10000.0
CMP_LEN = 32
CMP_STRIDE = 16
CMP_HIDDEN = 4 * HEAD_DIM
SEL_LEN = 64
TOP_N = 16
WINDOW = 512
FORCED_SCORE = 1e4
NEG = -1e30
LOG2E = float(np.log2(np.e))

LANES = 128
N_PAIR = N_RET // 2

CB_RQ, CB_RK, CB_RV, CB_RG = 0, 3, 6, 9
CB_NQ = 12
CB_FQ, CB_FK, CB_FV = 14, 17, 20
CB_CMP, CB_SEL, CB_WIN = 23, 24, 25
N_MAIN = 26 * LANES
SMALL_FF = 0
SMALL_GATE = 8

VMEM_LIMIT = 56 * 1024 * 1024


def _cparams(sem):
    return pltpu.CompilerParams(dimension_semantics=sem, vmem_limit_bytes=VMEM_LIMIT)


def _resident(shape, index_map):
    return pl.BlockSpec(shape, index_map, pipeline_mode=pl.Buffered(1))


def _in_proj_columns():
    sizes = (RET_W, RET_W, RET_W, RET_W, FOX_W, FOX_W, FOX_W, N_FOX, NSA_W) + (HEAD_DIM,) * 6 + (3 * N_NSA,)
    off = np.concatenate([[0], np.cumsum(sizes)])
    (o_rq, o_rk, o_rv, o_rg, o_fq, o_fk, o_fv, o_ff, o_nq, o_kc, o_vc, o_ks, o_vs, o_kw, o_vw, o_gt) = off[:-1]
    half = HEAD_DIM // 2
    inter = []
    for p in range(N_PAIR):
        a, b = 2 * p, 2 * p + 1
        for h, part in ((a, 0), (b, 0), (a, 1), (b, 1)):
            inter.extend(range(h * HEAD_DIM + part * half, h * HEAD_DIM + (part + 1) * half))
    inter = np.asarray(inter)
    nat = np.arange(RET_W)
    cols, scale = [], []

    def add(idx, s=1.0):
        cols.append(np.asarray(idx))
        scale.append(np.full(len(idx), s, np.float32))

    qk_scale = HEAD_DIM ** -0.5
    sm_scale = qk_scale * LOG2E
    add(o_rq + inter)
    add(o_rk + inter, qk_scale)
    add(o_rv + nat)
    add(o_rg + nat)
    add(o_nq + np.arange(NSA_W), sm_scale)
    add(o_fq + nat, sm_scale)
    add(o_fk + nat)
    add(o_fv + nat)
    for o in (o_kc, o_vc, o_ks, o_vs, o_kw, o_vw):
        add(o + np.arange(HEAD_DIM))
    cols = np.concatenate(cols)
    scale = np.concatenate(scale)
    assert cols.shape[0] == N_MAIN
    small_cols = np.zeros(LANES, np.int64)
    small_mask = np.zeros(LANES, np.float32)
    small_cols[SMALL_FF:SMALL_FF + N_FOX] = o_ff + np.arange(N_FOX)
    small_mask[SMALL_FF:SMALL_FF + N_FOX] = 1.0
    small_cols[SMALL_GATE:SMALL_GATE + 3 * N_NSA] = o_gt + np.arange(3 * N_NSA)
    small_mask[SMALL_GATE:SMALL_GATE + 3 * N_NSA] = 1.0
    return cols, scale, small_cols, small_mask


def _retention_tables(seq):
    half = HEAD_DIM // 2
    inv = 1.0 / (ROPE_BASE ** (jnp.arange(half, dtype=F32) / half))
    ang = jnp.arange(seq, dtype=F32)[:, None] * inv[None, :]
    cos, sin = jnp.cos(ang), jnp.sin(ang)
    cos_t = jnp.concatenate([cos, cos, cos, cos], axis=-1)
    sin_t = jnp.concatenate([-sin, -sin, sin, sin], axis=-1)
    log_gamma = np.log(1.0 - 2.0 ** (-5.0 - np.arange(N_RET, dtype=np.float32))).astype(np.float32)
    C = RET_BLOCK
    idx = np.arange(C, dtype=np.float32)
    diff = idx[:, None] - idx[None, :]
    lane = np.arange(LANES)
    head_k = (lane % HEAD_DIM) // half
    head_v = lane // HEAD_DIM
    d_in = np.zeros((N_PAIR, 2, C, C), np.float32)
    d_k = np.zeros((N_PAIR, C, LANES), np.float32)
    d_q = np.zeros((N_PAIR, C, LANES), np.float32)
    d_c = np.zeros((N_PAIR, LANES, LANES), np.float32)
    for p in range(N_PAIR):
        lg = log_gamma[2 * p:2 * p + 2]
        for j in range(2):
            d_in[p, j] = np.where(diff >= 0, np.exp(lg[j] * np.maximum(diff, 0.0)), 0.0)
        d_k[p] = np.exp(lg[head_k][None, :] * (C - 1.0 - idx)[:, None])
        d_q[p] = np.exp(lg[head_v][None, :] * (idx + 1.0)[:, None])
        d_c[p] = np.broadcast_to(np.exp(lg[head_v] * C)[None, :], (LANES, LANES))
    s_mask = (head_k[:, None] == head_v[None, :]).astype(np.float32)
    return cos_t, sin_t, jnp.asarray(d_in), jnp.asarray(d_k), jnp.asarray(d_q), jnp.asarray(d_c), jnp.asarray(s_mask)


def _rms(x, g):
    return x * lax.rsqrt(jnp.mean(x * x, axis=-1, keepdims=True) + RMS_EPS) * g


def _inproj_kernel(x_ref, g_ref, wm_ref, ws_ref, om_ref, os_ref, *, n_chunk):
    h = _rms(x_ref[...], g_ref[...]).astype(BF16)
    for n0 in range(0, N_MAIN, n_chunk):
        n1 = min(n0 + n_chunk, N_MAIN)
        om_ref[:, n0:n1] = jnp.dot(h, wm_ref[:, n0:n1], preferred_element_type=F32).astype(BF16)
    os_ref[...] = jnp.dot(h, ws_ref[...], preferred_element_type=F32)


def _inproj(x2d, g, wm, ws, tm):
    T = x2d.shape[0]
    return pl.pallas_call(
        functools.partial(_inproj_kernel, n_chunk=512),
        grid=(T // tm,),
        in_specs=[
            pl.BlockSpec((tm, D_MODEL), lambda i: (i, 0)),
            _resident((1, D_MODEL), lambda i: (0, 0)),
            _resident((D_MODEL, N_MAIN), lambda i: (0, 0)),
            _resident((D_MODEL, LANES), lambda i: (0, 0)),
        ],
        out_specs=[
            pl.BlockSpec((tm, N_MAIN), lambda i: (i, 0)),
            pl.BlockSpec((tm, LANES), lambda i: (i, 0)),
        ],
        out_shape=[jax.ShapeDtypeStruct((T, N_MAIN), BF16), jax.ShapeDtypeStruct((T, LANES), F32)],
        compiler_params=_cparams(("parallel",)),
        name="inproj",
    )(x2d, g, wm, ws)


N_SPLIT = 3
AUG_STRIDE = 2 * N_SPLIT


def _split_bf16(x):
    terms, rest = [], x
    for _ in range(N_SPLIT):
        t = rest.astype(BF16)
        rest = rest - t.astype(F32)
        terms.append(t)
    return terms


def _bias_placement():
    mq = np.zeros((N_SPLIT, LANES, LANES), np.float32)
    mk = np.zeros((N_SPLIT, LANES, LANES), np.float32)
    one_q = np.zeros((1, LANES), np.float32)
    one_k = np.zeros((1, LANES), np.float32)
    for h in range(N_FOX):
        base = h * AUG_STRIDE
        for i in range(N_SPLIT):
            mq[i, SMALL_FF + h, base + i] = 1.0
            mk[i, SMALL_FF + h, base + N_SPLIT + i] = -1.0
            one_q[0, base + N_SPLIT + i] = 1.0
            one_k[0, base + i] = 1.0
    return jnp.asarray(mq, BF16), jnp.asarray(mk, BF16), jnp.asarray(one_q), jnp.asarray(one_k)


def _cum_kernel(z_ref, b_ref, tri_ref, mq_ref, mk_ref, oq_ref, ok_ref, qa_ref, ka_ref, *, n_chunks):
    C = RET_CHUNK

    def body(c, carry):
        r0 = pl.multiple_of(c * C, C)
        z = z_ref[0, pl.ds(r0, C), :] + b_ref[...]
        lf = jnp.minimum(z, 0.0) - jnp.log1p(jnp.exp(-jnp.abs(z)))
        cs = carry
        for term in _split_bf16(lf):
            cs = cs + jnp.dot(tri_ref[...], term, preferred_element_type=F32)
        qa, ka = oq_ref[...], ok_ref[...]
        for i, term in enumerate(_split_bf16(cs * LOG2E)):
            qa = qa + jnp.dot(term, mq_ref[i], preferred_element_type=F32)
            ka = ka + jnp.dot(term, mk_ref[i], preferred_element_type=F32)
        qa_ref[0, pl.ds(r0, C), :] = qa.astype(BF16)
        ka_ref[0, pl.ds(r0, C), :] = ka.astype(BF16)
        return cs[C - 1:C, :]

    lax.fori_loop(0, n_chunks, body, jnp.zeros((1, LANES), F32))


def _forget_cumsum(small3, bias_row):
    B, S, _ = small3.shape
    tri = jnp.asarray(np.tril(np.ones((RET_CHUNK, RET_CHUNK), np.float32)), BF16)
    mq, mk, one_q, one_k = _bias_placement()
    const = lambda shape: pl.BlockSpec(shape, lambda b: (0,) * len(shape))
    return pl.pallas_call(
        functools.partial(_cum_kernel, n_chunks=S // RET_CHUNK),
        grid=(B,),
        in_specs=[
            pl.BlockSpec((1, S, LANES), lambda b: (b, 0, 0)),
            const((1, LANES)), const((RET_CHUNK, RET_CHUNK)),
            const((N_SPLIT, LANES, LANES)), const((N_SPLIT, LANES, LANES)), const((1, LANES)), const((1, LANES)),
        ],
        out_specs=[
            pl.BlockSpec((1, S, LANES), lambda b: (b, 0, 0)),
            pl.BlockSpec((1, S, LANES), lambda b: (b, 0, 0)),
        ],
        out_shape=[jax.ShapeDtypeStruct((B, S, LANES), BF16), jax.ShapeDtypeStruct((B, S, LANES), BF16)],
        compiler_params=_cparams(("parallel",)),
        name="forget_cumsum",
    )(small3, bias_row, tri, mq, mk, one_q, one_k)


def _dot_nt(a, b):
    return lax.dot_general(a, b, (((1,), (1,)), ((), ())), preferred_element_type=F32)


def _ret_kernel(q_ref, k_ref, v_ref, g_ref, cos_ref, sin_ref, din_ref, dk_ref, dq_ref, dc_ref, sm_ref, gain_ref,
                o_ref, state_ref, *, n_chunks, unroll):
    C = RET_BLOCK
    half = HEAD_DIM // 2
    lane = lax.broadcasted_iota(jnp.int32, (C, LANES), 1)
    khead = (lane % HEAD_DIM) // half
    vhead0 = lane < HEAD_DIM
    state_ref[...] = jnp.zeros_like(state_ref)

    def pair_chunk(p, r0):
        rows = pl.ds(r0, C)
        lanes = slice(p * LANES, (p + 1) * LANES)
        cs, sn = cos_ref[rows, :], sin_ref[rows, :]
        q = q_ref[0, rows, lanes].astype(F32)
        k = k_ref[0, rows, lanes].astype(F32)
        q = q * cs + pltpu.roll(q, HEAD_DIM, 1) * sn
        k = k * cs + pltpu.roll(k, HEAD_DIM, 1) * sn
        v = v_ref[0, rows, lanes]
        qb, kb = q.astype(BF16), k.astype(BF16)
        state = state_ref[p]
        cross = jnp.dot(qb, state.astype(BF16), preferred_element_type=F32) * dq_ref[p]
        outs = []
        for j in range(2):
            qm = jnp.where(khead == j, qb, jnp.zeros_like(qb))
            inner = _dot_nt(qm, kb) * din_ref[p, j]
            outs.append(jnp.dot(inner.astype(BF16), v, preferred_element_type=F32))
        out = jnp.where(vhead0, outs[0], outs[1]) + cross
        kd = (k * dk_ref[p]).T.astype(BF16)
        state_ref[p] = state * dc_ref[p] + jnp.dot(kd, v, preferred_element_type=F32) * sm_ref[...]
        inv = 1.0 / HEAD_DIM
        s0 = jnp.sum(jnp.where(vhead0, out, 0.0), axis=-1, keepdims=True)
        s1 = jnp.sum(jnp.where(vhead0, 0.0, out), axis=-1, keepdims=True)
        yc = out - jnp.where(vhead0, s0, s1) * inv
        yc2 = yc * yc
        v0 = jnp.sum(jnp.where(vhead0, yc2, 0.0), axis=-1, keepdims=True)
        v1 = jnp.sum(jnp.where(vhead0, 0.0, yc2), axis=-1, keepdims=True)
        y = yc * lax.rsqrt(jnp.where(vhead0, v0, v1) * inv + RMS_EPS) * gain_ref[:, lanes]
        g = g_ref[0, rows, lanes].astype(F32)
        o_ref[0, rows, lanes] = (y * (g * (1.0 / (1.0 + jnp.exp(-g))))).astype(BF16)

    def body(c, _):
        for u in range(unroll):
            for p in range(N_PAIR):
                pair_chunk(p, pl.multiple_of((c * unroll + u) * C, C))
        return 0

    lax.fori_loop(0, n_chunks // unroll, body, 0)


def _retention(main3, tables, gain_row):
    B, S, _ = main3.shape
    cos_t, sin_t, d_in, d_k, d_q, d_c, s_mask = tables
    C = RET_BLOCK
    n_chunks = S // C
    seq_spec = lambda cb: pl.BlockSpec((1, S, RET_W), lambda b: (b, 0, cb // N_PAIR))
    const = lambda shape: pl.BlockSpec(shape, lambda b: (0,) * len(shape))
    return pl.pallas_call(
        functools.partial(_ret_kernel, n_chunks=n_chunks, unroll=2 if n_chunks % 2 == 0 else 1),
        grid=(B,),
        in_specs=[
            seq_spec(CB_RQ), seq_spec(CB_RK), seq_spec(CB_RV), seq_spec(CB_RG),
            const((S, LANES)), const((S, LANES)),
            const((N_PAIR, 2, C, C)), const((N_PAIR, C, LANES)), const((N_PAIR, C, LANES)),
            const((N_PAIR, LANES, LANES)), const((LANES, LANES)), const((1, RET_W)),
        ],
        out_specs=pl.BlockSpec((1, S, RET_W), lambda b: (b, 0, 0)),
        out_shape=jax.ShapeDtypeStruct((B, S, RET_W), BF16),
        scratch_shapes=[pltpu.VMEM((N_PAIR, LANES, LANES), F32)],
        compiler_params=_cparams(("parallel",)),
        name="retention",
    )(main3, main3, main3, main3, cos_t, sin_t, d_in, d_k, d_q, d_c, s_mask, gain_row)


def _fox_kernel(q_ref, qa_ref, kh_ref, ka_ref, v_ref, gain_ref, havg_ref, o_ref, k_ref, *, tq, tk):
    p = pl.program_id(1)
    qi = pl.program_id(2)

    @pl.when(qi == 0)
    def _():
        k_ref[0, :, 0:LANES] = kh_ref[0]
        k_ref[0, :, LANES:2 * LANES] = ka_ref[0]

    q = q_ref[0]
    qa = qa_ref[0]
    lane_q = lax.broadcasted_iota(jnp.int32, (tq, LANES), 1)
    head0 = lane_q < HEAD_DIM
    qops = []
    for j in range(2):
        qm = jnp.where(head0 if j == 0 else jnp.logical_not(head0), q, jnp.zeros_like(q))
        lo = (2 * p + j) * AUG_STRIDE
        own = jnp.logical_and(lane_q >= lo, lane_q < lo + AUG_STRIDE)
        qops.append(jnp.concatenate([qm, jnp.where(own, qa, jnp.zeros_like(qa))], axis=1))
    q_pos = qi * tq + lax.broadcasted_iota(jnp.int32, (tq, tk), 0)
    k_off = lax.broadcasted_iota(jnp.int32, (tq, tk), 1)
    head0_k = lax.broadcasted_iota(jnp.int32, (tk, LANES), 1) < HEAD_DIM
    one = jnp.ones((tk, LANES), BF16)

    def tile(t, carry, masked):
        c0 = pl.multiple_of(t * tk, tk)
        k = k_ref[0, pl.ds(c0, tk), :]
        v = v_ref[0, pl.ds(c0, tk), :]
        new = []
        for j in range(2):
            m, acc = carry[2 * j:2 * j + 2]
            s = _dot_nt(qops[j], k)
            if masked:
                s = jnp.where(k_off + c0 <= q_pos, s, NEG)
            m_new = jnp.maximum(m, jnp.max(s, axis=-1, keepdims=True))
            pr = jnp.exp2(s - m_new)
            vj = jnp.where(head0_k if j == 0 else jnp.logical_not(head0_k), v, one)
            acc = jnp.exp2(m - m_new) * acc + jnp.dot(pr.astype(BF16), vj, preferred_element_type=F32)
            new += [m_new, acc]
        return tuple(new)

    init = []
    for _ in range(2):
        init += [jnp.full((tq, 1), NEG, F32), jnp.zeros((tq, LANES), F32)]
    n_full = qi
    carry = lax.fori_loop(0, n_full // 2, lambda t, c: tile(2 * t + 1, tile(2 * t, c, False), False), tuple(init))
    carry = lax.cond(n_full % 2 == 1, lambda c: tile(n_full - 1, c, False), lambda c: c, carry)
    carry = tile(n_full, carry, True)
    o0 = carry[1] * (1.0 / pltpu.roll(carry[1], HEAD_DIM, 1))
    o1 = carry[3] * (1.0 / pltpu.roll(carry[3], HEAD_DIM, 1))
    out = jnp.where(head0, o0, o1)
    sq = out * out
    sq_hi = sq.astype(BF16)
    sq_lo = (sq - sq_hi.astype(F32)).astype(BF16)
    ms = jnp.dot(sq_hi, havg_ref[...], preferred_element_type=F32) + jnp.dot(sq_lo, havg_ref[...], preferred_element_type=F32)
    o_ref[0] = (out * lax.rsqrt(ms + RMS_EPS) * gain_ref[...]).astype(BF16)


def _fox(main3, q_aug, k_aug, gain_row, tq, tk):
    B, S, _ = main3.shape
    assert tq == tk, "the kernel handles exactly one diagonal tile per query block"
    head_of = np.arange(LANES) // HEAD_DIM
    havg = (head_of[:, None] == head_of[None, :]).astype(np.float32) / HEAD_DIM
    return pl.pallas_call(
        functools.partial(_fox_kernel, tq=tq, tk=tk),
        grid=(B, N_PAIR, S // tq),
        in_specs=[
            pl.BlockSpec((1, tq, LANES), lambda b, p, i: (b, i, CB_FQ + p)),
            pl.BlockSpec((1, tq, LANES), lambda b, p, i: (b, i, 0)),
            pl.BlockSpec((1, S, LANES), lambda b, p, i: (b, 0, CB_FK + p)),
            pl.BlockSpec((1, S, LANES), lambda b, p, i: (b, 0, 0)),
            pl.BlockSpec((1, S, LANES), lambda b, p, i: (b, 0, CB_FV + p)),
            pl.BlockSpec((1, LANES), lambda b, p, i: (0, p)),
            pl.BlockSpec((LANES, LANES), lambda b, p, i: (0, 0)),
        ],
        out_specs=pl.BlockSpec((1, tq, LANES), lambda b, p, i: (b, i, p)),
        out_shape=jax.ShapeDtypeStruct((B, S, FOX_W), BF16),
        scratch_shapes=[pltpu.VMEM((1, S, 2 * LANES), BF16)],
        compiler_params=_cparams(("parallel", "parallel", "arbitrary")),
        name="fox_attention",
    )(main3, q_aug, main3, k_aug, main3, gain_row, jnp.asarray(havg, BF16))


def _gelu_tanh(x):
    return 0.5 * x * (1.0 + jnp.tanh(np.sqrt(2.0 / np.pi).astype(np.float32) * (x + 0.044715 * (x * x * x))))


def _cmp_kernel(r_ref, pos_ref, wt_ref, wb_ref, w2_ref, o_ref):
    r = r_ref[0]
    n = r.shape[0]
    top = jnp.dot(r, wt_ref[...], preferred_element_type=F32)
    bot = jnp.dot(r, wb_ref[...], preferred_element_type=F32)
    cpos = jnp.dot(pos_ref[0:8, :], wt_ref[...], preferred_element_type=F32)
    cpos = cpos + jnp.dot(pos_ref[8:16, :], wb_ref[...], preferred_element_type=F32)
    hid = _gelu_tanh(top + pltpu.roll(bot, n - 1, 0) + cpos[0:1, :])
    o_ref[0] = jnp.dot(hid.astype(BF16), w2_ref[...], preferred_element_type=F32)


def _nsa_compress(r, pos2, wt, wb, w2):
    B, n, _ = r.shape
    width = CMP_STRIDE * LANES
    return pl.pallas_call(
        _cmp_kernel,
        grid=(B,),
        in_specs=[
            pl.BlockSpec((1, n, width), lambda b: (b, 0, 0)),
            pl.BlockSpec((16, width), lambda b: (0, 0)),
            pl.BlockSpec((width, 2 * CMP_HIDDEN), lambda b: (0, 0)),
            pl.BlockSpec((width, 2 * CMP_HIDDEN), lambda b: (0, 0)),
            pl.BlockSpec((2 * CMP_HIDDEN, LANES), lambda b: (0, 0)),
        ],
        out_specs=pl.BlockSpec((1, n, LANES), lambda b: (b, 0, 0)),
        out_shape=jax.ShapeDtypeStruct((B, n, LANES), F32),
        compiler_params=_cparams(("parallel",)),
        name="nsa_compress",
    )(r, pos2, wt, wb, w2)


def _softmax_rows(s, mask):
    sm = jnp.where(mask, s, NEG)
    m = jnp.maximum(jnp.max(sm, axis=-1, keepdims=True), 0.1 * NEG)
    e = jnp.exp2(sm - m)
    l = jnp.sum(e, axis=-1, keepdims=True)
    return e / jnp.where(l > 0.0, l, 1.0)


def _nsa_kernel(q_ref, cmp_ref, c2s_ref, sel_ref, noh_ref, win_ref, gl_ref, gain_ref, gsel_ref, havg_ref, o_ref, ksa_ref, *, nb, tq, tk, n_cmp,
                n_sel, top_n):
    H = N_NSA
    qi = pl.program_id(1)
    start = qi * tq

    @pl.when(qi == 0)
    def _():
        key_lanes = lax.broadcasted_iota(jnp.int32, noh_ref.shape, 1) < HEAD_DIM
        for bb in range(nb):
            ksa_ref[bb] = jnp.where(key_lanes, sel_ref[bb], noh_ref[...])

    t_row = start + lax.broadcasted_iota(jnp.int32, (tq, 1), 0)
    t_all = jnp.concatenate([t_row] * H, axis=0)
    zeros64 = jnp.zeros((tq, HEAD_DIM), BF16)
    wlen = WINDOW + tq
    base = pl.multiple_of(jnp.maximum(start - WINDOW, 0), tq)
    n_pad = cmp_ref.shape[1]
    assert CMP_STRIDE == 16 and tq & (tq - 1) == 0
    last_cmp = (t_all - (CMP_LEN - 1)) >> 4
    row_in_block = lax.broadcasted_iota(jnp.int32, (H * tq, wlen), 0) & (tq - 1)
    win_lag = (start - base) - (lax.broadcasted_iota(jnp.int32, (H * tq, wlen), 1) - row_in_block)

    def prologue(bb):
        q = q_ref[bb]
        q_heads = [q[:, h * HEAD_DIM:(h + 1) * HEAD_DIM] for h in range(H)]
        q_pad = jnp.concatenate([jnp.concatenate([qh, zeros64], axis=1) for qh in q_heads], axis=0)

        win = win_ref[bb, pl.ds(base, wlen), :]
        s_w = _dot_nt(q_pad, win)
        s_w = jnp.where(lax.bitcast_convert_type(win_lag, jnp.uint32) < WINDOW, s_w, NEG)
        e_w = jnp.exp2(s_w - jnp.max(s_w, axis=-1, keepdims=True))
        win1 = jnp.where(lax.broadcasted_iota(jnp.int32, (wlen, LANES), 1) < HEAD_DIM, jnp.ones((wlen, LANES), BF16), win)
        o_w = jnp.dot(e_w.astype(BF16), win1, preferred_element_type=F32)

        cmp = cmp_ref[bb]
        cmp_hi = cmp.astype(BF16)
        cmp_lo = (cmp - cmp_hi.astype(F32)).astype(BF16)
        s_c = _dot_nt(q_pad, cmp_hi) + _dot_nt(q_pad, cmp_lo)
        n_id = lax.broadcasted_iota(jnp.int32, (H * tq, n_pad), 1)
        p_c = _softmax_rows(s_c, n_id <= last_cmp)
        o_c = jnp.dot(p_c.astype(BF16), cmp_hi, preferred_element_type=F32)

        p_sum = p_c[0:tq]
        for h in range(1, H):
            p_sum = p_sum + p_c[h * tq:(h + 1) * tq]
        p_hi = p_sum.astype(BF16)
        p_lo = (p_sum - p_hi.astype(F32)).astype(BF16)
        c2s_t = c2s_ref[...]
        imp = _dot_nt(c2s_t, p_hi) + _dot_nt(c2s_t, p_lo)
        t_lane = start + lax.broadcasted_iota(jnp.int32, (HEAD_DIM, tq), 1)
        s_id = lax.broadcasted_iota(jnp.int32, (HEAD_DIM, tq), 0)
        cur = t_lane >> 6
        forced = jnp.logical_or(s_id == 0, jnp.logical_or(s_id == cur, s_id == cur - 1))
        score = jnp.where(forced, FORCED_SCORE, imp)
        score = jnp.where(s_id * SEL_LEN <= t_lane, score, -1.0)
        SUB = 8
        groups = [score[g * SUB:(g + 1) * SUB, :] for g in range(HEAD_DIM // SUB)]
        ranks = [jnp.zeros((SUB, tq), jnp.int32) for _ in groups]
        sub_id = lax.broadcasted_iota(jnp.int32, (SUB, tq), 0)
        for c in range(n_sel):
            row = score[c:c + 1, :]
            for g in range(len(groups)):
                if g * SUB > c:
                    before = row >= groups[g]
                elif g * SUB + SUB - 1 < c:
                    before = row > groups[g]
                else:
                    before = jnp.logical_or(row > groups[g], jnp.logical_and(row == groups[g], sub_id > c - g * SUB))
                ranks[g] = ranks[g] + before.astype(jnp.int32)
        not_sel = jnp.concatenate([jnp.where(r < top_n, 0.0, 1.0) for r in ranks], axis=0)
        ns = jnp.concatenate([jnp.zeros((HEAD_DIM, tq), F32), not_sel], axis=0).T.astype(BF16)
        q_aug = q_pad + jnp.concatenate([ns] * H, axis=0)
        return q_aug, o_c, o_w

    pro = [prologue(bb) for bb in range(nb)]

    def sel_tile(bb, c0, size, carry, diagonal):
        m, acc = carry
        s = _dot_nt(pro[bb][0], ksa_ref[bb, pl.ds(c0, size), :])
        if diagonal:
            r_id = lax.broadcasted_iota(jnp.int32, (H * tq, size), 0) & (tq - 1)
            c_id = lax.broadcasted_iota(jnp.int32, (H * tq, size), 1)
            s = jnp.where(c_id <= r_id, s, NEG)
        m_new = jnp.maximum(m, jnp.max(s, axis=-1, keepdims=True))
        pr = jnp.exp2(s - m_new)
        key_lanes = lax.broadcasted_iota(jnp.int32, (size, LANES), 1) < HEAD_DIM
        v1 = jnp.where(key_lanes, jnp.ones((size, LANES), BF16), sel_ref[bb, pl.ds(c0, size), :])
        acc = jnp.exp2(m - m_new) * acc + jnp.dot(pr.astype(BF16), v1, preferred_element_type=F32)
        return m_new, acc

    def all_rows(c0, size, carries, diagonal):
        return tuple(sel_tile(bb, c0, size, carries[bb], diagonal) for bb in range(nb))

    carries = tuple((jnp.full((H * tq, 1), NEG, F32), jnp.zeros((H * tq, LANES), F32)) for _ in range(nb))
    n_full = start // tk
    carries = lax.fori_loop(0, n_full, lambda t, c: all_rows(pl.multiple_of(t * tk, tk), tk, c, False), carries)
    n_tail = (start - n_full * tk) // tq
    for d in range(tk // tq - 1):
        c0 = pl.multiple_of(n_full * tk + d * tq, tq)
        carries = lax.cond(d < n_tail, lambda c, c0=c0: all_rows(c0, tq, c, False), lambda c: c, carries)
    carries = all_rows(pl.multiple_of(start, tq), tq, carries, True)

    first_half = lax.broadcasted_iota(jnp.int32, (tq, LANES), 1) < HEAD_DIM

    def place(acc, normalise):
        cols = []
        for h in range(0, H, 2):
            a0, a1 = acc[h * tq:(h + 1) * tq], acc[(h + 1) * tq:(h + 2) * tq]
            r0 = pltpu.roll(a0, HEAD_DIM, 1)
            if normalise:
                v0 = r0 * (1.0 / a0)
                v1 = a1 * (1.0 / pltpu.roll(a1, HEAD_DIM, 1))
            else:
                v0, v1 = r0, a1
            cols.append(jnp.where(first_half, v0, v1))
        return jnp.concatenate(cols, axis=1)

    for bb in range(nb):
        _, o_c, o_w = pro[bb]
        gates = 1.0 / (1.0 + jnp.exp(-gl_ref[bb]))
        g_hi = gates.astype(BF16)
        g_lo = (gates - g_hi.astype(F32)).astype(BF16)
        mix = None
        for c, y in enumerate((place(o_c, False), place(carries[bb][1], True), place(o_w, True))):
            g = jnp.dot(g_hi, gsel_ref[c], preferred_element_type=F32) + jnp.dot(g_lo, gsel_ref[c], preferred_element_type=F32)
            mix = g * y if mix is None else mix + g * y
        sq = mix * mix
        sq_hi = sq.astype(BF16)
        sq_lo = (sq - sq_hi.astype(F32)).astype(BF16)
        ms = jnp.dot(sq_hi, havg_ref[...], preferred_element_type=F32) + jnp.dot(sq_lo, havg_ref[...], preferred_element_type=F32)
        o_ref[bb] = (mix * lax.rsqrt(ms + RMS_EPS) * gain_ref[...]).astype(BF16)


def _nsa(main3, cmp, c2s, neg_onehot, small3, gain_row, nb, tq, tk, n_cmp):
    B, S, _ = main3.shape
    n_sel = S // SEL_LEN
    n_pad = cmp.shape[1]
    gsel = np.zeros((3, LANES, NSA_W), np.float32)
    for h in range(N_NSA):
        for c in range(3):
            gsel[c, SMALL_GATE + 3 * h + c, h * HEAD_DIM:(h + 1) * HEAD_DIM] = 1.0
    head_of = np.arange(NSA_W) // HEAD_DIM
    havg = (head_of[:, None] == head_of[None, :]).astype(np.float32) / HEAD_DIM
    return pl.pallas_call(
        functools.partial(_nsa_kernel, nb=nb, tq=tq, tk=tk, n_cmp=n_cmp, n_sel=n_sel, top_n=min(TOP_N, n_sel)),
        grid=(B // nb, S // tq),
        in_specs=[
            pl.BlockSpec((nb, tq, NSA_W), lambda b, i: (b, i, CB_NQ // 2)),
            pl.BlockSpec((nb, n_pad, LANES), lambda b, i: (b, 0, 0)),
            pl.BlockSpec((HEAD_DIM, n_pad), lambda b, i: (0, 0)),
            pl.BlockSpec((nb, S, LANES), lambda b, i: (b, 0, CB_SEL)),
            pl.BlockSpec((S, LANES), lambda b, i: (0, 0)),
            pl.BlockSpec((nb, S, LANES), lambda b, i: (b, 0, CB_WIN)),
            pl.BlockSpec((nb, tq, LANES), lambda b, i: (b, i, 0)),
            pl.BlockSpec((1, NSA_W), lambda b, i: (0, 0)),
            pl.BlockSpec((3, LANES, NSA_W), lambda b, i: (0, 0, 0)),
            pl.BlockSpec((NSA_W, NSA_W), lambda b, i: (0, 0)),
        ],
        out_specs=pl.BlockSpec((nb, tq, NSA_W), lambda b, i: (b, i, 0)),
        out_shape=jax.ShapeDtypeStruct((B, S, NSA_W), BF16),
        scratch_shapes=[pltpu.VMEM((nb, S, LANES), BF16)],
        compiler_params=_cparams(("parallel", "arbitrary")),
        name="nsa_attention",
    )(main3, cmp, c2s, main3, neg_onehot, main3, small3, gain_row, jnp.asarray(gsel, BF16), jnp.asarray(havg, BF16))


def _post_kernel(x_ref, yr_ref, yf_ref, yn_ref, wo_ref, g_ref, w1_ref, w2_ref, gf_ref, o_ref, *, ff_chunk, final):
    x = x_ref[...]
    x = x + jnp.dot(yr_ref[...], wo_ref[0:RET_W, :], preferred_element_type=F32)
    x = x + jnp.dot(yf_ref[...], wo_ref[RET_W:RET_W + FOX_W, :], preferred_element_type=F32)
    x = x + jnp.dot(yn_ref[...], wo_ref[RET_W + FOX_W:, :], preferred_element_type=F32)
    h = _rms(x, g_ref[...]).astype(BF16)
    o_ref[...] = x
    for c0 in range(0, D_FF, ff_chunk):
        hid = jnp.maximum(jnp.dot(h, w1_ref[:, c0:c0 + ff_chunk], preferred_element_type=F32), 0.0)
        o_ref[...] += jnp.dot((hid * hid).astype(BF16), w2_ref[c0:c0 + ff_chunk, :], preferred_element_type=F32)
    if final:
        o_ref[...] = _rms(o_ref[...], gf_ref[...])


def _post(x2d, yr, yf, yn, wo, g, w1, w2, gf, tm, final):
    T = x2d.shape[0]
    row = lambda w: pl.BlockSpec((tm, w), lambda i: (i, 0))
    return pl.pallas_call(
        functools.partial(_post_kernel, ff_chunk=512, final=final),
        grid=(T // tm,),
        in_specs=[
            row(D_MODEL), row(RET_W), row(FOX_W), row(NSA_W),
            _resident((D_MODEL, D_MODEL), lambda i: (0, 0)),
            _resident((1, D_MODEL), lambda i: (0, 0)),
            _resident((D_MODEL, D_FF), lambda i: (0, 0)),
            _resident((D_FF, D_MODEL), lambda i: (0, 0)),
            _resident((1, D_MODEL), lambda i: (0, 0)),
        ],
        out_specs=row(D_MODEL),
        out_shape=jax.ShapeDtypeStruct((T, D_MODEL), F32),
        compiler_params=_cparams(("parallel",)),
        name="outproj_mlp",
    )(x2d, yr, yf, yn, wo, g, w1, w2, gf)


def _tile_sizes(batch):
    return dict(tm=512, fox_tq=512, fox_tk=512, nsa_tq=256, nsa_tk=512, nsa_nb=2 if batch % 2 == 0 else 1)


def kernel(x, norm_attn, w_in, fox_forget_bias, ret_norm_gain, fox_norm_gain, nsa_norm_gain, nsa_cmp_pos_k, nsa_cmp_pos_v, nsa_cmp_w1_k, nsa_cmp_w2_k, nsa_cmp_w1_v, nsa_cmp_w2_v, w_out, norm_mlp, w_mlp_in, w_mlp_out, norm_final):
    B, S, D = x.shape
    depth = w_in.shape[0]
    assert D == D_MODEL and S % 512 == 0 and S >= WINDOW + 256
    T = B * S
    ts = _tile_sizes(B)

    cols, scale, small_cols, small_mask = _in_proj_columns()
    wm = (w_in[:, :, cols] * scale).astype(BF16)
    ws = (w_in[:, :, small_cols] * small_mask).astype(BF16)
    wo = w_out.astype(BF16)
    w1 = w_mlp_in.astype(BF16)
    w2 = w_mlp_out.astype(BF16)

    def expand_w1(wk, wv):
        L = wk.shape[0]
        wk = wk.reshape(L, CMP_LEN, HEAD_DIM, CMP_HIDDEN)
        wv = wv.reshape(L, CMP_LEN, HEAD_DIM, CMP_HIDDEN)
        z = jnp.zeros_like(wk)
        full = jnp.concatenate([jnp.concatenate([wk, z], axis=-1), jnp.concatenate([z, wv], axis=-1)], axis=2)
        full = full.reshape(L, CMP_LEN * LANES, 2 * CMP_HIDDEN).astype(BF16)
        return full[:, :CMP_STRIDE * LANES], full[:, CMP_STRIDE * LANES:]

    wt_all, wb_all = expand_w1(nsa_cmp_w1_k, nsa_cmp_w1_v)
    zk = jnp.zeros_like(nsa_cmp_w2_k)
    w2c = jnp.concatenate([jnp.concatenate([nsa_cmp_w2_k, zk], axis=-1),
                           jnp.concatenate([zk, nsa_cmp_w2_v], axis=-1)], axis=1).astype(BF16)
    pos = jnp.concatenate([nsa_cmp_pos_k, nsa_cmp_pos_v], axis=-1)
    pos_tb = pos.reshape(depth, 2, 1, CMP_STRIDE * LANES)
    pos_tb = jnp.broadcast_to(pos_tb, (depth, 2, 8, CMP_STRIDE * LANES)).reshape(depth, 16, CMP_STRIDE * LANES).astype(BF16)

    n_cmp = (S - CMP_LEN) // CMP_STRIDE + 1
    n_sel = S // SEL_LEN
    n_rows = S // CMP_STRIDE
    cs = np.arange(n_rows) * CMP_STRIDE
    ss = np.arange(n_sel) * SEL_LEN
    overlap = np.clip(np.minimum(cs[:, None] + CMP_LEN, ss[None, :] + SEL_LEN) - np.maximum(cs[:, None], ss[None, :]), 0, None)
    overlap[n_cmp:] = 0
    c2s_np = np.zeros((HEAD_DIM, n_rows), np.float32)
    c2s_np[:n_sel] = (overlap / CMP_LEN).T
    c2s = jnp.asarray(c2s_np, dtype=BF16)
    assert n_sel <= HEAD_DIM, "the selected-branch key augmentation has 64 lanes, one per selection block"
    neg_onehot = jnp.asarray(np.where((np.arange(S)[:, None] // SEL_LEN) == np.arange(LANES)[None, :] - HEAD_DIM, NEG, 0.0), dtype=BF16)

    tables = _retention_tables(S)
    fbias = jnp.zeros((depth, 1, LANES), F32).at[:, 0, SMALL_FF:SMALL_FF + N_FOX].set(fox_forget_bias)

    x2d = x.reshape(T, D)
    for l in range(depth):
        main, small = _inproj(x2d, norm_attn[l][None, :], wm[l], ws[l], ts["tm"])
        main3 = main.reshape(B, S, N_MAIN)
        small3 = small.reshape(B, S, LANES)
        fox_qa, fox_ka = _forget_cumsum(small3, fbias[l])
        y_ret = _retention(main3, tables, ret_norm_gain[l][None, :])
        y_fox = _fox(main3, fox_qa, fox_ka, fox_norm_gain[l][None, :], ts["fox_tq"], ts["fox_tk"])
        r = main3[:, :, CB_CMP * LANES:(CB_CMP + 1) * LANES].reshape(B, n_rows, CMP_STRIDE * LANES)
        cmp = _nsa_compress(r, pos_tb[l], wt_all[l], wb_all[l], w2c[l])
        y_nsa = _nsa(main3, cmp, c2s, neg_onehot, small3, nsa_norm_gain[l][None, :], ts["nsa_nb"], ts["nsa_tq"], ts["nsa_tk"], n_cmp)
        x2d = _post(x2d, y_ret.reshape(T, RET_W), y_fox.reshape(T, FOX_W), y_nsa.reshape(T, NSA_W), wo[l],
                    norm_mlp[l][None, :], w1[l], w2[l], norm_final[None, :], ts["tm"], final=(l == depth - 1))
    return x2d.reshape(B, S, D)
```

```python
import functools

import numpy as np
import jax
import jax.numpy as jnp
from jax import lax
from jax.experimental import pallas as pl
from jax.experimental.pallas import tpu as pltpu

F32 = jnp.float32
BF16 = jnp.bfloat16

D_MODEL = 1024
HEAD_DIM = 64
N_RET = 6
N_FOX = 6
N_NSA = 4
RET_W = N_RET * HEAD_DIM
FOX_W = N_FOX * HEAD_DIM
NSA_W = N_NSA * HEAD_DIM
D_FF = 4 * D_MODEL
RMS_EPS = 1e-6
RET_CHUNK = 128
RET_BLOCK = 256
ROPE_BASE = 10000.0
CMP_LEN = 32
CMP_STRIDE = 16
CMP_HIDDEN = 4 * HEAD_DIM
SEL_LEN = 64
TOP_N = 16
WINDOW = 512
FORCED_SCORE = 1e4
NEG = -1e30
LOG2E = float(np.log2(np.e))

LANES = 128
N_PAIR = N_RET // 2

CB_RQ, CB_RK, CB_RV, CB_RG = 0, 3, 6, 9
CB_NQ = 12
CB_FQ, CB_FK, CB_FV = 14, 17, 20
CB_CMP, CB_SEL, CB_WIN = 23, 24, 25
N_MAIN = 26 * LANES
SMALL_FF = 0
SMALL_GATE = 8

VMEM_LIMIT = 56 * 1024 * 1024


def _cparams(sem):
    return pltpu.CompilerParams(dimension_semantics=sem, vmem_limit_bytes=VMEM_LIMIT)


def _resident(shape, index_map):
    return pl.BlockSpec(shape, index_map, pipeline_mode=pl.Buffered(1))


def _in_proj_columns():
    sizes = (RET_W, RET_W, RET_W, RET_W, FOX_W, FOX_W, FOX_W, N_FOX, NSA_W) + (HEAD_DIM,) * 6 + (3 * N_NSA,)
    off = np.concatenate([[0], np.cumsum(sizes)])
    (o_rq, o_rk, o_rv, o_rg, o_fq, o_fk, o_fv, o_ff, o_nq, o_kc, o_vc, o_ks, o_vs, o_kw, o_vw, o_gt) = off[:-1]
    half = HEAD_DIM // 2
    inter = []
    for p in range(N_PAIR):
        a, b = 2 * p, 2 * p + 1
        for h, part in ((a, 0), (b, 0), (a, 1), (b, 1)):
            inter.extend(range(h * HEAD_DIM + part * half, h * HEAD_DIM + (part + 1) * half))
    inter = np.asarray(inter)
    nat = np.arange(RET_W)
    cols, scale = [], []

    def add(idx, s=1.0):
        cols.append(np.asarray(idx))
        scale.append(np.full(len(idx), s, np.float32))

    qk_scale = HEAD_DIM ** -0.5
    sm_scale = qk_scale * LOG2E
    add(o_rq + inter)
    add(o_rk + inter, qk_scale)
    add(o_rv + nat)
    add(o_rg + nat)
    add(o_nq + np.arange(NSA_W), sm_scale)
    add(o_fq + nat, sm_scale)
    add(o_fk + nat)
    add(o_fv + nat)
    for o in (o_kc, o_vc, o_ks, o_vs, o_kw, o_vw):
        add(o + np.arange(HEAD_DIM))
    cols = np.concatenate(cols)
    scale = np.concatenate(scale)
    assert cols.shape[0] == N_MAIN
    small_cols = np.zeros(LANES, np.int64)
    small_mask = np.zeros(LANES, np.float32)
    small_cols[SMALL_FF:SMALL_FF + N_FOX] = o_ff + np.arange(N_FOX)
    small_mask[SMALL_FF:SMALL_FF + N_FOX] = 1.0
    small_cols[SMALL_GATE:SMALL_GATE + 3 * N_NSA] = o_gt + np.arange(3 * N_NSA)
    small_mask[SMALL_GATE:SMALL_GATE + 3 * N_NSA] = 1.0
    return cols, scale, small_cols, small_mask


def _retention_tables(seq):
    half = HEAD_DIM // 2
    inv = 1.0 / (ROPE_BASE ** (jnp.arange(half, dtype=F32) / half))
    ang = jnp.arange(seq, dtype=F32)[:, None] * inv[None, :]
    cos, sin = jnp.cos(ang), jnp.sin(ang)
    cos_t = jnp.concatenate([cos, cos, cos, cos], axis=-1)
    sin_t = jnp.concatenate([-sin, -sin, sin, sin], axis=-1)
    log_gamma = np.log(1.0 - 2.0 ** (-5.0 - np.arange(N_RET, dtype=np.float32))).astype(np.float32)
    C = RET_BLOCK
    idx = np.arange(C, dtype=np.float32)
    diff = idx[:, None] - idx[None, :]
    lane = np.arange(LANES)
    head_k = (lane % HEAD_DIM) // half
    head_v = lane // HEAD_DIM
    d_in = np.zeros((N_PAIR, 2, C, C), np.float32)
    d_k = np.zeros((N_PAIR, C, LANES), np.float32)
    d_q = np.zeros((N_PAIR, C, LANES), np.float32)
    d_c = np.zeros((N_PAIR, LANES, LANES), np.float32)
    for p in range(N_PAIR):
        lg = log_gamma[2 * p:2 * p + 2]
        for j in range(2):
            d_in[p, j] = np.where(diff >= 0, np.exp(lg[j] * np.maximum(diff, 0.0)), 0.0)
        d_k[p] = np.exp(lg[head_k][None, :] * (C - 1.0 - idx)[:, None])
        d_q[p] = np.exp(lg[head_v][None, :] * (idx + 1.0)[:, None])
        d_c[p] = np.broadcast_to(np.exp(lg[head_v] * C)[None, :], (LANES, LANES))
    s_mask = (head_k[:, None] == head_v[None, :]).astype(np.float32)
    return cos_t, sin_t, jnp.asarray(d_in), jnp.asarray(d_k), jnp.asarray(d_q), jnp.asarray(d_c), jnp.asarray(s_mask)


def _rms(x, g):
    return x * lax.rsqrt(jnp.mean(x * x, axis=-1, keepdims=True) + RMS_EPS) * g


def _inproj_kernel(x_ref, g_ref, wm_ref, ws_ref, om_ref, os_ref, *, n_chunk):
    h = _rms(x_ref[...], g_ref[...]).astype(BF16)
    for n0 in range(0, N_MAIN, n_chunk):
        n1 = min(n0 + n_chunk, N_MAIN)
        om_ref[:, n0:n1] = jnp.dot(h, wm_ref[:, n0:n1], preferred_element_type=F32).astype(BF16)
    os_ref[...] = jnp.dot(h, ws_ref[...], preferred_element_type=F32)


def _inproj(x2d, g, wm, ws, tm):
    T = x2d.shape[0]
    return pl.pallas_call(
        functools.partial(_inproj_kernel, n_chunk=512),
        grid=(T // tm,),
        in_specs=[
            pl.BlockSpec((tm, D_MODEL), lambda i: (i, 0)),
            _resident((1, D_MODEL), lambda i: (0, 0)),
            _resident((D_MODEL, N_MAIN), lambda i: (0, 0)),
            _resident((D_MODEL, LANES), lambda i: (0, 0)),
        ],
        out_specs=[
            pl.BlockSpec((tm, N_MAIN), lambda i: (i, 0)),
            pl.BlockSpec((tm, LANES), lambda i: (i, 0)),
        ],
        out_shape=[jax.ShapeDtypeStruct((T, N_MAIN), BF16), jax.ShapeDtypeStruct((T, LANES), F32)],
        compiler_params=_cparams(("parallel",)),
        name="inproj",
    )(x2d, g, wm, ws)


N_SPLIT = 3
AUG_STRIDE = 2 * N_SPLIT


def _split_bf16(x):
    terms, rest = [], x
    for _ in range(N_SPLIT):
        t = rest.astype(BF16)
        rest = rest - t.astype(F32)
        terms.append(t)
    return terms


def _bias_placement():
    mq = np.zeros((N_SPLIT, LANES, LANES), np.float32)
    mk = np.zeros((N_SPLIT, LANES, LANES), np.float32)
    one_q = np.zeros((1, LANES), np.float32)
    one_k = np.zeros((1, LANES), np.float32)
    for h in range(N_FOX):
        base = h * AUG_STRIDE
        for i in range(N_SPLIT):
            mq[i, SMALL_FF + h, base + i] = 1.0
            mk[i, SMALL_FF + h, base + N_SPLIT + i] = -1.0
            one_q[0, base + N_SPLIT + i] = 1.0
            one_k[0, base + i] = 1.0
    return jnp.asarray(mq, BF16), jnp.asarray(mk, BF16), jnp.asarray(one_q), jnp.asarray(one_k)


def _cum_kernel(z_ref, b_ref, tri_ref, mq_ref, mk_ref, oq_ref, ok_ref, qa_ref, ka_ref, edge_ref, *, n_chunks):
    C = RET_CHUNK

    def body(c, carry):
        r0 = pl.multiple_of(c * C, C)
        z = z_ref[0, pl.ds(r0, C), :] + b_ref[...]
        lf = jnp.minimum(z, 0.0) - jnp.log1p(jnp.exp(-jnp.abs(z)))
        cs = carry
        for term in _split_bf16(lf):
            cs = cs + jnp.dot(tri_ref[...], term, preferred_element_type=F32)
        cs2 = cs * LOG2E
        edge_ref[0, pl.ds(c, 1)] = jnp.concatenate([cs2[0:1], cs2[C - 1:C], jnp.zeros((6, LANES), F32)], axis=0)[None]
        qa, ka = oq_ref[...], ok_ref[...]
        for i, term in enumerate(_split_bf16(cs2)):
            qa = qa + jnp.dot(term, mq_ref[i], preferred_element_type=F32)
            ka = ka + jnp.dot(term, mk_ref[i], preferred_element_type=F32)
        qa_ref[0, pl.ds(r0, C), :] = qa.astype(BF16)
        ka_ref[0, pl.ds(r0, C), :] = ka.astype(BF16)
        return cs[C - 1:C, :]

    lax.fori_loop(0, n_chunks, body, jnp.zeros((1, LANES), F32))


def _forget_cumsum(small3, bias_row):
    B, S, _ = small3.shape
    tri = jnp.asarray(np.tril(np.ones((RET_CHUNK, RET_CHUNK), np.float32)), BF16)
    mq, mk, one_q, one_k = _bias_placement()
    const = lambda shape: pl.BlockSpec(shape, lambda b: (0,) * len(shape))
    return pl.pallas_call(
        functools.partial(_cum_kernel, n_chunks=S // RET_CHUNK),
        grid=(B,),
        in_specs=[
            pl.BlockSpec((1, S, LANES), lambda b: (b, 0, 0)),
            const((1, LANES)), const((RET_CHUNK, RET_CHUNK)),
            const((N_SPLIT, LANES, LANES)), const((N_SPLIT, LANES, LANES)), const((1, LANES)), const((1, LANES)),
        ],
        out_specs=[
            pl.BlockSpec((1, S, LANES), lambda b: (b, 0, 0)),
            pl.BlockSpec((1, S, LANES), lambda b: (b, 0, 0)),
            pl.BlockSpec((1, S // RET_CHUNK, 8, LANES), lambda b: (b, 0, 0, 0)),
        ],
        out_shape=[jax.ShapeDtypeStruct((B, S, LANES), BF16), jax.ShapeDtypeStruct((B, S, LANES), BF16),
                   jax.ShapeDtypeStruct((B, S // RET_CHUNK, 8, LANES), F32)],
        compiler_params=_cparams(("parallel",)),
        name="forget_cumsum",
    )(small3, bias_row, tri, mq, mk, one_q, one_k)


def _dot_nt(a, b):
    return lax.dot_general(a, b, (((1,), (1,)), ((), ())), preferred_element_type=F32)


def _ret_kernel(q_ref, k_ref, v_ref, g_ref, cos_ref, sin_ref, din_ref, dk_ref, dq_ref, dc_ref, sm_ref, gain_ref,
                o_ref, state_ref, *, n_chunks, unroll):
    C = RET_BLOCK
    half = HEAD_DIM // 2
    lane = lax.broadcasted_iota(jnp.int32, (C, LANES), 1)
    khead = (lane % HEAD_DIM) // half
    vhead0 = lane < HEAD_DIM
    state_ref[...] = jnp.zeros_like(state_ref)

    def pair_chunk(p, r0):
        rows = pl.ds(r0, C)
        lanes = slice(p * LANES, (p + 1) * LANES)
        cs, sn = cos_ref[rows, :], sin_ref[rows, :]
        q = q_ref[0, rows, lanes].astype(F32)
        k = k_ref[0, rows, lanes].astype(F32)
        q = q * cs + pltpu.roll(q, HEAD_DIM, 1) * sn
        k = k * cs + pltpu.roll(k, HEAD_DIM, 1) * sn
        v = v_ref[0, rows, lanes]
        qb, kb = q.astype(BF16), k.astype(BF16)
        state = state_ref[p]
        cross = jnp.dot(qb, state.astype(BF16), preferred_element_type=F32) * dq_ref[p]
        outs = []
        for j in range(2):
            qm = jnp.where(khead == j, qb, jnp.zeros_like(qb))
            inner = _dot_nt(qm, kb) * din_ref[p, j]
            outs.append(jnp.dot(inner.astype(BF16), v, preferred_element_type=F32))
        out = jnp.where(vhead0, outs[0], outs[1]) + cross
        kd = (k * dk_ref[p]).T.astype(BF16)
        state_ref[p] = state * dc_ref[p] + jnp.dot(kd, v, preferred_element_type=F32) * sm_ref[...]
        inv = 1.0 / HEAD_DIM
        s0 = jnp.sum(jnp.where(vhead0, out, 0.0), axis=-1, keepdims=True)
        s1 = jnp.sum(jnp.where(vhead0, 0.0, out), axis=-1, keepdims=True)
        yc = out - jnp.where(vhead0, s0, s1) * inv
        yc2 = yc * yc
        v0 = jnp.sum(jnp.where(vhead0, yc2, 0.0), axis=-1, keepdims=True)
        v1 = jnp.sum(jnp.where(vhead0, 0.0, yc2), axis=-1, keepdims=True)
        y = yc * lax.rsqrt(jnp.where(vhead0, v0, v1) * inv + RMS_EPS) * gain_ref[:, lanes]
        g = g_ref[0, rows, lanes].astype(F32)
        o_ref[0, rows, lanes] = (y * (g * (1.0 / (1.0 + jnp.exp(-g))))).astype(BF16)

    def body(c, _):
        for u in range(unroll):
            for p in range(N_PAIR):
                pair_chunk(p, pl.multiple_of((c * unroll + u) * C, C))
        return 0

    lax.fori_loop(0, n_chunks // unroll, body, 0)


def _retention(main3, tables, gain_row):
    B, S, _ = main3.shape
    cos_t, sin_t, d_in, d_k, d_q, d_c, s_mask = tables
    C = RET_BLOCK
    n_chunks = S // C
    seq_spec = lambda cb: pl.BlockSpec((1, S, RET_W), lambda b: (b, 0, cb // N_PAIR))
    const = lambda shape: pl.BlockSpec(shape, lambda b: (0,) * len(shape))
    return pl.pallas_call(
        functools.partial(_ret_kernel, n_chunks=n_chunks, unroll=2 if n_chunks % 2 == 0 else 1),
        grid=(B,),
        in_specs=[
            seq_spec(CB_RQ), seq_spec(CB_RK), seq_spec(CB_RV), seq_spec(CB_RG),
            const((S, LANES)), const((S, LANES)),
            const((N_PAIR, 2, C, C)), const((N_PAIR, C, LANES)), const((N_PAIR, C, LANES)),
            const((N_PAIR, LANES, LANES)), const((LANES, LANES)), const((1, RET_W)),
        ],
        out_specs=pl.BlockSpec((1, S, RET_W), lambda b: (b, 0, 0)),
        out_shape=jax.ShapeDtypeStruct((B, S, RET_W), BF16),
        scratch_shapes=[pltpu.VMEM((N_PAIR, LANES, LANES), F32)],
        compiler_params=_cparams(("parallel",)),
        name="retention",
    )(main3, main3, main3, main3, cos_t, sin_t, d_in, d_k, d_q, d_c, s_mask, gain_row)


UNDERFLOW_LOG2 = -170.0


def _head_sq_norm_max(x, havg, head0):
    xf = x.astype(F32)
    sq = xf * xf
    hi = sq.astype(BF16)
    lo = (sq - hi.astype(F32)).astype(BF16)
    mean = jnp.dot(hi, havg, preferred_element_type=F32) + jnp.dot(lo, havg, preferred_element_type=F32)
    bound = mean * (HEAD_DIM * 1.01)
    return jnp.max(jnp.where(head0, bound, 0.0)), jnp.max(jnp.where(head0, 0.0, bound))


def _fox_kernel(cb_ref, q_ref, qa_ref, kh_ref, ka_ref, v_ref, gain_ref, havg_ref, o_ref, k_ref, kn_ref, *, tq, tk, n_tiles):
    b = pl.program_id(0)
    p = pl.program_id(1)
    qi = pl.program_id(2)
    lane_q = lax.broadcasted_iota(jnp.int32, (tq, LANES), 1)
    head0 = lane_q < HEAD_DIM

    @pl.when(qi == 0)
    def _():
        k_ref[0, :, 0:LANES] = kh_ref[0]
        k_ref[0, :, LANES:2 * LANES] = ka_ref[0]
        head0_s = lax.broadcasted_iota(jnp.int32, kh_ref.shape[1:], 1) < HEAD_DIM
        kn_ref[0], kn_ref[1] = _head_sq_norm_max(kh_ref[0], havg_ref[...], head0_s)

    q = q_ref[0]
    qa = qa_ref[0]

    qn = _head_sq_norm_max(q, havg_ref[...], head0)
    first_tile = []
    for j in range(2):
        h = 2 * p + j
        slack = 2.0 * jnp.sqrt(qn[j] * kn_ref[j])
        cq = cb_ref[((b * N_FOX + h) * 2 + 0) * n_tiles + qi]
        skipped = jnp.int32(0)
        for t in range(n_tiles - 1):
            ck = cb_ref[((b * N_FOX + h) * 2 + 1) * n_tiles + t]
            skipped = skipped + jnp.logical_and(t < qi, slack + cq - ck < UNDERFLOW_LOG2).astype(jnp.int32)
        first_tile.append(skipped)
    qops = []
    for j in range(2):
        qm = jnp.where(head0 if j == 0 else jnp.logical_not(head0), q, jnp.zeros_like(q))
        lo = (2 * p + j) * AUG_STRIDE
        own = jnp.logical_and(lane_q >= lo, lane_q < lo + AUG_STRIDE)
        qops.append(jnp.concatenate([qm, jnp.where(own, qa, jnp.zeros_like(qa))], axis=1))
    q_pos = qi * tq + lax.broadcasted_iota(jnp.int32, (tq, tk), 0)
    k_off = lax.broadcasted_iota(jnp.int32, (tq, tk), 1)
    head0_k = lax.broadcasted_iota(jnp.int32, (tk, LANES), 1) < HEAD_DIM
    one = jnp.ones((tk, LANES), BF16)

    def tile(j, t, carry, masked):
        m, acc = carry
        c0 = pl.multiple_of(t * tk, tk)
        s = _dot_nt(qops[j], k_ref[0, pl.ds(c0, tk), :])
        if masked:
            s = jnp.where(k_off + c0 <= q_pos, s, NEG)
        m_new = jnp.maximum(m, jnp.max(s, axis=-1, keepdims=True))
        pr = jnp.exp2(s - m_new)
        vj = jnp.where(head0_k if j == 0 else jnp.logical_not(head0_k), v_ref[0, pl.ds(c0, tk), :], one)
        acc = jnp.exp2(m - m_new) * acc + jnp.dot(pr.astype(BF16), vj, preferred_element_type=F32)
        return m_new, acc

    carry = []
    for j in range(2):
        t0 = first_tile[j]
        n_act = qi - t0
        cj = (jnp.full((tq, 1), NEG, F32), jnp.zeros((tq, LANES), F32))
        cj = lax.fori_loop(0, n_act // 2, lambda i, c, j=j, t0=t0: tile(j, t0 + 2 * i + 1, tile(j, t0 + 2 * i, c, False), False), cj)
        cj = lax.cond(n_act % 2 == 1, lambda c, j=j: tile(j, qi - 1, c, False), lambda c: c, cj)
        carry += list(cj)
    for j in range(2):
        carry[2 * j:2 * j + 2] = tile(j, qi, tuple(carry[2 * j:2 * j + 2]), True)
    o0 = carry[1] * (1.0 / pltpu.roll(carry[1], HEAD_DIM, 1))
    o1 = carry[3] * (1.0 / pltpu.roll(carry[3], HEAD_DIM, 1))
    out = jnp.where(head0, o0, o1)
    sq = out * out
    sq_hi = sq.astype(BF16)
    sq_lo = (sq - sq_hi.astype(F32)).astype(BF16)
    ms = jnp.dot(sq_hi, havg_ref[...], preferred_element_type=F32) + jnp.dot(sq_lo, havg_ref[...], preferred_element_type=F32)
    o_ref[0] = (out * lax.rsqrt(ms + RMS_EPS) * gain_ref[...]).astype(BF16)


def _fox(main3, q_aug, k_aug, cum_edges, gain_row, tq, tk):
    B, S, _ = main3.shape
    assert tq == tk, "the kernel handles exactly one diagonal tile per query block"
    head_of = np.arange(LANES) // HEAD_DIM
    havg = (head_of[:, None] == head_of[None, :]).astype(np.float32) / HEAD_DIM
    grid_spec = pltpu.PrefetchScalarGridSpec(
        num_scalar_prefetch=1,
        grid=(B, N_PAIR, S // tq),
        in_specs=[
            pl.BlockSpec((1, tq, LANES), lambda b, p, i, cb: (b, i, CB_FQ + p)),
            pl.BlockSpec((1, tq, LANES), lambda b, p, i, cb: (b, i, 0)),
            pl.BlockSpec((1, S, LANES), lambda b, p, i, cb: (b, 0, CB_FK + p)),
            pl.BlockSpec((1, S, LANES), lambda b, p, i, cb: (b, 0, 0)),
            pl.BlockSpec((1, S, LANES), lambda b, p, i, cb: (b, 0, CB_FV + p)),
            pl.BlockSpec((1, LANES), lambda b, p, i, cb: (0, p)),
            pl.BlockSpec((LANES, LANES), lambda b, p, i, cb: (0, 0)),
        ],
        out_specs=pl.BlockSpec((1, tq, LANES), lambda b, p, i, cb: (b, i, p)),
        scratch_shapes=[pltpu.VMEM((1, S, 2 * LANES), BF16), pltpu.SMEM((2,), F32)],
    )
    return pl.pallas_call(
        functools.partial(_fox_kernel, tq=tq, tk=tk, n_tiles=S // tk),
        grid_spec=grid_spec,
        out_shape=jax.ShapeDtypeStruct((B, S, FOX_W), BF16),
        compiler_params=_cparams(("parallel", "parallel", "arbitrary")),
        name="fox_attention",
    )(cum_edges, main3, q_aug, main3, k_aug, main3, gain_row, jnp.asarray(havg, BF16))


def _gelu_tanh(x):
    return 0.5 * x * (1.0 + jnp.tanh(np.sqrt(2.0 / np.pi).astype(np.float32) * (x + 0.044715 * (x * x * x))))


def _cmp_kernel(r_ref, pos_ref, wt_ref, wb_ref, w2_ref, o_ref):
    r = r_ref[0]
    n = r.shape[0]
    top = jnp.dot(r, wt_ref[...], preferred_element_type=F32)
    bot = jnp.dot(r, wb_ref[...], preferred_element_type=F32)
    cpos = jnp.dot(pos_ref[0:8, :], wt_ref[...], preferred_element_type=F32)
    cpos = cpos + jnp.dot(pos_ref[8:16, :], wb_ref[...], preferred_element_type=F32)
    hid = _gelu_tanh(top + pltpu.roll(bot, n - 1, 0) + cpos[0:1, :])
    o_ref[0] = jnp.dot(hid.astype(BF16), w2_ref[...], preferred_element_type=F32)


def _nsa_compress(r, pos2, wt, wb, w2):
    B, n, _ = r.shape
    width = CMP_STRIDE * LANES
    return pl.pallas_call(
        _cmp_kernel,
        grid=(B,),
        in_specs=[
            pl.BlockSpec((1, n, width), lambda b: (b, 0, 0)),
            pl.BlockSpec((16, width), lambda b: (0, 0)),
            pl.BlockSpec((width, 2 * CMP_HIDDEN), lambda b: (0, 0)),
            pl.BlockSpec((width, 2 * CMP_HIDDEN), lambda b: (0, 0)),
            pl.BlockSpec((2 * CMP_HIDDEN, LANES), lambda b: (0, 0)),
        ],
        out_specs=pl.BlockSpec((1, n, LANES), lambda b: (b, 0, 0)),
        out_shape=jax.ShapeDtypeStruct((B, n, LANES), F32),
        compiler_params=_cparams(("parallel",)),
        name="nsa_compress",
    )(r, pos2, wt, wb, w2)


def _softmax_rows(s, mask):
    sm = jnp.where(mask, s, NEG)
    m = jnp.maximum(jnp.max(sm, axis=-1, keepdims=True), 0.1 * NEG)
    e = jnp.exp2(sm - m)
    l = jnp.sum(e, axis=-1, keepdims=True)
    return e / jnp.where(l > 0.0, l, 1.0)


def _nsa_kernel(q_ref, cmp_ref, c2s_ref, sel_ref, noh_ref, win_ref, gl_ref, gain_ref, gsel_ref, havg_ref, o_ref, ksa_ref, *, nb, tq, tk, n_cmp,
                n_sel, top_n):
    H = N_NSA
    qi = pl.program_id(1)
    start = qi * tq

    @pl.when(qi == 0)
    def _():
        key_lanes = lax.broadcasted_iota(jnp.int32, noh_ref.shape, 1) < HEAD_DIM
        for bb in range(nb):
            ksa_ref[bb] = jnp.where(key_lanes, sel_ref[bb], noh_ref[...])

    t_row = start + lax.broadcasted_iota(jnp.int32, (tq, 1), 0)
    t_all = jnp.concatenate([t_row] * H, axis=0)
    zeros64 = jnp.zeros((tq, HEAD_DIM), BF16)
    wlen = WINDOW + tq
    base = pl.multiple_of(jnp.maximum(start - WINDOW, 0), tq)
    n_pad = cmp_ref.shape[1]
    assert CMP_STRIDE == 16 and tq & (tq - 1) == 0
    last_cmp = (t_all - (CMP_LEN - 1)) >> 4
    row_in_block = lax.broadcasted_iota(jnp.int32, (H * tq, wlen), 0) & (tq - 1)
    win_lag = (start - base) - (lax.broadcasted_iota(jnp.int32, (H * tq, wlen), 1) - row_in_block)

    def prologue(bb):
        q = q_ref[bb]
        q_heads = [q[:, h * HEAD_DIM:(h + 1) * HEAD_DIM] for h in range(H)]
        q_pad = jnp.concatenate([jnp.concatenate([qh, zeros64], axis=1) for qh in q_heads], axis=0)

        win = win_ref[bb, pl.ds(base, wlen), :]
        s_w = _dot_nt(q_pad, win)
        s_w = jnp.where(lax.bitcast_convert_type(win_lag, jnp.uint32) < WINDOW, s_w, NEG)
        e_w = jnp.exp2(s_w - jnp.max(s_w, axis=-1, keepdims=True))
        win1 = jnp.where(lax.broadcasted_iota(jnp.int32, (wlen, LANES), 1) < HEAD_DIM, jnp.ones((wlen, LANES), BF16), win)
        o_w = jnp.dot(e_w.astype(BF16), win1, preferred_element_type=F32)

        cmp = cmp_ref[bb]
        cmp_hi = cmp.astype(BF16)
        cmp_lo = (cmp - cmp_hi.astype(F32)).astype(BF16)
        s_c = _dot_nt(q_pad, cmp_hi) + _dot_nt(q_pad, cmp_lo)
        n_id = lax.broadcasted_iota(jnp.int32, (H * tq, n_pad), 1)
        p_c = _softmax_rows(s_c, n_id <= last_cmp)
        o_c = jnp.dot(p_c.astype(BF16), cmp_hi, preferred_element_type=F32)

        p_sum = p_c[0:tq]
        for h in range(1, H):
            p_sum = p_sum + p_c[h * tq:(h + 1) * tq]
        p_hi = p_sum.astype(BF16)
        p_lo = (p_sum - p_hi.astype(F32)).astype(BF16)
        c2s_t = c2s_ref[...]
        imp = _dot_nt(c2s_t, p_hi) + _dot_nt(c2s_t, p_lo)
        t_lane = start + lax.broadcasted_iota(jnp.int32, (HEAD_DIM, tq), 1)
        s_id = lax.broadcasted_iota(jnp.int32, (HEAD_DIM, tq), 0)
        cur = t_lane >> 6
        forced = jnp.logical_or(s_id == 0, jnp.logical_or(s_id == cur, s_id == cur - 1))
        score = jnp.where(forced, FORCED_SCORE, imp)
        score = jnp.where(s_id * SEL_LEN <= t_lane, score, -1.0)
        SUB = 8
        groups = [score[g * SUB:(g + 1) * SUB, :] for g in range(HEAD_DIM // SUB)]
        ranks = [jnp.zeros((SUB, tq), jnp.int32) for _ in groups]
        sub_id = lax.broadcasted_iota(jnp.int32, (SUB, tq), 0)
        for c in range(n_sel):
            row = score[c:c + 1, :]
            for g in range(len(groups)):
                if g * SUB > c:
                    before = row >= groups[g]
                elif g * SUB + SUB - 1 < c:
                    before = row > groups[g]
                else:
                    before = jnp.logical_or(row > groups[g], jnp.logical_and(row == groups[g], sub_id > c - g * SUB))
                ranks[g] = ranks[g] + before.astype(jnp.int32)
        not_sel = jnp.concatenate([jnp.where(r < top_n, 0.0, 1.0) for r in ranks], axis=0)
        ns = jnp.concatenate([jnp.zeros((HEAD_DIM, tq), F32), not_sel], axis=0).T.astype(BF16)
        q_aug = q_pad + jnp.concatenate([ns] * H, axis=0)
        return q_aug, o_c, o_w

    pro = [prologue(bb) for bb in range(nb)]

    def sel_tile(bb, c0, size, carry, diagonal):
        m, acc = carry
        s = _dot_nt(pro[bb][0], ksa_ref[bb, pl.ds(c0, size), :])
        if diagonal:
            r_id = lax.broadcasted_iota(jnp.int32, (H * tq, size), 0) & (tq - 1)
            c_id = lax.broadcasted_iota(jnp.int32, (H * tq, size), 1)
            s = jnp.where(c_id <= r_id, s, NEG)
        m_new = jnp.maximum(m, jnp.max(s, axis=-1, keepdims=True))
        pr = jnp.exp2(s - m_new)
        key_lanes = lax.broadcasted_iota(jnp.int32, (size, LANES), 1) < HEAD_DIM
        v1 = jnp.where(key_lanes, jnp.ones((size, LANES), BF16), sel_ref[bb, pl.ds(c0, size), :])
        acc = jnp.exp2(m - m_new) * acc + jnp.dot(pr.astype(BF16), v1, preferred_element_type=F32)
        return m_new, acc

    def all_rows(c0, size, carries, diagonal):
        return tuple(sel_tile(bb, c0, size, carries[bb], diagonal) for bb in range(nb))

    carries = tuple((jnp.full((H * tq, 1), NEG, F32), jnp.zeros((H * tq, LANES), F32)) for _ in range(nb))
    n_full = start // tk
    carries = lax.fori_loop(0, n_full, lambda t, c: all_rows(pl.multiple_of(t * tk, tk), tk, c, False), carries)
    n_tail = (start - n_full * tk) // tq
    for d in range(tk // tq - 1):
        c0 = pl.multiple_of(n_full * tk + d * tq, tq)
        carries = lax.cond(d < n_tail, lambda c, c0=c0: all_rows(c0, tq, c, False), lambda c: c, carries)
    carries = all_rows(pl.multiple_of(start, tq), tq, carries, True)

    first_half = lax.broadcasted_iota(jnp.int32, (tq, LANES), 1) < HEAD_DIM

    def place(acc, normalise):
        cols = []
        for h in range(0, H, 2):
            a0, a1 = acc[h * tq:(h + 1) * tq], acc[(h + 1) * tq:(h + 2) * tq]
            r0 = pltpu.roll(a0, HEAD_DIM, 1)
            if normalise:
                v0 = r0 * (1.0 / a0)
                v1 = a1 * (1.0 / pltpu.roll(a1, HEAD_DIM, 1))
            else:
                v0, v1 = r0, a1
            cols.append(jnp.where(first_half, v0, v1))
        return jnp.concatenate(cols, axis=1)

    for bb in range(nb):
        _, o_c, o_w = pro[bb]
        gates = 1.0 / (1.0 + jnp.exp(-gl_ref[bb]))
        g_hi = gates.astype(BF16)
        g_lo = (gates - g_hi.astype(F32)).astype(BF16)
        mix = None
        for c, y in enumerate((place(o_c, False), place(carries[bb][1], True), place(o_w, True))):
            g = jnp.dot(g_hi, gsel_ref[c], preferred_element_type=F32) + jnp.dot(g_lo, gsel_ref[c], preferred_element_type=F32)
            mix = g * y if mix is None else mix + g * y
        sq = mix * mix
        sq_hi = sq.astype(BF16)
        sq_lo = (sq - sq_hi.astype(F32)).astype(BF16)
        ms = jnp.dot(sq_hi, havg_ref[...], preferred_element_type=F32) + jnp.dot(sq_lo, havg_ref[...], preferred_element_type=F32)
        o_ref[bb] = (mix * lax.rsqrt(ms + RMS_EPS) * gain_ref[...]).astype(BF16)


def _nsa(main3, cmp, c2s, neg_onehot, small3, gain_row, nb, tq, tk, n_cmp):
    B, S, _ = main3.shape
    n_sel = S // SEL_LEN
    n_pad = cmp.shape[1]
    gsel = np.zeros((3, LANES, NSA_W), np.float32)
    for h in range(N_NSA):
        for c in range(3):
            gsel[c, SMALL_GATE + 3 * h + c, h * HEAD_DIM:(h + 1) * HEAD_DIM] = 1.0
    head_of = np.arange(NSA_W) // HEAD_DIM
    havg = (head_of[:, None] == head_of[None, :]).astype(np.float32) / HEAD_DIM
    return pl.pallas_call(
        functools.partial(_nsa_kernel, nb=nb, tq=tq, tk=tk, n_cmp=n_cmp, n_sel=n_sel, top_n=min(TOP_N, n_sel)),
        grid=(B // nb, S // tq),
        in_specs=[
            pl.BlockSpec((nb, tq, NSA_W), lambda b, i: (b, i, CB_NQ // 2)),
            pl.BlockSpec((nb, n_pad, LANES), lambda b, i: (b, 0, 0)),
            pl.BlockSpec((HEAD_DIM, n_pad), lambda b, i: (0, 0)),
            pl.BlockSpec((nb, S, LANES), lambda b, i: (b, 0, CB_SEL)),
            pl.BlockSpec((S, LANES), lambda b, i: (0, 0)),
            pl.BlockSpec((nb, S, LANES), lambda b, i: (b, 0, CB_WIN)),
            pl.BlockSpec((nb, tq, LANES), lambda b, i: (b, i, 0)),
            pl.BlockSpec((1, NSA_W), lambda b, i: (0, 0)),
            pl.BlockSpec((3, LANES, NSA_W), lambda b, i: (0, 0, 0)),
            pl.BlockSpec((NSA_W, NSA_W), lambda b, i: (0, 0)),
        ],
        out_specs=pl.BlockSpec((nb, tq, NSA_W), lambda b, i: (b, i, 0)),
        out_shape=jax.ShapeDtypeStruct((B, S, NSA_W), BF16),
        scratch_shapes=[pltpu.VMEM((nb, S, LANES), BF16)],
        compiler_params=_cparams(("parallel", "arbitrary")),
        name="nsa_attention",
    )(main3, cmp, c2s, main3, neg_onehot, main3, small3, gain_row, jnp.asarray(gsel, BF16), jnp.asarray(havg, BF16))


def _post_kernel(x_ref, yr_ref, yf_ref, yn_ref, wo_ref, g_ref, w1_ref, w2_ref, gf_ref, o_ref, *, ff_chunk, final):
    x = x_ref[...]
    x = x + jnp.dot(yr_ref[...], wo_ref[0:RET_W, :], preferred_element_type=F32)
    x = x + jnp.dot(yf_ref[...], wo_ref[RET_W:RET_W + FOX_W, :], preferred_element_type=F32)
    x = x + jnp.dot(yn_ref[...], wo_ref[RET_W + FOX_W:, :], preferred_element_type=F32)
    h = _rms(x, g_ref[...]).astype(BF16)
    o_ref[...] = x
    for c0 in range(0, D_FF, ff_chunk):
        hid = jnp.maximum(jnp.dot(h, w1_ref[:, c0:c0 + ff_chunk], preferred_element_type=F32), 0.0)
        o_ref[...] += jnp.dot((hid * hid).astype(BF16), w2_ref[c0:c0 + ff_chunk, :], preferred_element_type=F32)
    if final:
        o_ref[...] = _rms(o_ref[...], gf_ref[...])


def _post(x2d, yr, yf, yn, wo, g, w1, w2, gf, tm, final):
    T = x2d.shape[0]
    row = lambda w: pl.BlockSpec((tm, w), lambda i: (i, 0))
    return pl.pallas_call(
        functools.partial(_post_kernel, ff_chunk=512, final=final),
        grid=(T // tm,),
        in_specs=[
            row(D_MODEL), row(RET_W), row(FOX_W), row(NSA_W),
            _resident((D_MODEL, D_MODEL), lambda i: (0, 0)),
            _resident((1, D_MODEL), lambda i: (0, 0)),
            _resident((D_MODEL, D_FF), lambda i: (0, 0)),
            _resident((D_FF, D_MODEL), lambda i: (0, 0)),
            _resident((1, D_MODEL), lambda i: (0, 0)),
        ],
        out_specs=row(D_MODEL),
        out_shape=jax.ShapeDtypeStruct((T, D_MODEL), F32),
        compiler_params=_cparams(("parallel",)),
        name="outproj_mlp",
    )(x2d, yr, yf, yn, wo, g, w1, w2, gf)


def _tile_sizes(batch):
    return dict(tm=512, fox_tq=512, fox_tk=512, nsa_tq=256, nsa_tk=512, nsa_nb=2 if batch % 2 == 0 else 1)


def kernel(x, norm_attn, w_in, fox_forget_bias, ret_norm_gain, fox_norm_gain, nsa_norm_gain, nsa_cmp_pos_k, nsa_cmp_pos_v, nsa_cmp_w1_k, nsa_cmp_w2_k, nsa_cmp_w1_v, nsa_cmp_w2_v, w_out, norm_mlp, w_mlp_in, w_mlp_out, norm_final):
    B, S, D = x.shape
    depth = w_in.shape[0]
    assert D == D_MODEL and S % 512 == 0 and S >= WINDOW + 256
    T = B * S
    ts = _tile_sizes(B)

    cols, scale, small_cols, small_mask = _in_proj_columns()
    wm = (w_in[:, :, cols] * scale).astype(BF16)
    ws = (w_in[:, :, small_cols] * small_mask).astype(BF16)
    wo = w_out.astype(BF16)
    w1 = w_mlp_in.astype(BF16)
    w2 = w_mlp_out.astype(BF16)

    def expand_w1(wk, wv):
        L = wk.shape[0]
        wk = wk.reshape(L, CMP_LEN, HEAD_DIM, CMP_HIDDEN)
        wv = wv.reshape(L, CMP_LEN, HEAD_DIM, CMP_HIDDEN)
        z = jnp.zeros_like(wk)
        full = jnp.concatenate([jnp.concatenate([wk, z], axis=-1), jnp.concatenate([z, wv], axis=-1)], axis=2)
        full = full.reshape(L, CMP_LEN * LANES, 2 * CMP_HIDDEN).astype(BF16)
        return full[:, :CMP_STRIDE * LANES], full[:, CMP_STRIDE * LANES:]

    wt_all, wb_all = expand_w1(nsa_cmp_w1_k, nsa_cmp_w1_v)
    zk = jnp.zeros_like(nsa_cmp_w2_k)
    w2c = jnp.concatenate([jnp.concatenate([nsa_cmp_w2_k, zk], axis=-1),
                           jnp.concatenate([zk, nsa_cmp_w2_v], axis=-1)], axis=1).astype(BF16)
    pos = jnp.concatenate([nsa_cmp_pos_k, nsa_cmp_pos_v], axis=-1)
    pos_tb = pos.reshape(depth, 2, 1, CMP_STRIDE * LANES)
    pos_tb = jnp.broadcast_to(pos_tb, (depth, 2, 8, CMP_STRIDE * LANES)).reshape(depth, 16, CMP_STRIDE * LANES).astype(BF16)

    n_cmp = (S - CMP_LEN) // CMP_STRIDE + 1
    n_sel = S // SEL_LEN
    n_rows = S // CMP_STRIDE
    cs = np.arange(n_rows) * CMP_STRIDE
    ss = np.arange(n_sel) * SEL_LEN
    overlap = np.clip(np.minimum(cs[:, None] + CMP_LEN, ss[None, :] + SEL_LEN) - np.maximum(cs[:, None], ss[None, :]), 0, None)
    overlap[n_cmp:] = 0
    c2s_np = np.zeros((HEAD_DIM, n_rows), np.float32)
    c2s_np[:n_sel] = (overlap / CMP_LEN).T
    c2s = jnp.asarray(c2s_np, dtype=BF16)
    assert n_sel <= HEAD_DIM, "the selected-branch key augmentation has 64 lanes, one per selection block"
    neg_onehot = jnp.asarray(np.where((np.arange(S)[:, None] // SEL_LEN) == np.arange(LANES)[None, :] - HEAD_DIM, NEG, 0.0), dtype=BF16)

    tables = _retention_tables(S)
    fbias = jnp.zeros((depth, 1, LANES), F32).at[:, 0, SMALL_FF:SMALL_FF + N_FOX].set(fox_forget_bias)

    x2d = x.reshape(T, D)
    for l in range(depth):
        main, small = _inproj(x2d, norm_attn[l][None, :], wm[l], ws[l], ts["tm"])
        main3 = main.reshape(B, S, N_MAIN)
        small3 = small.reshape(B, S, LANES)
        fox_qa, fox_ka, edges = _forget_cumsum(small3, fbias[l])
        y_ret = _retention(main3, tables, ret_norm_gain[l][None, :])
        per_tile = ts["fox_tk"] // RET_CHUNK
        cq = edges[:, 0::per_tile, 0, SMALL_FF:SMALL_FF + N_FOX]
        ck = edges[:, per_tile - 1::per_tile, 1, SMALL_FF:SMALL_FF + N_FOX]
        cum_edges = jnp.stack([cq.transpose(0, 2, 1), ck.transpose(0, 2, 1)], axis=2).reshape(-1)
        y_fox = _fox(main3, fox_qa, fox_ka, cum_edges, fox_norm_gain[l][None, :], ts["fox_tq"], ts["fox_tk"])
        r = main3[:, :, CB_CMP * LANES:(CB_CMP + 1) * LANES].reshape(B, n_rows, CMP_STRIDE * LANES)
        cmp = _nsa_compress(r, pos_tb[l], wt_all[l], wb_all[l], w2c[l])
        y_nsa = _nsa(main3, cmp, c2s, neg_onehot, small3, nsa_norm_gain[l][None, :], ts["nsa_nb"], ts["nsa_tq"], ts["nsa_tk"], n_cmp)
        x2d = _post(x2d, y_ret.reshape(T, RET_W), y_fox.reshape(T, FOX_W), y_nsa.reshape(T, NSA_W), wo[l],
                    norm_mlp[l][None, :], w1[l], w2[l], norm_final[None, :], ts["tm"], final=(l == depth - 1))
    return x2d.reshape(B, S, D)
```

```python
import functools

import numpy as np
import jax
import jax.numpy as jnp
from jax import lax
from jax.experimental import pallas as pl
from jax.experimental.pallas import tpu as pltpu

F32 = jnp.float32
BF16 = jnp.bfloat16

D_MODEL = 1024
HEAD_DIM = 64
N_RET = 6
N_FOX = 6
N_NSA = 4
RET_W = N_RET * HEAD_DIM
FOX_W = N_FOX * HEAD_DIM
NSA_W = N_NSA * HEAD_DIM
D_FF = 4 * D_MODEL
RMS_EPS = 1e-6
RET_CHUNK = 128
RET_BLOCK = 256
ROPE_BASE = 10000.0
CMP_LEN = 32
CMP_STRIDE = 16
CMP_HIDDEN = 4 * HEAD_DIM
SEL_LEN = 64
TOP_N = 16
WINDOW = 512
FORCED_SCORE = 1e4
NEG = -1e30
LOG2E = float(np.log2(np.e))

LANES = 128
N_PAIR = N_RET // 2

CB_RQ, CB_RK, CB_RV, CB_RG = 0, 3, 6, 9
CB_NQ = 12
CB_FQ, CB_FK, CB_FV = 14, 17, 20
CB_CMP, CB_SEL, CB_WIN = 23, 24, 25
N_MAIN = 26 * LANES
SMALL_FF = 0
SMALL_GATE = 8

VMEM_LIMIT = 56 * 1024 * 1024


def _cparams(sem):
    return pltpu.CompilerParams(dimension_semantics=sem, vmem_limit_bytes=VMEM_LIMIT)


def _resident(shape, index_map):
    return pl.BlockSpec(shape, index_map, pipeline_mode=pl.Buffered(1))


def _in_proj_columns():
    sizes = (RET_W, RET_W, RET_W, RET_W, FOX_W, FOX_W, FOX_W, N_FOX, NSA_W) + (HEAD_DIM,) * 6 + (3 * N_NSA,)
    off = np.concatenate([[0], np.cumsum(sizes)])
    (o_rq, o_rk, o_rv, o_rg, o_fq, o_fk, o_fv, o_ff, o_nq, o_kc, o_vc, o_ks, o_vs, o_kw, o_vw, o_gt) = off[:-1]
    half = HEAD_DIM // 2
    inter = []
    for p in range(N_PAIR):
        a, b = 2 * p, 2 * p + 1
        for h, part in ((a, 0), (b, 0), (a, 1), (b, 1)):
            inter.extend(range(h * HEAD_DIM + part * half, h * HEAD_DIM + (part + 1) * half))
    inter = np.asarray(inter)
    nat = np.arange(RET_W)
    cols, scale = [], []

    def add(idx, s=1.0):
        cols.append(np.asarray(idx))
        scale.append(np.full(len(idx), s, np.float32))

    qk_scale = HEAD_DIM ** -0.5
    sm_scale = qk_scale * LOG2E
    add(o_rq + inter)
    add(o_rk + inter, qk_scale)
    add(o_rv + nat)
    add(o_rg + nat)
    add(o_nq + np.arange(NSA_W), sm_scale)
    add(o_fq + nat, sm_scale)
    add(o_fk + nat)
    add(o_fv + nat)
    for o in (o_kc, o_vc, o_ks, o_vs, o_kw, o_vw):
        add(o + np.arange(HEAD_DIM))
    cols = np.concatenate(cols)
    scale = np.concatenate(scale)
    assert cols.shape[0] == N_MAIN
    small_cols = np.zeros(LANES, np.int64)
    small_mask = np.zeros(LANES, np.float32)
    small_cols[SMALL_FF:SMALL_FF + N_FOX] = o_ff + np.arange(N_FOX)
    small_mask[SMALL_FF:SMALL_FF + N_FOX] = 1.0
    small_cols[SMALL_GATE:SMALL_GATE + 3 * N_NSA] = o_gt + np.arange(3 * N_NSA)
    small_mask[SMALL_GATE:SMALL_GATE + 3 * N_NSA] = 1.0
    return cols, scale, small_cols, small_mask


def _retention_tables(seq):
    half = HEAD_DIM // 2
    inv = 1.0 / (ROPE_BASE ** (jnp.arange(half, dtype=F32) / half))
    ang = jnp.arange(seq, dtype=F32)[:, None] * inv[None, :]
    cos, sin = jnp.cos(ang), jnp.sin(ang)
    cos_t = jnp.concatenate([cos, cos, cos, cos], axis=-1)
    sin_t = jnp.concatenate([-sin, -sin, sin, sin], axis=-1)
    log_gamma = np.log(1.0 - 2.0 ** (-5.0 - np.arange(N_RET, dtype=np.float32))).astype(np.float32)
    C = RET_BLOCK
    idx = np.arange(C, dtype=np.float32)
    diff = idx[:, None] - idx[None, :]
    lane = np.arange(LANES)
    head_k = (lane % HEAD_DIM) // half
    head_v = lane // HEAD_DIM
    d_in = np.zeros((N_PAIR, 2, C, C), np.float32)
    d_k = np.zeros((N_PAIR, C, LANES), np.float32)
    d_q = np.zeros((N_PAIR, C, LANES), np.float32)
    d_c = np.zeros((N_PAIR, LANES, LANES), np.float32)
    for p in range(N_PAIR):
        lg = log_gamma[2 * p:2 * p + 2]
        for j in range(2):
            d_in[p, j] = np.where(diff >= 0, np.exp(lg[j] * np.maximum(diff, 0.0)), 0.0)
        d_k[p] = np.exp(lg[head_k][None, :] * (C - 1.0 - idx)[:, None])
        d_q[p] = np.exp(lg[head_v][None, :] * (idx + 1.0)[:, None])
        d_c[p] = np.broadcast_to(np.exp(lg[head_v] * C)[None, :], (LANES, LANES))
    s_mask = (head_k[:, None] == head_v[None, :]).astype(np.float32)
    return cos_t, sin_t, jnp.asarray(d_in), jnp.asarray(d_k), jnp.asarray(d_q), jnp.asarray(d_c), jnp.asarray(s_mask)


def _rms(x, g):
    return x * lax.rsqrt(jnp.mean(x * x, axis=-1, keepdims=True) + RMS_EPS) * g


def _inproj_kernel(x_ref, g_ref, wm_ref, ws_ref, om_ref, os_ref, *, n_chunk):
    h = _rms(x_ref[...], g_ref[...]).astype(BF16)
    for n0 in range(0, N_MAIN, n_chunk):
        n1 = min(n0 + n_chunk, N_MAIN)
        om_ref[:, n0:n1] = jnp.dot(h, wm_ref[:, n0:n1], preferred_element_type=F32).astype(BF16)
    os_ref[...] = jnp.dot(h, ws_ref[...], preferred_element_type=F32)


def _inproj(x2d, g, wm, ws, tm):
    T = x2d.shape[0]
    return pl.pallas_call(
        functools.partial(_inproj_kernel, n_chunk=512),
        grid=(T // tm,),
        in_specs=[
            pl.BlockSpec((tm, D_MODEL), lambda i: (i, 0)),
            _resident((1, D_MODEL), lambda i: (0, 0)),
            _resident((D_MODEL, N_MAIN), lambda i: (0, 0)),
            _resident((D_MODEL, LANES), lambda i: (0, 0)),
        ],
        out_specs=[
            pl.BlockSpec((tm, N_MAIN), lambda i: (i, 0)),
            pl.BlockSpec((tm, LANES), lambda i: (i, 0)),
        ],
        out_shape=[jax.ShapeDtypeStruct((T, N_MAIN), BF16), jax.ShapeDtypeStruct((T, LANES), F32)],
        compiler_params=_cparams(("parallel",)),
        name="inproj",
    )(x2d, g, wm, ws)


N_SPLIT = 3
AUG_STRIDE = 2 * N_SPLIT


def _split_bf16(x):
    terms, rest = [], x
    for _ in range(N_SPLIT):
        t = rest.astype(BF16)
        rest = rest - t.astype(F32)
        terms.append(t)
    return terms


def _bias_placement():
    mq = np.zeros((N_SPLIT, LANES, LANES), np.float32)
    mk = np.zeros((N_SPLIT, LANES, LANES), np.float32)
    one_q = np.zeros((1, LANES), np.float32)
    one_k = np.zeros((1, LANES), np.float32)
    for h in range(N_FOX):
        base = h * AUG_STRIDE
        for i in range(N_SPLIT):
            mq[i, SMALL_FF + h, base + i] = 1.0
            mk[i, SMALL_FF + h, base + N_SPLIT + i] = -1.0
            one_q[0, base + N_SPLIT + i] = 1.0
            one_k[0, base + i] = 1.0
    return jnp.asarray(mq, BF16), jnp.asarray(mk, BF16), jnp.asarray(one_q), jnp.asarray(one_k)


def _cum_kernel(z_ref, b_ref, tri_ref, mq_ref, mk_ref, oq_ref, ok_ref, qa_ref, ka_ref, edge_ref, *, n_chunks):
    C = RET_CHUNK

    def chunk(c, carry):
        r0 = pl.multiple_of(c * C, C)
        z = z_ref[0, pl.ds(r0, C), :] + b_ref[...]
        lf = jnp.minimum(z, 0.0) - jnp.log1p(jnp.exp(-jnp.abs(z)))
        terms = _split_bf16(lf)
        local = jnp.dot(tri_ref[...], terms[0], preferred_element_type=F32)
        for term in terms[1:]:
            local = local + jnp.dot(tri_ref[...], term, preferred_element_type=F32)
        cs = local + carry
        cs2 = cs * LOG2E
        edge_ref[0, pl.ds(c, 1)] = jnp.concatenate([cs2[0:1], cs2[C - 1:C], jnp.zeros((6, LANES), F32)], axis=0)[None]
        qa, ka = oq_ref[...], ok_ref[...]
        for i, term in enumerate(_split_bf16(cs2)):
            qa = qa + jnp.dot(term, mq_ref[i], preferred_element_type=F32)
            ka = ka + jnp.dot(term, mk_ref[i], preferred_element_type=F32)
        qa_ref[0, pl.ds(r0, C), :] = qa.astype(BF16)
        ka_ref[0, pl.ds(r0, C), :] = ka.astype(BF16)
        return cs[C - 1:C, :]

    unroll = max(u for u in (1, 2, 4) if n_chunks % u == 0)

    def body(i, carry):
        for u in range(unroll):
            carry = chunk(i * unroll + u, carry)
        return carry

    lax.fori_loop(0, n_chunks // unroll, body, jnp.zeros((1, LANES), F32))


def _forget_cumsum(small3, bias_row):
    B, S, _ = small3.shape
    tri = jnp.asarray(np.tril(np.ones((RET_CHUNK, RET_CHUNK), np.float32)), BF16)
    mq, mk, one_q, one_k = _bias_placement()
    const = lambda shape: pl.BlockSpec(shape, lambda b: (0,) * len(shape))
    return pl.pallas_call(
        functools.partial(_cum_kernel, n_chunks=S // RET_CHUNK),
        grid=(B,),
        in_specs=[
            pl.BlockSpec((1, S, LANES), lambda b: (b, 0, 0)),
            const((1, LANES)), const((RET_CHUNK, RET_CHUNK)),
            const((N_SPLIT, LANES, LANES)), const((N_SPLIT, LANES, LANES)), const((1, LANES)), const((1, LANES)),
        ],
        out_specs=[
            pl.BlockSpec((1, S, LANES), lambda b: (b, 0, 0)),
            pl.BlockSpec((1, S, LANES), lambda b: (b, 0, 0)),
            pl.BlockSpec((1, S // RET_CHUNK, 8, LANES), lambda b: (b, 0, 0, 0)),
        ],
        out_shape=[jax.ShapeDtypeStruct((B, S, LANES), BF16), jax.ShapeDtypeStruct((B, S, LANES), BF16),
                   jax.ShapeDtypeStruct((B, S // RET_CHUNK, 8, LANES), F32)],
        compiler_params=_cparams(("parallel",)),
        name="forget_cumsum",
    )(small3, bias_row, tri, mq, mk, one_q, one_k)


def _dot_nt(a, b):
    return lax.dot_general(a, b, (((1,), (1,)), ((), ())), preferred_element_type=F32)


def _ret_kernel(q_ref, k_ref, v_ref, g_ref, cos_ref, sin_ref, din_ref, dk_ref, dq_ref, dc_ref, sm_ref, gain_ref,
                o_ref, state_ref, *, n_chunks, unroll):
    C = RET_BLOCK
    half = HEAD_DIM // 2
    lane = lax.broadcasted_iota(jnp.int32, (C, LANES), 1)
    khead = (lane % HEAD_DIM) // half
    vhead0 = lane < HEAD_DIM
    state_ref[...] = jnp.zeros_like(state_ref)

    def pair_chunk(p, r0):
        rows = pl.ds(r0, C)
        lanes = slice(p * LANES, (p + 1) * LANES)
        cs, sn = cos_ref[rows, :], sin_ref[rows, :]
        q = q_ref[0, rows, lanes].astype(F32)
        k = k_ref[0, rows, lanes].astype(F32)
        q = q * cs + pltpu.roll(q, HEAD_DIM, 1) * sn
        k = k * cs + pltpu.roll(k, HEAD_DIM, 1) * sn
        v = v_ref[0, rows, lanes]
        qb, kb = q.astype(BF16), k.astype(BF16)
        state = state_ref[p]
        cross = jnp.dot(qb, state.astype(BF16), preferred_element_type=F32) * dq_ref[p]
        outs = []
        for j in range(2):
            qm = jnp.where(khead == j, qb, jnp.zeros_like(qb))
            inner = _dot_nt(qm, kb) * din_ref[p, j]
            outs.append(jnp.dot(inner.astype(BF16), v, preferred_element_type=F32))
        out = jnp.where(vhead0, outs[0], outs[1]) + cross
        kd = (k * dk_ref[p]).T.astype(BF16)
        state_ref[p] = state * dc_ref[p] + jnp.dot(kd, v, preferred_element_type=F32) * sm_ref[...]
        inv = 1.0 / HEAD_DIM
        s0 = jnp.sum(jnp.where(vhead0, out, 0.0), axis=-1, keepdims=True)
        s1 = jnp.sum(jnp.where(vhead0, 0.0, out), axis=-1, keepdims=True)
        yc = out - jnp.where(vhead0, s0, s1) * inv
        yc2 = yc * yc
        v0 = jnp.sum(jnp.where(vhead0, yc2, 0.0), axis=-1, keepdims=True)
        v1 = jnp.sum(jnp.where(vhead0, 0.0, yc2), axis=-1, keepdims=True)
        y = yc * lax.rsqrt(jnp.where(vhead0, v0, v1) * inv + RMS_EPS) * gain_ref[:, lanes]
        g = g_ref[0, rows, lanes].astype(F32)
        o_ref[0, rows, lanes] = (y * (g * (1.0 / (1.0 + jnp.exp(-g))))).astype(BF16)

    def body(c, _):
        for u in range(unroll):
            for p in range(N_PAIR):
                pair_chunk(p, pl.multiple_of((c * unroll + u) * C, C))
        return 0

    lax.fori_loop(0, n_chunks // unroll, body, 0)


def _retention(main3, tables, gain_row):
    B, S, _ = main3.shape
    cos_t, sin_t, d_in, d_k, d_q, d_c, s_mask = tables
    C = RET_BLOCK
    n_chunks = S // C
    seq_spec = lambda cb: pl.BlockSpec((1, S, RET_W), lambda b: (b, 0, cb // N_PAIR))
    const = lambda shape: pl.BlockSpec(shape, lambda b: (0,) * len(shape))
    return pl.pallas_call(
        functools.partial(_ret_kernel, n_chunks=n_chunks, unroll=2 if n_chunks % 2 == 0 else 1),
        grid=(B,),
        in_specs=[
            seq_spec(CB_RQ), seq_spec(CB_RK), seq_spec(CB_RV), seq_spec(CB_RG),
            const((S, LANES)), const((S, LANES)),
            const((N_PAIR, 2, C, C)), const((N_PAIR, C, LANES)), const((N_PAIR, C, LANES)),
            const((N_PAIR, LANES, LANES)), const((LANES, LANES)), const((1, RET_W)),
        ],
        out_specs=pl.BlockSpec((1, S, RET_W), lambda b: (b, 0, 0)),
        out_shape=jax.ShapeDtypeStruct((B, S, RET_W), BF16),
        scratch_shapes=[pltpu.VMEM((N_PAIR, LANES, LANES), F32)],
        compiler_params=_cparams(("parallel",)),
        name="retention",
    )(main3, main3, main3, main3, cos_t, sin_t, d_in, d_k, d_q, d_c, s_mask, gain_row)


UNDERFLOW_LOG2 = -170.0


def _head_sq_norm_max(x, havg, head0):
    xf = x.astype(F32)
    sq = xf * xf
    hi = sq.astype(BF16)
    lo = (sq - hi.astype(F32)).astype(BF16)
    mean = jnp.dot(hi, havg, preferred_element_type=F32) + jnp.dot(lo, havg, preferred_element_type=F32)
    bound = mean * (HEAD_DIM * 1.01)
    return jnp.max(jnp.where(head0, bound, 0.0)), jnp.max(jnp.where(head0, 0.0, bound))


def _fox_kernel(cb_ref, q_ref, qa_ref, qs_ref, kh_ref, ka_ref, v_ref, gain_ref, havg_ref, o_ref, k_ref, ft_ref, *, tq, tk, n_tiles):
    b = pl.program_id(0)
    p = pl.program_id(1)
    qi = pl.program_id(2)
    lane_q = lax.broadcasted_iota(jnp.int32, (tq, LANES), 1)
    head0 = lane_q < HEAD_DIM

    @pl.when(qi == 0)
    def _():
        k_ref[0, :, 0:LANES] = kh_ref[0]
        k_ref[0, :, LANES:2 * LANES] = ka_ref[0]
        head0_s = lax.broadcasted_iota(jnp.int32, kh_ref.shape[1:], 1) < HEAD_DIM
        kn = _head_sq_norm_max(kh_ref[0], havg_ref[...], head0_s)
        for i in range(n_tiles):
            qn = _head_sq_norm_max(qs_ref[0, i * tq:(i + 1) * tq, :], havg_ref[...], head0)
            for j in range(2):
                h = 2 * p + j
                slack = 2.0 * jnp.sqrt(qn[j] * kn[j])
                cq = cb_ref[((b * N_FOX + h) * 2 + 0) * n_tiles + i]
                skipped = jnp.int32(0)
                for t in range(i):
                    ck = cb_ref[((b * N_FOX + h) * 2 + 1) * n_tiles + t]
                    skipped = skipped + (slack + cq - ck < UNDERFLOW_LOG2).astype(jnp.int32)
                ft_ref[2 * i + j] = skipped

    q = q_ref[0]
    qa = qa_ref[0]
    first_tile = [ft_ref[2 * qi + j] for j in range(2)]
    qops = []
    for j in range(2):
        qm = jnp.where(head0 if j == 0 else jnp.logical_not(head0), q, jnp.zeros_like(q))
        lo = (2 * p + j) * AUG_STRIDE
        own = jnp.logical_and(lane_q >= lo, lane_q < lo + AUG_STRIDE)
        qops.append(jnp.concatenate([qm, jnp.where(own, qa, jnp.zeros_like(qa))], axis=1))
    q_pos = qi * tq + lax.broadcasted_iota(jnp.int32, (tq, tk), 0)
    k_off = lax.broadcasted_iota(jnp.int32, (tq, tk), 1)
    head0_k = lax.broadcasted_iota(jnp.int32, (tk, LANES), 1) < HEAD_DIM
    one = jnp.ones((tk, LANES), BF16)

    def tile(j, t, carry, masked):
        m, acc = carry
        c0 = pl.multiple_of(t * tk, tk)
        s = _dot_nt(qops[j], k_ref[0, pl.ds(c0, tk), :])
        if masked:
            s = jnp.where(k_off + c0 <= q_pos, s, NEG)
        m_new = jnp.maximum(m, jnp.max(s, axis=-1, keepdims=True))
        pr = jnp.exp2(s - m_new)
        vj = jnp.where(head0_k if j == 0 else jnp.logical_not(head0_k), v_ref[0, pl.ds(c0, tk), :], one)
        acc = jnp.exp2(m - m_new) * acc + jnp.dot(pr.astype(BF16), vj, preferred_element_type=F32)
        return m_new, acc

    carry = []
    for j in range(2):
        t0 = first_tile[j]
        n_act = qi - t0
        cj = (jnp.full((tq, 1), NEG, F32), jnp.zeros((tq, LANES), F32))
        cj = lax.fori_loop(0, n_act // 2, lambda i, c, j=j, t0=t0: tile(j, t0 + 2 * i + 1, tile(j, t0 + 2 * i, c, False), False), cj)
        cj = lax.cond(n_act % 2 == 1, lambda c, j=j: tile(j, qi - 1, c, False), lambda c: c, cj)
        carry += list(cj)
    for j in range(2):
        carry[2 * j:2 * j + 2] = tile(j, qi, tuple(carry[2 * j:2 * j + 2]), True)
    o0 = carry[1] * (1.0 / pltpu.roll(carry[1], HEAD_DIM, 1))
    o1 = carry[3] * (1.0 / pltpu.roll(carry[3], HEAD_DIM, 1))
    out = jnp.where(head0, o0, o1)
    sq = out * out
    sq_hi = sq.astype(BF16)
    sq_lo = (sq - sq_hi.astype(F32)).astype(BF16)
    ms = jnp.dot(sq_hi, havg_ref[...], preferred_element_type=F32) + jnp.dot(sq_lo, havg_ref[...], preferred_element_type=F32)
    o_ref[0] = (out * lax.rsqrt(ms + RMS_EPS) * gain_ref[...]).astype(BF16)


def _fox(main3, q_aug, k_aug, cum_edges, gain_row, tq, tk):
    B, S, _ = main3.shape
    assert tq == tk, "the kernel handles exactly one diagonal tile per query block"
    head_of = np.arange(LANES) // HEAD_DIM
    havg = (head_of[:, None] == head_of[None, :]).astype(np.float32) / HEAD_DIM
    grid_spec = pltpu.PrefetchScalarGridSpec(
        num_scalar_prefetch=1,
        grid=(B, N_PAIR, S // tq),
        in_specs=[
            pl.BlockSpec((1, tq, LANES), lambda b, p, i, cb: (b, i, CB_FQ + p)),
            pl.BlockSpec((1, tq, LANES), lambda b, p, i, cb: (b, i, 0)),
            pl.BlockSpec((1, S, LANES), lambda b, p, i, cb: (b, 0, CB_FQ + p)),
            pl.BlockSpec((1, S, LANES), lambda b, p, i, cb: (b, 0, CB_FK + p)),
            pl.BlockSpec((1, S, LANES), lambda b, p, i, cb: (b, 0, 0)),
            pl.BlockSpec((1, S, LANES), lambda b, p, i, cb: (b, 0, CB_FV + p)),
            pl.BlockSpec((1, LANES), lambda b, p, i, cb: (0, p)),
            pl.BlockSpec((LANES, LANES), lambda b, p, i, cb: (0, 0)),
        ],
        out_specs=pl.BlockSpec((1, tq, LANES), lambda b, p, i, cb: (b, i, p)),
        scratch_shapes=[pltpu.VMEM((1, S, 2 * LANES), BF16), pltpu.SMEM((2 * (S // tk),), jnp.int32)],
    )
    return pl.pallas_call(
        functools.partial(_fox_kernel, tq=tq, tk=tk, n_tiles=S // tk),
        grid_spec=grid_spec,
        out_shape=jax.ShapeDtypeStruct((B, S, FOX_W), BF16),
        compiler_params=_cparams(("parallel", "parallel", "arbitrary")),
        name="fox_attention",
    )(cum_edges, main3, q_aug, main3, main3, k_aug, main3, gain_row, jnp.asarray(havg, BF16))


def _gelu_tanh(x):
    return 0.5 * x * (1.0 + jnp.tanh(np.sqrt(2.0 / np.pi).astype(np.float32) * (x + 0.044715 * (x * x * x))))


def _cmp_kernel(r_ref, pos_ref, wt_ref, wb_ref, w2_ref, o_ref):
    r = r_ref[0]
    n = r.shape[0]
    top = jnp.dot(r, wt_ref[...], preferred_element_type=F32)
    bot = jnp.dot(r, wb_ref[...], preferred_element_type=F32)
    cpos = jnp.dot(pos_ref[0:8, :], wt_ref[...], preferred_element_type=F32)
    cpos = cpos + jnp.dot(pos_ref[8:16, :], wb_ref[...], preferred_element_type=F32)
    hid = _gelu_tanh(top + pltpu.roll(bot, n - 1, 0) + cpos[0:1, :])
    o_ref[0] = jnp.dot(hid.astype(BF16), w2_ref[...], preferred_element_type=F32)


def _nsa_compress(r, pos2, wt, wb, w2):
    B, n, _ = r.shape
    width = CMP_STRIDE * LANES
    return pl.pallas_call(
        _cmp_kernel,
        grid=(B,),
        in_specs=[
            pl.BlockSpec((1, n, width), lambda b: (b, 0, 0)),
            pl.BlockSpec((16, width), lambda b: (0, 0)),
            pl.BlockSpec((width, 2 * CMP_HIDDEN), lambda b: (0, 0)),
            pl.BlockSpec((width, 2 * CMP_HIDDEN), lambda b: (0, 0)),
            pl.BlockSpec((2 * CMP_HIDDEN, LANES), lambda b: (0, 0)),
        ],
        out_specs=pl.BlockSpec((1, n, LANES), lambda b: (b, 0, 0)),
        out_shape=jax.ShapeDtypeStruct((B, n, LANES), F32),
        compiler_params=_cparams(("parallel",)),
        name="nsa_compress",
    )(r, pos2, wt, wb, w2)


def _softmax_rows(s, mask):
    sm = jnp.where(mask, s, NEG)
    m = jnp.maximum(jnp.max(sm, axis=-1, keepdims=True), 0.1 * NEG)
    e = jnp.exp2(sm - m)
    l = jnp.sum(e, axis=-1, keepdims=True)
    return e / jnp.where(l > 0.0, l, 1.0)


def _nsa_kernel(q_ref, cmp_ref, c2s_ref, sel_ref, noh_ref, win_ref, gl_ref, gain_ref, gsel_ref, havg_ref, o_ref, ksa_ref, *, nb, tq, tk, n_cmp,
                n_sel, top_n):
    H = N_NSA
    qi = pl.program_id(1)
    start = qi * tq

    @pl.when(qi == 0)
    def _():
        key_lanes = lax.broadcasted_iota(jnp.int32, noh_ref.shape, 1) < HEAD_DIM
        for bb in range(nb):
            ksa_ref[bb] = jnp.where(key_lanes, sel_ref[bb], noh_ref[...])

    t_row = start + lax.broadcasted_iota(jnp.int32, (tq, 1), 0)
    t_all = jnp.concatenate([t_row] * H, axis=0)
    zeros64 = jnp.zeros((tq, HEAD_DIM), BF16)
    wlen = WINDOW + tq
    base = pl.multiple_of(jnp.maximum(start - WINDOW, 0), tq)
    n_pad = cmp_ref.shape[1]
    assert CMP_STRIDE == 16 and tq & (tq - 1) == 0
    last_cmp = (t_all - (CMP_LEN - 1)) >> 4
    row_in_block = lax.broadcasted_iota(jnp.int32, (H * tq, wlen), 0) & (tq - 1)
    win_lag = (start - base) - (lax.broadcasted_iota(jnp.int32, (H * tq, wlen), 1) - row_in_block)

    def prologue(bb):
        q = q_ref[bb]
        q_heads = [q[:, h * HEAD_DIM:(h + 1) * HEAD_DIM] for h in range(H)]
        q_pad = jnp.concatenate([jnp.concatenate([qh, zeros64], axis=1) for qh in q_heads], axis=0)

        win = win_ref[bb, pl.ds(base, wlen), :]
        s_w = _dot_nt(q_pad, win)
        s_w = jnp.where(lax.bitcast_convert_type(win_lag, jnp.uint32) < WINDOW, s_w, NEG)
        e_w = jnp.exp2(s_w - jnp.max(s_w, axis=-1, keepdims=True))
        win1 = jnp.where(lax.broadcasted_iota(jnp.int32, (wlen, LANES), 1) < HEAD_DIM, jnp.ones((wlen, LANES), BF16), win)
        o_w = jnp.dot(e_w.astype(BF16), win1, preferred_element_type=F32)

        cmp = cmp_ref[bb]
        cmp_hi = cmp.astype(BF16)
        cmp_lo = (cmp - cmp_hi.astype(F32)).astype(BF16)
        s_c = _dot_nt(q_pad, cmp_hi) + _dot_nt(q_pad, cmp_lo)
        n_id = lax.broadcasted_iota(jnp.int32, (H * tq, n_pad), 1)
        p_c = _softmax_rows(s_c, n_id <= last_cmp)
        o_c = jnp.dot(p_c.astype(BF16), cmp_hi, preferred_element_type=F32)

        p_sum = p_c[0:tq]
        for h in range(1, H):
            p_sum = p_sum + p_c[h * tq:(h + 1) * tq]
        p_hi = p_sum.astype(BF16)
        p_lo = (p_sum - p_hi.astype(F32)).astype(BF16)
        c2s_t = c2s_ref[...]
        imp = _dot_nt(c2s_t, p_hi) + _dot_nt(c2s_t, p_lo)
        t_lane = start + lax.broadcasted_iota(jnp.int32, (HEAD_DIM, tq), 1)
        s_id = lax.broadcasted_iota(jnp.int32, (HEAD_DIM, tq), 0)
        cur = t_lane >> 6
        forced = jnp.logical_or(s_id == 0, jnp.logical_or(s_id == cur, s_id == cur - 1))
        score = jnp.where(forced, FORCED_SCORE, imp)
        score = jnp.where(s_id * SEL_LEN <= t_lane, score, -1.0)
        SUB = 8
        groups = [score[g * SUB:(g + 1) * SUB, :] for g in range(HEAD_DIM // SUB)]
        ranks = [jnp.zeros((SUB, tq), jnp.int32) for _ in groups]
        sub_id = lax.broadcasted_iota(jnp.int32, (SUB, tq), 0)
        for c in range(n_sel):
            row = score[c:c + 1, :]
            for g in range(len(groups)):
                if g * SUB > c:
                    before = row >= groups[g]
                elif g * SUB + SUB - 1 < c:
                    before = row > groups[g]
                else:
                    before = jnp.logical_or(row > groups[g], jnp.logical_and(row == groups[g], sub_id > c - g * SUB))
                ranks[g] = ranks[g] + before.astype(jnp.int32)
        not_sel = jnp.concatenate([jnp.where(r < top_n, 0.0, 1.0) for r in ranks], axis=0)
        ns = jnp.concatenate([jnp.zeros((HEAD_DIM, tq), F32), not_sel], axis=0).T.astype(BF16)
        q_aug = q_pad + jnp.concatenate([ns] * H, axis=0)
        return q_aug, o_c, o_w

    pro = [prologue(bb) for bb in range(nb)]

    def sel_tile(bb, c0, size, carry, diagonal):
        m, acc = carry
        s = _dot_nt(pro[bb][0], ksa_ref[bb, pl.ds(c0, size), :])
        if diagonal:
            r_id = lax.broadcasted_iota(jnp.int32, (H * tq, size), 0) & (tq - 1)
            c_id = lax.broadcasted_iota(jnp.int32, (H * tq, size), 1)
            s = jnp.where(c_id <= r_id, s, NEG)
        m_new = jnp.maximum(m, jnp.max(s, axis=-1, keepdims=True))
        pr = jnp.exp2(s - m_new)
        key_lanes = lax.broadcasted_iota(jnp.int32, (size, LANES), 1) < HEAD_DIM
        v1 = jnp.where(key_lanes, jnp.ones((size, LANES), BF16), sel_ref[bb, pl.ds(c0, size), :])
        acc = jnp.exp2(m - m_new) * acc + jnp.dot(pr.astype(BF16), v1, preferred_element_type=F32)
        return m_new, acc

    def all_rows(c0, size, carries, diagonal):
        return tuple(sel_tile(bb, c0, size, carries[bb], diagonal) for bb in range(nb))

    carries = tuple((jnp.full((H * tq, 1), NEG, F32), jnp.zeros((H * tq, LANES), F32)) for _ in range(nb))
    n_full = start // tk
    carries = lax.fori_loop(0, n_full, lambda t, c: all_rows(pl.multiple_of(t * tk, tk), tk, c, False), carries)
    n_tail = (start - n_full * tk) // tq
    for d in range(tk // tq - 1):
        c0 = pl.multiple_of(n_full * tk + d * tq, tq)
        carries = lax.cond(d < n_tail, lambda c, c0=c0: all_rows(c0, tq, c, False), lambda c: c, carries)
    carries = all_rows(pl.multiple_of(start, tq), tq, carries, True)

    first_half = lax.broadcasted_iota(jnp.int32, (tq, LANES), 1) < HEAD_DIM

    def place(acc, normalise):
        cols = []
        for h in range(0, H, 2):
            a0, a1 = acc[h * tq:(h + 1) * tq], acc[(h + 1) * tq:(h + 2) * tq]
            r0 = pltpu.roll(a0, HEAD_DIM, 1)
            if normalise:
                v0 = r0 * (1.0 / a0)
                v1 = a1 * (1.0 / pltpu.roll(a1, HEAD_DIM, 1))
            else:
                v0, v1 = r0, a1
            cols.append(jnp.where(first_half, v0, v1))
        return jnp.concatenate(cols, axis=1)

    for bb in range(nb):
        _, o_c, o_w = pro[bb]
        gates = 1.0 / (1.0 + jnp.exp(-gl_ref[bb]))
        g_hi = gates.astype(BF16)
        g_lo = (gates - g_hi.astype(F32)).astype(BF16)
        mix = None
        for c, y in enumerate((place(o_c, False), place(carries[bb][1], True), place(o_w, True))):
            g = jnp.dot(g_hi, gsel_ref[c], preferred_element_type=F32) + jnp.dot(g_lo, gsel_ref[c], preferred_element_type=F32)
            mix = g * y if mix is None else mix + g * y
        sq = mix * mix
        sq_hi = sq.astype(BF16)
        sq_lo = (sq - sq_hi.astype(F32)).astype(BF16)
        ms = jnp.dot(sq_hi, havg_ref[...], preferred_element_type=F32) + jnp.dot(sq_lo, havg_ref[...], preferred_element_type=F32)
        o_ref[bb] = (mix * lax.rsqrt(ms + RMS_EPS) * gain_ref[...]).astype(BF16)


def _nsa(main3, cmp, c2s, neg_onehot, small3, gain_row, nb, tq, tk, n_cmp):
    B, S, _ = main3.shape
    n_sel = S // SEL_LEN
    n_pad = cmp.shape[1]
    gsel = np.zeros((3, LANES, NSA_W), np.float32)
    for h in range(N_NSA):
        for c in range(3):
            gsel[c, SMALL_GATE + 3 * h + c, h * HEAD_DIM:(h + 1) * HEAD_DIM] = 1.0
    head_of = np.arange(NSA_W) // HEAD_DIM
    havg = (head_of[:, None] == head_of[None, :]).astype(np.float32) / HEAD_DIM
    return pl.pallas_call(
        functools.partial(_nsa_kernel, nb=nb, tq=tq, tk=tk, n_cmp=n_cmp, n_sel=n_sel, top_n=min(TOP_N, n_sel)),
        grid=(B // nb, S // tq),
        in_specs=[
            pl.BlockSpec((nb, tq, NSA_W), lambda b, i: (b, i, CB_NQ // 2)),
            pl.BlockSpec((nb, n_pad, LANES), lambda b, i: (b, 0, 0)),
            pl.BlockSpec((HEAD_DIM, n_pad), lambda b, i: (0, 0)),
            pl.BlockSpec((nb, S, LANES), lambda b, i: (b, 0, CB_SEL)),
            pl.BlockSpec((S, LANES), lambda b, i: (0, 0)),
            pl.BlockSpec((nb, S, LANES), lambda b, i: (b, 0, CB_WIN)),
            pl.BlockSpec((nb, tq, LANES), lambda b, i: (b, i, 0)),
            pl.BlockSpec((1, NSA_W), lambda b, i: (0, 0)),
            pl.BlockSpec((3, LANES, NSA_W), lambda b, i: (0, 0, 0)),
            pl.BlockSpec((NSA_W, NSA_W), lambda b, i: (0, 0)),
        ],
        out_specs=pl.BlockSpec((nb, tq, NSA_W), lambda b, i: (b, i, 0)),
        out_shape=jax.ShapeDtypeStruct((B, S, NSA_W), BF16),
        scratch_shapes=[pltpu.VMEM((nb, S, LANES), BF16)],
        compiler_params=_cparams(("parallel", "arbitrary")),
        name="nsa_attention",
    )(main3, cmp, c2s, main3, neg_onehot, main3, small3, gain_row, jnp.asarray(gsel, BF16), jnp.asarray(havg, BF16))


def _post_kernel(x_ref, yr_ref, yf_ref, yn_ref, wo_ref, g_ref, w1_ref, w2_ref, gf_ref, o_ref, *, ff_chunk, final):
    x = x_ref[...]
    x = x + jnp.dot(yr_ref[...], wo_ref[0:RET_W, :], preferred_element_type=F32)
    x = x + jnp.dot(yf_ref[...], wo_ref[RET_W:RET_W + FOX_W, :], preferred_element_type=F32)
    x = x + jnp.dot(yn_ref[...], wo_ref[RET_W + FOX_W:, :], preferred_element_type=F32)
    h = _rms(x, g_ref[...]).astype(BF16)
    o_ref[...] = x
    for c0 in range(0, D_FF, ff_chunk):
        hid = jnp.maximum(jnp.dot(h, w1_ref[:, c0:c0 + ff_chunk], preferred_element_type=F32), 0.0)
        o_ref[...] += jnp.dot((hid * hid).astype(BF16), w2_ref[c0:c0 + ff_chunk, :], preferred_element_type=F32)
    if final:
        o_ref[...] = _rms(o_ref[...], gf_ref[...])


def _post(x2d, yr, yf, yn, wo, g, w1, w2, gf, tm, final):
    T = x2d.shape[0]
    row = lambda w: pl.BlockSpec((tm, w), lambda i: (i, 0))
    return pl.pallas_call(
        functools.partial(_post_kernel, ff_chunk=512, final=final),
        grid=(T // tm,),
        in_specs=[
            row(D_MODEL), row(RET_W), row(FOX_W), row(NSA_W),
            _resident((D_MODEL, D_MODEL), lambda i: (0, 0)),
            _resident((1, D_MODEL), lambda i: (0, 0)),
            _resident((D_MODEL, D_FF), lambda i: (0, 0)),
            _resident((D_FF, D_MODEL), lambda i: (0, 0)),
            _resident((1, D_MODEL), lambda i: (0, 0)),
        ],
        out_specs=row(D_MODEL),
        out_shape=jax.ShapeDtypeStruct((T, D_MODEL), F32),
        compiler_params=_cparams(("parallel",)),
        name="outproj_mlp",
    )(x2d, yr, yf, yn, wo, g, w1, w2, gf)


def _tile_sizes(batch):
    return dict(tm=512, fox_tq=512, fox_tk=512, nsa_tq=256, nsa_tk=512, nsa_nb=2 if batch % 2 == 0 else 1)


def kernel(x, norm_attn, w_in, fox_forget_bias, ret_norm_gain, fox_norm_gain, nsa_norm_gain, nsa_cmp_pos_k, nsa_cmp_pos_v, nsa_cmp_w1_k, nsa_cmp_w2_k, nsa_cmp_w1_v, nsa_cmp_w2_v, w_out, norm_mlp, w_mlp_in, w_mlp_out, norm_final):
    B, S, D = x.shape
    depth = w_in.shape[0]
    assert D == D_MODEL and S % 512 == 0 and S >= WINDOW + 256
    T = B * S
    ts = _tile_sizes(B)

    cols, scale, small_cols, small_mask = _in_proj_columns()
    wm = (w_in[:, :, cols] * scale).astype(BF16)
    ws = (w_in[:, :, small_cols] * small_mask).astype(BF16)
    wo = w_out.astype(BF16)
    w1 = w_mlp_in.astype(BF16)
    w2 = w_mlp_out.astype(BF16)

    def expand_w1(wk, wv):
        L = wk.shape[0]
        wk = wk.reshape(L, CMP_LEN, HEAD_DIM, CMP_HIDDEN)
        wv = wv.reshape(L, CMP_LEN, HEAD_DIM, CMP_HIDDEN)
        z = jnp.zeros_like(wk)
        full = jnp.concatenate([jnp.concatenate([wk, z], axis=-1), jnp.concatenate([z, wv], axis=-1)], axis=2)
        full = full.reshape(L, CMP_LEN * LANES, 2 * CMP_HIDDEN).astype(BF16)
        return full[:, :CMP_STRIDE * LANES], full[:, CMP_STRIDE * LANES:]

    wt_all, wb_all = expand_w1(nsa_cmp_w1_k, nsa_cmp_w1_v)
    zk = jnp.zeros_like(nsa_cmp_w2_k)
    w2c = jnp.concatenate([jnp.concatenate([nsa_cmp_w2_k, zk], axis=-1),
                           jnp.concatenate([zk, nsa_cmp_w2_v], axis=-1)], axis=1).astype(BF16)
    pos = jnp.concatenate([nsa_cmp_pos_k, nsa_cmp_pos_v], axis=-1)
    pos_tb = pos.reshape(depth, 2, 1, CMP_STRIDE * LANES)
    pos_tb = jnp.broadcast_to(pos_tb, (depth, 2, 8, CMP_STRIDE * LANES)).reshape(depth, 16, CMP_STRIDE * LANES).astype(BF16)

    n_cmp = (S - CMP_LEN) // CMP_STRIDE + 1
    n_sel = S // SEL_LEN
    n_rows = S // CMP_STRIDE
    cs = np.arange(n_rows) * CMP_STRIDE
    ss = np.arange(n_sel) * SEL_LEN
    overlap = np.clip(np.minimum(cs[:, None] + CMP_LEN, ss[None, :] + SEL_LEN) - np.maximum(cs[:, None], ss[None, :]), 0, None)
    overlap[n_cmp:] = 0
    c2s_np = np.zeros((HEAD_DIM, n_rows), np.float32)
    c2s_np[:n_sel] = (overlap / CMP_LEN).T
    c2s = jnp.asarray(c2s_np, dtype=BF16)
    assert n_sel <= HEAD_DIM, "the selected-branch key augmentation has 64 lanes, one per selection block"
    neg_onehot = jnp.asarray(np.where((np.arange(S)[:, None] // SEL_LEN) == np.arange(LANES)[None, :] - HEAD_DIM, NEG, 0.0), dtype=BF16)

    tables = _retention_tables(S)
    fbias = jnp.zeros((depth, 1, LANES), F32).at[:, 0, SMALL_FF:SMALL_FF + N_FOX].set(fox_forget_bias)

    x2d = x.reshape(T, D)
    for l in range(depth):
        main, small = _inproj(x2d, norm_attn[l][None, :], wm[l], ws[l], ts["tm"])
        main3 = main.reshape(B, S, N_MAIN)
        small3 = small.reshape(B, S, LANES)
        fox_qa, fox_ka, edges = _forget_cumsum(small3, fbias[l])
        y_ret = _retention(main3, tables, ret_norm_gain[l][None, :])
        per_tile = ts["fox_tk"] // RET_CHUNK
        cq = edges[:, 0::per_tile, 0, SMALL_FF:SMALL_FF + N_FOX]
        ck = edges[:, per_tile - 1::per_tile, 1, SMALL_FF:SMALL_FF + N_FOX]
        cum_edges = jnp.stack([cq.transpose(0, 2, 1), ck.transpose(0, 2, 1)], axis=2).reshape(-1)
        y_fox = _fox(main3, fox_qa, fox_ka, cum_edges, fox_norm_gain[l][None, :], ts["fox_tq"], ts["fox_tk"])
        r = main3[:, :, CB_CMP * LANES:(CB_CMP + 1) * LANES].reshape(B, n_rows, CMP_STRIDE * LANES)
        cmp = _nsa_compress(r, pos_tb[l], wt_all[l], wb_all[l], w2c[l])
        y_nsa = _nsa(main3, cmp, c2s, neg_onehot, small3, nsa_norm_gain[l][None, :], ts["nsa_nb"], ts["nsa_tq"], ts["nsa_tk"], n_cmp)
        x2d = _post(x2d, y_ret.reshape(T, RET_W), y_fox.reshape(T, FOX_W), y_nsa.reshape(T, NSA_W), wo[l],
                    norm_mlp[l][None, :], w1[l], w2[l], norm_final[None, :], ts["tm"], final=(l == depth - 1))
    return x2d.reshape(B, S, D)
```

```python
import functools

import numpy as np
import jax
import jax.numpy as jnp
from jax import lax
from jax.experimental import pallas as pl
from jax.experimental.pallas import tpu as pltpu

F32 = jnp.float32
BF16 = jnp.bfloat16

D_MODEL = 1024
HEAD_DIM = 64
N_RET = 6
N_FOX = 6
N_NSA = 4
RET_W = N_RET * HEAD_DIM
FOX_W = N_FOX * HEAD_DIM
NSA_W = N_NSA * HEAD_DIM
D_FF = 4 * D_MODEL
RMS_EPS = 1e-6
RET_CHUNK = 128
RET_BLOCK = 256
ROPE_BASE = 10000.0
CMP_LEN = 32
CMP_STRIDE = 16
CMP_HIDDEN = 4 * HEAD_DIM
SEL_LEN = 64
TOP_N = 16
WINDOW = 512
FORCED_SCORE = 1e4
NEG = -1e30
LOG2E = float(np.log2(np.e))

LANES = 128
N_PAIR = N_RET // 2

CB_RQ, CB_RK, CB_RV, CB_RG = 0, 3, 6, 9
CB_NQ = 12
CB_FQ, CB_FK, CB_FV = 14, 17, 20
CB_CMP, CB_SEL, CB_WIN = 23, 24, 25
N_MAIN = 26 * LANES
SMALL_FF = 0
SMALL_GATE = 8

VMEM_LIMIT = 56 * 1024 * 1024


def _cparams(sem):
    return pltpu.CompilerParams(dimension_semantics=sem, vmem_limit_bytes=VMEM_LIMIT)


def _resident(shape, index_map):
    return pl.BlockSpec(shape, index_map, pipeline_mode=pl.Buffered(1))


def _in_proj_columns():
    sizes = (RET_W, RET_W, RET_W, RET_W, FOX_W, FOX_W, FOX_W, N_FOX, NSA_W) + (HEAD_DIM,) * 6 + (3 * N_NSA,)
    off = np.concatenate([[0], np.cumsum(sizes)])
    (o_rq, o_rk, o_rv, o_rg, o_fq, o_fk, o_fv, o_ff, o_nq, o_kc, o_vc, o_ks, o_vs, o_kw, o_vw, o_gt) = off[:-1]
    half = HEAD_DIM // 2
    inter = []
    for p in range(N_PAIR):
        a, b = 2 * p, 2 * p + 1
        for h, part in ((a, 0), (b, 0), (a, 1), (b, 1)):
            inter.extend(range(h * HEAD_DIM + part * half, h * HEAD_DIM + (part + 1) * half))
    inter = np.asarray(inter)
    nat = np.arange(RET_W)
    cols, scale = [], []

    def add(idx, s=1.0):
        cols.append(np.asarray(idx))
        scale.append(np.full(len(idx), s, np.float32))

    qk_scale = HEAD_DIM ** -0.5
    sm_scale = qk_scale * LOG2E
    add(o_rq + inter)
    add(o_rk + inter, qk_scale)
    add(o_rv + nat)
    add(o_rg + nat)
    add(o_nq + np.arange(NSA_W), sm_scale)
    add(o_fq + nat, sm_scale)
    add(o_fk + nat)
    add(o_fv + nat)
    for o in (o_kc, o_vc, o_ks, o_vs, o_kw, o_vw):
        add(o + np.arange(HEAD_DIM))
    cols = np.concatenate(cols)
    scale = np.concatenate(scale)
    assert cols.shape[0] == N_MAIN
    small_cols = np.zeros(LANES, np.int64)
    small_mask = np.zeros(LANES, np.float32)
    small_cols[SMALL_FF:SMALL_FF + N_FOX] = o_ff + np.arange(N_FOX)
    small_mask[SMALL_FF:SMALL_FF + N_FOX] = 1.0
    small_cols[SMALL_GATE:SMALL_GATE + 3 * N_NSA] = o_gt + np.arange(3 * N_NSA)
    small_mask[SMALL_GATE:SMALL_GATE + 3 * N_NSA] = 1.0
    return cols, scale, small_cols, small_mask


def _retention_tables(seq):
    half = HEAD_DIM // 2
    inv = 1.0 / (ROPE_BASE ** (jnp.arange(half, dtype=F32) / half))
    ang = jnp.arange(seq, dtype=F32)[:, None] * inv[None, :]
    cos, sin = jnp.cos(ang), jnp.sin(ang)
    cos_t = jnp.concatenate([cos, cos, cos, cos], axis=-1)
    sin_t = jnp.concatenate([-sin, -sin, sin, sin], axis=-1)
    log_gamma = np.log(1.0 - 2.0 ** (-5.0 - np.arange(N_RET, dtype=np.float32))).astype(np.float32)
    C = RET_BLOCK
    idx = np.arange(C, dtype=np.float32)
    diff = idx[:, None] - idx[None, :]
    lane = np.arange(LANES)
    head_k = (lane % HEAD_DIM) // half
    head_v = lane // HEAD_DIM
    d_in = np.zeros((N_PAIR, 2, C, C), np.float32)
    d_k = np.zeros((N_PAIR, C, LANES), np.float32)
    d_q = np.zeros((N_PAIR, C, LANES), np.float32)
    d_c = np.zeros((N_PAIR, LANES, LANES), np.float32)
    for p in range(N_PAIR):
        lg = log_gamma[2 * p:2 * p + 2]
        for j in range(2):
            d_in[p, j] = np.where(diff >= 0, np.exp(lg[j] * np.maximum(diff, 0.0)), 0.0)
        d_k[p] = np.exp(lg[head_k][None, :] * (C - 1.0 - idx)[:, None])
        d_q[p] = np.exp(lg[head_v][None, :] * (idx + 1.0)[:, None])
        d_c[p] = np.broadcast_to(np.exp(lg[head_v] * C)[None, :], (LANES, LANES))
    s_mask = (head_k[:, None] == head_v[None, :]).astype(np.float32)
    return cos_t, sin_t, jnp.asarray(d_in), jnp.asarray(d_k), jnp.asarray(d_q), jnp.asarray(d_c), jnp.asarray(s_mask)


def _rms(x, g):
    return x * lax.rsqrt(jnp.mean(x * x, axis=-1, keepdims=True) + RMS_EPS) * g


def _inproj_kernel(x_ref, g_ref, wm_ref, ws_ref, om_ref, os_ref, *, n_chunk):
    h = _rms(x_ref[...], g_ref[...]).astype(BF16)
    for n0 in range(0, N_MAIN, n_chunk):
        n1 = min(n0 + n_chunk, N_MAIN)
        om_ref[:, n0:n1] = jnp.dot(h, wm_ref[:, n0:n1], preferred_element_type=F32).astype(BF16)
    os_ref[...] = jnp.dot(h, ws_ref[...], preferred_element_type=F32)


def _inproj(x2d, g, wm, ws, tm):
    T = x2d.shape[0]
    return pl.pallas_call(
        functools.partial(_inproj_kernel, n_chunk=512),
        grid=(T // tm,),
        in_specs=[
            pl.BlockSpec((tm, D_MODEL), lambda i: (i, 0)),
            _resident((1, D_MODEL), lambda i: (0, 0)),
            _resident((D_MODEL, N_MAIN), lambda i: (0, 0)),
            _resident((D_MODEL, LANES), lambda i: (0, 0)),
        ],
        out_specs=[
            pl.BlockSpec((tm, N_MAIN), lambda i: (i, 0)),
            pl.BlockSpec((tm, LANES), lambda i: (i, 0)),
        ],
        out_shape=[jax.ShapeDtypeStruct((T, N_MAIN), BF16), jax.ShapeDtypeStruct((T, LANES), F32)],
        compiler_params=_cparams(("parallel",)),
        name="inproj",
    )(x2d, g, wm, ws)


N_SPLIT = 3
AUG_STRIDE = 2 * N_SPLIT


def _split_bf16(x):
    terms, rest = [], x
    for _ in range(N_SPLIT):
        t = rest.astype(BF16)
        rest = rest - t.astype(F32)
        terms.append(t)
    return terms


def _bias_placement():
    mq = np.zeros((N_SPLIT, LANES, LANES), np.float32)
    mk = np.zeros((N_SPLIT, LANES, LANES), np.float32)
    one_q = np.zeros((1, LANES), np.float32)
    one_k = np.zeros((1, LANES), np.float32)
    for h in range(N_FOX):
        base = h * AUG_STRIDE
        for i in range(N_SPLIT):
            mq[i, SMALL_FF + h, base + i] = 1.0
            mk[i, SMALL_FF + h, base + N_SPLIT + i] = -1.0
            one_q[0, base + N_SPLIT + i] = 1.0
            one_k[0, base + i] = 1.0
    return jnp.asarray(mq, BF16), jnp.asarray(mk, BF16), jnp.asarray(one_q), jnp.asarray(one_k)


def _cum_kernel(z_ref, b_ref, tri_ref, mq_ref, mk_ref, oq_ref, ok_ref, qa_ref, ka_ref, edge_ref, *, n_chunks):
    C = RET_CHUNK

    def chunk(c, carry):
        r0 = pl.multiple_of(c * C, C)
        z = z_ref[0, pl.ds(r0, C), :] + b_ref[...]
        lf = jnp.minimum(z, 0.0) - jnp.log1p(jnp.exp(-jnp.abs(z)))
        terms = _split_bf16(lf)
        local = jnp.dot(tri_ref[...], terms[0], preferred_element_type=F32)
        for term in terms[1:]:
            local = local + jnp.dot(tri_ref[...], term, preferred_element_type=F32)
        cs = local + carry
        cs2 = cs * LOG2E
        edge_ref[0, pl.ds(c, 1)] = jnp.concatenate([cs2[0:1], cs2[C - 1:C], jnp.zeros((6, LANES), F32)], axis=0)[None]
        qa, ka = oq_ref[...], ok_ref[...]
        for i, term in enumerate(_split_bf16(cs2)):
            qa = qa + jnp.dot(term, mq_ref[i], preferred_element_type=F32)
            ka = ka + jnp.dot(term, mk_ref[i], preferred_element_type=F32)
        qa_ref[0, pl.ds(r0, C), :] = qa.astype(BF16)
        ka_ref[0, pl.ds(r0, C), :] = ka.astype(BF16)
        return cs[C - 1:C, :]

    unroll = max(u for u in (1, 2, 4) if n_chunks % u == 0)

    def body(i, carry):
        for u in range(unroll):
            carry = chunk(i * unroll + u, carry)
        return carry

    lax.fori_loop(0, n_chunks // unroll, body, jnp.zeros((1, LANES), F32))


def _forget_cumsum(small3, bias_row):
    B, S, _ = small3.shape
    tri = jnp.asarray(np.tril(np.ones((RET_CHUNK, RET_CHUNK), np.float32)), BF16)
    mq, mk, one_q, one_k = _bias_placement()
    const = lambda shape: pl.BlockSpec(shape, lambda b: (0,) * len(shape))
    return pl.pallas_call(
        functools.partial(_cum_kernel, n_chunks=S // RET_CHUNK),
        grid=(B,),
        in_specs=[
            pl.BlockSpec((1, S, LANES), lambda b: (b, 0, 0)),
            const((1, LANES)), const((RET_CHUNK, RET_CHUNK)),
            const((N_SPLIT, LANES, LANES)), const((N_SPLIT, LANES, LANES)), const((1, LANES)), const((1, LANES)),
        ],
        out_specs=[
            pl.BlockSpec((1, S, LANES), lambda b: (b, 0, 0)),
            pl.BlockSpec((1, S, LANES), lambda b: (b, 0, 0)),
            pl.BlockSpec((1, S // RET_CHUNK, 8, LANES), lambda b: (b, 0, 0, 0)),
        ],
        out_shape=[jax.ShapeDtypeStruct((B, S, LANES), BF16), jax.ShapeDtypeStruct((B, S, LANES), BF16),
                   jax.ShapeDtypeStruct((B, S // RET_CHUNK, 8, LANES), F32)],
        compiler_params=_cparams(("parallel",)),
        name="forget_cumsum",
    )(small3, bias_row, tri, mq, mk, one_q, one_k)


def _dot_nt(a, b):
    return lax.dot_general(a, b, (((1,), (1,)), ((), ())), preferred_element_type=F32)


def _ret_kernel(q_ref, k_ref, v_ref, g_ref, cos_ref, sin_ref, din_ref, dk_ref, dq_ref, dc_ref, sm_ref, gain_ref,
                o_ref, state_ref, *, n_chunks, unroll):
    C = RET_BLOCK
    half = HEAD_DIM // 2
    lane = lax.broadcasted_iota(jnp.int32, (C, LANES), 1)
    khead = (lane % HEAD_DIM) // half
    vhead0 = lane < HEAD_DIM
    state_ref[...] = jnp.zeros_like(state_ref)

    def pair_chunk(p, r0):
        rows = pl.ds(r0, C)
        lanes = slice(p * LANES, (p + 1) * LANES)
        cs, sn = cos_ref[rows, :], sin_ref[rows, :]
        q = q_ref[0, rows, lanes].astype(F32)
        k = k_ref[0, rows, lanes].astype(F32)
        q = q * cs + pltpu.roll(q, HEAD_DIM, 1) * sn
        k = k * cs + pltpu.roll(k, HEAD_DIM, 1) * sn
        v = v_ref[0, rows, lanes]
        qb, kb = q.astype(BF16), k.astype(BF16)
        state = state_ref[p]
        cross = jnp.dot(qb, state.astype(BF16), preferred_element_type=F32) * dq_ref[p]
        outs = []
        for j in range(2):
            qm = jnp.where(khead == j, qb, jnp.zeros_like(qb))
            inner = _dot_nt(qm, kb) * din_ref[p, j]
            outs.append(jnp.dot(inner.astype(BF16), v, preferred_element_type=F32))
        out = jnp.where(vhead0, outs[0], outs[1]) + cross
        kd = (k * dk_ref[p]).T.astype(BF16)
        state_ref[p] = state * dc_ref[p] + jnp.dot(kd, v, preferred_element_type=F32) * sm_ref[...]
        inv = 1.0 / HEAD_DIM
        s0 = jnp.sum(jnp.where(vhead0, out, 0.0), axis=-1, keepdims=True)
        s1 = jnp.sum(jnp.where(vhead0, 0.0, out), axis=-1, keepdims=True)
        yc = out - jnp.where(vhead0, s0, s1) * inv
        yc2 = yc * yc
        v0 = jnp.sum(jnp.where(vhead0, yc2, 0.0), axis=-1, keepdims=True)
        v1 = jnp.sum(jnp.where(vhead0, 0.0, yc2), axis=-1, keepdims=True)
        y = yc * lax.rsqrt(jnp.where(vhead0, v0, v1) * inv + RMS_EPS) * gain_ref[:, lanes]
        g = g_ref[0, rows, lanes].astype(F32)
        o_ref[0, rows, lanes] = (y * (g * (1.0 / (1.0 + jnp.exp(-g))))).astype(BF16)

    def body(c, _):
        for u in range(unroll):
            for p in range(N_PAIR):
                pair_chunk(p, pl.multiple_of((c * unroll + u) * C, C))
        return 0

    lax.fori_loop(0, n_chunks // unroll, body, 0)


def _retention(main3, tables, gain_row):
    B, S, _ = main3.shape
    cos_t, sin_t, d_in, d_k, d_q, d_c, s_mask = tables
    C = RET_BLOCK
    n_chunks = S // C
    seq_spec = lambda cb: pl.BlockSpec((1, S, RET_W), lambda b: (b, 0, cb // N_PAIR))
    const = lambda shape: pl.BlockSpec(shape, lambda b: (0,) * len(shape))
    return pl.pallas_call(
        functools.partial(_ret_kernel, n_chunks=n_chunks, unroll=2 if n_chunks % 2 == 0 else 1),
        grid=(B,),
        in_specs=[
            seq_spec(CB_RQ), seq_spec(CB_RK), seq_spec(CB_RV), seq_spec(CB_RG),
            const((S, LANES)), const((S, LANES)),
            const((N_PAIR, 2, C, C)), const((N_PAIR, C, LANES)), const((N_PAIR, C, LANES)),
            const((N_PAIR, LANES, LANES)), const((LANES, LANES)), const((1, RET_W)),
        ],
        out_specs=pl.BlockSpec((1, S, RET_W), lambda b: (b, 0, 0)),
        out_shape=jax.ShapeDtypeStruct((B, S, RET_W), BF16),
        scratch_shapes=[pltpu.VMEM((N_PAIR, LANES, LANES), F32)],
        compiler_params=_cparams(("parallel",)),
        name="retention",
    )(main3, main3, main3, main3, cos_t, sin_t, d_in, d_k, d_q, d_c, s_mask, gain_row)


UNDERFLOW_LOG2 = -170.0


def _head_sq_norm_max(x, havg, head0):
    xf = x.astype(F32)
    sq = xf * xf
    hi = sq.astype(BF16)
    lo = (sq - hi.astype(F32)).astype(BF16)
    mean = jnp.dot(hi, havg, preferred_element_type=F32) + jnp.dot(lo, havg, preferred_element_type=F32)
    bound = mean * (HEAD_DIM * 1.01)
    return jnp.max(jnp.where(head0, bound, 0.0)), jnp.max(jnp.where(head0, 0.0, bound))


def _fox_kernel(cb_ref, q_ref, qa_ref, qs_ref, kh_ref, ka_ref, v_ref, gain_ref, havg_ref, o_ref, k_ref, ft_ref, *, tq, tk, n_tiles):
    b = pl.program_id(0)
    p = pl.program_id(1)
    qi = pl.program_id(2)
    lane_q = lax.broadcasted_iota(jnp.int32, (tq, LANES), 1)
    head0 = lane_q < HEAD_DIM

    @pl.when(qi == 0)
    def _():
        k_ref[0, :, 0:LANES] = kh_ref[0]
        k_ref[0, :, LANES:2 * LANES] = ka_ref[0]
        head0_s = lax.broadcasted_iota(jnp.int32, kh_ref.shape[1:], 1) < HEAD_DIM
        kn = _head_sq_norm_max(kh_ref[0], havg_ref[...], head0_s)
        for i in range(n_tiles):
            qn = _head_sq_norm_max(qs_ref[0, i * tq:(i + 1) * tq, :], havg_ref[...], head0)
            for j in range(2):
                h = 2 * p + j
                slack = 2.0 * jnp.sqrt(qn[j] * kn[j])
                cq = cb_ref[((b * N_FOX + h) * 2 + 0) * n_tiles + i]
                skipped = jnp.int32(0)
                for t in range(i):
                    ck = cb_ref[((b * N_FOX + h) * 2 + 1) * n_tiles + t]
                    skipped = skipped + (slack + cq - ck < UNDERFLOW_LOG2).astype(jnp.int32)
                ft_ref[2 * i + j] = skipped

    q = q_ref[0]
    qa = qa_ref[0]
    first_tile = [ft_ref[2 * qi + j] for j in range(2)]
    qops = []
    for j in range(2):
        qm = jnp.where(head0 if j == 0 else jnp.logical_not(head0), q, jnp.zeros_like(q))
        lo = (2 * p + j) * AUG_STRIDE
        own = jnp.logical_and(lane_q >= lo, lane_q < lo + AUG_STRIDE)
        qops.append(jnp.concatenate([qm, jnp.where(own, qa, jnp.zeros_like(qa))], axis=1))
    q_pos = qi * tq + lax.broadcasted_iota(jnp.int32, (tq, tk), 0)
    k_off = lax.broadcasted_iota(jnp.int32, (tq, tk), 1)

    def tile(j, t, n, carry, masked):
        m, acc = carry
        c0 = pl.multiple_of(t * tk, tk)
        s = _dot_nt(qops[j], k_ref[0, pl.ds(c0, n * tk), :])
        if masked:
            s = jnp.where(k_off + c0 <= q_pos, s, NEG)
        m_new = jnp.maximum(m, jnp.max(s, axis=-1, keepdims=True))
        pr = jnp.exp2(s - m_new)
        own = lax.broadcasted_iota(jnp.int32, (n * tk, LANES), 1) < HEAD_DIM
        vj = jnp.where(own if j == 0 else jnp.logical_not(own), v_ref[0, pl.ds(c0, n * tk), :], jnp.ones((n * tk, LANES), BF16))
        acc = jnp.exp2(m - m_new) * acc + jnp.dot(pr.astype(BF16), vj, preferred_element_type=F32)
        return m_new, acc

    carry = []
    for j in range(2):
        t0 = first_tile[j]
        n_act = qi - t0
        cj = (jnp.full((tq, 1), NEG, F32), jnp.zeros((tq, LANES), F32))
        cj = lax.fori_loop(0, n_act // 2, lambda i, c, j=j, t0=t0: tile(j, t0 + 2 * i, 2, c, False), cj)
        cj = lax.cond(n_act % 2 == 1, lambda c, j=j: tile(j, qi - 1, 1, c, False), lambda c: c, cj)
        carry += list(cj)
    for j in range(2):
        carry[2 * j:2 * j + 2] = tile(j, qi, 1, tuple(carry[2 * j:2 * j + 2]), True)
    o0 = carry[1] * (1.0 / pltpu.roll(carry[1], HEAD_DIM, 1))
    o1 = carry[3] * (1.0 / pltpu.roll(carry[3], HEAD_DIM, 1))
    out = jnp.where(head0, o0, o1)
    sq = out * out
    sq_hi = sq.astype(BF16)
    sq_lo = (sq - sq_hi.astype(F32)).astype(BF16)
    ms = jnp.dot(sq_hi, havg_ref[...], preferred_element_type=F32) + jnp.dot(sq_lo, havg_ref[...], preferred_element_type=F32)
    o_ref[0] = (out * lax.rsqrt(ms + RMS_EPS) * gain_ref[...]).astype(BF16)


def _fox(main3, q_aug, k_aug, cum_edges, gain_row, tq, tk):
    B, S, _ = main3.shape
    assert tq == tk, "the kernel handles exactly one diagonal tile per query block"
    head_of = np.arange(LANES) // HEAD_DIM
    havg = (head_of[:, None] == head_of[None, :]).astype(np.float32) / HEAD_DIM
    grid_spec = pltpu.PrefetchScalarGridSpec(
        num_scalar_prefetch=1,
        grid=(B, N_PAIR, S // tq),
        in_specs=[
            pl.BlockSpec((1, tq, LANES), lambda b, p, i, cb: (b, i, CB_FQ + p)),
            pl.BlockSpec((1, tq, LANES), lambda b, p, i, cb: (b, i, 0)),
            pl.BlockSpec((1, S, LANES), lambda b, p, i, cb: (b, 0, CB_FQ + p)),
            pl.BlockSpec((1, S, LANES), lambda b, p, i, cb: (b, 0, CB_FK + p)),
            pl.BlockSpec((1, S, LANES), lambda b, p, i, cb: (b, 0, 0)),
            pl.BlockSpec((1, S, LANES), lambda b, p, i, cb: (b, 0, CB_FV + p)),
            pl.BlockSpec((1, LANES), lambda b, p, i, cb: (0, p)),
            pl.BlockSpec((LANES, LANES), lambda b, p, i, cb: (0, 0)),
        ],
        out_specs=pl.BlockSpec((1, tq, LANES), lambda b, p, i, cb: (b, i, p)),
        scratch_shapes=[pltpu.VMEM((1, S, 2 * LANES), BF16), pltpu.SMEM((2 * (S // tk),), jnp.int32)],
    )
    return pl.pallas_call(
        functools.partial(_fox_kernel, tq=tq, tk=tk, n_tiles=S // tk),
        grid_spec=grid_spec,
        out_shape=jax.ShapeDtypeStruct((B, S, FOX_W), BF16),
        compiler_params=_cparams(("parallel", "parallel", "arbitrary")),
        name="fox_attention",
    )(cum_edges, main3, q_aug, main3, main3, k_aug, main3, gain_row, jnp.asarray(havg, BF16))


def _gelu_tanh(x):
    return 0.5 * x * (1.0 + jnp.tanh(np.sqrt(2.0 / np.pi).astype(np.float32) * (x + 0.044715 * (x * x * x))))


def _cmp_kernel(r_ref, pos_ref, wt_ref, wb_ref, w2_ref, o_ref):
    r = r_ref[0]
    n = r.shape[0]
    top = jnp.dot(r, wt_ref[...], preferred_element_type=F32)
    bot = jnp.dot(r, wb_ref[...], preferred_element_type=F32)
    cpos = jnp.dot(pos_ref[0:8, :], wt_ref[...], preferred_element_type=F32)
    cpos = cpos + jnp.dot(pos_ref[8:16, :], wb_ref[...], preferred_element_type=F32)
    hid = _gelu_tanh(top + pltpu.roll(bot, n - 1, 0) + cpos[0:1, :])
    o_ref[0] = jnp.dot(hid.astype(BF16), w2_ref[...], preferred_element_type=F32)


def _nsa_compress(r, pos2, wt, wb, w2):
    B, n, _ = r.shape
    width = CMP_STRIDE * LANES
    return pl.pallas_call(
        _cmp_kernel,
        grid=(B,),
        in_specs=[
            pl.BlockSpec((1, n, width), lambda b: (b, 0, 0)),
            pl.BlockSpec((16, width), lambda b: (0, 0)),
            pl.BlockSpec((width, 2 * CMP_HIDDEN), lambda b: (0, 0)),
            pl.BlockSpec((width, 2 * CMP_HIDDEN), lambda b: (0, 0)),
            pl.BlockSpec((2 * CMP_HIDDEN, LANES), lambda b: (0, 0)),
        ],
        out_specs=pl.BlockSpec((1, n, LANES), lambda b: (b, 0, 0)),
        out_shape=jax.ShapeDtypeStruct((B, n, LANES), F32),
        compiler_params=_cparams(("parallel",)),
        name="nsa_compress",
    )(r, pos2, wt, wb, w2)


def _softmax_rows(s, mask):
    sm = jnp.where(mask, s, NEG)
    m = jnp.maximum(jnp.max(sm, axis=-1, keepdims=True), 0.1 * NEG)
    e = jnp.exp2(sm - m)
    l = jnp.sum(e, axis=-1, keepdims=True)
    return e / jnp.where(l > 0.0, l, 1.0)


def _nsa_kernel(q_ref, cmp_ref, c2s_ref, sel_ref, noh_ref, win_ref, gl_ref, gain_ref, gsel_ref, havg_ref, o_ref, ksa_ref, *, nb, tq, tk, n_cmp,
                n_sel, top_n):
    H = N_NSA
    qi = pl.program_id(1)
    start = qi * tq

    @pl.when(qi == 0)
    def _():
        key_lanes = lax.broadcasted_iota(jnp.int32, noh_ref.shape, 1) < HEAD_DIM
        for bb in range(nb):
            ksa_ref[bb] = jnp.where(key_lanes, sel_ref[bb], noh_ref[...])

    t_row = start + lax.broadcasted_iota(jnp.int32, (tq, 1), 0)
    t_all = jnp.concatenate([t_row] * H, axis=0)
    zeros64 = jnp.zeros((tq, HEAD_DIM), BF16)
    wlen = WINDOW + tq
    base = pl.multiple_of(jnp.maximum(start - WINDOW, 0), tq)
    n_pad = cmp_ref.shape[1]
    assert CMP_STRIDE == 16 and tq & (tq - 1) == 0
    last_cmp = (t_all - (CMP_LEN - 1)) >> 4
    row_in_block = lax.broadcasted_iota(jnp.int32, (H * tq, wlen), 0) & (tq - 1)
    win_lag = (start - base) - (lax.broadcasted_iota(jnp.int32, (H * tq, wlen), 1) - row_in_block)

    def prologue(bb):
        q = q_ref[bb]
        q_heads = [q[:, h * HEAD_DIM:(h + 1) * HEAD_DIM] for h in range(H)]
        q_pad = jnp.concatenate([jnp.concatenate([qh, zeros64], axis=1) for qh in q_heads], axis=0)

        win = win_ref[bb, pl.ds(base, wlen), :]
        s_w = _dot_nt(q_pad, win)
        s_w = jnp.where(lax.bitcast_convert_type(win_lag, jnp.uint32) < WINDOW, s_w, NEG)
        e_w = jnp.exp2(s_w - jnp.max(s_w, axis=-1, keepdims=True))
        win1 = jnp.where(lax.broadcasted_iota(jnp.int32, (wlen, LANES), 1) < HEAD_DIM, jnp.ones((wlen, LANES), BF16), win)
        o_w = jnp.dot(e_w.astype(BF16), win1, preferred_element_type=F32)

        cmp = cmp_ref[bb]
        cmp_hi = cmp.astype(BF16)
        cmp_lo = (cmp - cmp_hi.astype(F32)).astype(BF16)
        s_c = _dot_nt(q_pad, cmp_hi) + _dot_nt(q_pad, cmp_lo)
        n_id = lax.broadcasted_iota(jnp.int32, (H * tq, n_pad), 1)
        p_c = _softmax_rows(s_c, n_id <= last_cmp)
        o_c = jnp.dot(p_c.astype(BF16), cmp_hi, preferred_element_type=F32)

        p_sum = p_c[0:tq]
        for h in range(1, H):
            p_sum = p_sum + p_c[h * tq:(h + 1) * tq]
        p_hi = p_sum.astype(BF16)
        p_lo = (p_sum - p_hi.astype(F32)).astype(BF16)
        c2s_t = c2s_ref[...]
        imp = _dot_nt(c2s_t, p_hi) + _dot_nt(c2s_t, p_lo)
        t_lane = start + lax.broadcasted_iota(jnp.int32, (HEAD_DIM, tq), 1)
        s_id = lax.broadcasted_iota(jnp.int32, (HEAD_DIM, tq), 0)
        cur = t_lane >> 6
        forced = jnp.logical_or(s_id == 0, jnp.logical_or(s_id == cur, s_id == cur - 1))
        score = jnp.where(forced, FORCED_SCORE, imp)
        score = jnp.where(s_id * SEL_LEN <= t_lane, score, -1.0)
        SUB = 8
        groups = [score[g * SUB:(g + 1) * SUB, :] for g in range(HEAD_DIM // SUB)]
        ranks = [jnp.zeros((SUB, tq), jnp.int32) for _ in groups]
        sub_id = lax.broadcasted_iota(jnp.int32, (SUB, tq), 0)
        for c in range(n_sel):
            row = score[c:c + 1, :]
            for g in range(len(groups)):
                if g * SUB > c:
                    before = row >= groups[g]
                elif g * SUB + SUB - 1 < c:
                    before = row > groups[g]
                else:
                    before = jnp.logical_or(row > groups[g], jnp.logical_and(row == groups[g], sub_id > c - g * SUB))
                ranks[g] = ranks[g] + before.astype(jnp.int32)
        not_sel = jnp.concatenate([jnp.where(r < top_n, 0.0, 1.0) for r in ranks], axis=0)
        ns = jnp.concatenate([jnp.zeros((HEAD_DIM, tq), F32), not_sel], axis=0).T.astype(BF16)
        q_aug = q_pad + jnp.concatenate([ns] * H, axis=0)
        return q_aug, o_c, o_w

    pro = [prologue(bb) for bb in range(nb)]

    def sel_tile(bb, c0, size, carry, diagonal):
        m, acc = carry
        s = _dot_nt(pro[bb][0], ksa_ref[bb, pl.ds(c0, size), :])
        if diagonal:
            r_id = lax.broadcasted_iota(jnp.int32, (H * tq, size), 0) & (tq - 1)
            c_id = lax.broadcasted_iota(jnp.int32, (H * tq, size), 1)
            s = jnp.where(c_id <= r_id, s, NEG)
        m_new = jnp.maximum(m, jnp.max(s, axis=-1, keepdims=True))
        pr = jnp.exp2(s - m_new)
        key_lanes = lax.broadcasted_iota(jnp.int32, (size, LANES), 1) < HEAD_DIM
        v1 = jnp.where(key_lanes, jnp.ones((size, LANES), BF16), sel_ref[bb, pl.ds(c0, size), :])
        acc = jnp.exp2(m - m_new) * acc + jnp.dot(pr.astype(BF16), v1, preferred_element_type=F32)
        return m_new, acc

    def all_rows(c0, size, carries, diagonal):
        return tuple(sel_tile(bb, c0, size, carries[bb], diagonal) for bb in range(nb))

    carries = tuple((jnp.full((H * tq, 1), NEG, F32), jnp.zeros((H * tq, LANES), F32)) for _ in range(nb))
    n_full = start // tk
    carries = lax.fori_loop(0, n_full, lambda t, c: all_rows(pl.multiple_of(t * tk, tk), tk, c, False), carries)
    pos = n_full * tk
    size = tk // 2
    while size >= tq:
        take = start - pos >= size
        carries = lax.cond(take, lambda c, pos=pos, size=size: all_rows(pl.multiple_of(pos, tq), size, c, False), lambda c: c, carries)
        pos = pos + jnp.where(take, size, 0)
        size //= 2
    carries = all_rows(pl.multiple_of(start, tq), tq, carries, True)

    first_half = lax.broadcasted_iota(jnp.int32, (tq, LANES), 1) < HEAD_DIM

    def place(acc, normalise):
        cols = []
        for h in range(0, H, 2):
            a0, a1 = acc[h * tq:(h + 1) * tq], acc[(h + 1) * tq:(h + 2) * tq]
            r0 = pltpu.roll(a0, HEAD_DIM, 1)
            if normalise:
                v0 = r0 * (1.0 / a0)
                v1 = a1 * (1.0 / pltpu.roll(a1, HEAD_DIM, 1))
            else:
                v0, v1 = r0, a1
            cols.append(jnp.where(first_half, v0, v1))
        return jnp.concatenate(cols, axis=1)

    for bb in range(nb):
        _, o_c, o_w = pro[bb]
        gates = 1.0 / (1.0 + jnp.exp(-gl_ref[bb]))
        g_hi = gates.astype(BF16)
        g_lo = (gates - g_hi.astype(F32)).astype(BF16)
        mix = None
        for c, y in enumerate((place(o_c, False), place(carries[bb][1], True), place(o_w, True))):
            g = jnp.dot(g_hi, gsel_ref[c], preferred_element_type=F32) + jnp.dot(g_lo, gsel_ref[c], preferred_element_type=F32)
            mix = g * y if mix is None else mix + g * y
        sq = mix * mix
        sq_hi = sq.astype(BF16)
        sq_lo = (sq - sq_hi.astype(F32)).astype(BF16)
        ms = jnp.dot(sq_hi, havg_ref[...], preferred_element_type=F32) + jnp.dot(sq_lo, havg_ref[...], preferred_element_type=F32)
        o_ref[bb] = (mix * lax.rsqrt(ms + RMS_EPS) * gain_ref[...]).astype(BF16)


def _nsa(main3, cmp, c2s, neg_onehot, small3, gain_row, nb, tq, tk, n_cmp):
    B, S, _ = main3.shape
    n_sel = S // SEL_LEN
    n_pad = cmp.shape[1]
    gsel = np.zeros((3, LANES, NSA_W), np.float32)
    for h in range(N_NSA):
        for c in range(3):
            gsel[c, SMALL_GATE + 3 * h + c, h * HEAD_DIM:(h + 1) * HEAD_DIM] = 1.0
    head_of = np.arange(NSA_W) // HEAD_DIM
    havg = (head_of[:, None] == head_of[None, :]).astype(np.float32) / HEAD_DIM
    return pl.pallas_call(
        functools.partial(_nsa_kernel, nb=nb, tq=tq, tk=tk, n_cmp=n_cmp, n_sel=n_sel, top_n=min(TOP_N, n_sel)),
        grid=(B // nb, S // tq),
        in_specs=[
            pl.BlockSpec((nb, tq, NSA_W), lambda b, i: (b, i, CB_NQ // 2)),
            pl.BlockSpec((nb, n_pad, LANES), lambda b, i: (b, 0, 0)),
            pl.BlockSpec((HEAD_DIM, n_pad), lambda b, i: (0, 0)),
            pl.BlockSpec((nb, S, LANES), lambda b, i: (b, 0, CB_SEL)),
            pl.BlockSpec((S, LANES), lambda b, i: (0, 0)),
            pl.BlockSpec((nb, S, LANES), lambda b, i: (b, 0, CB_WIN)),
            pl.BlockSpec((nb, tq, LANES), lambda b, i: (b, i, 0)),
            pl.BlockSpec((1, NSA_W), lambda b, i: (0, 0)),
            pl.BlockSpec((3, LANES, NSA_W), lambda b, i: (0, 0, 0)),
            pl.BlockSpec((NSA_W, NSA_W), lambda b, i: (0, 0)),
        ],
        out_specs=pl.BlockSpec((nb, tq, NSA_W), lambda b, i: (b, i, 0)),
        out_shape=jax.ShapeDtypeStruct((B, S, NSA_W), BF16),
        scratch_shapes=[pltpu.VMEM((nb, S, LANES), BF16)],
        compiler_params=_cparams(("parallel", "arbitrary")),
        name="nsa_attention",
    )(main3, cmp, c2s, main3, neg_onehot, main3, small3, gain_row, jnp.asarray(gsel, BF16), jnp.asarray(havg, BF16))


def _post_kernel(x_ref, yr_ref, yf_ref, yn_ref, wo_ref, g_ref, w1_ref, w2_ref, gf_ref, o_ref, *, ff_chunk, final):
    x = x_ref[...]
    x = x + jnp.dot(yr_ref[...], wo_ref[0:RET_W, :], preferred_element_type=F32)
    x = x + jnp.dot(yf_ref[...], wo_ref[RET_W:RET_W + FOX_W, :], preferred_element_type=F32)
    x = x + jnp.dot(yn_ref[...], wo_ref[RET_W + FOX_W:, :], preferred_element_type=F32)
    h = _rms(x, g_ref[...]).astype(BF16)
    o_ref[...] = x
    for c0 in range(0, D_FF, ff_chunk):
        hid = jnp.maximum(jnp.dot(h, w1_ref[:, c0:c0 + ff_chunk], preferred_element_type=F32), 0.0)
        o_ref[...] += jnp.dot((hid * hid).astype(BF16), w2_ref[c0:c0 + ff_chunk, :], preferred_element_type=F32)
    if final:
        o_ref[...] = _rms(o_ref[...], gf_ref[...])


def _post(x2d, yr, yf, yn, wo, g, w1, w2, gf, tm, final):
    T = x2d.shape[0]
    row = lambda w: pl.BlockSpec((tm, w), lambda i: (i, 0))
    return pl.pallas_call(
        functools.partial(_post_kernel, ff_chunk=512, final=final),
        grid=(T // tm,),
        in_specs=[
            row(D_MODEL), row(RET_W), row(FOX_W), row(NSA_W),
            _resident((D_MODEL, D_MODEL), lambda i: (0, 0)),
            _resident((1, D_MODEL), lambda i: (0, 0)),
            _resident((D_MODEL, D_FF), lambda i: (0, 0)),
            _resident((D_FF, D_MODEL), lambda i: (0, 0)),
            _resident((1, D_MODEL), lambda i: (0, 0)),
        ],
        out_specs=row(D_MODEL),
        out_shape=jax.ShapeDtypeStruct((T, D_MODEL), F32),
        compiler_params=_cparams(("parallel",)),
        name="outproj_mlp",
    )(x2d, yr, yf, yn, wo, g, w1, w2, gf)


def _tile_sizes(batch):
    return dict(tm=512, fox_tq=512, fox_tk=512, nsa_tq=256, nsa_tk=1024, nsa_nb=2 if batch % 2 == 0 else 1)


def kernel(x, norm_attn, w_in, fox_forget_bias, ret_norm_gain, fox_norm_gain, nsa_norm_gain, nsa_cmp_pos_k, nsa_cmp_pos_v, nsa_cmp_w1_k, nsa_cmp_w2_k, nsa_cmp_w1_v, nsa_cmp_w2_v, w_out, norm_mlp, w_mlp_in, w_mlp_out, norm_final):
    B, S, D = x.shape
    depth = w_in.shape[0]
    assert D == D_MODEL and S % 512 == 0 and S >= WINDOW + 256
    T = B * S
    ts = _tile_sizes(B)

    cols, scale, small_cols, small_mask = _in_proj_columns()
    wm = (w_in[:, :, cols] * scale).astype(BF16)
    ws = (w_in[:, :, small_cols] * small_mask).astype(BF16)
    wo = w_out.astype(BF16)
    w1 = w_mlp_in.astype(BF16)
    w2 = w_mlp_out.astype(BF16)

    def expand_w1(wk, wv):
        L = wk.shape[0]
        wk = wk.reshape(L, CMP_LEN, HEAD_DIM, CMP_HIDDEN)
        wv = wv.reshape(L, CMP_LEN, HEAD_DIM, CMP_HIDDEN)
        z = jnp.zeros_like(wk)
        full = jnp.concatenate([jnp.concatenate([wk, z], axis=-1), jnp.concatenate([z, wv], axis=-1)], axis=2)
        full = full.reshape(L, CMP_LEN * LANES, 2 * CMP_HIDDEN).astype(BF16)
        return full[:, :CMP_STRIDE * LANES], full[:, CMP_STRIDE * LANES:]

    wt_all, wb_all = expand_w1(nsa_cmp_w1_k, nsa_cmp_w1_v)
    zk = jnp.zeros_like(nsa_cmp_w2_k)
    w2c = jnp.concatenate([jnp.concatenate([nsa_cmp_w2_k, zk], axis=-1),
                           jnp.concatenate([zk, nsa_cmp_w2_v], axis=-1)], axis=1).astype(BF16)
    pos = jnp.concatenate([nsa_cmp_pos_k, nsa_cmp_pos_v], axis=-1)
    pos_tb = pos.reshape(depth, 2, 1, CMP_STRIDE * LANES)
    pos_tb = jnp.broadcast_to(pos_tb, (depth, 2, 8, CMP_STRIDE * LANES)).reshape(depth, 16, CMP_STRIDE * LANES).astype(BF16)

    n_cmp = (S - CMP_LEN) // CMP_STRIDE + 1
    n_sel = S // SEL_LEN
    n_rows = S // CMP_STRIDE
    cs = np.arange(n_rows) * CMP_STRIDE
    ss = np.arange(n_sel) * SEL_LEN
    overlap = np.clip(np.minimum(cs[:, None] + CMP_LEN, ss[None, :] + SEL_LEN) - np.maximum(cs[:, None], ss[None, :]), 0, None)
    overlap[n_cmp:] = 0
    c2s_np = np.zeros((HEAD_DIM, n_rows), np.float32)
    c2s_np[:n_sel] = (overlap / CMP_LEN).T
    c2s = jnp.asarray(c2s_np, dtype=BF16)
    assert n_sel <= HEAD_DIM, "the selected-branch key augmentation has 64 lanes, one per selection block"
    neg_onehot = jnp.asarray(np.where((np.arange(S)[:, None] // SEL_LEN) == np.arange(LANES)[None, :] - HEAD_DIM, NEG, 0.0), dtype=BF16)

    tables = _retention_tables(S)
    fbias = jnp.zeros((depth, 1, LANES), F32).at[:, 0, SMALL_FF:SMALL_FF + N_FOX].set(fox_forget_bias)

    x2d = x.reshape(T, D)
    for l in range(depth):
        main, small = _inproj(x2d, norm_attn[l][None, :], wm[l], ws[l], ts["tm"])
        main3 = main.reshape(B, S, N_MAIN)
        small3 = small.reshape(B, S, LANES)
        fox_qa, fox_ka, edges = _forget_cumsum(small3, fbias[l])
        y_ret = _retention(main3, tables, ret_norm_gain[l][None, :])
        per_tile = ts["fox_tk"] // RET_CHUNK
        cq = edges[:, 0::per_tile, 0, SMALL_FF:SMALL_FF + N_FOX]
        ck = edges[:, per_tile - 1::per_tile, 1, SMALL_FF:SMALL_FF + N_FOX]
        cum_edges = jnp.stack([cq.transpose(0, 2, 1), ck.transpose(0, 2, 1)], axis=2).reshape(-1)
        y_fox = _fox(main3, fox_qa, fox_ka, cum_edges, fox_norm_gain[l][None, :], ts["fox_tq"], ts["fox_tk"])
        r = main3[:, :, CB_CMP * LANES:(CB_CMP + 1) * LANES].reshape(B, n_rows, CMP_STRIDE * LANES)
        cmp = _nsa_compress(r, pos_tb[l], wt_all[l], wb_all[l], w2c[l])
        y_nsa = _nsa(main3, cmp, c2s, neg_onehot, small3, nsa_norm_gain[l][None, :], ts["nsa_nb"], ts["nsa_tq"], ts["nsa_tk"], n_cmp)
        x2d = _post(x2d, y_ret.reshape(T, RET_W), y_fox.reshape(T, FOX_W), y_nsa.reshape(T, NSA_W), wo[l],
                    norm_mlp[l][None, :], w1[l], w2[l], norm_final[None, :], ts["tm"], final=(l == depth - 1))
    return x2d.reshape(B, S, D)
```

```python
import functools

import numpy as np
import jax
import jax.numpy as jnp
from jax import lax
from jax.experimental import pallas as pl
from jax.experimental.pallas import tpu as pltpu

F32 = jnp.float32
BF16 = jnp.bfloat16

D_MODEL = 1024
HEAD_DIM = 64
N_RET = 6
N_FOX = 6
N_NSA = 4
RET_W = N_RET * HEAD_DIM
FOX_W = N_FOX * HEAD_DIM
NSA_W = N_NSA * HEAD_DIM
D_FF = 4 * D_MODEL
RMS_EPS = 1e-6
RET_CHUNK = 128
RET_BLOCK = 256
ROPE_BASE = 10000.0
CMP_LEN = 32
CMP_STRIDE = 16
CMP_HIDDEN = 4 * HEAD_DIM
SEL_LEN = 64
TOP_N = 16
WINDOW = 512
FORCED_SCORE = 1e4
NEG = -1e30
LOG2E = float(np.log2(np.e))

LANES = 128
N_PAIR = N_RET // 2

CB_RQ, CB_RK, CB_RV, CB_RG = 0, 3, 6, 9
CB_NQ = 12
CB_FQ, CB_FK, CB_FV = 14, 17, 20
CB_CMP, CB_SEL, CB_WIN = 23, 24, 25
N_MAIN = 26 * LANES
SMALL_FF = 0
SMALL_GATE = 8

VMEM_LIMIT = 56 * 1024 * 1024


def _cparams(sem):
    return pltpu.CompilerParams(dimension_semantics=sem, vmem_limit_bytes=VMEM_LIMIT)


def _resident(shape, index_map):
    return pl.BlockSpec(shape, index_map, pipeline_mode=pl.Buffered(1))


def _in_proj_columns():
    sizes = (RET_W, RET_W, RET_W, RET_W, FOX_W, FOX_W, FOX_W, N_FOX, NSA_W) + (HEAD_DIM,) * 6 + (3 * N_NSA,)
    off = np.concatenate([[0], np.cumsum(sizes)])
    (o_rq, o_rk, o_rv, o_rg, o_fq, o_fk, o_fv, o_ff, o_nq, o_kc, o_vc, o_ks, o_vs, o_kw, o_vw, o_gt) = off[:-1]
    half = HEAD_DIM // 2
    inter = []
    for p in range(N_PAIR):
        a, b = 2 * p, 2 * p + 1
        for h, part in ((a, 0), (b, 0), (a, 1), (b, 1)):
            inter.extend(range(h * HEAD_DIM + part * half, h * HEAD_DIM + (part + 1) * half))
    inter = np.asarray(inter)
    nat = np.arange(RET_W)
    cols, scale = [], []

    def add(idx, s=1.0):
        cols.append(np.asarray(idx))
        scale.append(np.full(len(idx), s, np.float32))

    qk_scale = HEAD_DIM ** -0.5
    sm_scale = qk_scale * LOG2E
    add(o_rq + inter)
    add(o_rk + inter, qk_scale)
    add(o_rv + nat)
    add(o_rg + nat)
    add(o_nq + np.arange(NSA_W), sm_scale)
    add(o_fq + nat, sm_scale)
    add(o_fk + nat)
    add(o_fv + nat)
    for o in (o_kc, o_vc, o_ks, o_vs, o_kw, o_vw):
        add(o + np.arange(HEAD_DIM))
    cols = np.concatenate(cols)
    scale = np.concatenate(scale)
    assert cols.shape[0] == N_MAIN
    small_cols = np.zeros(LANES, np.int64)
    small_mask = np.zeros(LANES, np.float32)
    small_cols[SMALL_FF:SMALL_FF + N_FOX] = o_ff + np.arange(N_FOX)
    small_mask[SMALL_FF:SMALL_FF + N_FOX] = 1.0
    small_cols[SMALL_GATE:SMALL_GATE + 3 * N_NSA] = o_gt + np.arange(3 * N_NSA)
    small_mask[SMALL_GATE:SMALL_GATE + 3 * N_NSA] = 1.0
    return cols, scale, small_cols, small_mask


def _retention_tables(seq):
    half = HEAD_DIM // 2
    inv = 1.0 / (ROPE_BASE ** (jnp.arange(half, dtype=F32) / half))
    ang = jnp.arange(seq, dtype=F32)[:, None] * inv[None, :]
    cos, sin = jnp.cos(ang), jnp.sin(ang)
    cos_t = jnp.concatenate([cos, cos, cos, cos], axis=-1)
    sin_t = jnp.concatenate([-sin, -sin, sin, sin], axis=-1)
    log_gamma = np.log(1.0 - 2.0 ** (-5.0 - np.arange(N_RET, dtype=np.float32))).astype(np.float32)
    C = RET_BLOCK
    idx = np.arange(C, dtype=np.float32)
    diff = idx[:, None] - idx[None, :]
    lane = np.arange(LANES)
    head_k = (lane % HEAD_DIM) // half
    head_v = lane // HEAD_DIM
    d_in = np.zeros((N_PAIR, 2, C, C), np.float32)
    d_k = np.zeros((N_PAIR, C, LANES), np.float32)
    d_q = np.zeros((N_PAIR, C, LANES), np.float32)
    d_c = np.zeros((N_PAIR, LANES, LANES), np.float32)
    for p in range(N_PAIR):
        lg = log_gamma[2 * p:2 * p + 2]
        for j in range(2):
            d_in[p, j] = np.where(diff >= 0, np.exp(lg[j] * np.maximum(diff, 0.0)), 0.0)
        d_k[p] = np.exp(lg[head_k][None, :] * (C - 1.0 - idx)[:, None])
        d_q[p] = np.exp(lg[head_v][None, :] * (idx + 1.0)[:, None])
        d_c[p] = np.broadcast_to(np.exp(lg[head_v] * C)[None, :], (LANES, LANES))
    s_mask = (head_k[:, None] == head_v[None, :]).astype(np.float32)
    return cos_t, sin_t, jnp.asarray(d_in), jnp.asarray(d_k), jnp.asarray(d_q), jnp.asarray(d_c), jnp.asarray(s_mask)


def _rms(x, g):
    return x * lax.rsqrt(jnp.mean(x * x, axis=-1, keepdims=True) + RMS_EPS) * g


def _inproj_kernel(x_ref, g_ref, wm_ref, ws_ref, om_ref, os_ref, *, n_chunk):
    h = _rms(x_ref[...], g_ref[...]).astype(BF16)
    for n0 in range(0, N_MAIN, n_chunk):
        n1 = min(n0 + n_chunk, N_MAIN)
        om_ref[:, n0:n1] = jnp.dot(h, wm_ref[:, n0:n1], preferred_element_type=F32).astype(BF16)
    os_ref[...] = jnp.dot(h, ws_ref[...], preferred_element_type=F32)


def _inproj(x2d, g, wm, ws, tm):
    T = x2d.shape[0]
    return pl.pallas_call(
        functools.partial(_inproj_kernel, n_chunk=512),
        grid=(T // tm,),
        in_specs=[
            pl.BlockSpec((tm, D_MODEL), lambda i: (i, 0)),
            _resident((1, D_MODEL), lambda i: (0, 0)),
            _resident((D_MODEL, N_MAIN), lambda i: (0, 0)),
            _resident((D_MODEL, LANES), lambda i: (0, 0)),
        ],
        out_specs=[
            pl.BlockSpec((tm, N_MAIN), lambda i: (i, 0)),
            pl.BlockSpec((tm, LANES), lambda i: (i, 0)),
        ],
        out_shape=[jax.ShapeDtypeStruct((T, N_MAIN), BF16), jax.ShapeDtypeStruct((T, LANES), F32)],
        compiler_params=_cparams(("parallel",)),
        name="inproj",
    )(x2d, g, wm, ws)


N_SPLIT = 3
AUG_STRIDE = 2 * N_SPLIT


def _split_bf16(x):
    terms, rest = [], x
    for _ in range(N_SPLIT):
        t = rest.astype(BF16)
        rest = rest - t.astype(F32)
        terms.append(t)
    return terms


def _bias_placement():
    mq = np.zeros((N_SPLIT, LANES, LANES), np.float32)
    mk = np.zeros((N_SPLIT, LANES, LANES), np.float32)
    one_q = np.zeros((1, LANES), np.float32)
    one_k = np.zeros((1, LANES), np.float32)
    for h in range(N_FOX):
        base = h * AUG_STRIDE
        for i in range(N_SPLIT):
            mq[i, SMALL_FF + h, base + i] = 1.0
            mk[i, SMALL_FF + h, base + N_SPLIT + i] = -1.0
            one_q[0, base + N_SPLIT + i] = 1.0
            one_k[0, base + i] = 1.0
    return jnp.asarray(mq, BF16), jnp.asarray(mk, BF16), jnp.asarray(one_q), jnp.asarray(one_k)


def _cum_kernel(z_ref, b_ref, tri_ref, mq_ref, mk_ref, oq_ref, ok_ref, qa_ref, ka_ref, edge_ref, *, n_chunks):
    C = RET_CHUNK

    def chunk(c, carry):
        r0 = pl.multiple_of(c * C, C)
        z = z_ref[0, pl.ds(r0, C), :] + b_ref[...]
        lf = jnp.minimum(z, 0.0) - jnp.log1p(jnp.exp(-jnp.abs(z)))
        terms = _split_bf16(lf)
        local = jnp.dot(tri_ref[...], terms[0], preferred_element_type=F32)
        for term in terms[1:]:
            local = local + jnp.dot(tri_ref[...], term, preferred_element_type=F32)
        cs = local + carry
        cs2 = cs * LOG2E
        edge_ref[0, pl.ds(c, 1)] = jnp.concatenate([cs2[0:1], cs2[C - 1:C], jnp.zeros((6, LANES), F32)], axis=0)[None]
        qa, ka = oq_ref[...], ok_ref[...]
        for i, term in enumerate(_split_bf16(cs2)):
            qa = qa + jnp.dot(term, mq_ref[i], preferred_element_type=F32)
            ka = ka + jnp.dot(term, mk_ref[i], preferred_element_type=F32)
        qa_ref[0, pl.ds(r0, C), :] = qa.astype(BF16)
        ka_ref[0, pl.ds(r0, C), :] = ka.astype(BF16)
        return cs[C - 1:C, :]

    unroll = max(u for u in (1, 2, 4) if n_chunks % u == 0)

    def body(i, carry):
        for u in range(unroll):
            carry = chunk(i * unroll + u, carry)
        return carry

    lax.fori_loop(0, n_chunks // unroll, body, jnp.zeros((1, LANES), F32))


def _forget_cumsum(small3, bias_row):
    B, S, _ = small3.shape
    tri = jnp.asarray(np.tril(np.ones((RET_CHUNK, RET_CHUNK), np.float32)), BF16)
    mq, mk, one_q, one_k = _bias_placement()
    const = lambda shape: pl.BlockSpec(shape, lambda b: (0,) * len(shape))
    return pl.pallas_call(
        functools.partial(_cum_kernel, n_chunks=S // RET_CHUNK),
        grid=(B,),
        in_specs=[
            pl.BlockSpec((1, S, LANES), lambda b: (b, 0, 0)),
            const((1, LANES)), const((RET_CHUNK, RET_CHUNK)),
            const((N_SPLIT, LANES, LANES)), const((N_SPLIT, LANES, LANES)), const((1, LANES)), const((1, LANES)),
        ],
        out_specs=[
            pl.BlockSpec((1, S, LANES), lambda b: (b, 0, 0)),
            pl.BlockSpec((1, S, LANES), lambda b: (b, 0, 0)),
            pl.BlockSpec((1, S // RET_CHUNK, 8, LANES), lambda b: (b, 0, 0, 0)),
        ],
        out_shape=[jax.ShapeDtypeStruct((B, S, LANES), BF16), jax.ShapeDtypeStruct((B, S, LANES), BF16),
                   jax.ShapeDtypeStruct((B, S // RET_CHUNK, 8, LANES), F32)],
        compiler_params=_cparams(("parallel",)),
        name="forget_cumsum",
    )(small3, bias_row, tri, mq, mk, one_q, one_k)


def _dot_nt(a, b):
    return lax.dot_general(a, b, (((1,), (1,)), ((), ())), preferred_element_type=F32)


def _ret_kernel(q_ref, k_ref, v_ref, g_ref, cos_ref, sin_ref, din_ref, dk_ref, dq_ref, dc_ref, sm_ref, gain_ref,
                o_ref, state_ref, *, n_chunks, unroll):
    C = RET_BLOCK
    half = HEAD_DIM // 2
    lane = lax.broadcasted_iota(jnp.int32, (C, LANES), 1)
    khead = (lane % HEAD_DIM) // half
    vhead0 = lane < HEAD_DIM
    state_ref[...] = jnp.zeros_like(state_ref)

    def pair_chunk(p, r0):
        rows = pl.ds(r0, C)
        lanes = slice(p * LANES, (p + 1) * LANES)
        cs, sn = cos_ref[rows, :], sin_ref[rows, :]
        q = q_ref[0, rows, lanes].astype(F32)
        k = k_ref[0, rows, lanes].astype(F32)
        q = q * cs + pltpu.roll(q, HEAD_DIM, 1) * sn
        k = k * cs + pltpu.roll(k, HEAD_DIM, 1) * sn
        v = v_ref[0, rows, lanes]
        qb, kb = q.astype(BF16), k.astype(BF16)
        state = state_ref[p]
        cross = jnp.dot(qb, state.astype(BF16), preferred_element_type=F32) * dq_ref[p]
        outs = []
        for j in range(2):
            qm = jnp.where(khead == j, qb, jnp.zeros_like(qb))
            inner = _dot_nt(qm, kb) * din_ref[p, j]
            outs.append(jnp.dot(inner.astype(BF16), v, preferred_element_type=F32))
        out = jnp.where(vhead0, outs[0], outs[1]) + cross
        kd = (k * dk_ref[p]).T.astype(BF16)
        state_ref[p] = state * dc_ref[p] + jnp.dot(kd, v, preferred_element_type=F32) * sm_ref[...]
        inv = 1.0 / HEAD_DIM
        s0 = jnp.sum(jnp.where(vhead0, out, 0.0), axis=-1, keepdims=True)
        s1 = jnp.sum(jnp.where(vhead0, 0.0, out), axis=-1, keepdims=True)
        yc = out - jnp.where(vhead0, s0, s1) * inv
        yc2 = yc * yc
        v0 = jnp.sum(jnp.where(vhead0, yc2, 0.0), axis=-1, keepdims=True)
        v1 = jnp.sum(jnp.where(vhead0, 0.0, yc2), axis=-1, keepdims=True)
        y = yc * lax.rsqrt(jnp.where(vhead0, v0, v1) * inv + RMS_EPS) * gain_ref[:, lanes]
        g = g_ref[0, rows, lanes].astype(F32)
        o_ref[0, rows, lanes] = (y * (g * (1.0 / (1.0 + jnp.exp(-g))))).astype(BF16)

    def body(c, _):
        for u in range(unroll):
            for p in range(N_PAIR):
                pair_chunk(p, pl.multiple_of((c * unroll + u) * C, C))
        return 0

    lax.fori_loop(0, n_chunks // unroll, body, 0)


def _retention(main3, tables, gain_row):
    B, S, _ = main3.shape
    cos_t, sin_t, d_in, d_k, d_q, d_c, s_mask = tables
    C = RET_BLOCK
    n_chunks = S // C
    seq_spec = lambda cb: pl.BlockSpec((1, S, RET_W), lambda b: (b, 0, cb // N_PAIR))
    const = lambda shape: pl.BlockSpec(shape, lambda b: (0,) * len(shape))
    return pl.pallas_call(
        functools.partial(_ret_kernel, n_chunks=n_chunks, unroll=2 if n_chunks % 2 == 0 else 1),
        grid=(B,),
        in_specs=[
            seq_spec(CB_RQ), seq_spec(CB_RK), seq_spec(CB_RV), seq_spec(CB_RG),
            const((S, LANES)), const((S, LANES)),
            const((N_PAIR, 2, C, C)), const((N_PAIR, C, LANES)), const((N_PAIR, C, LANES)),
            const((N_PAIR, LANES, LANES)), const((LANES, LANES)), const((1, RET_W)),
        ],
        out_specs=pl.BlockSpec((1, S, RET_W), lambda b: (b, 0, 0)),
        out_shape=jax.ShapeDtypeStruct((B, S, RET_W), BF16),
        scratch_shapes=[pltpu.VMEM((N_PAIR, LANES, LANES), F32)],
        compiler_params=_cparams(("parallel",)),
        name="retention",
    )(main3, main3, main3, main3, cos_t, sin_t, d_in, d_k, d_q, d_c, s_mask, gain_row)


UNDERFLOW_LOG2 = -170.0


def _head_sq_norm_max(x, havg, head0):
    xf = x.astype(F32)
    sq = xf * xf
    hi = sq.astype(BF16)
    lo = (sq - hi.astype(F32)).astype(BF16)
    mean = jnp.dot(hi, havg, preferred_element_type=F32) + jnp.dot(lo, havg, preferred_element_type=F32)
    bound = mean * (HEAD_DIM * 1.01)
    return jnp.max(jnp.where(head0, bound, 0.0)), jnp.max(jnp.where(head0, 0.0, bound))


def _fox_kernel(cb_ref, q_ref, qa_ref, qs_ref, kh_ref, ka_ref, v_ref, gain_ref, havg_ref, o_ref, k_ref, ft_ref, *, tq, tk, n_tiles):
    b = pl.program_id(0)
    p = pl.program_id(1)
    qi = pl.program_id(2)
    lane_q = lax.broadcasted_iota(jnp.int32, (tq, LANES), 1)
    head0 = lane_q < HEAD_DIM

    @pl.when(qi == 0)
    def _():
        k_ref[0, :, 0:LANES] = kh_ref[0]
        k_ref[0, :, LANES:2 * LANES] = ka_ref[0]
        head0_s = lax.broadcasted_iota(jnp.int32, kh_ref.shape[1:], 1) < HEAD_DIM
        kn = _head_sq_norm_max(kh_ref[0], havg_ref[...], head0_s)
        for i in range(n_tiles):
            qn = _head_sq_norm_max(qs_ref[0, i * tq:(i + 1) * tq, :], havg_ref[...], head0)
            for j in range(2):
                h = 2 * p + j
                slack = 2.0 * jnp.sqrt(qn[j] * kn[j])
                cq = cb_ref[((b * N_FOX + h) * 2 + 0) * n_tiles + i]
                skipped = jnp.int32(0)
                for t in range(i):
                    ck = cb_ref[((b * N_FOX + h) * 2 + 1) * n_tiles + t]
                    skipped = skipped + (slack + cq - ck < UNDERFLOW_LOG2).astype(jnp.int32)
                ft_ref[2 * i + j] = skipped

    q = q_ref[0]
    qa = qa_ref[0]
    first_tile = [ft_ref[2 * qi + j] for j in range(2)]
    qops = []
    for j in range(2):
        qm = jnp.where(head0 if j == 0 else jnp.logical_not(head0), q, jnp.zeros_like(q))
        lo = (2 * p + j) * AUG_STRIDE
        own = jnp.logical_and(lane_q >= lo, lane_q < lo + AUG_STRIDE)
        qops.append(jnp.concatenate([qm, jnp.where(own, qa, jnp.zeros_like(qa))], axis=1))
    q_pos = qi * tq + lax.broadcasted_iota(jnp.int32, (tq, tk), 0)
    k_off = lax.broadcasted_iota(jnp.int32, (tq, tk), 1)

    def tile(j, t, n, carry, masked):
        m, acc = carry
        c0 = pl.multiple_of(t * tk, tk)
        s = _dot_nt(qops[j], k_ref[0, pl.ds(c0, n * tk), :])
        if masked:
            s = jnp.where(k_off + c0 <= q_pos, s, NEG)
        m_new = jnp.maximum(m, jnp.max(s, axis=-1, keepdims=True))
        pr = jnp.exp2(s - m_new)
        own = lax.broadcasted_iota(jnp.int32, (n * tk, LANES), 1) < HEAD_DIM
        vj = jnp.where(own if j == 0 else jnp.logical_not(own), v_ref[0, pl.ds(c0, n * tk), :], jnp.ones((n * tk, LANES), BF16))
        acc = jnp.exp2(m - m_new) * acc + jnp.dot(pr.astype(BF16), vj, preferred_element_type=F32)
        return m_new, acc

    carry = []
    for j in range(2):
        t0 = first_tile[j]
        n_act = qi - t0
        cj = (jnp.full((tq, 1), NEG, F32), jnp.zeros((tq, LANES), F32))
        cj = lax.fori_loop(0, n_act // 2, lambda i, c, j=j, t0=t0: tile(j, t0 + 2 * i, 2, c, False), cj)
        cj = lax.cond(n_act % 2 == 1, lambda c, j=j: tile(j, qi - 1, 1, c, False), lambda c: c, cj)
        carry += list(cj)
    for j in range(2):
        carry[2 * j:2 * j + 2] = tile(j, qi, 1, tuple(carry[2 * j:2 * j + 2]), True)
    o0 = carry[1] * (1.0 / pltpu.roll(carry[1], HEAD_DIM, 1))
    o1 = carry[3] * (1.0 / pltpu.roll(carry[3], HEAD_DIM, 1))
    out = jnp.where(head0, o0, o1)
    sq = out * out
    sq_hi = sq.astype(BF16)
    sq_lo = (sq - sq_hi.astype(F32)).astype(BF16)
    ms = jnp.dot(sq_hi, havg_ref[...], preferred_element_type=F32) + jnp.dot(sq_lo, havg_ref[...], preferred_element_type=F32)
    o_ref[0] = (out * lax.rsqrt(ms + RMS_EPS) * gain_ref[...]).astype(BF16)


def _fox(main3, q_aug, k_aug, cum_edges, gain_row, tq, tk):
    B, S, _ = main3.shape
    assert tq == tk, "the kernel handles exactly one diagonal tile per query block"
    head_of = np.arange(LANES) // HEAD_DIM
    havg = (head_of[:, None] == head_of[None, :]).astype(np.float32) / HEAD_DIM
    grid_spec = pltpu.PrefetchScalarGridSpec(
        num_scalar_prefetch=1,
        grid=(B, N_PAIR, S // tq),
        in_specs=[
            pl.BlockSpec((1, tq, LANES), lambda b, p, i, cb: (b, i, CB_FQ + p)),
            pl.BlockSpec((1, tq, LANES), lambda b, p, i, cb: (b, i, 0)),
            pl.BlockSpec((1, S, LANES), lambda b, p, i, cb: (b, 0, CB_FQ + p)),
            pl.BlockSpec((1, S, LANES), lambda b, p, i, cb: (b, 0, CB_FK + p)),
            pl.BlockSpec((1, S, LANES), lambda b, p, i, cb: (b, 0, 0)),
            pl.BlockSpec((1, S, LANES), lambda b, p, i, cb: (b, 0, CB_FV + p)),
            pl.BlockSpec((1, LANES), lambda b, p, i, cb: (0, p)),
            pl.BlockSpec((LANES, LANES), lambda b, p, i, cb: (0, 0)),
        ],
        out_specs=pl.BlockSpec((1, tq, LANES), lambda b, p, i, cb: (b, i, p)),
        scratch_shapes=[pltpu.VMEM((1, S, 2 * LANES), BF16), pltpu.SMEM((2 * (S // tk),), jnp.int32)],
    )
    return pl.pallas_call(
        functools.partial(_fox_kernel, tq=tq, tk=tk, n_tiles=S // tk),
        grid_spec=grid_spec,
        out_shape=jax.ShapeDtypeStruct((B, S, FOX_W), BF16),
        compiler_params=_cparams(("parallel", "parallel", "arbitrary")),
        name="fox_attention",
    )(cum_edges, main3, q_aug, main3, main3, k_aug, main3, gain_row, jnp.asarray(havg, BF16))


def _gelu_tanh(x):
    return 0.5 * x * (1.0 + jnp.tanh(np.sqrt(2.0 / np.pi).astype(np.float32) * (x + 0.044715 * (x * x * x))))


def _cmp_kernel(r_ref, pos_ref, wt_ref, wb_ref, w2_ref, o_ref):
    r = r_ref[0]
    n = r.shape[0]
    top = jnp.dot(r, wt_ref[...], preferred_element_type=F32)
    bot = jnp.dot(r, wb_ref[...], preferred_element_type=F32)
    cpos = jnp.dot(pos_ref[0:8, :], wt_ref[...], preferred_element_type=F32)
    cpos = cpos + jnp.dot(pos_ref[8:16, :], wb_ref[...], preferred_element_type=F32)
    hid = _gelu_tanh(top + pltpu.roll(bot, n - 1, 0) + cpos[0:1, :])
    o_ref[0] = jnp.dot(hid.astype(BF16), w2_ref[...], preferred_element_type=F32)


def _nsa_compress(r, pos2, wt, wb, w2):
    B, n, _ = r.shape
    width = CMP_STRIDE * LANES
    return pl.pallas_call(
        _cmp_kernel,
        grid=(B,),
        in_specs=[
            pl.BlockSpec((1, n, width), lambda b: (b, 0, 0)),
            pl.BlockSpec((16, width), lambda b: (0, 0)),
            pl.BlockSpec((width, 2 * CMP_HIDDEN), lambda b: (0, 0)),
            pl.BlockSpec((width, 2 * CMP_HIDDEN), lambda b: (0, 0)),
            pl.BlockSpec((2 * CMP_HIDDEN, LANES), lambda b: (0, 0)),
        ],
        out_specs=pl.BlockSpec((1, n, LANES), lambda b: (b, 0, 0)),
        out_shape=jax.ShapeDtypeStruct((B, n, LANES), F32),
        compiler_params=_cparams(("parallel",)),
        name="nsa_compress",
    )(r, pos2, wt, wb, w2)


def _softmax_rows(s, mask):
    sm = jnp.where(mask, s, NEG)
    m = jnp.maximum(jnp.max(sm, axis=-1, keepdims=True), 0.1 * NEG)
    e = jnp.exp2(sm - m)
    l = jnp.sum(e, axis=-1, keepdims=True)
    return e / jnp.where(l > 0.0, l, 1.0)


def _nsa_kernel(q_ref, cmp_ref, c2s_ref, sel_ref, noh_ref, win_ref, gl_ref, gain_ref, gsel_ref, havg_ref, o_ref, ksa_ref, *, nb, tq, tk, n_cmp,
                n_sel, top_n):
    H = N_NSA
    qi = pl.program_id(1)
    start = qi * tq

    @pl.when(qi == 0)
    def _():
        key_lanes = lax.broadcasted_iota(jnp.int32, noh_ref.shape, 1) < HEAD_DIM
        for bb in range(nb):
            ksa_ref[bb] = jnp.where(key_lanes, sel_ref[bb], noh_ref[...])

    t_row = start + lax.broadcasted_iota(jnp.int32, (tq, 1), 0)
    t_all = jnp.concatenate([t_row] * H, axis=0)
    zeros64 = jnp.zeros((tq, HEAD_DIM), BF16)
    wlen = WINDOW + tq
    base = pl.multiple_of(jnp.maximum(start - WINDOW, 0), tq)
    n_pad = cmp_ref.shape[1]
    assert CMP_STRIDE == 16 and tq & (tq - 1) == 0
    last_cmp = (t_all - (CMP_LEN - 1)) >> 4
    row_in_block = lax.broadcasted_iota(jnp.int32, (H * tq, wlen), 0) & (tq - 1)
    win_lag = (start - base) - (lax.broadcasted_iota(jnp.int32, (H * tq, wlen), 1) - row_in_block)

    def prologue(bb):
        q = q_ref[bb]
        q_heads = [q[:, h * HEAD_DIM:(h + 1) * HEAD_DIM] for h in range(H)]
        q_pad = jnp.concatenate([jnp.concatenate([qh, zeros64], axis=1) for qh in q_heads], axis=0)

        win = win_ref[bb, pl.ds(base, wlen), :]
        s_w = _dot_nt(q_pad, win)
        s_w = jnp.where(lax.bitcast_convert_type(win_lag, jnp.uint32) < WINDOW, s_w, NEG)
        e_w = jnp.exp2(s_w - jnp.max(s_w, axis=-1, keepdims=True))
        win1 = jnp.where(lax.broadcasted_iota(jnp.int32, (wlen, LANES), 1) < HEAD_DIM, jnp.ones((wlen, LANES), BF16), win)
        o_w = jnp.dot(e_w.astype(BF16), win1, preferred_element_type=F32)

        cmp = cmp_ref[bb]
        cmp_hi = cmp.astype(BF16)
        cmp_lo = (cmp - cmp_hi.astype(F32)).astype(BF16)
        s_c = _dot_nt(q_pad, cmp_hi) + _dot_nt(q_pad, cmp_lo)
        n_id = lax.broadcasted_iota(jnp.int32, (H * tq, n_pad), 1)
        p_c = _softmax_rows(s_c, n_id <= last_cmp)
        o_c = jnp.dot(p_c.astype(BF16), cmp_hi, preferred_element_type=F32)

        p_sum = p_c[0:tq]
        for h in range(1, H):
            p_sum = p_sum + p_c[h * tq:(h + 1) * tq]
        p_hi = p_sum.astype(BF16)
        p_lo = (p_sum - p_hi.astype(F32)).astype(BF16)
        c2s_t = c2s_ref[...]
        imp = _dot_nt(c2s_t, p_hi) + _dot_nt(c2s_t, p_lo)
        t_lane = start + lax.broadcasted_iota(jnp.int32, (HEAD_DIM, tq), 1)
        s_id = lax.broadcasted_iota(jnp.int32, (HEAD_DIM, tq), 0)
        cur = t_lane >> 6
        forced = jnp.logical_or(s_id == 0, jnp.logical_or(s_id == cur, s_id == cur - 1))
        score = jnp.where(forced, FORCED_SCORE, imp)
        score = jnp.where(s_id * SEL_LEN <= t_lane, score, -1.0)
        SUB = 8
        groups = [score[g * SUB:(g + 1) * SUB, :] for g in range(HEAD_DIM // SUB)]
        ranks = [jnp.zeros((SUB, tq), jnp.int32) for _ in groups]
        sub_id = lax.broadcasted_iota(jnp.int32, (SUB, tq), 0)
        for c in range(n_sel):
            row = score[c:c + 1, :]
            for g in range(len(groups)):
                if g * SUB > c:
                    before = row >= groups[g]
                elif g * SUB + SUB - 1 < c:
                    before = row > groups[g]
                else:
                    before = jnp.logical_or(row > groups[g], jnp.logical_and(row == groups[g], sub_id > c - g * SUB))
                ranks[g] = ranks[g] + before.astype(jnp.int32)
        not_sel = jnp.concatenate([jnp.where(r < top_n, 0.0, 1.0) for r in ranks], axis=0)
        ns = jnp.concatenate([jnp.zeros((HEAD_DIM, tq), F32), not_sel], axis=0).T.astype(BF16)
        q_aug = q_pad + jnp.concatenate([ns] * H, axis=0)
        return q_aug, o_c, o_w

    pro = [prologue(bb) for bb in range(nb)]

    def sel_tile(bb, c0, size, carry, diagonal):
        m, acc = carry
        s = _dot_nt(pro[bb][0], ksa_ref[bb, pl.ds(c0, size), :])
        if diagonal:
            r_id = lax.broadcasted_iota(jnp.int32, (H * tq, size), 0) & (tq - 1)
            c_id = lax.broadcasted_iota(jnp.int32, (H * tq, size), 1)
            s = jnp.where(c_id <= r_id, s, NEG)
        m_new = jnp.maximum(m, jnp.max(s, axis=-1, keepdims=True))
        pr = jnp.exp2(s - m_new)
        key_lanes = lax.broadcasted_iota(jnp.int32, (size, LANES), 1) < HEAD_DIM
        v1 = jnp.where(key_lanes, jnp.ones((size, LANES), BF16), sel_ref[bb, pl.ds(c0, size), :])
        acc = jnp.exp2(m - m_new) * acc + jnp.dot(pr.astype(BF16), v1, preferred_element_type=F32)
        return m_new, acc

    def all_rows(c0, size, carries, diagonal):
        return tuple(sel_tile(bb, c0, size, carries[bb], diagonal) for bb in range(nb))

    carries = tuple((jnp.full((H * tq, 1), NEG, F32), jnp.zeros((H * tq, LANES), F32)) for _ in range(nb))
    n_full = start // tk
    carries = lax.fori_loop(0, n_full, lambda t, c: all_rows(pl.multiple_of(t * tk, tk), tk, c, False), carries)
    pos = n_full * tk
    size = tk // 2
    while size >= tq:
        take = start - pos >= size
        carries = lax.cond(take, lambda c, pos=pos, size=size: all_rows(pl.multiple_of(pos, tq), size, c, False), lambda c: c, carries)
        pos = pos + jnp.where(take, size, 0)
        size //= 2
    carries = all_rows(pl.multiple_of(start, tq), tq, carries, True)

    first_half = lax.broadcasted_iota(jnp.int32, (tq, LANES), 1) < HEAD_DIM

    def place(acc, normalise):
        cols = []
        for h in range(0, H, 2):
            a0, a1 = acc[h * tq:(h + 1) * tq], acc[(h + 1) * tq:(h + 2) * tq]
            r0 = pltpu.roll(a0, HEAD_DIM, 1)
            if normalise:
                v0 = r0 * (1.0 / a0)
                v1 = a1 * (1.0 / pltpu.roll(a1, HEAD_DIM, 1))
            else:
                v0, v1 = r0, a1
            cols.append(jnp.where(first_half, v0, v1))
        return jnp.concatenate(cols, axis=1)

    for bb in range(nb):
        _, o_c, o_w = pro[bb]
        gates = 1.0 / (1.0 + jnp.exp(-gl_ref[bb]))
        g_hi = gates.astype(BF16)
        g_lo = (gates - g_hi.astype(F32)).astype(BF16)
        mix = None
        for c, y in enumerate((place(o_c, False), place(carries[bb][1], True), place(o_w, True))):
            g = jnp.dot(g_hi, gsel_ref[c], preferred_element_type=F32) + jnp.dot(g_lo, gsel_ref[c], preferred_element_type=F32)
            mix = g * y if mix is None else mix + g * y
        sq = mix * mix
        sq_hi = sq.astype(BF16)
        sq_lo = (sq - sq_hi.astype(F32)).astype(BF16)
        ms = jnp.dot(sq_hi, havg_ref[...], preferred_element_type=F32) + jnp.dot(sq_lo, havg_ref[...], preferred_element_type=F32)
        o_ref[bb] = (mix * lax.rsqrt(ms + RMS_EPS) * gain_ref[...]).astype(BF16)


def _nsa(main3, cmp, c2s, neg_onehot, small3, gain_row, nb, tq, tk, n_cmp):
    B, S, _ = main3.shape
    n_sel = S // SEL_LEN
    n_pad = cmp.shape[1]
    gsel = np.zeros((3, LANES, NSA_W), np.float32)
    for h in range(N_NSA):
        for c in range(3):
            gsel[c, SMALL_GATE + 3 * h + c, h * HEAD_DIM:(h + 1) * HEAD_DIM] = 1.0
    head_of = np.arange(NSA_W) // HEAD_DIM
    havg = (head_of[:, None] == head_of[None, :]).astype(np.float32) / HEAD_DIM
    return pl.pallas_call(
        functools.partial(_nsa_kernel, nb=nb, tq=tq, tk=tk, n_cmp=n_cmp, n_sel=n_sel, top_n=min(TOP_N, n_sel)),
        grid=(B // nb, S // tq),
        in_specs=[
            pl.BlockSpec((nb, tq, NSA_W), lambda b, i: (b, i, CB_NQ // 2)),
            pl.BlockSpec((nb, n_pad, LANES), lambda b, i: (b, 0, 0)),
            pl.BlockSpec((HEAD_DIM, n_pad), lambda b, i: (0, 0)),
            pl.BlockSpec((nb, S, LANES), lambda b, i: (b, 0, CB_SEL)),
            pl.BlockSpec((S, LANES), lambda b, i: (0, 0)),
            pl.BlockSpec((nb, S, LANES), lambda b, i: (b, 0, CB_WIN)),
            pl.BlockSpec((nb, tq, LANES), lambda b, i: (b, i, 0)),
            pl.BlockSpec((1, NSA_W), lambda b, i: (0, 0)),
            pl.BlockSpec((3, LANES, NSA_W), lambda b, i: (0, 0, 0)),
            pl.BlockSpec((NSA_W, NSA_W), lambda b, i: (0, 0)),
        ],
        out_specs=pl.BlockSpec((nb, tq, NSA_W), lambda b, i: (b, i, 0)),
        out_shape=jax.ShapeDtypeStruct((B, S, NSA_W), BF16),
        scratch_shapes=[pltpu.VMEM((nb, S, LANES), BF16)],
        compiler_params=_cparams(("parallel", "arbitrary")),
        name="nsa_attention",
    )(main3, cmp, c2s, main3, neg_onehot, main3, small3, gain_row, jnp.asarray(gsel, BF16), jnp.asarray(havg, BF16))


def _post_kernel(x_ref, yr_ref, yf_ref, yn_ref, wo_ref, g_ref, w1_ref, w2_ref, gf_ref, o_ref, *, ff_chunk, final):
    x = x_ref[...]
    x = x + jnp.dot(yr_ref[...], wo_ref[0:RET_W, :], preferred_element_type=F32)
    x = x + jnp.dot(yf_ref[...], wo_ref[RET_W:RET_W + FOX_W, :], preferred_element_type=F32)
    x = x + jnp.dot(yn_ref[...], wo_ref[RET_W + FOX_W:, :], preferred_element_type=F32)
    h = _rms(x, g_ref[...]).astype(BF16)
    o_ref[...] = x
    for c0 in range(0, D_FF, ff_chunk):
        hid = jnp.maximum(jnp.dot(h, w1_ref[:, c0:c0 + ff_chunk], preferred_element_type=F32), 0.0)
        o_ref[...] += jnp.dot((hid * hid).astype(BF16), w2_ref[c0:c0 + ff_chunk, :], preferred_element_type=F32)
    if final:
        o_ref[...] = _rms(o_ref[...], gf_ref[...])


def _post(x2d, yr, yf, yn, wo, g, w1, w2, gf, tm, final):
    T = x2d.shape[0]
    row = lambda w: pl.BlockSpec((tm, w), lambda i: (i, 0))
    return pl.pallas_call(
        functools.partial(_post_kernel, ff_chunk=512, final=final),
        grid=(T // tm,),
        in_specs=[
            row(D_MODEL), row(RET_W), row(FOX_W), row(NSA_W),
            _resident((D_MODEL, D_MODEL), lambda i: (0, 0)),
            _resident((1, D_MODEL), lambda i: (0, 0)),
            _resident((D_MODEL, D_FF), lambda i: (0, 0)),
            _resident((D_FF, D_MODEL), lambda i: (0, 0)),
            _resident((1, D_MODEL), lambda i: (0, 0)),
        ],
        out_specs=row(D_MODEL),
        out_shape=jax.ShapeDtypeStruct((T, D_MODEL), F32),
        compiler_params=_cparams(("parallel",)),
        name="outproj_mlp",
    )(x2d, yr, yf, yn, wo, g, w1, w2, gf)


def _tile_sizes(batch):
    return dict(tm=512, fox_tq=512, fox_tk=512, nsa_tq=256, nsa_tk=1024, nsa_nb=4 if batch % 4 == 0 else 1)


def kernel(x, norm_attn, w_in, fox_forget_bias, ret_norm_gain, fox_norm_gain, nsa_norm_gain, nsa_cmp_pos_k, nsa_cmp_pos_v, nsa_cmp_w1_k, nsa_cmp_w2_k, nsa_cmp_w1_v, nsa_cmp_w2_v, w_out, norm_mlp, w_mlp_in, w_mlp_out, norm_final):
    B, S, D = x.shape
    depth = w_in.shape[0]
    assert D == D_MODEL and S % 512 == 0 and S >= WINDOW + 256
    T = B * S
    ts = _tile_sizes(B)

    cols, scale, small_cols, small_mask = _in_proj_columns()
    wm = (w_in[:, :, cols] * scale).astype(BF16)
    ws = (w_in[:, :, small_cols] * small_mask).astype(BF16)
    wo = w_out.astype(BF16)
    w1 = w_mlp_in.astype(BF16)
    w2 = w_mlp_out.astype(BF16)

    def expand_w1(wk, wv):
        L = wk.shape[0]
        wk = wk.reshape(L, CMP_LEN, HEAD_DIM, CMP_HIDDEN)
        wv = wv.reshape(L, CMP_LEN, HEAD_DIM, CMP_HIDDEN)
        z = jnp.zeros_like(wk)
        full = jnp.concatenate([jnp.concatenate([wk, z], axis=-1), jnp.concatenate([z, wv], axis=-1)], axis=2)
        full = full.reshape(L, CMP_LEN * LANES, 2 * CMP_HIDDEN).astype(BF16)
        return full[:, :CMP_STRIDE * LANES], full[:, CMP_STRIDE * LANES:]

    wt_all, wb_all = expand_w1(nsa_cmp_w1_k, nsa_cmp_w1_v)
    zk = jnp.zeros_like(nsa_cmp_w2_k)
    w2c = jnp.concatenate([jnp.concatenate([nsa_cmp_w2_k, zk], axis=-1),
                           jnp.concatenate([zk, nsa_cmp_w2_v], axis=-1)], axis=1).astype(BF16)
    pos = jnp.concatenate([nsa_cmp_pos_k, nsa_cmp_pos_v], axis=-1)
    pos_tb = pos.reshape(depth, 2, 1, CMP_STRIDE * LANES)
    pos_tb = jnp.broadcast_to(pos_tb, (depth, 2, 8, CMP_STRIDE * LANES)).reshape(depth, 16, CMP_STRIDE * LANES).astype(BF16)

    n_cmp = (S - CMP_LEN) // CMP_STRIDE + 1
    n_sel = S // SEL_LEN
    n_rows = S // CMP_STRIDE
    cs = np.arange(n_rows) * CMP_STRIDE
    ss = np.arange(n_sel) * SEL_LEN
    overlap = np.clip(np.minimum(cs[:, None] + CMP_LEN, ss[None, :] + SEL_LEN) - np.maximum(cs[:, None], ss[None, :]), 0, None)
    overlap[n_cmp:] = 0
    c2s_np = np.zeros((HEAD_DIM, n_rows), np.float32)
    c2s_np[:n_sel] = (overlap / CMP_LEN).T
    c2s = jnp.asarray(c2s_np, dtype=BF16)
    assert n_sel <= HEAD_DIM, "the selected-branch key augmentation has 64 lanes, one per selection block"
    neg_onehot = jnp.asarray(np.where((np.arange(S)[:, None] // SEL_LEN) == np.arange(LANES)[None, :] - HEAD_DIM, NEG, 0.0), dtype=BF16)

    tables = _retention_tables(S)
    fbias = jnp.zeros((depth, 1, LANES), F32).at[:, 0, SMALL_FF:SMALL_FF + N_FOX].set(fox_forget_bias)

    x2d = x.reshape(T, D)
    for l in range(depth):
        main, small = _inproj(x2d, norm_attn[l][None, :], wm[l], ws[l], ts["tm"])
        main3 = main.reshape(B, S, N_MAIN)
        small3 = small.reshape(B, S, LANES)
        fox_qa, fox_ka, edges = _forget_cumsum(small3, fbias[l])
        y_ret = _retention(main3, tables, ret_norm_gain[l][None, :])
        per_tile = ts["fox_tk"] // RET_CHUNK
        cq = edges[:, 0::per_tile, 0, SMALL_FF:SMALL_FF + N_FOX]
        ck = edges[:, per_tile - 1::per_tile, 1, SMALL_FF:SMALL_FF + N_FOX]
        cum_edges = jnp.stack([cq.transpose(0, 2, 1), ck.transpose(0, 2, 1)], axis=2).reshape(-1)
        y_fox = _fox(main3, fox_qa, fox_ka, cum_edges, fox_norm_gain[l][None, :], ts["fox_tq"], ts["fox_tk"])
        r = main3[:, :, CB_CMP * LANES:(CB_CMP + 1) * LANES].reshape(B, n_rows, CMP_STRIDE * LANES)
        cmp = _nsa_compress(r, pos_tb[l], wt_all[l], wb_all[l], w2c[l])
        y_nsa = _nsa(main3, cmp, c2s, neg_onehot, small3, nsa_norm_gain[l][None, :], ts["nsa_nb"], ts["nsa_tq"], ts["nsa_tk"], n_cmp)
        x2d = _post(x2d, y_ret.reshape(T, RET_W), y_fox.reshape(T, FOX_W), y_nsa.reshape(T, NSA_W), wo[l],
                    norm_mlp[l][None, :], w1[l], w2[l], norm_final[None, :], ts["tm"], final=(l == depth - 1))
    return x2d.reshape(B, S, D)
```

```python
import functools

import numpy as np
import jax
import jax.numpy as jnp
from jax import lax
from jax.experimental import pallas as pl
from jax.experimental.pallas import tpu as pltpu

F32 = jnp.float32
BF16 = jnp.bfloat16

D_MODEL = 1024
HEAD_DIM = 64
N_RET = 6
N_FOX = 6
N_NSA = 4
RET_W = N_RET * HEAD_DIM
FOX_W = N_FOX * HEAD_DIM
NSA_W = N_NSA * HEAD_DIM
D_FF = 4 * D_MODEL
RMS_EPS = 1e-6
RET_CHUNK = 128
RET_BLOCK = 256
ROPE_BASE = 10000.0
CMP_LEN = 32
CMP_STRIDE = 16
CMP_HIDDEN = 4 * HEAD_DIM
SEL_LEN = 64
TOP_N = 16
WINDOW = 512
FORCED_SCORE = 1e4
NEG = -1e30
LOG2E = float(np.log2(np.e))

LANES = 128
N_PAIR = N_RET // 2

CB_RQ, CB_RK, CB_RV, CB_RG = 0, 3, 6, 9
CB_NQ = 12
CB_FQ, CB_FK, CB_FV = 14, 17, 20
CB_CMP, CB_SEL, CB_WIN = 23, 24, 25
N_MAIN = 26 * LANES
SMALL_FF = 0
SMALL_GATE = 8

VMEM_LIMIT = 56 * 1024 * 1024


def _cparams(sem):
    return pltpu.CompilerParams(dimension_semantics=sem, vmem_limit_bytes=VMEM_LIMIT)


def _resident(shape, index_map):
    return pl.BlockSpec(shape, index_map, pipeline_mode=pl.Buffered(1))


def _in_proj_columns():
    sizes = (RET_W, RET_W, RET_W, RET_W, FOX_W, FOX_W, FOX_W, N_FOX, NSA_W) + (HEAD_DIM,) * 6 + (3 * N_NSA,)
    off = np.concatenate([[0], np.cumsum(sizes)])
    (o_rq, o_rk, o_rv, o_rg, o_fq, o_fk, o_fv, o_ff, o_nq, o_kc, o_vc, o_ks, o_vs, o_kw, o_vw, o_gt) = off[:-1]
    half = HEAD_DIM // 2
    inter = []
    for p in range(N_PAIR):
        a, b = 2 * p, 2 * p + 1
        for h, part in ((a, 0), (b, 0), (a, 1), (b, 1)):
            inter.extend(range(h * HEAD_DIM + part * half, h * HEAD_DIM + (part + 1) * half))
    inter = np.asarray(inter)
    nat = np.arange(RET_W)
    cols, scale = [], []

    def add(idx, s=1.0):
        cols.append(np.asarray(idx))
        scale.append(np.full(len(idx), s, np.float32))

    qk_scale = HEAD_DIM ** -0.5
    sm_scale = qk_scale * LOG2E
    add(o_rq + inter)
    add(o_rk + inter, qk_scale)
    add(o_rv + nat)
    add(o_rg + nat)
    add(o_nq + np.arange(NSA_W), sm_scale)
    add(o_fq + nat, sm_scale)
    add(o_fk + nat)
    add(o_fv + nat)
    for o in (o_kc, o_vc, o_ks, o_vs, o_kw, o_vw):
        add(o + np.arange(HEAD_DIM))
    cols = np.concatenate(cols)
    scale = np.concatenate(scale)
    assert cols.shape[0] == N_MAIN
    small_cols = np.zeros(LANES, np.int64)
    small_mask = np.zeros(LANES, np.float32)
    small_cols[SMALL_FF:SMALL_FF + N_FOX] = o_ff + np.arange(N_FOX)
    small_mask[SMALL_FF:SMALL_FF + N_FOX] = 1.0
    small_cols[SMALL_GATE:SMALL_GATE + 3 * N_NSA] = o_gt + np.arange(3 * N_NSA)
    small_mask[SMALL_GATE:SMALL_GATE + 3 * N_NSA] = 1.0
    return cols, scale, small_cols, small_mask


def _retention_tables(seq):
    half = HEAD_DIM // 2
    inv = 1.0 / (ROPE_BASE ** (jnp.arange(half, dtype=F32) / half))
    ang = jnp.arange(seq, dtype=F32)[:, None] * inv[None, :]
    cos, sin = jnp.cos(ang), jnp.sin(ang)
    cos_t = jnp.concatenate([cos, cos, cos, cos], axis=-1)
    sin_t = jnp.concatenate([-sin, -sin, sin, sin], axis=-1)
    log_gamma = np.log(1.0 - 2.0 ** (-5.0 - np.arange(N_RET, dtype=np.float32))).astype(np.float32)
    C = RET_BLOCK
    idx = np.arange(C, dtype=np.float32)
    diff = idx[:, None] - idx[None, :]
    lane = np.arange(LANES)
    head_k = (lane % HEAD_DIM) // half
    head_v = lane // HEAD_DIM
    d_in = np.zeros((N_PAIR, 2, C, C), np.float32)
    d_k = np.zeros((N_PAIR, C, LANES), np.float32)
    d_q = np.zeros((N_PAIR, C, LANES), np.float32)
    d_c = np.zeros((N_PAIR, LANES, LANES), np.float32)
    for p in range(N_PAIR):
        lg = log_gamma[2 * p:2 * p + 2]
        for j in range(2):
            d_in[p, j] = np.where(diff >= 0, np.exp(lg[j] * np.maximum(diff, 0.0)), 0.0)
        d_k[p] = np.exp(lg[head_k][None, :] * (C - 1.0 - idx)[:, None])
        d_q[p] = np.exp(lg[head_v][None, :] * (idx + 1.0)[:, None])
        d_c[p] = np.broadcast_to(np.exp(lg[head_v] * C)[None, :], (LANES, LANES))
    s_mask = (head_k[:, None] == head_v[None, :]).astype(np.float32)
    return cos_t, sin_t, jnp.asarray(d_in), jnp.asarray(d_k), jnp.asarray(d_q), jnp.asarray(d_c), jnp.asarray(s_mask)


def _rms(x, g):
    return x * lax.rsqrt(jnp.mean(x * x, axis=-1, keepdims=True) + RMS_EPS) * g


def _inproj_kernel(x_ref, g_ref, wm_ref, ws_ref, om_ref, os_ref, *, n_chunk):
    h = _rms(x_ref[...], g_ref[...]).astype(BF16)
    for n0 in range(0, N_MAIN, n_chunk):
        n1 = min(n0 + n_chunk, N_MAIN)
        om_ref[:, n0:n1] = jnp.dot(h, wm_ref[:, n0:n1], preferred_element_type=F32).astype(BF16)
    os_ref[...] = jnp.dot(h, ws_ref[...], preferred_element_type=F32)


def _inproj(x2d, g, wm, ws, tm):
    T = x2d.shape[0]
    return pl.pallas_call(
        functools.partial(_inproj_kernel, n_chunk=512),
        grid=(T // tm,),
        in_specs=[
            pl.BlockSpec((tm, D_MODEL), lambda i: (i, 0)),
            _resident((1, D_MODEL), lambda i: (0, 0)),
            _resident((D_MODEL, N_MAIN), lambda i: (0, 0)),
            _resident((D_MODEL, LANES), lambda i: (0, 0)),
        ],
        out_specs=[
            pl.BlockSpec((tm, N_MAIN), lambda i: (i, 0)),
            pl.BlockSpec((tm, LANES), lambda i: (i, 0)),
        ],
        out_shape=[jax.ShapeDtypeStruct((T, N_MAIN), BF16), jax.ShapeDtypeStruct((T, LANES), F32)],
        compiler_params=_cparams(("parallel",)),
        name="inproj",
    )(x2d, g, wm, ws)


N_SPLIT = 3
AUG_STRIDE = 2 * N_SPLIT


def _split_bf16(x):
    terms, rest = [], x
    for _ in range(N_SPLIT):
        t = rest.astype(BF16)
        rest = rest - t.astype(F32)
        terms.append(t)
    return terms


def _bias_placement():
    mq = np.zeros((N_SPLIT, LANES, LANES), np.float32)
    mk = np.zeros((N_SPLIT, LANES, LANES), np.float32)
    one_q = np.zeros((1, LANES), np.float32)
    one_k = np.zeros((1, LANES), np.float32)
    for h in range(N_FOX):
        base = h * AUG_STRIDE
        for i in range(N_SPLIT):
            mq[i, SMALL_FF + h, base + i] = 1.0
            mk[i, SMALL_FF + h, base + N_SPLIT + i] = -1.0
            one_q[0, base + N_SPLIT + i] = 1.0
            one_k[0, base + i] = 1.0
    return jnp.asarray(mq, BF16), jnp.asarray(mk, BF16), jnp.asarray(one_q), jnp.asarray(one_k)


def _cum_kernel(z_ref, b_ref, tri_ref, mq_ref, mk_ref, oq_ref, ok_ref, qa_ref, ka_ref, edge_ref, *, n_chunks):
    C = RET_CHUNK

    def chunk(c, carry):
        r0 = pl.multiple_of(c * C, C)
        z = z_ref[0, pl.ds(r0, C), :] + b_ref[...]
        lf = jnp.minimum(z, 0.0) - jnp.log1p(jnp.exp(-jnp.abs(z)))
        terms = _split_bf16(lf)
        local = jnp.dot(tri_ref[...], terms[0], preferred_element_type=F32)
        for term in terms[1:]:
            local = local + jnp.dot(tri_ref[...], term, preferred_element_type=F32)
        cs = local + carry
        cs2 = cs * LOG2E
        edge_ref[0, pl.ds(c, 1)] = jnp.concatenate([cs2[0:1], cs2[C - 1:C], jnp.zeros((6, LANES), F32)], axis=0)[None]
        qa, ka = oq_ref[...], ok_ref[...]
        for i, term in enumerate(_split_bf16(cs2)):
            qa = qa + jnp.dot(term, mq_ref[i], preferred_element_type=F32)
            ka = ka + jnp.dot(term, mk_ref[i], preferred_element_type=F32)
        qa_ref[0, pl.ds(r0, C), :] = qa.astype(BF16)
        ka_ref[0, pl.ds(r0, C), :] = ka.astype(BF16)
        return cs[C - 1:C, :]

    unroll = max(u for u in (1, 2, 4) if n_chunks % u == 0)

    def body(i, carry):
        for u in range(unroll):
            carry = chunk(i * unroll + u, carry)
        return carry

    lax.fori_loop(0, n_chunks // unroll, body, jnp.zeros((1, LANES), F32))


def _forget_cumsum(small3, bias_row):
    B, S, _ = small3.shape
    tri = jnp.asarray(np.tril(np.ones((RET_CHUNK, RET_CHUNK), np.float32)), BF16)
    mq, mk, one_q, one_k = _bias_placement()
    const = lambda shape: pl.BlockSpec(shape, lambda b: (0,) * len(shape))
    return pl.pallas_call(
        functools.partial(_cum_kernel, n_chunks=S // RET_CHUNK),
        grid=(B,),
        in_specs=[
            pl.BlockSpec((1, S, LANES), lambda b: (b, 0, 0)),
            const((1, LANES)), const((RET_CHUNK, RET_CHUNK)),
            const((N_SPLIT, LANES, LANES)), const((N_SPLIT, LANES, LANES)), const((1, LANES)), const((1, LANES)),
        ],
        out_specs=[
            pl.BlockSpec((1, S, LANES), lambda b: (b, 0, 0)),
            pl.BlockSpec((1, S, LANES), lambda b: (b, 0, 0)),
            pl.BlockSpec((1, S // RET_CHUNK, 8, LANES), lambda b: (b, 0, 0, 0)),
        ],
        out_shape=[jax.ShapeDtypeStruct((B, S, LANES), BF16), jax.ShapeDtypeStruct((B, S, LANES), BF16),
                   jax.ShapeDtypeStruct((B, S // RET_CHUNK, 8, LANES), F32)],
        compiler_params=_cparams(("parallel",)),
        name="forget_cumsum",
    )(small3, bias_row, tri, mq, mk, one_q, one_k)


def _dot_nt(a, b):
    return lax.dot_general(a, b, (((1,), (1,)), ((), ())), preferred_element_type=F32)


def _ret_kernel(q_ref, k_ref, v_ref, g_ref, cos_ref, sin_ref, din_ref, dk_ref, dq_ref, dc_ref, sm_ref, gain_ref,
                o_ref, state_ref, *, n_chunks, unroll):
    C = RET_BLOCK
    half = HEAD_DIM // 2
    lane = lax.broadcasted_iota(jnp.int32, (C, LANES), 1)
    khead = (lane % HEAD_DIM) // half
    vhead0 = lane < HEAD_DIM
    state_ref[...] = jnp.zeros_like(state_ref)

    def pair_chunk(p, r0):
        rows = pl.ds(r0, C)
        lanes = slice(p * LANES, (p + 1) * LANES)
        cs, sn = cos_ref[rows, :], sin_ref[rows, :]
        q = q_ref[0, rows, lanes].astype(F32)
        k = k_ref[0, rows, lanes].astype(F32)
        q = q * cs + pltpu.roll(q, HEAD_DIM, 1) * sn
        k = k * cs + pltpu.roll(k, HEAD_DIM, 1) * sn
        v = v_ref[0, rows, lanes]
        qb, kb = q.astype(BF16), k.astype(BF16)
        state = state_ref[p]
        cross = jnp.dot(qb, state.astype(BF16), preferred_element_type=F32) * dq_ref[p]
        outs = []
        for j in range(2):
            qm = jnp.where(khead == j, qb, jnp.zeros_like(qb))
            inner = _dot_nt(qm, kb) * din_ref[p, j]
            outs.append(jnp.dot(inner.astype(BF16), v, preferred_element_type=F32))
        out = jnp.where(vhead0, outs[0], outs[1]) + cross
        kd = (k * dk_ref[p]).T.astype(BF16)
        state_ref[p] = state * dc_ref[p] + jnp.dot(kd, v, preferred_element_type=F32) * sm_ref[...]
        inv = 1.0 / HEAD_DIM
        s0 = jnp.sum(jnp.where(vhead0, out, 0.0), axis=-1, keepdims=True)
        s1 = jnp.sum(jnp.where(vhead0, 0.0, out), axis=-1, keepdims=True)
        yc = out - jnp.where(vhead0, s0, s1) * inv
        yc2 = yc * yc
        v0 = jnp.sum(jnp.where(vhead0, yc2, 0.0), axis=-1, keepdims=True)
        v1 = jnp.sum(jnp.where(vhead0, 0.0, yc2), axis=-1, keepdims=True)
        y = yc * lax.rsqrt(jnp.where(vhead0, v0, v1) * inv + RMS_EPS) * gain_ref[:, lanes]
        g = g_ref[0, rows, lanes].astype(F32)
        o_ref[0, rows, lanes] = (y * (g * (1.0 / (1.0 + jnp.exp(-g))))).astype(BF16)

    def body(c, _):
        for u in range(unroll):
            for p in range(N_PAIR):
                pair_chunk(p, pl.multiple_of((c * unroll + u) * C, C))
        return 0

    lax.fori_loop(0, n_chunks // unroll, body, 0)


def _retention(main3, tables, gain_row):
    B, S, _ = main3.shape
    cos_t, sin_t, d_in, d_k, d_q, d_c, s_mask = tables
    C = RET_BLOCK
    n_chunks = S // C
    seq_spec = lambda cb: pl.BlockSpec((1, S, RET_W), lambda b: (b, 0, cb // N_PAIR))
    const = lambda shape: pl.BlockSpec(shape, lambda b: (0,) * len(shape))
    return pl.pallas_call(
        functools.partial(_ret_kernel, n_chunks=n_chunks, unroll=2 if n_chunks % 2 == 0 else 1),
        grid=(B,),
        in_specs=[
            seq_spec(CB_RQ), seq_spec(CB_RK), seq_spec(CB_RV), seq_spec(CB_RG),
            const((S, LANES)), const((S, LANES)),
            const((N_PAIR, 2, C, C)), const((N_PAIR, C, LANES)), const((N_PAIR, C, LANES)),
            const((N_PAIR, LANES, LANES)), const((LANES, LANES)), const((1, RET_W)),
        ],
        out_specs=pl.BlockSpec((1, S, RET_W), lambda b: (b, 0, 0)),
        out_shape=jax.ShapeDtypeStruct((B, S, RET_W), BF16),
        scratch_shapes=[pltpu.VMEM((N_PAIR, LANES, LANES), F32)],
        compiler_params=_cparams(("parallel",)),
        name="retention",
    )(main3, main3, main3, main3, cos_t, sin_t, d_in, d_k, d_q, d_c, s_mask, gain_row)


UNDERFLOW_LOG2 = -170.0


def _head_sq_norm_max(x, havg, head0):
    xf = x.astype(F32)
    sq = xf * xf
    hi = sq.astype(BF16)
    lo = (sq - hi.astype(F32)).astype(BF16)
    mean = jnp.dot(hi, havg, preferred_element_type=F32) + jnp.dot(lo, havg, preferred_element_type=F32)
    bound = mean * (HEAD_DIM * 1.01)
    return jnp.max(jnp.where(head0, bound, 0.0)), jnp.max(jnp.where(head0, 0.0, bound))


def _fox_kernel(cb_ref, q_ref, qa_ref, qs_ref, kh_ref, ka_ref, v_ref, gain_ref, havg_ref, o_ref, k_ref, ft_ref, *, nb, tq, tk, n_tiles):
    p = pl.program_id(1)
    qi = pl.program_id(2)
    lane_q = lax.broadcasted_iota(jnp.int32, (tq, LANES), 1)
    head0 = lane_q < HEAD_DIM

    @pl.when(qi == 0)
    def _():
        head0_s = lax.broadcasted_iota(jnp.int32, kh_ref.shape[1:], 1) < HEAD_DIM
        for bb in range(nb):
            k_ref[bb, :, 0:LANES] = kh_ref[bb]
            k_ref[bb, :, LANES:2 * LANES] = ka_ref[bb]
            b = pl.program_id(0) * nb + bb
            kn = _head_sq_norm_max(kh_ref[bb], havg_ref[...], head0_s)
            for i in range(n_tiles):
                qn = _head_sq_norm_max(qs_ref[bb, i * tq:(i + 1) * tq, :], havg_ref[...], head0)
                for j in range(2):
                    h = 2 * p + j
                    slack = 2.0 * jnp.sqrt(qn[j] * kn[j])
                    cq = cb_ref[((b * N_FOX + h) * 2 + 0) * n_tiles + i]
                    skipped = jnp.int32(0)
                    for t in range(i):
                        ck = cb_ref[((b * N_FOX + h) * 2 + 1) * n_tiles + t]
                        skipped = skipped + (slack + cq - ck < UNDERFLOW_LOG2).astype(jnp.int32)
                    ft_ref[(bb * n_tiles + i) * 2 + j] = skipped

    q_pos = qi * tq + lax.broadcasted_iota(jnp.int32, (tq, tk), 0)
    k_off = lax.broadcasted_iota(jnp.int32, (tq, tk), 1)
    chains = [(bb, j) for bb in range(nb) for j in range(2)]

    qops = {}
    for bb in range(nb):
        q = q_ref[bb]
        qa = qa_ref[bb]
        for j in range(2):
            qm = jnp.where(head0 if j == 0 else jnp.logical_not(head0), q, jnp.zeros_like(q))
            lo = (2 * p + j) * AUG_STRIDE
            own = jnp.logical_and(lane_q >= lo, lane_q < lo + AUG_STRIDE)
            qops[bb, j] = jnp.concatenate([qm, jnp.where(own, qa, jnp.zeros_like(qa))], axis=1)

    def tile(ch, t, n, carry, masked):
        bb, j = ch
        m, acc = carry
        c0 = pl.multiple_of(t * tk, tk)
        s = _dot_nt(qops[ch], k_ref[bb, pl.ds(c0, n * tk), :])
        if masked:
            s = jnp.where(k_off + c0 <= q_pos, s, NEG)
        m_new = jnp.maximum(m, jnp.max(s, axis=-1, keepdims=True))
        pr = jnp.exp2(s - m_new)
        own = lax.broadcasted_iota(jnp.int32, (n * tk, LANES), 1) < HEAD_DIM
        vj = jnp.where(own if j == 0 else jnp.logical_not(own), v_ref[bb, pl.ds(c0, n * tk), :], jnp.ones((n * tk, LANES), BF16))
        acc = jnp.exp2(m - m_new) * acc + jnp.dot(pr.astype(BF16), vj, preferred_element_type=F32)
        return m_new, acc

    carry = {}
    for ch in chains:
        bb, j = ch
        t0 = ft_ref[(bb * n_tiles + qi) * 2 + j]
        n_act = qi - t0
        c = (jnp.full((tq, 1), NEG, F32), jnp.zeros((tq, LANES), F32))
        c = lax.fori_loop(0, n_act // 2, lambda i, c, ch=ch, t0=t0: tile(ch, t0 + 2 * i, 2, c, False), c)
        carry[ch] = lax.cond(n_act % 2 == 1, lambda c, ch=ch: tile(ch, qi - 1, 1, c, False), lambda c: c, c)
    for ch in chains:
        carry[ch] = tile(ch, qi, 1, carry[ch], True)
    for bb in range(nb):
        a0, a1 = carry[bb, 0][1], carry[bb, 1][1]
        out = jnp.where(head0, a0 * (1.0 / pltpu.roll(a0, HEAD_DIM, 1)), a1 * (1.0 / pltpu.roll(a1, HEAD_DIM, 1)))
        sq = out * out
        sq_hi = sq.astype(BF16)
        sq_lo = (sq - sq_hi.astype(F32)).astype(BF16)
        ms = jnp.dot(sq_hi, havg_ref[...], preferred_element_type=F32) + jnp.dot(sq_lo, havg_ref[...], preferred_element_type=F32)
        o_ref[bb] = (out * lax.rsqrt(ms + RMS_EPS) * gain_ref[...]).astype(BF16)


def _fox(main3, q_aug, k_aug, cum_edges, gain_row, nb, tq, tk):
    B, S, _ = main3.shape
    assert tq == tk, "the kernel handles exactly one diagonal tile per query block"
    head_of = np.arange(LANES) // HEAD_DIM
    havg = (head_of[:, None] == head_of[None, :]).astype(np.float32) / HEAD_DIM
    grid_spec = pltpu.PrefetchScalarGridSpec(
        num_scalar_prefetch=1,
        grid=(B // nb, N_PAIR, S // tq),
        in_specs=[
            pl.BlockSpec((nb, tq, LANES), lambda b, p, i, cb: (b, i, CB_FQ + p)),
            pl.BlockSpec((nb, tq, LANES), lambda b, p, i, cb: (b, i, 0)),
            pl.BlockSpec((nb, S, LANES), lambda b, p, i, cb: (b, 0, CB_FQ + p)),
            pl.BlockSpec((nb, S, LANES), lambda b, p, i, cb: (b, 0, CB_FK + p)),
            pl.BlockSpec((nb, S, LANES), lambda b, p, i, cb: (b, 0, 0)),
            pl.BlockSpec((nb, S, LANES), lambda b, p, i, cb: (b, 0, CB_FV + p)),
            pl.BlockSpec((1, LANES), lambda b, p, i, cb: (0, p)),
            pl.BlockSpec((LANES, LANES), lambda b, p, i, cb: (0, 0)),
        ],
        out_specs=pl.BlockSpec((nb, tq, LANES), lambda b, p, i, cb: (b, i, p)),
        scratch_shapes=[pltpu.VMEM((nb, S, 2 * LANES), BF16), pltpu.SMEM((nb * 2 * (S // tk),), jnp.int32)],
    )
    return pl.pallas_call(
        functools.partial(_fox_kernel, nb=nb, tq=tq, tk=tk, n_tiles=S // tk),
        grid_spec=grid_spec,
        out_shape=jax.ShapeDtypeStruct((B, S, FOX_W), BF16),
        compiler_params=_cparams(("parallel", "parallel", "arbitrary")),
        name="fox_attention",
    )(cum_edges, main3, q_aug, main3, main3, k_aug, main3, gain_row, jnp.asarray(havg, BF16))


def _gelu_tanh(x):
    return 0.5 * x * (1.0 + jnp.tanh(np.sqrt(2.0 / np.pi).astype(np.float32) * (x + 0.044715 * (x * x * x))))


def _cmp_kernel(r_ref, pos_ref, wt_ref, wb_ref, w2_ref, o_ref):
    r = r_ref[0]
    n = r.shape[0]
    top = jnp.dot(r, wt_ref[...], preferred_element_type=F32)
    bot = jnp.dot(r, wb_ref[...], preferred_element_type=F32)
    cpos = jnp.dot(pos_ref[0:8, :], wt_ref[...], preferred_element_type=F32)
    cpos = cpos + jnp.dot(pos_ref[8:16, :], wb_ref[...], preferred_element_type=F32)
    hid = _gelu_tanh(top + pltpu.roll(bot, n - 1, 0) + cpos[0:1, :])
    o_ref[0] = jnp.dot(hid.astype(BF16), w2_ref[...], preferred_element_type=F32)


def _nsa_compress(r, pos2, wt, wb, w2):
    B, n, _ = r.shape
    width = CMP_STRIDE * LANES
    return pl.pallas_call(
        _cmp_kernel,
        grid=(B,),
        in_specs=[
            pl.BlockSpec((1, n, width), lambda b: (b, 0, 0)),
            pl.BlockSpec((16, width), lambda b: (0, 0)),
            pl.BlockSpec((width, 2 * CMP_HIDDEN), lambda b: (0, 0)),
            pl.BlockSpec((width, 2 * CMP_HIDDEN), lambda b: (0, 0)),
            pl.BlockSpec((2 * CMP_HIDDEN, LANES), lambda b: (0, 0)),
        ],
        out_specs=pl.BlockSpec((1, n, LANES), lambda b: (b, 0, 0)),
        out_shape=jax.ShapeDtypeStruct((B, n, LANES), F32),
        compiler_params=_cparams(("parallel",)),
        name="nsa_compress",
    )(r, pos2, wt, wb, w2)


def _softmax_rows(s, mask):
    sm = jnp.where(mask, s, NEG)
    m = jnp.maximum(jnp.max(sm, axis=-1, keepdims=True), 0.1 * NEG)
    e = jnp.exp2(sm - m)
    l = jnp.sum(e, axis=-1, keepdims=True)
    return e / jnp.where(l > 0.0, l, 1.0)


def _nsa_kernel(q_ref, cmp_ref, c2s_ref, sel_ref, noh_ref, win_ref, gl_ref, gain_ref, gsel_ref, havg_ref, o_ref, ksa_ref, *, nb, tq, tk, n_cmp,
                n_sel, top_n):
    H = N_NSA
    qi = pl.program_id(1)
    start = qi * tq

    @pl.when(qi == 0)
    def _():
        key_lanes = lax.broadcasted_iota(jnp.int32, noh_ref.shape, 1) < HEAD_DIM
        for bb in range(nb):
            ksa_ref[bb] = jnp.where(key_lanes, sel_ref[bb], noh_ref[...])

    t_row = start + lax.broadcasted_iota(jnp.int32, (tq, 1), 0)
    t_all = jnp.concatenate([t_row] * H, axis=0)
    zeros64 = jnp.zeros((tq, HEAD_DIM), BF16)
    wlen = WINDOW + tq
    base = pl.multiple_of(jnp.maximum(start - WINDOW, 0), tq)
    n_pad = cmp_ref.shape[1]
    assert CMP_STRIDE == 16 and tq & (tq - 1) == 0
    last_cmp = (t_all - (CMP_LEN - 1)) >> 4
    row_in_block = lax.broadcasted_iota(jnp.int32, (H * tq, wlen), 0) & (tq - 1)
    win_lag = (start - base) - (lax.broadcasted_iota(jnp.int32, (H * tq, wlen), 1) - row_in_block)

    def prologue(bb):
        q = q_ref[bb]
        q_heads = [q[:, h * HEAD_DIM:(h + 1) * HEAD_DIM] for h in range(H)]
        q_pad = jnp.concatenate([jnp.concatenate([qh, zeros64], axis=1) for qh in q_heads], axis=0)

        win = win_ref[bb, pl.ds(base, wlen), :]
        s_w = _dot_nt(q_pad, win)
        s_w = jnp.where(lax.bitcast_convert_type(win_lag, jnp.uint32) < WINDOW, s_w, NEG)
        e_w = jnp.exp2(s_w - jnp.max(s_w, axis=-1, keepdims=True))
        win1 = jnp.where(lax.broadcasted_iota(jnp.int32, (wlen, LANES), 1) < HEAD_DIM, jnp.ones((wlen, LANES), BF16), win)
        o_w = jnp.dot(e_w.astype(BF16), win1, preferred_element_type=F32)

        cmp = cmp_ref[bb]
        cmp_hi = cmp.astype(BF16)
        cmp_lo = (cmp - cmp_hi.astype(F32)).astype(BF16)
        s_c = _dot_nt(q_pad, cmp_hi) + _dot_nt(q_pad, cmp_lo)
        n_id = lax.broadcasted_iota(jnp.int32, (H * tq, n_pad), 1)
        p_c = _softmax_rows(s_c, n_id <= last_cmp)
        o_c = jnp.dot(p_c.astype(BF16), cmp_hi, preferred_element_type=F32)

        p_sum = p_c[0:tq]
        for h in range(1, H):
            p_sum = p_sum + p_c[h * tq:(h + 1) * tq]
        p_hi = p_sum.astype(BF16)
        p_lo = (p_sum - p_hi.astype(F32)).astype(BF16)
        c2s_t = c2s_ref[...]
        imp = _dot_nt(c2s_t, p_hi) + _dot_nt(c2s_t, p_lo)
        t_lane = start + lax.broadcasted_iota(jnp.int32, (HEAD_DIM, tq), 1)
        s_id = lax.broadcasted_iota(jnp.int32, (HEAD_DIM, tq), 0)
        cur = t_lane >> 6
        forced = jnp.logical_or(s_id == 0, jnp.logical_or(s_id == cur, s_id == cur - 1))
        score = jnp.where(forced, FORCED_SCORE, imp)
        score = jnp.where(s_id * SEL_LEN <= t_lane, score, -1.0)
        SUB = 8
        groups = [score[g * SUB:(g + 1) * SUB, :] for g in range(HEAD_DIM // SUB)]
        ranks = [jnp.zeros((SUB, tq), jnp.int32) for _ in groups]
        sub_id = lax.broadcasted_iota(jnp.int32, (SUB, tq), 0)
        for c in range(n_sel):
            row = score[c:c + 1, :]
            for g in range(len(groups)):
                if g * SUB > c:
                    before = row >= groups[g]
                elif g * SUB + SUB - 1 < c:
                    before = row > groups[g]
                else:
                    before = jnp.logical_or(row > groups[g], jnp.logical_and(row == groups[g], sub_id > c - g * SUB))
                ranks[g] = ranks[g] + before.astype(jnp.int32)
        not_sel = jnp.concatenate([jnp.where(r < top_n, 0.0, 1.0) for r in ranks], axis=0)
        ns = jnp.concatenate([jnp.zeros((HEAD_DIM, tq), F32), not_sel], axis=0).T.astype(BF16)
        q_aug = q_pad + jnp.concatenate([ns] * H, axis=0)
        return q_aug, o_c, o_w

    pro = [prologue(bb) for bb in range(nb)]

    def sel_tile(bb, c0, size, carry, diagonal):
        m, acc = carry
        s = _dot_nt(pro[bb][0], ksa_ref[bb, pl.ds(c0, size), :])
        if diagonal:
            r_id = lax.broadcasted_iota(jnp.int32, (H * tq, size), 0) & (tq - 1)
            c_id = lax.broadcasted_iota(jnp.int32, (H * tq, size), 1)
            s = jnp.where(c_id <= r_id, s, NEG)
        m_new = jnp.maximum(m, jnp.max(s, axis=-1, keepdims=True))
        pr = jnp.exp2(s - m_new)
        key_lanes = lax.broadcasted_iota(jnp.int32, (size, LANES), 1) < HEAD_DIM
        v1 = jnp.where(key_lanes, jnp.ones((size, LANES), BF16), sel_ref[bb, pl.ds(c0, size), :])
        acc = jnp.exp2(m - m_new) * acc + jnp.dot(pr.astype(BF16), v1, preferred_element_type=F32)
        return m_new, acc

    def all_rows(c0, size, carries, diagonal):
        return tuple(sel_tile(bb, c0, size, carries[bb], diagonal) for bb in range(nb))

    carries = tuple((jnp.full((H * tq, 1), NEG, F32), jnp.zeros((H * tq, LANES), F32)) for _ in range(nb))
    n_full = start // tk
    carries = lax.fori_loop(0, n_full, lambda t, c: all_rows(pl.multiple_of(t * tk, tk), tk, c, False), carries)
    pos = n_full * tk
    size = tk // 2
    while size >= tq:
        take = start - pos >= size
        carries = lax.cond(take, lambda c, pos=pos, size=size: all_rows(pl.multiple_of(pos, tq), size, c, False), lambda c: c, carries)
        pos = pos + jnp.where(take, size, 0)
        size //= 2
    carries = all_rows(pl.multiple_of(start, tq), tq, carries, True)

    first_half = lax.broadcasted_iota(jnp.int32, (tq, LANES), 1) < HEAD_DIM

    def place(acc, normalise):
        cols = []
        for h in range(0, H, 2):
            a0, a1 = acc[h * tq:(h + 1) * tq], acc[(h + 1) * tq:(h + 2) * tq]
            r0 = pltpu.roll(a0, HEAD_DIM, 1)
            if normalise:
                v0 = r0 * (1.0 / a0)
                v1 = a1 * (1.0 / pltpu.roll(a1, HEAD_DIM, 1))
            else:
                v0, v1 = r0, a1
            cols.append(jnp.where(first_half, v0, v1))
        return jnp.concatenate(cols, axis=1)

    for bb in range(nb):
        _, o_c, o_w = pro[bb]
        gates = 1.0 / (1.0 + jnp.exp(-gl_ref[bb]))
        g_hi = gates.astype(BF16)
        g_lo = (gates - g_hi.astype(F32)).astype(BF16)
        mix = None
        for c, y in enumerate((place(o_c, False), place(carries[bb][1], True), place(o_w, True))):
            g = jnp.dot(g_hi, gsel_ref[c], preferred_element_type=F32) + jnp.dot(g_lo, gsel_ref[c], preferred_element_type=F32)
            mix = g * y if mix is None else mix + g * y
        sq = mix * mix
        sq_hi = sq.astype(BF16)
        sq_lo = (sq - sq_hi.astype(F32)).astype(BF16)
        ms = jnp.dot(sq_hi, havg_ref[...], preferred_element_type=F32) + jnp.dot(sq_lo, havg_ref[...], preferred_element_type=F32)
        o_ref[bb] = (mix * lax.rsqrt(ms + RMS_EPS) * gain_ref[...]).astype(BF16)


def _nsa(main3, cmp, c2s, neg_onehot, small3, gain_row, nb, tq, tk, n_cmp):
    B, S, _ = main3.shape
    n_sel = S // SEL_LEN
    n_pad = cmp.shape[1]
    gsel = np.zeros((3, LANES, NSA_W), np.float32)
    for h in range(N_NSA):
        for c in range(3):
            gsel[c, SMALL_GATE + 3 * h + c, h * HEAD_DIM:(h + 1) * HEAD_DIM] = 1.0
    head_of = np.arange(NSA_W) // HEAD_DIM
    havg = (head_of[:, None] == head_of[None, :]).astype(np.float32) / HEAD_DIM
    return pl.pallas_call(
        functools.partial(_nsa_kernel, nb=nb, tq=tq, tk=tk, n_cmp=n_cmp, n_sel=n_sel, top_n=min(TOP_N, n_sel)),
        grid=(B // nb, S // tq),
        in_specs=[
            pl.BlockSpec((nb, tq, NSA_W), lambda b, i: (b, i, CB_NQ // 2)),
            pl.BlockSpec((nb, n_pad, LANES), lambda b, i: (b, 0, 0)),
            pl.BlockSpec((HEAD_DIM, n_pad), lambda b, i: (0, 0)),
            pl.BlockSpec((nb, S, LANES), lambda b, i: (b, 0, CB_SEL)),
            pl.BlockSpec((S, LANES), lambda b, i: (0, 0)),
            pl.BlockSpec((nb, S, LANES), lambda b, i: (b, 0, CB_WIN)),
            pl.BlockSpec((nb, tq, LANES), lambda b, i: (b, i, 0)),
            pl.BlockSpec((1, NSA_W), lambda b, i: (0, 0)),
            pl.BlockSpec((3, LANES, NSA_W), lambda b, i: (0, 0, 0)),
            pl.BlockSpec((NSA_W, NSA_W), lambda b, i: (0, 0)),
        ],
        out_specs=pl.BlockSpec((nb, tq, NSA_W), lambda b, i: (b, i, 0)),
        out_shape=jax.ShapeDtypeStruct((B, S, NSA_W), BF16),
        scratch_shapes=[pltpu.VMEM((nb, S, LANES), BF16)],
        compiler_params=_cparams(("parallel", "arbitrary")),
        name="nsa_attention",
    )(main3, cmp, c2s, main3, neg_onehot, main3, small3, gain_row, jnp.asarray(gsel, BF16), jnp.asarray(havg, BF16))


def _post_kernel(x_ref, yr_ref, yf_ref, yn_ref, wo_ref, g_ref, w1_ref, w2_ref, gf_ref, o_ref, *, ff_chunk, final):
    x = x_ref[...]
    x = x + jnp.dot(yr_ref[...], wo_ref[0:RET_W, :], preferred_element_type=F32)
    x = x + jnp.dot(yf_ref[...], wo_ref[RET_W:RET_W + FOX_W, :], preferred_element_type=F32)
    x = x + jnp.dot(yn_ref[...], wo_ref[RET_W + FOX_W:, :], preferred_element_type=F32)
    h = _rms(x, g_ref[...]).astype(BF16)
    o_ref[...] = x
    for c0 in range(0, D_FF, ff_chunk):
        hid = jnp.maximum(jnp.dot(h, w1_ref[:, c0:c0 + ff_chunk], preferred_element_type=F32), 0.0)
        o_ref[...] += jnp.dot((hid * hid).astype(BF16), w2_ref[c0:c0 + ff_chunk, :], preferred_element_type=F32)
    if final:
        o_ref[...] = _rms(o_ref[...], gf_ref[...])


def _post(x2d, yr, yf, yn, wo, g, w1, w2, gf, tm, final):
    T = x2d.shape[0]
    row = lambda w: pl.BlockSpec((tm, w), lambda i: (i, 0))
    return pl.pallas_call(
        functools.partial(_post_kernel, ff_chunk=512, final=final),
        grid=(T // tm,),
        in_specs=[
            row(D_MODEL), row(RET_W), row(FOX_W), row(NSA_W),
            _resident((D_MODEL, D_MODEL), lambda i: (0, 0)),
            _resident((1, D_MODEL), lambda i: (0, 0)),
            _resident((D_MODEL, D_FF), lambda i: (0, 0)),
            _resident((D_FF, D_MODEL), lambda i: (0, 0)),
            _resident((1, D_MODEL), lambda i: (0, 0)),
        ],
        out_specs=row(D_MODEL),
        out_shape=jax.ShapeDtypeStruct((T, D_MODEL), F32),
        compiler_params=_cparams(("parallel",)),
        name="outproj_mlp",
    )(x2d, yr, yf, yn, wo, g, w1, w2, gf)


def _tile_sizes(batch):
    return dict(tm=512, fox_nb=4 if batch % 4 == 0 else 1, fox_tq=512, fox_tk=512,
                nsa_tq=256, nsa_tk=1024, nsa_nb=4 if batch % 4 == 0 else 1)


def kernel(x, norm_attn, w_in, fox_forget_bias, ret_norm_gain, fox_norm_gain, nsa_norm_gain, nsa_cmp_pos_k, nsa_cmp_pos_v, nsa_cmp_w1_k, nsa_cmp_w2_k, nsa_cmp_w1_v, nsa_cmp_w2_v, w_out, norm_mlp, w_mlp_in, w_mlp_out, norm_final):
    B, S, D = x.shape
    depth = w_in.shape[0]
    assert D == D_MODEL and S % 512 == 0 and S >= WINDOW + 256
    T = B * S
    ts = _tile_sizes(B)

    cols, scale, small_cols, small_mask = _in_proj_columns()
    wm = (w_in[:, :, cols] * scale).astype(BF16)
    ws = (w_in[:, :, small_cols] * small_mask).astype(BF16)
    wo = w_out.astype(BF16)
    w1 = w_mlp_in.astype(BF16)
    w2 = w_mlp_out.astype(BF16)

    def expand_w1(wk, wv):
        L = wk.shape[0]
        wk = wk.reshape(L, CMP_LEN, HEAD_DIM, CMP_HIDDEN)
        wv = wv.reshape(L, CMP_LEN, HEAD_DIM, CMP_HIDDEN)
        z = jnp.zeros_like(wk)
        full = jnp.concatenate([jnp.concatenate([wk, z], axis=-1), jnp.concatenate([z, wv], axis=-1)], axis=2)
        full = full.reshape(L, CMP_LEN * LANES, 2 * CMP_HIDDEN).astype(BF16)
        return full[:, :CMP_STRIDE * LANES], full[:, CMP_STRIDE * LANES:]

    wt_all, wb_all = expand_w1(nsa_cmp_w1_k, nsa_cmp_w1_v)
    zk = jnp.zeros_like(nsa_cmp_w2_k)
    w2c = jnp.concatenate([jnp.concatenate([nsa_cmp_w2_k, zk], axis=-1),
                           jnp.concatenate([zk, nsa_cmp_w2_v], axis=-1)], axis=1).astype(BF16)
    pos = jnp.concatenate([nsa_cmp_pos_k, nsa_cmp_pos_v], axis=-1)
    pos_tb = pos.reshape(depth, 2, 1, CMP_STRIDE * LANES)
    pos_tb = jnp.broadcast_to(pos_tb, (depth, 2, 8, CMP_STRIDE * LANES)).reshape(depth, 16, CMP_STRIDE * LANES).astype(BF16)

    n_cmp = (S - CMP_LEN) // CMP_STRIDE + 1
    n_sel = S // SEL_LEN
    n_rows = S // CMP_STRIDE
    cs = np.arange(n_rows) * CMP_STRIDE
    ss = np.arange(n_sel) * SEL_LEN
    overlap = np.clip(np.minimum(cs[:, None] + CMP_LEN, ss[None, :] + SEL_LEN) - np.maximum(cs[:, None], ss[None, :]), 0, None)
    overlap[n_cmp:] = 0
    c2s_np = np.zeros((HEAD_DIM, n_rows), np.float32)
    c2s_np[:n_sel] = (overlap / CMP_LEN).T
    c2s = jnp.asarray(c2s_np, dtype=BF16)
    assert n_sel <= HEAD_DIM, "the selected-branch key augmentation has 64 lanes, one per selection block"
    neg_onehot = jnp.asarray(np.where((np.arange(S)[:, None] // SEL_LEN) == np.arange(LANES)[None, :] - HEAD_DIM, NEG, 0.0), dtype=BF16)

    tables = _retention_tables(S)
    fbias = jnp.zeros((depth, 1, LANES), F32).at[:, 0, SMALL_FF:SMALL_FF + N_FOX].set(fox_forget_bias)

    x2d = x.reshape(T, D)
    for l in range(depth):
        main, small = _inproj(x2d, norm_attn[l][None, :], wm[l], ws[l], ts["tm"])
        main3 = main.reshape(B, S, N_MAIN)
        small3 = small.reshape(B, S, LANES)
        fox_qa, fox_ka, edges = _forget_cumsum(small3, fbias[l])
        y_ret = _retention(main3, tables, ret_norm_gain[l][None, :])
        per_tile = ts["fox_tk"] // RET_CHUNK
        cq = edges[:, 0::per_tile, 0, SMALL_FF:SMALL_FF + N_FOX]
        ck = edges[:, per_tile - 1::per_tile, 1, SMALL_FF:SMALL_FF + N_FOX]
        cum_edges = jnp.stack([cq.transpose(0, 2, 1), ck.transpose(0, 2, 1)], axis=2).reshape(-1)
        y_fox = _fox(main3, fox_qa, fox_ka, cum_edges, fox_norm_gain[l][None, :], ts["fox_nb"], ts["fox_tq"], ts["fox_tk"])
        r = main3[:, :, CB_CMP * LANES:(CB_CMP + 1) * LANES].reshape(B, n_rows, CMP_STRIDE * LANES)
        cmp = _nsa_compress(r, pos_tb[l], wt_all[l], wb_all[l], w2c[l])
        y_nsa = _nsa(main3, cmp, c2s, neg_onehot, small3, nsa_norm_gain[l][None, :], ts["nsa_nb"], ts["nsa_tq"], ts["nsa_tk"], n_cmp)
        x2d = _post(x2d, y_ret.reshape(T, RET_W), y_fox.reshape(T, FOX_W), y_nsa.reshape(T, NSA_W), wo[l],
                    norm_mlp[l][None, :], w1[l], w2[l], norm_final[None, :], ts["tm"], final=(l == depth - 1))
    return x2d.reshape(B, S, D)
```

```python
import functools

import numpy as np
import jax
import jax.numpy as jnp
from jax import lax
from jax.experimental import pallas as pl
from jax.experimental.pallas import tpu as pltpu

F32 = jnp.float32
BF16 = jnp.bfloat16

D_MODEL = 1024
HEAD_DIM = 64
N_RET = 6
N_FOX = 6
N_NSA = 4
RET_W = N_RET * HEAD_DIM
FOX_W = N_FOX * HEAD_DIM
NSA_W = N_NSA * HEAD_DIM
D_FF = 4 * D_MODEL
RMS_EPS = 1e-6
RET_CHUNK = 128
RET_BLOCK = 256
ROPE_BASE = 10000.0
CMP_LEN = 32
CMP_STRIDE = 16
CMP_HIDDEN = 4 * HEAD_DIM
SEL_LEN = 64
TOP_N = 16
WINDOW = 512
FORCED_SCORE = 1e4
NEG = -1e30
LOG2E = float(np.log2(np.e))

LANES = 128
N_PAIR = N_RET // 2

CB_RQ, CB_RK, CB_RV, CB_RG = 0, 3, 6, 9
CB_NQ = 12
CB_FQ, CB_FK, CB_FV = 14, 17, 20
CB_CMP, CB_SEL, CB_WIN = 23, 24, 25
N_MAIN = 26 * LANES
SMALL_FF = 0
SMALL_GATE = 8

VMEM_LIMIT = 56 * 1024 * 1024


def _cparams(sem):
    return pltpu.CompilerParams(dimension_semantics=sem, vmem_limit_bytes=VMEM_LIMIT)


def _resident(shape, index_map):
    return pl.BlockSpec(shape, index_map, pipeline_mode=pl.Buffered(1))


def _in_proj_weights(w_in):
    sizes = (RET_W, RET_W, RET_W, RET_W, FOX_W, FOX_W, FOX_W, N_FOX, NSA_W) + (HEAD_DIM,) * 6 + (3 * N_NSA,)
    off = np.concatenate([[0], np.cumsum(sizes)])
    (o_rq, o_rk, o_rv, o_rg, o_fq, o_fk, o_fv, o_ff, o_nq, o_kc, o_vc, o_ks, o_vs, o_kw, o_vw, o_gt) = [int(o) for o in off[:-1]]
    L, D, _ = w_in.shape
    half = HEAD_DIM // 2
    cut = lambda o, n: w_in[:, :, o:o + n]

    def interleave(w):
        return w.reshape(L, D, N_PAIR, 2, 2, half).transpose(0, 1, 2, 4, 3, 5).reshape(L, D, RET_W)

    qk_scale = HEAD_DIM ** -0.5
    sm_scale = qk_scale * LOG2E
    main = jnp.concatenate([
        interleave(cut(o_rq, RET_W)), interleave(cut(o_rk, RET_W)) * qk_scale, cut(o_rv, RET_W), cut(o_rg, RET_W),
        cut(o_nq, NSA_W) * sm_scale, cut(o_fq, FOX_W) * sm_scale, cut(o_fk, FOX_W), cut(o_fv, FOX_W),
        cut(o_kc, 6 * HEAD_DIM),
    ], axis=-1)
    assert main.shape[-1] == N_MAIN and o_vw == o_kc + 5 * HEAD_DIM
    zeros = lambda n: jnp.zeros((L, D, n), w_in.dtype)
    small = jnp.concatenate([
        zeros(SMALL_FF), cut(o_ff, N_FOX), zeros(SMALL_GATE - SMALL_FF - N_FOX),
        cut(o_gt, 3 * N_NSA), zeros(LANES - SMALL_GATE - 3 * N_NSA)], axis=-1)
    return main.astype(BF16), small.astype(BF16)


def _retention_tables(seq):
    half = HEAD_DIM // 2
    inv = 1.0 / (ROPE_BASE ** (jnp.arange(half, dtype=F32) / half))
    ang = jnp.arange(seq, dtype=F32)[:, None] * inv[None, :]
    cos, sin = jnp.cos(ang), jnp.sin(ang)
    cos_t = jnp.concatenate([cos, cos, cos, cos], axis=-1)
    sin_t = jnp.concatenate([-sin, -sin, sin, sin], axis=-1)
    log_gamma = np.log(1.0 - 2.0 ** (-5.0 - np.arange(N_RET, dtype=np.float32))).astype(np.float32)
    C = RET_BLOCK
    idx = np.arange(C, dtype=np.float32)
    diff = idx[:, None] - idx[None, :]
    lane = np.arange(LANES)
    head_k = (lane % HEAD_DIM) // half
    head_v = lane // HEAD_DIM
    d_in = np.zeros((N_PAIR, 2, C, C), np.float32)
    d_k = np.zeros((N_PAIR, C, LANES), np.float32)
    d_q = np.zeros((N_PAIR, C, LANES), np.float32)
    d_c = np.zeros((N_PAIR, LANES, LANES), np.float32)
    for p in range(N_PAIR):
        lg = log_gamma[2 * p:2 * p + 2]
        for j in range(2):
            d_in[p, j] = np.where(diff >= 0, np.exp(lg[j] * np.maximum(diff, 0.0)), 0.0)
        d_k[p] = np.exp(lg[head_k][None, :] * (C - 1.0 - idx)[:, None])
        d_q[p] = np.exp(lg[head_v][None, :] * (idx + 1.0)[:, None])
        d_c[p] = np.broadcast_to(np.exp(lg[head_v] * C)[None, :], (LANES, LANES))
    s_mask = (head_k[:, None] == head_v[None, :]).astype(np.float32)
    return cos_t, sin_t, jnp.asarray(d_in), jnp.asarray(d_k), jnp.asarray(d_q), jnp.asarray(d_c), jnp.asarray(s_mask)


def _rms(x, g):
    return x * lax.rsqrt(jnp.mean(x * x, axis=-1, keepdims=True) + RMS_EPS) * g


def _inproj_kernel(x_ref, g_ref, wm_ref, ws_ref, om_ref, os_ref, *, n_chunk):
    h = _rms(x_ref[...], g_ref[...]).astype(BF16)
    for n0 in range(0, N_MAIN, n_chunk):
        n1 = min(n0 + n_chunk, N_MAIN)
        om_ref[:, n0:n1] = jnp.dot(h, wm_ref[:, n0:n1], preferred_element_type=F32).astype(BF16)
    os_ref[...] = jnp.dot(h, ws_ref[...], preferred_element_type=F32)


def _inproj(x2d, g, wm, ws, tm):
    T = x2d.shape[0]
    return pl.pallas_call(
        functools.partial(_inproj_kernel, n_chunk=512),
        grid=(T // tm,),
        in_specs=[
            pl.BlockSpec((tm, D_MODEL), lambda i: (i, 0)),
            _resident((1, D_MODEL), lambda i: (0, 0)),
            _resident((D_MODEL, N_MAIN), lambda i: (0, 0)),
            _resident((D_MODEL, LANES), lambda i: (0, 0)),
        ],
        out_specs=[
            pl.BlockSpec((tm, N_MAIN), lambda i: (i, 0)),
            pl.BlockSpec((tm, LANES), lambda i: (i, 0)),
        ],
        out_shape=[jax.ShapeDtypeStruct((T, N_MAIN), BF16), jax.ShapeDtypeStruct((T, LANES), F32)],
        compiler_params=_cparams(("parallel",)),
        name="inproj",
    )(x2d, g, wm, ws)


N_SPLIT = 3
AUG_STRIDE = 2 * N_SPLIT


def _split_bf16(x):
    terms, rest = [], x
    for _ in range(N_SPLIT):
        t = rest.astype(BF16)
        rest = rest - t.astype(F32)
        terms.append(t)
    return terms


def _bias_placement():
    mq = np.zeros((N_SPLIT, LANES, LANES), np.float32)
    mk = np.zeros((N_SPLIT, LANES, LANES), np.float32)
    one_q = np.zeros((1, LANES), np.float32)
    one_k = np.zeros((1, LANES), np.float32)
    for h in range(N_FOX):
        base = h * AUG_STRIDE
        for i in range(N_SPLIT):
            mq[i, SMALL_FF + h, base + i] = 1.0
            mk[i, SMALL_FF + h, base + N_SPLIT + i] = -1.0
            one_q[0, base + N_SPLIT + i] = 1.0
            one_k[0, base + i] = 1.0
    return jnp.asarray(mq, BF16), jnp.asarray(mk, BF16), jnp.asarray(one_q), jnp.asarray(one_k)


def _cum_kernel(z_ref, b_ref, tri_ref, mq_ref, mk_ref, oq_ref, ok_ref, qa_ref, ka_ref, edge_ref, *, n_chunks):
    C = RET_CHUNK

    def chunk(c, carry):
        r0 = pl.multiple_of(c * C, C)
        z = z_ref[0, pl.ds(r0, C), :] + b_ref[...]
        lf = jnp.minimum(z, 0.0) - jnp.log1p(jnp.exp(-jnp.abs(z)))
        terms = _split_bf16(lf)
        local = jnp.dot(tri_ref[...], terms[0], preferred_element_type=F32)
        for term in terms[1:]:
            local = local + jnp.dot(tri_ref[...], term, preferred_element_type=F32)
        cs = local + carry
        cs2 = cs * LOG2E
        edge_ref[0, pl.ds(c, 1)] = jnp.concatenate([cs2[0:1], cs2[C - 1:C], jnp.zeros((6, LANES), F32)], axis=0)[None]
        qa, ka = oq_ref[...], ok_ref[...]
        for i, term in enumerate(_split_bf16(cs2)):
            qa = qa + jnp.dot(term, mq_ref[i], preferred_element_type=F32)
            ka = ka + jnp.dot(term, mk_ref[i], preferred_element_type=F32)
        qa_ref[0, pl.ds(r0, C), :] = qa.astype(BF16)
        ka_ref[0, pl.ds(r0, C), :] = ka.astype(BF16)
        return cs[C - 1:C, :]

    unroll = max(u for u in (1, 2, 4) if n_chunks % u == 0)

    def body(i, carry):
        for u in range(unroll):
            carry = chunk(i * unroll + u, carry)
        return carry

    lax.fori_loop(0, n_chunks // unroll, body, jnp.zeros((1, LANES), F32))


def _forget_cumsum(small3, bias_row):
    B, S, _ = small3.shape
    tri = jnp.asarray(np.tril(np.ones((RET_CHUNK, RET_CHUNK), np.float32)), BF16)
    mq, mk, one_q, one_k = _bias_placement()
    const = lambda shape: pl.BlockSpec(shape, lambda b: (0,) * len(shape))
    return pl.pallas_call(
        functools.partial(_cum_kernel, n_chunks=S // RET_CHUNK),
        grid=(B,),
        in_specs=[
            pl.BlockSpec((1, S, LANES), lambda b: (b, 0, 0)),
            const((1, LANES)), const((RET_CHUNK, RET_CHUNK)),
            const((N_SPLIT, LANES, LANES)), const((N_SPLIT, LANES, LANES)), const((1, LANES)), const((1, LANES)),
        ],
        out_specs=[
            pl.BlockSpec((1, S, LANES), lambda b: (b, 0, 0)),
            pl.BlockSpec((1, S, LANES), lambda b: (b, 0, 0)),
            pl.BlockSpec((1, S // RET_CHUNK, 8, LANES), lambda b: (b, 0, 0, 0)),
        ],
        out_shape=[jax.ShapeDtypeStruct((B, S, LANES), BF16), jax.ShapeDtypeStruct((B, S, LANES), BF16),
                   jax.ShapeDtypeStruct((B, S // RET_CHUNK, 8, LANES), F32)],
        compiler_params=_cparams(("parallel",)),
        name="forget_cumsum",
    )(small3, bias_row, tri, mq, mk, one_q, one_k)


def _dot_nt(a, b):
    return lax.dot_general(a, b, (((1,), (1,)), ((), ())), preferred_element_type=F32)


def _ret_kernel(q_ref, k_ref, v_ref, g_ref, cos_ref, sin_ref, din_ref, dk_ref, dq_ref, dc_ref, sm_ref, gain_ref,
                o_ref, state_ref, *, n_chunks, unroll):
    C = RET_BLOCK
    half = HEAD_DIM // 2
    lane = lax.broadcasted_iota(jnp.int32, (C, LANES), 1)
    khead = (lane % HEAD_DIM) // half
    vhead0 = lane < HEAD_DIM
    state_ref[...] = jnp.zeros_like(state_ref)

    def pair_chunk(p, r0):
        rows = pl.ds(r0, C)
        lanes = slice(p * LANES, (p + 1) * LANES)
        cs, sn = cos_ref[rows, :], sin_ref[rows, :]
        q = q_ref[0, rows, lanes].astype(F32)
        k = k_ref[0, rows, lanes].astype(F32)
        q = q * cs + pltpu.roll(q, HEAD_DIM, 1) * sn
        k = k * cs + pltpu.roll(k, HEAD_DIM, 1) * sn
        v = v_ref[0, rows, lanes]
        qb, kb = q.astype(BF16), k.astype(BF16)
        state = state_ref[p]
        cross = jnp.dot(qb, state.astype(BF16), preferred_element_type=F32) * dq_ref[p]
        outs = []
        for j in range(2):
            qm = jnp.where(khead == j, qb, jnp.zeros_like(qb))
            inner = _dot_nt(qm, kb) * din_ref[p, j]
            outs.append(jnp.dot(inner.astype(BF16), v, preferred_element_type=F32))
        out = jnp.where(vhead0, outs[0], outs[1]) + cross
        kd = (k * dk_ref[p]).T.astype(BF16)
        state_ref[p] = state * dc_ref[p] + jnp.dot(kd, v, preferred_element_type=F32) * sm_ref[...]
        inv = 1.0 / HEAD_DIM
        s0 = jnp.sum(jnp.where(vhead0, out, 0.0), axis=-1, keepdims=True)
        s1 = jnp.sum(jnp.where(vhead0, 0.0, out), axis=-1, keepdims=True)
        yc = out - jnp.where(vhead0, s0, s1) * inv
        yc2 = yc * yc
        v0 = jnp.sum(jnp.where(vhead0, yc2, 0.0), axis=-1, keepdims=True)
        v1 = jnp.sum(jnp.where(vhead0, 0.0, yc2), axis=-1, keepdims=True)
        y = yc * lax.rsqrt(jnp.where(vhead0, v0, v1) * inv + RMS_EPS) * gain_ref[:, lanes]
        g = g_ref[0, rows, lanes].astype(F32)
        o_ref[0, rows, lanes] = (y * (g * (1.0 / (1.0 + jnp.exp(-g))))).astype(BF16)

    def body(c, _):
        for u in range(unroll):
            for p in range(N_PAIR):
                pair_chunk(p, pl.multiple_of((c * unroll + u) * C, C))
        return 0

    lax.fori_loop(0, n_chunks // unroll, body, 0)


def _retention(main3, tables, gain_row):
    B, S, _ = main3.shape
    cos_t, sin_t, d_in, d_k, d_q, d_c, s_mask = tables
    C = RET_BLOCK
    n_chunks = S // C
    seq_spec = lambda cb: pl.BlockSpec((1, S, RET_W), lambda b: (b, 0, cb // N_PAIR))
    const = lambda shape: pl.BlockSpec(shape, lambda b: (0,) * len(shape))
    return pl.pallas_call(
        functools.partial(_ret_kernel, n_chunks=n_chunks, unroll=2 if n_chunks % 2 == 0 else 1),
        grid=(B,),
        in_specs=[
            seq_spec(CB_RQ), seq_spec(CB_RK), seq_spec(CB_RV), seq_spec(CB_RG),
            const((S, LANES)), const((S, LANES)),
            const((N_PAIR, 2, C, C)), const((N_PAIR, C, LANES)), const((N_PAIR, C, LANES)),
            const((N_PAIR, LANES, LANES)), const((LANES, LANES)), const((1, RET_W)),
        ],
        out_specs=pl.BlockSpec((1, S, RET_W), lambda b: (b, 0, 0)),
        out_shape=jax.ShapeDtypeStruct((B, S, RET_W), BF16),
        scratch_shapes=[pltpu.VMEM((N_PAIR, LANES, LANES), F32)],
        compiler_params=_cparams(("parallel",)),
        name="retention",
    )(main3, main3, main3, main3, cos_t, sin_t, d_in, d_k, d_q, d_c, s_mask, gain_row)


UNDERFLOW_LOG2 = -170.0


def _head_sq_norm_max(x, havg, head0):
    xf = x.astype(F32)
    sq = xf * xf
    hi = sq.astype(BF16)
    lo = (sq - hi.astype(F32)).astype(BF16)
    mean = jnp.dot(hi, havg, preferred_element_type=F32) + jnp.dot(lo, havg, preferred_element_type=F32)
    bound = mean * (HEAD_DIM * 1.01)
    return jnp.max(jnp.where(head0, bound, 0.0)), jnp.max(jnp.where(head0, 0.0, bound))


def _fox_kernel(cb_ref, q_ref, qa_ref, qs_ref, kh_ref, ka_ref, v_ref, gain_ref, havg_ref, o_ref, k_ref, ft_ref, *, nb, tq, tk, n_tiles):
    p = pl.program_id(1)
    qi = pl.program_id(2)
    lane_q = lax.broadcasted_iota(jnp.int32, (tq, LANES), 1)
    head0 = lane_q < HEAD_DIM

    @pl.when(qi == 0)
    def _():
        head0_s = lax.broadcasted_iota(jnp.int32, kh_ref.shape[1:], 1) < HEAD_DIM
        for bb in range(nb):
            k_ref[bb, :, 0:LANES] = kh_ref[bb]
            k_ref[bb, :, LANES:2 * LANES] = ka_ref[bb]
            b = pl.program_id(0) * nb + bb
            kn = _head_sq_norm_max(kh_ref[bb], havg_ref[...], head0_s)
            for i in range(n_tiles):
                qn = _head_sq_norm_max(qs_ref[bb, i * tq:(i + 1) * tq, :], havg_ref[...], head0)
                for j in range(2):
                    h = 2 * p + j
                    slack = 2.0 * jnp.sqrt(qn[j] * kn[j])
                    cq = cb_ref[((b * N_FOX + h) * 2 + 0) * n_tiles + i]
                    skipped = jnp.int32(0)
                    for t in range(i):
                        ck = cb_ref[((b * N_FOX + h) * 2 + 1) * n_tiles + t]
                        skipped = skipped + (slack + cq - ck < UNDERFLOW_LOG2).astype(jnp.int32)
                    ft_ref[(bb * n_tiles + i) * 2 + j] = skipped

    q_pos = qi * tq + lax.broadcasted_iota(jnp.int32, (tq, tk), 0)
    k_off = lax.broadcasted_iota(jnp.int32, (tq, tk), 1)
    chains = [(bb, j) for bb in range(nb) for j in range(2)]

    qops = {}
    for bb in range(nb):
        q = q_ref[bb]
        qa = qa_ref[bb]
        for j in range(2):
            qm = jnp.where(head0 if j == 0 else jnp.logical_not(head0), q, jnp.zeros_like(q))
            lo = (2 * p + j) * AUG_STRIDE
            own = jnp.logical_and(lane_q >= lo, lane_q < lo + AUG_STRIDE)
            qops[bb, j] = jnp.concatenate([qm, jnp.where(own, qa, jnp.zeros_like(qa))], axis=1)

    def tile(ch, t, n, carry, masked):
        bb, j = ch
        m, acc = carry
        c0 = pl.multiple_of(t * tk, tk)
        s = _dot_nt(qops[ch], k_ref[bb, pl.ds(c0, n * tk), :])
        if masked:
            s = jnp.where(k_off + c0 <= q_pos, s, NEG)
        m_new = jnp.maximum(m, jnp.max(s, axis=-1, keepdims=True))
        pr = jnp.exp2(s - m_new)
        own = lax.broadcasted_iota(jnp.int32, (n * tk, LANES), 1) < HEAD_DIM
        vj = jnp.where(own if j == 0 else jnp.logical_not(own), v_ref[bb, pl.ds(c0, n * tk), :], jnp.ones((n * tk, LANES), BF16))
        acc = jnp.exp2(m - m_new) * acc + jnp.dot(pr.astype(BF16), vj, preferred_element_type=F32)
        return m_new, acc

    carry = {}
    for ch in chains:
        bb, j = ch
        t0 = ft_ref[(bb * n_tiles + qi) * 2 + j]
        n_act = qi - t0
        c = (jnp.full((tq, 1), NEG, F32), jnp.zeros((tq, LANES), F32))
        c = lax.fori_loop(0, n_act // 2, lambda i, c, ch=ch, t0=t0: tile(ch, t0 + 2 * i, 2, c, False), c)
        carry[ch] = lax.cond(n_act % 2 == 1, lambda c, ch=ch: tile(ch, qi - 1, 1, c, False), lambda c: c, c)
    for ch in chains:
        carry[ch] = tile(ch, qi, 1, carry[ch], True)
    for bb in range(nb):
        a0, a1 = carry[bb, 0][1], carry[bb, 1][1]
        out = jnp.where(head0, a0 * (1.0 / pltpu.roll(a0, HEAD_DIM, 1)), a1 * (1.0 / pltpu.roll(a1, HEAD_DIM, 1)))
        sq = out * out
        sq_hi = sq.astype(BF16)
        sq_lo = (sq - sq_hi.astype(F32)).astype(BF16)
        ms = jnp.dot(sq_hi, havg_ref[...], preferred_element_type=F32) + jnp.dot(sq_lo, havg_ref[...], preferred_element_type=F32)
        o_ref[bb] = (out * lax.rsqrt(ms + RMS_EPS) * gain_ref[...]).astype(BF16)


def _fox(main3, q_aug, k_aug, cum_edges, gain_row, nb, tq, tk):
    B, S, _ = main3.shape
    assert tq == tk, "the kernel handles exactly one diagonal tile per query block"
    head_of = np.arange(LANES) // HEAD_DIM
    havg = (head_of[:, None] == head_of[None, :]).astype(np.float32) / HEAD_DIM
    grid_spec = pltpu.PrefetchScalarGridSpec(
        num_scalar_prefetch=1,
        grid=(B // nb, N_PAIR, S // tq),
        in_specs=[
            pl.BlockSpec((nb, tq, LANES), lambda b, p, i, cb: (b, i, CB_FQ + p)),
            pl.BlockSpec((nb, tq, LANES), lambda b, p, i, cb: (b, i, 0)),
            pl.BlockSpec((nb, S, LANES), lambda b, p, i, cb: (b, 0, CB_FQ + p)),
            pl.BlockSpec((nb, S, LANES), lambda b, p, i, cb: (b, 0, CB_FK + p)),
            pl.BlockSpec((nb, S, LANES), lambda b, p, i, cb: (b, 0, 0)),
            pl.BlockSpec((nb, S, LANES), lambda b, p, i, cb: (b, 0, CB_FV + p)),
            pl.BlockSpec((1, LANES), lambda b, p, i, cb: (0, p)),
            pl.BlockSpec((LANES, LANES), lambda b, p, i, cb: (0, 0)),
        ],
        out_specs=pl.BlockSpec((nb, tq, LANES), lambda b, p, i, cb: (b, i, p)),
        scratch_shapes=[pltpu.VMEM((nb, S, 2 * LANES), BF16), pltpu.SMEM((nb * 2 * (S // tk),), jnp.int32)],
    )
    return pl.pallas_call(
        functools.partial(_fox_kernel, nb=nb, tq=tq, tk=tk, n_tiles=S // tk),
        grid_spec=grid_spec,
        out_shape=jax.ShapeDtypeStruct((B, S, FOX_W), BF16),
        compiler_params=_cparams(("parallel", "parallel", "arbitrary")),
        name="fox_attention",
    )(cum_edges, main3, q_aug, main3, main3, k_aug, main3, gain_row, jnp.asarray(havg, BF16))


def _gelu_tanh(x):
    return 0.5 * x * (1.0 + jnp.tanh(np.sqrt(2.0 / np.pi).astype(np.float32) * (x + 0.044715 * (x * x * x))))


def _cmp_kernel(r_ref, pos_ref, wt_ref, wb_ref, w2_ref, o_ref):
    r = r_ref[0]
    n = r.shape[0]
    top = jnp.dot(r, wt_ref[...], preferred_element_type=F32)
    bot = jnp.dot(r, wb_ref[...], preferred_element_type=F32)
    cpos = jnp.dot(pos_ref[0:8, :], wt_ref[...], preferred_element_type=F32)
    cpos = cpos + jnp.dot(pos_ref[8:16, :], wb_ref[...], preferred_element_type=F32)
    hid = _gelu_tanh(top + pltpu.roll(bot, n - 1, 0) + cpos[0:1, :])
    o_ref[0] = jnp.dot(hid.astype(BF16), w2_ref[...], preferred_element_type=F32)


def _nsa_compress(r, pos2, wt, wb, w2):
    B, n, _ = r.shape
    width = CMP_STRIDE * LANES
    return pl.pallas_call(
        _cmp_kernel,
        grid=(B,),
        in_specs=[
            pl.BlockSpec((1, n, width), lambda b: (b, 0, 0)),
            pl.BlockSpec((16, width), lambda b: (0, 0)),
            pl.BlockSpec((width, 2 * CMP_HIDDEN), lambda b: (0, 0)),
            pl.BlockSpec((width, 2 * CMP_HIDDEN), lambda b: (0, 0)),
            pl.BlockSpec((2 * CMP_HIDDEN, LANES), lambda b: (0, 0)),
        ],
        out_specs=pl.BlockSpec((1, n, LANES), lambda b: (b, 0, 0)),
        out_shape=jax.ShapeDtypeStruct((B, n, LANES), F32),
        compiler_params=_cparams(("parallel",)),
        name="nsa_compress",
    )(r, pos2, wt, wb, w2)


def _softmax_rows(s, mask):
    sm = jnp.where(mask, s, NEG)
    m = jnp.maximum(jnp.max(sm, axis=-1, keepdims=True), 0.1 * NEG)
    e = jnp.exp2(sm - m)
    l = jnp.sum(e, axis=-1, keepdims=True)
    return e / jnp.where(l > 0.0, l, 1.0)


def _nsa_kernel(q_ref, cmp_ref, c2s_ref, sel_ref, noh_ref, win_ref, gl_ref, gain_ref, gsel_ref, havg_ref, o_ref, ksa_ref, *, nb, tq, tk, n_sel,
                top_n):
    H = N_NSA
    qi = pl.program_id(1)
    start = qi * tq

    @pl.when(qi == 0)
    def _():
        key_lanes = lax.broadcasted_iota(jnp.int32, noh_ref.shape, 1) < HEAD_DIM
        for bb in range(nb):
            ksa_ref[bb] = jnp.where(key_lanes, sel_ref[bb], noh_ref[...])

    t_row = start + lax.broadcasted_iota(jnp.int32, (tq, 1), 0)
    t_all = jnp.concatenate([t_row] * H, axis=0)
    zeros64 = jnp.zeros((tq, HEAD_DIM), BF16)
    wlen = WINDOW + tq
    base = pl.multiple_of(jnp.maximum(start - WINDOW, 0), tq)
    n_pad = cmp_ref.shape[1]
    assert CMP_STRIDE == 16 and tq & (tq - 1) == 0
    last_cmp = (t_all - (CMP_LEN - 1)) >> 4
    row_in_block = lax.broadcasted_iota(jnp.int32, (H * tq, wlen), 0) & (tq - 1)
    win_lag = (start - base) - (lax.broadcasted_iota(jnp.int32, (H * tq, wlen), 1) - row_in_block)

    def prologue(bb):
        q = q_ref[bb]
        q_heads = [q[:, h * HEAD_DIM:(h + 1) * HEAD_DIM] for h in range(H)]
        q_pad = jnp.concatenate([jnp.concatenate([qh, zeros64], axis=1) for qh in q_heads], axis=0)

        win = win_ref[bb, pl.ds(base, wlen), :]
        s_w = _dot_nt(q_pad, win)
        s_w = jnp.where(lax.bitcast_convert_type(win_lag, jnp.uint32) < WINDOW, s_w, NEG)
        e_w = jnp.exp2(s_w - jnp.max(s_w, axis=-1, keepdims=True))
        win1 = jnp.where(lax.broadcasted_iota(jnp.int32, (wlen, LANES), 1) < HEAD_DIM, jnp.ones((wlen, LANES), BF16), win)
        o_w = jnp.dot(e_w.astype(BF16), win1, preferred_element_type=F32)

        cmp = cmp_ref[bb]
        cmp_hi = cmp.astype(BF16)
        cmp_lo = (cmp - cmp_hi.astype(F32)).astype(BF16)
        s_c = _dot_nt(q_pad, cmp_hi) + _dot_nt(q_pad, cmp_lo)
        n_id = lax.broadcasted_iota(jnp.int32, (H * tq, n_pad), 1)
        p_c = _softmax_rows(s_c, n_id <= last_cmp)
        o_c = jnp.dot(p_c.astype(BF16), cmp_hi, preferred_element_type=F32)

        p_sum = p_c[0:tq]
        for h in range(1, H):
            p_sum = p_sum + p_c[h * tq:(h + 1) * tq]
        p_hi = p_sum.astype(BF16)
        p_lo = (p_sum - p_hi.astype(F32)).astype(BF16)
        c2s_t = c2s_ref[...]
        imp = _dot_nt(c2s_t, p_hi) + _dot_nt(c2s_t, p_lo)
        t_lane = start + lax.broadcasted_iota(jnp.int32, (HEAD_DIM, tq), 1)
        s_id = lax.broadcasted_iota(jnp.int32, (HEAD_DIM, tq), 0)
        cur = t_lane >> 6
        forced = jnp.logical_or(s_id == 0, jnp.logical_or(s_id == cur, s_id == cur - 1))
        score = jnp.where(forced, FORCED_SCORE, imp)
        score = jnp.where(s_id * SEL_LEN <= t_lane, score, -1.0)
        return q_pad, o_c, o_w, [score[g * SUB:(g + 1) * SUB, :] for g in range(N_GROUP)]

    SUB = 8
    N_GROUP = HEAD_DIM // SUB
    stage = [prologue(bb) for bb in range(nb)]

    n_live = (((start + tq - 1) >> 6) >> 3) + 1
    sub_id = lax.broadcasted_iota(jnp.int32, (SUB, tq), 0)

    def rank_level(level, ranks):
        ranks = list(ranks)
        pairs = [(level, g) for g in range(level + 1)] + [(gc, level) for gc in range(level)]
        for bb in range(nb):
            groups = stage[bb][3]
            for gc, g in pairs:
                for k in range(SUB):
                    c = gc * SUB + k
                    if c >= n_sel:
                        continue
                    row = groups[gc][k:k + 1, :]
                    if g * SUB > c:
                        before = row >= groups[g]
                    elif g * SUB + SUB - 1 < c:
                        before = row > groups[g]
                    else:
                        before = jnp.logical_or(row > groups[g], jnp.logical_and(row == groups[g], sub_id > k))
                    ranks[bb * N_GROUP + g] = ranks[bb * N_GROUP + g] + before.astype(jnp.int32)
        return tuple(ranks)

    ranks = tuple(jnp.zeros((SUB, tq), jnp.int32) for _ in range(nb * N_GROUP))
    for level in range(N_GROUP):
        ranks = lax.cond(level < n_live, functools.partial(rank_level, level), lambda r: r, ranks)

    pro = []
    for bb in range(nb):
        q_pad, o_c, o_w, _ = stage[bb]
        not_sel = jnp.concatenate([jnp.where(r < top_n, 0.0, 1.0) for r in ranks[bb * N_GROUP:(bb + 1) * N_GROUP]], axis=0)
        ns = jnp.concatenate([jnp.zeros((HEAD_DIM, tq), F32), not_sel], axis=0).T.astype(BF16)
        pro.append((q_pad + jnp.concatenate([ns] * H, axis=0), o_c, o_w))

    def sel_tile(bb, c0, size, carry, diagonal):
        m, acc = carry
        s = _dot_nt(pro[bb][0], ksa_ref[bb, pl.ds(c0, size), :])
        if diagonal:
            r_id = lax.broadcasted_iota(jnp.int32, (H * tq, size), 0) & (tq - 1)
            c_id = lax.broadcasted_iota(jnp.int32, (H * tq, size), 1)
            s = jnp.where(c_id <= r_id, s, NEG)
        m_new = jnp.maximum(m, jnp.max(s, axis=-1, keepdims=True))
        pr = jnp.exp2(s - m_new)
        key_lanes = lax.broadcasted_iota(jnp.int32, (size, LANES), 1) < HEAD_DIM
        v1 = jnp.where(key_lanes, jnp.ones((size, LANES), BF16), sel_ref[bb, pl.ds(c0, size), :])
        acc = jnp.exp2(m - m_new) * acc + jnp.dot(pr.astype(BF16), v1, preferred_element_type=F32)
        return m_new, acc

    def all_rows(c0, size, carries, diagonal):
        return tuple(sel_tile(bb, c0, size, carries[bb], diagonal) for bb in range(nb))

    carries = tuple((jnp.full((H * tq, 1), NEG, F32), jnp.zeros((H * tq, LANES), F32)) for _ in range(nb))
    n_full = start // tk
    carries = lax.fori_loop(0, n_full, lambda t, c: all_rows(pl.multiple_of(t * tk, tk), tk, c, False), carries)
    pos = n_full * tk
    size = tk // 2
    while size >= tq:
        take = start - pos >= size
        carries = lax.cond(take, lambda c, pos=pos, size=size: all_rows(pl.multiple_of(pos, tq), size, c, False), lambda c: c, carries)
        pos = pos + jnp.where(take, size, 0)
        size //= 2
    carries = all_rows(pl.multiple_of(start, tq), tq, carries, True)

    first_half = lax.broadcasted_iota(jnp.int32, (tq, LANES), 1) < HEAD_DIM

    def place(acc, normalise):
        cols = []
        for h in range(0, H, 2):
            a0, a1 = acc[h * tq:(h + 1) * tq], acc[(h + 1) * tq:(h + 2) * tq]
            r0 = pltpu.roll(a0, HEAD_DIM, 1)
            if normalise:
                v0 = r0 * (1.0 / a0)
                v1 = a1 * (1.0 / pltpu.roll(a1, HEAD_DIM, 1))
            else:
                v0, v1 = r0, a1
            cols.append(jnp.where(first_half, v0, v1))
        return jnp.concatenate(cols, axis=1)

    for bb in range(nb):
        _, o_c, o_w = pro[bb]
        gates = 1.0 / (1.0 + jnp.exp(-gl_ref[bb]))
        g_hi = gates.astype(BF16)
        g_lo = (gates - g_hi.astype(F32)).astype(BF16)
        mix = None
        for c, y in enumerate((place(o_c, False), place(carries[bb][1], True), place(o_w, True))):
            g = jnp.dot(g_hi, gsel_ref[c], preferred_element_type=F32) + jnp.dot(g_lo, gsel_ref[c], preferred_element_type=F32)
            mix = g * y if mix is None else mix + g * y
        sq = mix * mix
        sq_hi = sq.astype(BF16)
        sq_lo = (sq - sq_hi.astype(F32)).astype(BF16)
        ms = jnp.dot(sq_hi, havg_ref[...], preferred_element_type=F32) + jnp.dot(sq_lo, havg_ref[...], preferred_element_type=F32)
        o_ref[bb] = (mix * lax.rsqrt(ms + RMS_EPS) * gain_ref[...]).astype(BF16)


def _nsa(main3, cmp, c2s, neg_onehot, small3, gain_row, nb, tq, tk):
    B, S, _ = main3.shape
    n_sel = S // SEL_LEN
    n_pad = cmp.shape[1]
    gsel = np.zeros((3, LANES, NSA_W), np.float32)
    for h in range(N_NSA):
        for c in range(3):
            gsel[c, SMALL_GATE + 3 * h + c, h * HEAD_DIM:(h + 1) * HEAD_DIM] = 1.0
    head_of = np.arange(NSA_W) // HEAD_DIM
    havg = (head_of[:, None] == head_of[None, :]).astype(np.float32) / HEAD_DIM
    return pl.pallas_call(
        functools.partial(_nsa_kernel, nb=nb, tq=tq, tk=tk, n_sel=n_sel, top_n=min(TOP_N, n_sel)),
        grid=(B // nb, S // tq),
        in_specs=[
            pl.BlockSpec((nb, tq, NSA_W), lambda b, i: (b, i, CB_NQ // 2)),
            pl.BlockSpec((nb, n_pad, LANES), lambda b, i: (b, 0, 0)),
            pl.BlockSpec((HEAD_DIM, n_pad), lambda b, i: (0, 0)),
            pl.BlockSpec((nb, S, LANES), lambda b, i: (b, 0, CB_SEL)),
            pl.BlockSpec((S, LANES), lambda b, i: (0, 0)),
            pl.BlockSpec((nb, S, LANES), lambda b, i: (b, 0, CB_WIN)),
            pl.BlockSpec((nb, tq, LANES), lambda b, i: (b, i, 0)),
            pl.BlockSpec((1, NSA_W), lambda b, i: (0, 0)),
            pl.BlockSpec((3, LANES, NSA_W), lambda b, i: (0, 0, 0)),
            pl.BlockSpec((NSA_W, NSA_W), lambda b, i: (0, 0)),
        ],
        out_specs=pl.BlockSpec((nb, tq, NSA_W), lambda b, i: (b, i, 0)),
        out_shape=jax.ShapeDtypeStruct((B, S, NSA_W), BF16),
        scratch_shapes=[pltpu.VMEM((nb, S, LANES), BF16)],
        compiler_params=_cparams(("parallel", "arbitrary")),
        name="nsa_attention",
    )(main3, cmp, c2s, main3, neg_onehot, main3, small3, gain_row, jnp.asarray(gsel, BF16), jnp.asarray(havg, BF16))


def _post_kernel(x_ref, yr_ref, yf_ref, yn_ref, wo_ref, g_ref, w1_ref, w2_ref, gf_ref, o_ref, *, ff_chunk, final):
    x = x_ref[...]
    x = x + jnp.dot(yr_ref[...], wo_ref[0:RET_W, :], preferred_element_type=F32)
    x = x + jnp.dot(yf_ref[...], wo_ref[RET_W:RET_W + FOX_W, :], preferred_element_type=F32)
    x = x + jnp.dot(yn_ref[...], wo_ref[RET_W + FOX_W:, :], preferred_element_type=F32)
    h = _rms(x, g_ref[...]).astype(BF16)
    o_ref[...] = x
    for c0 in range(0, D_FF, ff_chunk):
        hid = jnp.maximum(jnp.dot(h, w1_ref[:, c0:c0 + ff_chunk], preferred_element_type=F32), 0.0)
        o_ref[...] += jnp.dot((hid * hid).astype(BF16), w2_ref[c0:c0 + ff_chunk, :], preferred_element_type=F32)
    if final:
        o_ref[...] = _rms(o_ref[...], gf_ref[...])


def _post(x2d, yr, yf, yn, wo, g, w1, w2, gf, tm, final):
    T = x2d.shape[0]
    row = lambda w: pl.BlockSpec((tm, w), lambda i: (i, 0))
    return pl.pallas_call(
        functools.partial(_post_kernel, ff_chunk=512, final=final),
        grid=(T // tm,),
        in_specs=[
            row(D_MODEL), row(RET_W), row(FOX_W), row(NSA_W),
            _resident((D_MODEL, D_MODEL), lambda i: (0, 0)),
            _resident((1, D_MODEL), lambda i: (0, 0)),
            _resident((D_MODEL, D_FF), lambda i: (0, 0)),
            _resident((D_FF, D_MODEL), lambda i: (0, 0)),
            _resident((1, D_MODEL), lambda i: (0, 0)),
        ],
        out_specs=row(D_MODEL),
        out_shape=jax.ShapeDtypeStruct((T, D_MODEL), F32),
        compiler_params=_cparams(("parallel",)),
        name="outproj_mlp",
    )(x2d, yr, yf, yn, wo, g, w1, w2, gf)


def _tile_sizes(batch):
    nb = 4 if batch % 4 == 0 else (2 if batch % 2 == 0 else 1)
    return dict(tm=1024, fox_nb=nb, fox_tq=512, fox_tk=512, nsa_nb=nb, nsa_tq=256, nsa_tk=1024)


def kernel(x, norm_attn, w_in, fox_forget_bias, ret_norm_gain, fox_norm_gain, nsa_norm_gain, nsa_cmp_pos_k, nsa_cmp_pos_v, nsa_cmp_w1_k, nsa_cmp_w2_k, nsa_cmp_w1_v, nsa_cmp_w2_v, w_out, norm_mlp, w_mlp_in, w_mlp_out, norm_final):
    B, S, D = x.shape
    depth = w_in.shape[0]
    ts = _tile_sizes(B)
    assert D == D_MODEL and S % ts["fox_tq"] == 0 and S >= WINDOW + ts["nsa_tq"] and (B * S) % ts["tm"] == 0
    T = B * S

    wm, ws = _in_proj_weights(w_in)
    wo = w_out.astype(BF16)
    w1 = w_mlp_in.astype(BF16)
    w2 = w_mlp_out.astype(BF16)

    def expand_w1(wk, wv):
        L = wk.shape[0]
        wk = wk.reshape(L, CMP_LEN, HEAD_DIM, CMP_HIDDEN)
        wv = wv.reshape(L, CMP_LEN, HEAD_DIM, CMP_HIDDEN)
        z = jnp.zeros_like(wk)
        full = jnp.concatenate([jnp.concatenate([wk, z], axis=-1), jnp.concatenate([z, wv], axis=-1)], axis=2)
        full = full.reshape(L, CMP_LEN * LANES, 2 * CMP_HIDDEN).astype(BF16)
        return full[:, :CMP_STRIDE * LANES], full[:, CMP_STRIDE * LANES:]

    wt_all, wb_all = expand_w1(nsa_cmp_w1_k, nsa_cmp_w1_v)
    zk = jnp.zeros_like(nsa_cmp_w2_k)
    w2c = jnp.concatenate([jnp.concatenate([nsa_cmp_w2_k, zk], axis=-1),
                           jnp.concatenate([zk, nsa_cmp_w2_v], axis=-1)], axis=1).astype(BF16)
    pos = jnp.concatenate([nsa_cmp_pos_k, nsa_cmp_pos_v], axis=-1)
    pos_tb = pos.reshape(depth, 2, 1, CMP_STRIDE * LANES)
    pos_tb = jnp.broadcast_to(pos_tb, (depth, 2, 8, CMP_STRIDE * LANES)).reshape(depth, 16, CMP_STRIDE * LANES).astype(BF16)

    n_cmp = (S - CMP_LEN) // CMP_STRIDE + 1
    n_sel = S // SEL_LEN
    n_rows = S // CMP_STRIDE
    cs = np.arange(n_rows) * CMP_STRIDE
    ss = np.arange(n_sel) * SEL_LEN
    overlap = np.clip(np.minimum(cs[:, None] + CMP_LEN, ss[None, :] + SEL_LEN) - np.maximum(cs[:, None], ss[None, :]), 0, None)
    overlap[n_cmp:] = 0
    c2s_np = np.zeros((HEAD_DIM, n_rows), np.float32)
    c2s_np[:n_sel] = (overlap / CMP_LEN).T
    c2s = jnp.asarray(c2s_np, dtype=BF16)
    assert n_sel <= HEAD_DIM, "the selected-branch key augmentation has 64 lanes, one per selection block"
    neg_onehot = jnp.asarray(np.where((np.arange(S)[:, None] // SEL_LEN) == np.arange(LANES)[None, :] - HEAD_DIM, NEG, 0.0), dtype=BF16)

    tables = _retention_tables(S)
    fbias = jnp.zeros((depth, 1, LANES), F32).at[:, 0, SMALL_FF:SMALL_FF + N_FOX].set(fox_forget_bias)

    x2d = x.reshape(T, D)
    for l in range(depth):
        main, small = _inproj(x2d, norm_attn[l][None, :], wm[l], ws[l], ts["tm"])
        main3 = main.reshape(B, S, N_MAIN)
        small3 = small.reshape(B, S, LANES)
        fox_qa, fox_ka, edges = _forget_cumsum(small3, fbias[l])
        y_ret = _retention(main3, tables, ret_norm_gain[l][None, :])
        per_tile = ts["fox_tk"] // RET_CHUNK
        cq = edges[:, 0::per_tile, 0, SMALL_FF:SMALL_FF + N_FOX]
        ck = edges[:, per_tile - 1::per_tile, 1, SMALL_FF:SMALL_FF + N_FOX]
        cum_edges = jnp.stack([cq.transpose(0, 2, 1), ck.transpose(0, 2, 1)], axis=2).reshape(-1)
        y_fox = _fox(main3, fox_qa, fox_ka, cum_edges, fox_norm_gain[l][None, :], ts["fox_nb"], ts["fox_tq"], ts["fox_tk"])
        r = main3[:, :, CB_CMP * LANES:(CB_CMP + 1) * LANES].reshape(B, n_rows, CMP_STRIDE * LANES)
        cmp = _nsa_compress(r, pos_tb[l], wt_all[l], wb_all[l], w2c[l])
        y_nsa = _nsa(main3, cmp, c2s, neg_onehot, small3, nsa_norm_gain[l][None, :], ts["nsa_nb"], ts["nsa_tq"], ts["nsa_tk"])
        x2d = _post(x2d, y_ret.reshape(T, RET_W), y_fox.reshape(T, FOX_W), y_nsa.reshape(T, NSA_W), wo[l],
                    norm_mlp[l][None, :], w1[l], w2[l], norm_final[None, :], ts["tm"], final=(l == depth - 1))
    return x2d.reshape(B, S, D)
```

```python
import functools

import numpy as np
import jax
import jax.numpy as jnp
from jax import lax
from jax.experimental import pallas as pl
from jax.experimental.pallas import tpu as pltpu

F32 = jnp.float32
BF16 = jnp.bfloat16

D_MODEL = 1024
HEAD_DIM = 64
N_RET = 6
N_FOX = 6
N_NSA = 4
RET_W = N_RET * HEAD_DIM
FOX_W = N_FOX * HEAD_DIM
NSA_W = N_NSA * HEAD_DIM
D_FF = 4 * D_MODEL
RMS_EPS = 1e-6
RET_CHUNK = 128
RET_BLOCK = 256
ROPE_BASE = 10000.0
CMP_LEN = 32
CMP_STRIDE = 16
CMP_HIDDEN = 4 * HEAD_DIM
SEL_LEN = 64
TOP_N = 16
WINDOW = 512
FORCED_SCORE = 1e4
NEG = -1e30
LOG2E = float(np.log2(np.e))

LANES = 128
N_PAIR = N_RET // 2

CB_RQ, CB_RK, CB_RV, CB_RG = 0, 3, 6, 9
CB_NQ = 12
CB_FQ, CB_FK, CB_FV = 14, 17, 20
CB_CMP, CB_SEL, CB_WIN = 23, 24, 25
N_MAIN = 26 * LANES
SMALL_FF = 0
SMALL_GATE = 8

VMEM_LIMIT = 56 * 1024 * 1024


def _cparams(sem):
    return pltpu.CompilerParams(dimension_semantics=sem, vmem_limit_bytes=VMEM_LIMIT)


def _resident(shape, index_map):
    return pl.BlockSpec(shape, index_map, pipeline_mode=pl.Buffered(1))


def _in_proj_weights(w_in):
    sizes = (RET_W, RET_W, RET_W, RET_W, FOX_W, FOX_W, FOX_W, N_FOX, NSA_W) + (HEAD_DIM,) * 6 + (3 * N_NSA,)
    off = np.concatenate([[0], np.cumsum(sizes)])
    (o_rq, o_rk, o_rv, o_rg, o_fq, o_fk, o_fv, o_ff, o_nq, o_kc, o_vc, o_ks, o_vs, o_kw, o_vw, o_gt) = [int(o) for o in off[:-1]]
    L, D, _ = w_in.shape
    half = HEAD_DIM // 2
    cut = lambda o, n: w_in[:, :, o:o + n]

    def interleave(w):
        return w.reshape(L, D, N_PAIR, 2, 2, half).transpose(0, 1, 2, 4, 3, 5).reshape(L, D, RET_W)

    qk_scale = HEAD_DIM ** -0.5
    sm_scale = qk_scale * LOG2E
    main = jnp.concatenate([
        interleave(cut(o_rq, RET_W)), interleave(cut(o_rk, RET_W)) * qk_scale, cut(o_rv, RET_W), cut(o_rg, RET_W),
        cut(o_nq, NSA_W) * sm_scale, cut(o_fq, FOX_W) * sm_scale, cut(o_fk, FOX_W), cut(o_fv, FOX_W),
        cut(o_kc, 6 * HEAD_DIM),
    ], axis=-1)
    assert main.shape[-1] == N_MAIN and o_vw == o_kc + 5 * HEAD_DIM
    zeros = lambda n: jnp.zeros((L, D, n), w_in.dtype)
    small = jnp.concatenate([
        zeros(SMALL_FF), cut(o_ff, N_FOX), zeros(SMALL_GATE - SMALL_FF - N_FOX),
        cut(o_gt, 3 * N_NSA), zeros(LANES - SMALL_GATE - 3 * N_NSA)], axis=-1)
    return main.astype(BF16), small.astype(BF16)


def _retention_tables(seq):
    half = HEAD_DIM // 2
    inv = 1.0 / (ROPE_BASE ** (jnp.arange(half, dtype=F32) / half))
    ang = jnp.arange(seq, dtype=F32)[:, None] * inv[None, :]
    cos, sin = jnp.cos(ang), jnp.sin(ang)
    cos_t = jnp.concatenate([cos, cos, cos, cos], axis=-1)
    sin_t = jnp.concatenate([-sin, -sin, sin, sin], axis=-1)
    log_gamma = np.log(1.0 - 2.0 ** (-5.0 - np.arange(N_RET, dtype=np.float32))).astype(np.float32)
    C = RET_BLOCK
    idx = np.arange(C, dtype=np.float32)
    diff = idx[:, None] - idx[None, :]
    lane = np.arange(LANES)
    head_k = (lane % HEAD_DIM) // half
    head_v = lane // HEAD_DIM
    d_in = np.zeros((N_PAIR, 2, C, C), np.float32)
    d_k = np.zeros((N_PAIR, C, LANES), np.float32)
    d_q = np.zeros((N_PAIR, C, LANES), np.float32)
    d_c = np.zeros((N_PAIR, LANES, LANES), np.float32)
    for p in range(N_PAIR):
        lg = log_gamma[2 * p:2 * p + 2]
        for j in range(2):
            d_in[p, j] = np.where(diff >= 0, np.exp(lg[j] * np.maximum(diff, 0.0)), 0.0)
        d_k[p] = np.exp(lg[head_k][None, :] * (C - 1.0 - idx)[:, None])
        d_q[p] = np.exp(lg[head_v][None, :] * (idx + 1.0)[:, None])
        d_c[p] = np.broadcast_to(np.exp(lg[head_v] * C)[None, :], (LANES, LANES))
    s_mask = (head_k[:, None] == head_v[None, :]).astype(np.float32)
    return cos_t, sin_t, jnp.asarray(d_in), jnp.asarray(d_k), jnp.asarray(d_q), jnp.asarray(d_c), jnp.asarray(s_mask)


def _rms(x, g):
    return x * lax.rsqrt(jnp.mean(x * x, axis=-1, keepdims=True) + RMS_EPS) * g


def _inproj_kernel(x_ref, g_ref, wm_ref, ws_ref, om_ref, os_ref, *, n_chunk):
    h = _rms(x_ref[...], g_ref[...]).astype(BF16)
    for n0 in range(0, N_MAIN, n_chunk):
        n1 = min(n0 + n_chunk, N_MAIN)
        om_ref[:, n0:n1] = jnp.dot(h, wm_ref[:, n0:n1], preferred_element_type=F32).astype(BF16)
    os_ref[...] = jnp.dot(h, ws_ref[...], preferred_element_type=F32)


def _inproj(x2d, g, wm, ws, tm):
    T = x2d.shape[0]
    return pl.pallas_call(
        functools.partial(_inproj_kernel, n_chunk=512),
        grid=(T // tm,),
        in_specs=[
            pl.BlockSpec((tm, D_MODEL), lambda i: (i, 0)),
            _resident((1, D_MODEL), lambda i: (0, 0)),
            _resident((D_MODEL, N_MAIN), lambda i: (0, 0)),
            _resident((D_MODEL, LANES), lambda i: (0, 0)),
        ],
        out_specs=[
            pl.BlockSpec((tm, N_MAIN), lambda i: (i, 0)),
            pl.BlockSpec((tm, LANES), lambda i: (i, 0)),
        ],
        out_shape=[jax.ShapeDtypeStruct((T, N_MAIN), BF16), jax.ShapeDtypeStruct((T, LANES), F32)],
        compiler_params=_cparams(("parallel",)),
        name="inproj",
    )(x2d, g, wm, ws)


N_SPLIT = 3
AUG_STRIDE = 2 * N_SPLIT


def _split_bf16(x):
    terms, rest = [], x
    for _ in range(N_SPLIT):
        t = rest.astype(BF16)
        rest = rest - t.astype(F32)
        terms.append(t)
    return terms


def _bias_placement():
    mq = np.zeros((N_SPLIT, LANES, LANES), np.float32)
    mk = np.zeros((N_SPLIT, LANES, LANES), np.float32)
    one_q = np.zeros((1, LANES), np.float32)
    one_k = np.zeros((1, LANES), np.float32)
    for h in range(N_FOX):
        base = h * AUG_STRIDE
        for i in range(N_SPLIT):
            mq[i, SMALL_FF + h, base + i] = 1.0
            mk[i, SMALL_FF + h, base + N_SPLIT + i] = -1.0
            one_q[0, base + N_SPLIT + i] = 1.0
            one_k[0, base + i] = 1.0
    return jnp.asarray(mq, BF16), jnp.asarray(mk, BF16), jnp.asarray(one_q), jnp.asarray(one_k)


def _cum_kernel(z_ref, b_ref, tri_ref, mq_ref, mk_ref, oq_ref, ok_ref, qa_ref, ka_ref, edge_ref, *, n_chunks):
    C = RET_CHUNK

    def chunk(c, carry):
        r0 = pl.multiple_of(c * C, C)
        z = z_ref[0, pl.ds(r0, C), :] + b_ref[...]
        lf = jnp.minimum(z, 0.0) - jnp.log1p(jnp.exp(-jnp.abs(z)))
        terms = _split_bf16(lf)
        local = jnp.dot(tri_ref[...], terms[0], preferred_element_type=F32)
        for term in terms[1:]:
            local = local + jnp.dot(tri_ref[...], term, preferred_element_type=F32)
        cs = local + carry
        cs2 = cs * LOG2E
        edge_ref[0, pl.ds(c, 1)] = jnp.concatenate([cs2[0:1], cs2[C - 1:C], jnp.zeros((6, LANES), F32)], axis=0)[None]
        qa, ka = oq_ref[...], ok_ref[...]
        for i, term in enumerate(_split_bf16(cs2)):
            qa = qa + jnp.dot(term, mq_ref[i], preferred_element_type=F32)
            ka = ka + jnp.dot(term, mk_ref[i], preferred_element_type=F32)
        qa_ref[0, pl.ds(r0, C), :] = qa.astype(BF16)
        ka_ref[0, pl.ds(r0, C), :] = ka.astype(BF16)
        return cs[C - 1:C, :]

    unroll = max(u for u in (1, 2, 4) if n_chunks % u == 0)

    def body(i, carry):
        for u in range(unroll):
            carry = chunk(i * unroll + u, carry)
        return carry

    lax.fori_loop(0, n_chunks // unroll, body, jnp.zeros((1, LANES), F32))


def _forget_cumsum(small3, bias_row):
    B, S, _ = small3.shape
    tri = jnp.asarray(np.tril(np.ones((RET_CHUNK, RET_CHUNK), np.float32)), BF16)
    mq, mk, one_q, one_k = _bias_placement()
    const = lambda shape: pl.BlockSpec(shape, lambda b: (0,) * len(shape))
    return pl.pallas_call(
        functools.partial(_cum_kernel, n_chunks=S // RET_CHUNK),
        grid=(B,),
        in_specs=[
            pl.BlockSpec((1, S, LANES), lambda b: (b, 0, 0)),
            const((1, LANES)), const((RET_CHUNK, RET_CHUNK)),
            const((N_SPLIT, LANES, LANES)), const((N_SPLIT, LANES, LANES)), const((1, LANES)), const((1, LANES)),
        ],
        out_specs=[
            pl.BlockSpec((1, S, LANES), lambda b: (b, 0, 0)),
            pl.BlockSpec((1, S, LANES), lambda b: (b, 0, 0)),
            pl.BlockSpec((1, S // RET_CHUNK, 8, LANES), lambda b: (b, 0, 0, 0)),
        ],
        out_shape=[jax.ShapeDtypeStruct((B, S, LANES), BF16), jax.ShapeDtypeStruct((B, S, LANES), BF16),
                   jax.ShapeDtypeStruct((B, S // RET_CHUNK, 8, LANES), F32)],
        compiler_params=_cparams(("parallel",)),
        name="forget_cumsum",
    )(small3, bias_row, tri, mq, mk, one_q, one_k)


def _dot_nt(a, b):
    return lax.dot_general(a, b, (((1,), (1,)), ((), ())), preferred_element_type=F32)


def _ret_kernel(q_ref, k_ref, v_ref, g_ref, cos_ref, sin_ref, din_ref, dk_ref, dq_ref, dc_ref, sm_ref, gain_ref,
                o_ref, state_ref, *, n_chunks, unroll):
    C = RET_BLOCK
    half = HEAD_DIM // 2
    lane = lax.broadcasted_iota(jnp.int32, (C, LANES), 1)
    khead = (lane % HEAD_DIM) // half
    vhead0 = lane < HEAD_DIM
    state_ref[...] = jnp.zeros_like(state_ref)

    def pair_chunk(p, r0):
        rows = pl.ds(r0, C)
        lanes = slice(p * LANES, (p + 1) * LANES)
        cs, sn = cos_ref[rows, :], sin_ref[rows, :]
        q = q_ref[0, rows, lanes].astype(F32)
        k = k_ref[0, rows, lanes].astype(F32)
        q = q * cs + pltpu.roll(q, HEAD_DIM, 1) * sn
        k = k * cs + pltpu.roll(k, HEAD_DIM, 1) * sn
        v = v_ref[0, rows, lanes]
        qb, kb = q.astype(BF16), k.astype(BF16)
        state = state_ref[p]
        cross = jnp.dot(qb, state.astype(BF16), preferred_element_type=F32) * dq_ref[p]
        outs = []
        for j in range(2):
            qm = jnp.where(khead == j, qb, jnp.zeros_like(qb))
            inner = _dot_nt(qm, kb) * din_ref[p, j]
            outs.append(jnp.dot(inner.astype(BF16), v, preferred_element_type=F32))
        out = jnp.where(vhead0, outs[0], outs[1]) + cross
        kd = (k * dk_ref[p]).T.astype(BF16)
        state_ref[p] = state * dc_ref[p] + jnp.dot(kd, v, preferred_element_type=F32) * sm_ref[...]
        inv = 1.0 / HEAD_DIM
        s0 = jnp.sum(jnp.where(vhead0, out, 0.0), axis=-1, keepdims=True)
        s1 = jnp.sum(jnp.where(vhead0, 0.0, out), axis=-1, keepdims=True)
        yc = out - jnp.where(vhead0, s0, s1) * inv
        yc2 = yc * yc
        v0 = jnp.sum(jnp.where(vhead0, yc2, 0.0), axis=-1, keepdims=True)
        v1 = jnp.sum(jnp.where(vhead0, 0.0, yc2), axis=-1, keepdims=True)
        y = yc * lax.rsqrt(jnp.where(vhead0, v0, v1) * inv + RMS_EPS) * gain_ref[:, lanes]
        g = g_ref[0, rows, lanes].astype(F32)
        o_ref[0, rows, lanes] = (y * (g * (1.0 / (1.0 + jnp.exp(-g))))).astype(BF16)

    def body(c, _):
        for u in range(unroll):
            for p in range(N_PAIR):
                pair_chunk(p, pl.multiple_of((c * unroll + u) * C, C))
        return 0

    lax.fori_loop(0, n_chunks // unroll, body, 0)


def _retention(main3, tables, gain_row):
    B, S, _ = main3.shape
    cos_t, sin_t, d_in, d_k, d_q, d_c, s_mask = tables
    C = RET_BLOCK
    n_chunks = S // C
    seq_spec = lambda cb: pl.BlockSpec((1, S, RET_W), lambda b: (b, 0, cb // N_PAIR))
    const = lambda shape: pl.BlockSpec(shape, lambda b: (0,) * len(shape))
    return pl.pallas_call(
        functools.partial(_ret_kernel, n_chunks=n_chunks, unroll=2 if n_chunks % 2 == 0 else 1),
        grid=(B,),
        in_specs=[
            seq_spec(CB_RQ), seq_spec(CB_RK), seq_spec(CB_RV), seq_spec(CB_RG),
            const((S, LANES)), const((S, LANES)),
            const((N_PAIR, 2, C, C)), const((N_PAIR, C, LANES)), const((N_PAIR, C, LANES)),
            const((N_PAIR, LANES, LANES)), const((LANES, LANES)), const((1, RET_W)),
        ],
        out_specs=pl.BlockSpec((1, S, RET_W), lambda b: (b, 0, 0)),
        out_shape=jax.ShapeDtypeStruct((B, S, RET_W), BF16),
        scratch_shapes=[pltpu.VMEM((N_PAIR, LANES, LANES), F32)],
        compiler_params=_cparams(("parallel",)),
        name="retention",
    )(main3, main3, main3, main3, cos_t, sin_t, d_in, d_k, d_q, d_c, s_mask, gain_row)


UNDERFLOW_LOG2 = -170.0


def _head_sq_norm_max(x, havg, head0):
    xf = x.astype(F32)
    mean = jnp.dot((xf * xf).astype(BF16), havg, preferred_element_type=F32)
    bound = mean * (HEAD_DIM * 1.01)
    return jnp.max(jnp.where(head0, bound, 0.0)), jnp.max(jnp.where(head0, 0.0, bound))


def _fox_kernel(cb_ref, q_ref, qa_ref, qs_ref, kh_ref, ka_ref, v_ref, gain_ref, havg_ref, o_ref, k_ref, ft_ref, *, nb, tq, tk, n_tiles):
    p = pl.program_id(1)
    qi = pl.program_id(2)
    lane_q = lax.broadcasted_iota(jnp.int32, (tq, LANES), 1)
    head0 = lane_q < HEAD_DIM

    @pl.when(qi == 0)
    def _():
        head0_s = lax.broadcasted_iota(jnp.int32, kh_ref.shape[1:], 1) < HEAD_DIM
        for bb in range(nb):
            k_ref[bb, :, 0:LANES] = kh_ref[bb]
            k_ref[bb, :, LANES:2 * LANES] = ka_ref[bb]
            b = pl.program_id(0) * nb + bb
            kn = _head_sq_norm_max(kh_ref[bb], havg_ref[...], head0_s)
            for i in range(n_tiles):
                qn = _head_sq_norm_max(qs_ref[bb, i * tq:(i + 1) * tq, :], havg_ref[...], head0)
                for j in range(2):
                    h = 2 * p + j
                    slack = 2.0 * jnp.sqrt(qn[j] * kn[j])
                    cq = cb_ref[((b * N_FOX + h) * 2 + 0) * n_tiles + i]
                    skipped = jnp.int32(0)
                    for t in range(i):
                        ck = cb_ref[((b * N_FOX + h) * 2 + 1) * n_tiles + t]
                        skipped = skipped + (slack + cq - ck < UNDERFLOW_LOG2).astype(jnp.int32)
                    ft_ref[(bb * n_tiles + i) * 2 + j] = skipped

    q_pos = qi * tq + lax.broadcasted_iota(jnp.int32, (tq, tk), 0)
    k_off = lax.broadcasted_iota(jnp.int32, (tq, tk), 1)
    chains = [(bb, j) for bb in range(nb) for j in range(2)]

    qops = {}
    for bb in range(nb):
        q = q_ref[bb]
        qa = qa_ref[bb]
        for j in range(2):
            qm = jnp.where(head0 if j == 0 else jnp.logical_not(head0), q, jnp.zeros_like(q))
            lo = (2 * p + j) * AUG_STRIDE
            own = jnp.logical_and(lane_q >= lo, lane_q < lo + AUG_STRIDE)
            qops[bb, j] = jnp.concatenate([qm, jnp.where(own, qa, jnp.zeros_like(qa))], axis=1)

    def tile(ch, t, n, carry, masked):
        bb, j = ch
        m, acc = carry
        c0 = pl.multiple_of(t * tk, tk)
        s = _dot_nt(qops[ch], k_ref[bb, pl.ds(c0, n * tk), :])
        if masked:
            s = jnp.where(k_off + c0 <= q_pos, s, NEG)
        m_new = jnp.maximum(m, jnp.max(s, axis=-1, keepdims=True))
        pr = jnp.exp2(s - m_new)
        own = lax.broadcasted_iota(jnp.int32, (n * tk, LANES), 1) < HEAD_DIM
        vj = jnp.where(own if j == 0 else jnp.logical_not(own), v_ref[bb, pl.ds(c0, n * tk), :], jnp.ones((n * tk, LANES), BF16))
        acc = jnp.exp2(m - m_new) * acc + jnp.dot(pr.astype(BF16), vj, preferred_element_type=F32)
        return m_new, acc

    carry = {}
    for ch in chains:
        bb, j = ch
        t0 = ft_ref[(bb * n_tiles + qi) * 2 + j]
        n_act = qi - t0
        c = (jnp.full((tq, 1), NEG, F32), jnp.zeros((tq, LANES), F32))
        c = lax.fori_loop(0, n_act // 2, lambda i, c, ch=ch, t0=t0: tile(ch, t0 + 2 * i, 2, c, False), c)
        carry[ch] = lax.cond(n_act % 2 == 1, lambda c, ch=ch: tile(ch, qi - 1, 1, c, False), lambda c: c, c)
    for ch in chains:
        carry[ch] = tile(ch, qi, 1, carry[ch], True)
    for bb in range(nb):
        a0, a1 = carry[bb, 0][1], carry[bb, 1][1]
        out = jnp.where(head0, a0 * (1.0 / pltpu.roll(a0, HEAD_DIM, 1)), a1 * (1.0 / pltpu.roll(a1, HEAD_DIM, 1)))
        sq = out * out
        sq_hi = sq.astype(BF16)
        sq_lo = (sq - sq_hi.astype(F32)).astype(BF16)
        ms = jnp.dot(sq_hi, havg_ref[...], preferred_element_type=F32) + jnp.dot(sq_lo, havg_ref[...], preferred_element_type=F32)
        o_ref[bb] = (out * lax.rsqrt(ms + RMS_EPS) * gain_ref[...]).astype(BF16)


def _fox(main3, q_aug, k_aug, cum_edges, gain_row, nb, tq, tk):
    B, S, _ = main3.shape
    assert tq == tk, "the kernel handles exactly one diagonal tile per query block"
    head_of = np.arange(LANES) // HEAD_DIM
    havg = (head_of[:, None] == head_of[None, :]).astype(np.float32) / HEAD_DIM
    grid_spec = pltpu.PrefetchScalarGridSpec(
        num_scalar_prefetch=1,
        grid=(B // nb, N_PAIR, S // tq),
        in_specs=[
            pl.BlockSpec((nb, tq, LANES), lambda b, p, i, cb: (b, i, CB_FQ + p)),
            pl.BlockSpec((nb, tq, LANES), lambda b, p, i, cb: (b, i, 0)),
            pl.BlockSpec((nb, S, LANES), lambda b, p, i, cb: (b, 0, CB_FQ + p)),
            pl.BlockSpec((nb, S, LANES), lambda b, p, i, cb: (b, 0, CB_FK + p)),
            pl.BlockSpec((nb, S, LANES), lambda b, p, i, cb: (b, 0, 0)),
            pl.BlockSpec((nb, S, LANES), lambda b, p, i, cb: (b, 0, CB_FV + p)),
            pl.BlockSpec((1, LANES), lambda b, p, i, cb: (0, p)),
            pl.BlockSpec((LANES, LANES), lambda b, p, i, cb: (0, 0)),
        ],
        out_specs=pl.BlockSpec((nb, tq, LANES), lambda b, p, i, cb: (b, i, p)),
        scratch_shapes=[pltpu.VMEM((nb, S, 2 * LANES), BF16), pltpu.SMEM((nb * 2 * (S // tk),), jnp.int32)],
    )
    return pl.pallas_call(
        functools.partial(_fox_kernel, nb=nb, tq=tq, tk=tk, n_tiles=S // tk),
        grid_spec=grid_spec,
        out_shape=jax.ShapeDtypeStruct((B, S, FOX_W), BF16),
        compiler_params=_cparams(("parallel", "parallel", "arbitrary")),
        name="fox_attention",
    )(cum_edges, main3, q_aug, main3, main3, k_aug, main3, gain_row, jnp.asarray(havg, BF16))


def _gelu_tanh(x):
    return 0.5 * x * (1.0 + jnp.tanh(np.sqrt(2.0 / np.pi).astype(np.float32) * (x + 0.044715 * (x * x * x))))


def _cmp_kernel(r_ref, pos_ref, wt_ref, wb_ref, w2_ref, o_ref):
    r = r_ref[0]
    n = r.shape[0]
    top = jnp.dot(r, wt_ref[...], preferred_element_type=F32)
    bot = jnp.dot(r, wb_ref[...], preferred_element_type=F32)
    cpos = jnp.dot(pos_ref[0:8, :], wt_ref[...], preferred_element_type=F32)
    cpos = cpos + jnp.dot(pos_ref[8:16, :], wb_ref[...], preferred_element_type=F32)
    hid = _gelu_tanh(top + pltpu.roll(bot, n - 1, 0) + cpos[0:1, :])
    o_ref[0] = jnp.dot(hid.astype(BF16), w2_ref[...], preferred_element_type=F32)


def _nsa_compress(r, pos2, wt, wb, w2):
    B, n, _ = r.shape
    width = CMP_STRIDE * LANES
    return pl.pallas_call(
        _cmp_kernel,
        grid=(B,),
        in_specs=[
            pl.BlockSpec((1, n, width), lambda b: (b, 0, 0)),
            pl.BlockSpec((16, width), lambda b: (0, 0)),
            pl.BlockSpec((width, 2 * CMP_HIDDEN), lambda b: (0, 0)),
            pl.BlockSpec((width, 2 * CMP_HIDDEN), lambda b: (0, 0)),
            pl.BlockSpec((2 * CMP_HIDDEN, LANES), lambda b: (0, 0)),
        ],
        out_specs=pl.BlockSpec((1, n, LANES), lambda b: (b, 0, 0)),
        out_shape=jax.ShapeDtypeStruct((B, n, LANES), F32),
        compiler_params=_cparams(("parallel",)),
        name="nsa_compress",
    )(r, pos2, wt, wb, w2)


def _softmax_rows(s, mask):
    sm = jnp.where(mask, s, NEG)
    m = jnp.maximum(jnp.max(sm, axis=-1, keepdims=True), 0.1 * NEG)
    e = jnp.exp2(sm - m)
    l = jnp.sum(e, axis=-1, keepdims=True)
    return e / jnp.where(l > 0.0, l, 1.0)


def _nsa_kernel(q_ref, cmp_ref, c2s_ref, sel_ref, noh_ref, win_ref, gl_ref, gain_ref, gsel_ref, havg_ref, o_ref, ksa_ref, *, nb, tq, tk, n_sel,
                top_n):
    H = N_NSA
    qi = pl.program_id(1)
    start = qi * tq

    @pl.when(qi == 0)
    def _():
        key_lanes = lax.broadcasted_iota(jnp.int32, noh_ref.shape, 1) < HEAD_DIM
        for bb in range(nb):
            ksa_ref[bb] = jnp.where(key_lanes, sel_ref[bb], noh_ref[...])

    t_row = start + lax.broadcasted_iota(jnp.int32, (tq, 1), 0)
    t_all = jnp.concatenate([t_row] * H, axis=0)
    zeros64 = jnp.zeros((tq, HEAD_DIM), BF16)
    wlen = WINDOW + tq
    base = pl.multiple_of(jnp.maximum(start - WINDOW, 0), tq)
    n_pad = cmp_ref.shape[1]
    assert CMP_STRIDE == 16 and tq & (tq - 1) == 0
    last_cmp = (t_all - (CMP_LEN - 1)) >> 4
    row_in_block = lax.broadcasted_iota(jnp.int32, (H * tq, wlen), 0) & (tq - 1)
    win_lag = (start - base) - (lax.broadcasted_iota(jnp.int32, (H * tq, wlen), 1) - row_in_block)

    def prologue(bb):
        q = q_ref[bb]
        q_heads = [q[:, h * HEAD_DIM:(h + 1) * HEAD_DIM] for h in range(H)]
        q_pad = jnp.concatenate([jnp.concatenate([qh, zeros64], axis=1) for qh in q_heads], axis=0)

        win = win_ref[bb, pl.ds(base, wlen), :]
        s_w = _dot_nt(q_pad, win)
        s_w = jnp.where(lax.bitcast_convert_type(win_lag, jnp.uint32) < WINDOW, s_w, NEG)
        e_w = jnp.exp2(s_w - jnp.max(s_w, axis=-1, keepdims=True))
        win1 = jnp.where(lax.broadcasted_iota(jnp.int32, (wlen, LANES), 1) < HEAD_DIM, jnp.ones((wlen, LANES), BF16), win)
        o_w = jnp.dot(e_w.astype(BF16), win1, preferred_element_type=F32)

        cmp = cmp_ref[bb]
        cmp_hi = cmp.astype(BF16)
        cmp_lo = (cmp - cmp_hi.astype(F32)).astype(BF16)
        s_c = _dot_nt(q_pad, cmp_hi) + _dot_nt(q_pad, cmp_lo)
        n_id = lax.broadcasted_iota(jnp.int32, (H * tq, n_pad), 1)
        p_c = _softmax_rows(s_c, n_id <= last_cmp)
        o_c = jnp.dot(p_c.astype(BF16), cmp_hi, preferred_element_type=F32)

        p_sum = p_c[0:tq]
        for h in range(1, H):
            p_sum = p_sum + p_c[h * tq:(h + 1) * tq]
        p_hi = p_sum.astype(BF16)
        p_lo = (p_sum - p_hi.astype(F32)).astype(BF16)
        c2s_t = c2s_ref[...]
        imp = _dot_nt(c2s_t, p_hi) + _dot_nt(c2s_t, p_lo)
        t_lane = start + lax.broadcasted_iota(jnp.int32, (HEAD_DIM, tq), 1)
        s_id = lax.broadcasted_iota(jnp.int32, (HEAD_DIM, tq), 0)
        cur = t_lane >> 6
        forced = jnp.logical_or(s_id == 0, jnp.logical_or(s_id == cur, s_id == cur - 1))
        score = jnp.where(forced, FORCED_SCORE, imp)
        score = jnp.where(s_id * SEL_LEN <= t_lane, score, -1.0)
        return q_pad, o_c, o_w, [score[g * SUB:(g + 1) * SUB, :] for g in range(N_GROUP)]

    SUB = 8
    N_GROUP = HEAD_DIM // SUB
    stage = [prologue(bb) for bb in range(nb)]

    n_live = (((start + tq - 1) >> 6) >> 3) + 1
    sub_id = lax.broadcasted_iota(jnp.int32, (SUB, tq), 0)

    def rank_level(level, ranks):
        ranks = list(ranks)
        pairs = [(level, g) for g in range(level + 1)] + [(gc, level) for gc in range(level)]
        for bb in range(nb):
            groups = stage[bb][3]
            for gc, g in pairs:
                for k in range(SUB):
                    c = gc * SUB + k
                    if c >= n_sel:
                        continue
                    row = groups[gc][k:k + 1, :]
                    if g * SUB > c:
                        before = row >= groups[g]
                    elif g * SUB + SUB - 1 < c:
                        before = row > groups[g]
                    else:
                        before = jnp.logical_or(row > groups[g], jnp.logical_and(row == groups[g], sub_id > k))
                    ranks[bb * N_GROUP + g] = ranks[bb * N_GROUP + g] + before.astype(jnp.int32)
        return tuple(ranks)

    ranks = tuple(jnp.zeros((SUB, tq), jnp.int32) for _ in range(nb * N_GROUP))
    for level in range(N_GROUP):
        ranks = lax.cond(level < n_live, functools.partial(rank_level, level), lambda r: r, ranks)

    pro = []
    for bb in range(nb):
        q_pad, o_c, o_w, _ = stage[bb]
        not_sel = jnp.concatenate([jnp.where(r < top_n, 0.0, 1.0) for r in ranks[bb * N_GROUP:(bb + 1) * N_GROUP]], axis=0)
        ns = jnp.concatenate([jnp.zeros((HEAD_DIM, tq), F32), not_sel], axis=0).T.astype(BF16)
        pro.append((q_pad + jnp.concatenate([ns] * H, axis=0), o_c, o_w))

    def sel_tile(bb, c0, size, carry, diagonal):
        m, acc = carry
        s = _dot_nt(pro[bb][0], ksa_ref[bb, pl.ds(c0, size), :])
        if diagonal:
            r_id = lax.broadcasted_iota(jnp.int32, (H * tq, size), 0) & (tq - 1)
            c_id = lax.broadcasted_iota(jnp.int32, (H * tq, size), 1)
            s = jnp.where(c_id <= r_id, s, NEG)
        m_new = jnp.maximum(m, jnp.max(s, axis=-1, keepdims=True))
        pr = jnp.exp2(s - m_new)
        key_lanes = lax.broadcasted_iota(jnp.int32, (size, LANES), 1) < HEAD_DIM
        v1 = jnp.where(key_lanes, jnp.ones((size, LANES), BF16), sel_ref[bb, pl.ds(c0, size), :])
        acc = jnp.exp2(m - m_new) * acc + jnp.dot(pr.astype(BF16), v1, preferred_element_type=F32)
        return m_new, acc

    def all_rows(c0, size, carries, diagonal):
        return tuple(sel_tile(bb, c0, size, carries[bb], diagonal) for bb in range(nb))

    carries = tuple((jnp.full((H * tq, 1), NEG, F32), jnp.zeros((H * tq, LANES), F32)) for _ in range(nb))
    n_full = start // tk
    carries = lax.fori_loop(0, n_full, lambda t, c: all_rows(pl.multiple_of(t * tk, tk), tk, c, False), carries)
    pos = n_full * tk
    size = tk // 2
    while size >= tq:
        take = start - pos >= size
        carries = lax.cond(take, lambda c, pos=pos, size=size: all_rows(pl.multiple_of(pos, tq), size, c, False), lambda c: c, carries)
        pos = pos + jnp.where(take, size, 0)
        size //= 2
    carries = all_rows(pl.multiple_of(start, tq), tq, carries, True)

    first_half = lax.broadcasted_iota(jnp.int32, (tq, LANES), 1) < HEAD_DIM

    def place(acc, normalise):
        cols = []
        for h in range(0, H, 2):
            a0, a1 = acc[h * tq:(h + 1) * tq], acc[(h + 1) * tq:(h + 2) * tq]
            r0 = pltpu.roll(a0, HEAD_DIM, 1)
            if normalise:
                v0 = r0 * (1.0 / a0)
                v1 = a1 * (1.0 / pltpu.roll(a1, HEAD_DIM, 1))
            else:
                v0, v1 = r0, a1
            cols.append(jnp.where(first_half, v0, v1))
        return jnp.concatenate(cols, axis=1)

    for bb in range(nb):
        _, o_c, o_w = pro[bb]
        gates = 1.0 / (1.0 + jnp.exp(-gl_ref[bb]))
        g_hi = gates.astype(BF16)
        g_lo = (gates - g_hi.astype(F32)).astype(BF16)
        mix = None
        for c, y in enumerate((place(o_c, False), place(carries[bb][1], True), place(o_w, True))):
            g = jnp.dot(g_hi, gsel_ref[c], preferred_element_type=F32) + jnp.dot(g_lo, gsel_ref[c], preferred_element_type=F32)
            mix = g * y if mix is None else mix + g * y
        sq = mix * mix
        sq_hi = sq.astype(BF16)
        sq_lo = (sq - sq_hi.astype(F32)).astype(BF16)
        ms = jnp.dot(sq_hi, havg_ref[...], preferred_element_type=F32) + jnp.dot(sq_lo, havg_ref[...], preferred_element_type=F32)
        o_ref[bb] = (mix * lax.rsqrt(ms + RMS_EPS) * gain_ref[...]).astype(BF16)


def _nsa(main3, cmp, c2s, neg_onehot, small3, gain_row, nb, tq, tk):
    B, S, _ = main3.shape
    n_sel = S // SEL_LEN
    n_pad = cmp.shape[1]
    gsel = np.zeros((3, LANES, NSA_W), np.float32)
    for h in range(N_NSA):
        for c in range(3):
            gsel[c, SMALL_GATE + 3 * h + c, h * HEAD_DIM:(h + 1) * HEAD_DIM] = 1.0
    head_of = np.arange(NSA_W) // HEAD_DIM
    havg = (head_of[:, None] == head_of[None, :]).astype(np.float32) / HEAD_DIM
    return pl.pallas_call(
        functools.partial(_nsa_kernel, nb=nb, tq=tq, tk=tk, n_sel=n_sel, top_n=min(TOP_N, n_sel)),
        grid=(B // nb, S // tq),
        in_specs=[
            pl.BlockSpec((nb, tq, NSA_W), lambda b, i: (b, i, CB_NQ // 2)),
            pl.BlockSpec((nb, n_pad, LANES), lambda b, i: (b, 0, 0)),
            pl.BlockSpec((HEAD_DIM, n_pad), lambda b, i: (0, 0)),
            pl.BlockSpec((nb, S, LANES), lambda b, i: (b, 0, CB_SEL)),
            pl.BlockSpec((S, LANES), lambda b, i: (0, 0)),
            pl.BlockSpec((nb, S, LANES), lambda b, i: (b, 0, CB_WIN)),
            pl.BlockSpec((nb, tq, LANES), lambda b, i: (b, i, 0)),
            pl.BlockSpec((1, NSA_W), lambda b, i: (0, 0)),
            pl.BlockSpec((3, LANES, NSA_W), lambda b, i: (0, 0, 0)),
            pl.BlockSpec((NSA_W, NSA_W), lambda b, i: (0, 0)),
        ],
        out_specs=pl.BlockSpec((nb, tq, NSA_W), lambda b, i: (b, i, 0)),
        out_shape=jax.ShapeDtypeStruct((B, S, NSA_W), BF16),
        scratch_shapes=[pltpu.VMEM((nb, S, LANES), BF16)],
        compiler_params=_cparams(("parallel", "arbitrary")),
        name="nsa_attention",
    )(main3, cmp, c2s, main3, neg_onehot, main3, small3, gain_row, jnp.asarray(gsel, BF16), jnp.asarray(havg, BF16))


def _post_kernel(x_ref, yr_ref, yf_ref, yn_ref, wo_ref, g_ref, w1_ref, w2_ref, gf_ref, o_ref, *, ff_chunk, final):
    x = x_ref[...]
    x = x + jnp.dot(yr_ref[...], wo_ref[0:RET_W, :], preferred_element_type=F32)
    x = x + jnp.dot(yf_ref[...], wo_ref[RET_W:RET_W + FOX_W, :], preferred_element_type=F32)
    x = x + jnp.dot(yn_ref[...], wo_ref[RET_W + FOX_W:, :], preferred_element_type=F32)
    h = _rms(x, g_ref[...]).astype(BF16)
    o_ref[...] = x
    for c0 in range(0, D_FF, ff_chunk):
        hid = jnp.maximum(jnp.dot(h, w1_ref[:, c0:c0 + ff_chunk], preferred_element_type=F32), 0.0)
        o_ref[...] += jnp.dot((hid * hid).astype(BF16), w2_ref[c0:c0 + ff_chunk, :], preferred_element_type=F32)
    if final:
        o_ref[...] = _rms(o_ref[...], gf_ref[...])


def _post(x2d, yr, yf, yn, wo, g, w1, w2, gf, tm, final):
    T = x2d.shape[0]
    row = lambda w: pl.BlockSpec((tm, w), lambda i: (i, 0))
    return pl.pallas_call(
        functools.partial(_post_kernel, ff_chunk=512, final=final),
        grid=(T // tm,),
        in_specs=[
            row(D_MODEL), row(RET_W), row(FOX_W), row(NSA_W),
            _resident((D_MODEL, D_MODEL), lambda i: (0, 0)),
            _resident((1, D_MODEL), lambda i: (0, 0)),
            _resident((D_MODEL, D_FF), lambda i: (0, 0)),
            _resident((D_FF, D_MODEL), lambda i: (0, 0)),
            _resident((1, D_MODEL), lambda i: (0, 0)),
        ],
        out_specs=row(D_MODEL),
        out_shape=jax.ShapeDtypeStruct((T, D_MODEL), F32),
        compiler_params=_cparams(("parallel",)),
        name="outproj_mlp",
    )(x2d, yr, yf, yn, wo, g, w1, w2, gf)


def _tile_sizes(batch):
    nb = 4 if batch % 4 == 0 else (2 if batch % 2 == 0 else 1)
    return dict(tm=1024, fox_nb=nb, fox_tq=512, fox_tk=512, nsa_nb=nb, nsa_tq=256, nsa_tk=1024)


def kernel(x, norm_attn, w_in, fox_forget_bias, ret_norm_gain, fox_norm_gain, nsa_norm_gain, nsa_cmp_pos_k, nsa_cmp_pos_v, nsa_cmp_w1_k, nsa_cmp_w2_k, nsa_cmp_w1_v, nsa_cmp_w2_v, w_out, norm_mlp, w_mlp_in, w_mlp_out, norm_final):
    B, S, D = x.shape
    depth = w_in.shape[0]
    ts = _tile_sizes(B)
    assert D == D_MODEL and S % ts["fox_tq"] == 0 and S >= WINDOW + ts["nsa_tq"] and (B * S) % ts["tm"] == 0
    T = B * S

    wm, ws = _in_proj_weights(w_in)
    wo = w_out.astype(BF16)
    w1 = w_mlp_in.astype(BF16)
    w2 = w_mlp_out.astype(BF16)

    def expand_w1(wk, wv):
        L = wk.shape[0]
        wk = wk.astype(BF16).reshape(L, CMP_LEN, HEAD_DIM, CMP_HIDDEN)
        wv = wv.astype(BF16).reshape(L, CMP_LEN, HEAD_DIM, CMP_HIDDEN)
        z = jnp.zeros_like(wk)
        full = jnp.concatenate([jnp.concatenate([wk, z], axis=-1), jnp.concatenate([z, wv], axis=-1)], axis=2)
        full = full.reshape(L, CMP_LEN * LANES, 2 * CMP_HIDDEN)
        return full[:, :CMP_STRIDE * LANES], full[:, CMP_STRIDE * LANES:]

    wt_all, wb_all = expand_w1(nsa_cmp_w1_k, nsa_cmp_w1_v)
    zk = jnp.zeros_like(nsa_cmp_w2_k)
    w2c = jnp.concatenate([jnp.concatenate([nsa_cmp_w2_k, zk], axis=-1),
                           jnp.concatenate([zk, nsa_cmp_w2_v], axis=-1)], axis=1).astype(BF16)
    pos = jnp.concatenate([nsa_cmp_pos_k, nsa_cmp_pos_v], axis=-1)
    pos_tb = pos.reshape(depth, 2, 1, CMP_STRIDE * LANES)
    pos_tb = jnp.broadcast_to(pos_tb, (depth, 2, 8, CMP_STRIDE * LANES)).reshape(depth, 16, CMP_STRIDE * LANES).astype(BF16)

    n_cmp = (S - CMP_LEN) // CMP_STRIDE + 1
    n_sel = S // SEL_LEN
    n_rows = S // CMP_STRIDE
    cs = np.arange(n_rows) * CMP_STRIDE
    ss = np.arange(n_sel) * SEL_LEN
    overlap = np.clip(np.minimum(cs[:, None] + CMP_LEN, ss[None, :] + SEL_LEN) - np.maximum(cs[:, None], ss[None, :]), 0, None)
    overlap[n_cmp:] = 0
    c2s_np = np.zeros((HEAD_DIM, n_rows), np.float32)
    c2s_np[:n_sel] = (overlap / CMP_LEN).T
    c2s = jnp.asarray(c2s_np, dtype=BF16)
    assert n_sel <= HEAD_DIM, "the selected-branch key augmentation has 64 lanes, one per selection block"
    neg_onehot = jnp.asarray(np.where((np.arange(S)[:, None] // SEL_LEN) == np.arange(LANES)[None, :] - HEAD_DIM, NEG, 0.0), dtype=BF16)

    tables = _retention_tables(S)
    fbias = jnp.zeros((depth, 1, LANES), F32).at[:, 0, SMALL_FF:SMALL_FF + N_FOX].set(fox_forget_bias)

    x2d = x.reshape(T, D)
    for l in range(depth):
        main, small = _inproj(x2d, norm_attn[l][None, :], wm[l], ws[l], ts["tm"])
        main3 = main.reshape(B, S, N_MAIN)
        small3 = small.reshape(B, S, LANES)
        fox_qa, fox_ka, edges = _forget_cumsum(small3, fbias[l])
        y_ret = _retention(main3, tables, ret_norm_gain[l][None, :])
        per_tile = ts["fox_tk"] // RET_CHUNK
        cq = edges[:, 0::per_tile, 0, SMALL_FF:SMALL_FF + N_FOX]
        ck = edges[:, per_tile - 1::per_tile, 1, SMALL_FF:SMALL_FF + N_FOX]
        cum_edges = jnp.stack([cq.transpose(0, 2, 1), ck.transpose(0, 2, 1)], axis=2).reshape(-1)
        y_fox = _fox(main3, fox_qa, fox_ka, cum_edges, fox_norm_gain[l][None, :], ts["fox_nb"], ts["fox_tq"], ts["fox_tk"])
        r = main3[:, :, CB_CMP * LANES:(CB_CMP + 1) * LANES].reshape(B, n_rows, CMP_STRIDE * LANES)
        cmp = _nsa_compress(r, pos_tb[l], wt_all[l], wb_all[l], w2c[l])
        y_nsa = _nsa(main3, cmp, c2s, neg_onehot, small3, nsa_norm_gain[l][None, :], ts["nsa_nb"], ts["nsa_tq"], ts["nsa_tk"])
        x2d = _post(x2d, y_ret.reshape(T, RET_W), y_fox.reshape(T, FOX_W), y_nsa.reshape(T, NSA_W), wo[l],
                    norm_mlp[l][None, :], w1[l], w2[l], norm_final[None, :], ts["tm"], final=(l == depth - 1))
    return x2d.reshape(B, S, D)
```

```python
import functools

import numpy as np
import jax
import jax.numpy as jnp
from jax import lax
from jax.experimental import pallas as pl
from jax.experimental.pallas import tpu as pltpu

F32 = jnp.float32
BF16 = jnp.bfloat16

D_MODEL = 1024
HEAD_DIM = 64
N_RET = 6
N_FOX = 6
N_NSA = 4
RET_W = N_RET * HEAD_DIM
FOX_W = N_FOX * HEAD_DIM
NSA_W = N_NSA * HEAD_DIM
D_FF = 4 * D_MODEL
RMS_EPS = 1e-6
RET_CHUNK = 128
RET_BLOCK = 256
ROPE_BASE = 10000.0
CMP_LEN = 32
CMP_STRIDE = 16
CMP_HIDDEN = 4 * HEAD_DIM
SEL_LEN = 64
TOP_N = 16
WINDOW = 512
FORCED_SCORE = 1e4
NEG = -1e30
LOG2E = float(np.log2(np.e))

LANES = 128
N_PAIR = N_RET // 2

CB_RQ, CB_RK, CB_RV, CB_RG = 0, 3, 6, 9
CB_NQ = 12
CB_FQ, CB_FK, CB_FV = 14, 17, 20
CB_CMP, CB_SEL, CB_WIN = 23, 24, 25
N_MAIN = 26 * LANES
SMALL_FF = 0
SMALL_GATE = 8

VMEM_LIMIT = 56 * 1024 * 1024


def _cparams(sem):
    return pltpu.CompilerParams(dimension_semantics=sem, vmem_limit_bytes=VMEM_LIMIT)


def _resident(shape, index_map):
    return pl.BlockSpec(shape, index_map, pipeline_mode=pl.Buffered(1))


def _in_proj_weights(w_in):
    sizes = (RET_W, RET_W, RET_W, RET_W, FOX_W, FOX_W, FOX_W, N_FOX, NSA_W) + (HEAD_DIM,) * 6 + (3 * N_NSA,)
    off = np.concatenate([[0], np.cumsum(sizes)])
    (o_rq, o_rk, o_rv, o_rg, o_fq, o_fk, o_fv, o_ff, o_nq, o_kc, o_vc, o_ks, o_vs, o_kw, o_vw, o_gt) = [int(o) for o in off[:-1]]
    L, D, _ = w_in.shape
    half = HEAD_DIM // 2
    cut = lambda o, n: w_in[:, :, o:o + n]

    def interleave(w):
        return w.reshape(L, D, N_PAIR, 2, 2, half).transpose(0, 1, 2, 4, 3, 5).reshape(L, D, RET_W)

    qk_scale = HEAD_DIM ** -0.5
    sm_scale = qk_scale * LOG2E
    main = jnp.concatenate([
        interleave(cut(o_rq, RET_W)), interleave(cut(o_rk, RET_W)) * qk_scale, cut(o_rv, RET_W), cut(o_rg, RET_W),
        cut(o_nq, NSA_W) * sm_scale, cut(o_fq, FOX_W) * sm_scale, cut(o_fk, FOX_W), cut(o_fv, FOX_W),
        cut(o_kc, 6 * HEAD_DIM),
    ], axis=-1)
    assert main.shape[-1] == N_MAIN and o_vw == o_kc + 5 * HEAD_DIM
    zeros = lambda n: jnp.zeros((L, D, n), w_in.dtype)
    small = jnp.concatenate([
        zeros(SMALL_FF), cut(o_ff, N_FOX), zeros(SMALL_GATE - SMALL_FF - N_FOX),
        cut(o_gt, 3 * N_NSA), zeros(LANES - SMALL_GATE - 3 * N_NSA)], axis=-1)
    return main.astype(BF16), small.astype(BF16)


def _retention_tables(seq):
    half = HEAD_DIM // 2
    inv = 1.0 / (ROPE_BASE ** (jnp.arange(half, dtype=F32) / half))
    ang = jnp.arange(seq, dtype=F32)[:, None] * inv[None, :]
    cos, sin = jnp.cos(ang), jnp.sin(ang)
    cos_t = jnp.concatenate([cos, cos, cos, cos], axis=-1)
    sin_t = jnp.concatenate([-sin, -sin, sin, sin], axis=-1)
    log_gamma = np.log(1.0 - 2.0 ** (-5.0 - np.arange(N_RET, dtype=np.float32))).astype(np.float32)
    C = RET_BLOCK
    idx = np.arange(C, dtype=np.float32)
    diff = idx[:, None] - idx[None, :]
    lane = np.arange(LANES)
    head_k = (lane % HEAD_DIM) // half
    head_v = lane // HEAD_DIM
    d_in = np.zeros((N_PAIR, 2, C, C), np.float32)
    d_k = np.zeros((N_PAIR, C, LANES), np.float32)
    d_q = np.zeros((N_PAIR, C, LANES), np.float32)
    d_c = np.zeros((N_PAIR, LANES, LANES), np.float32)
    for p in range(N_PAIR):
        lg = log_gamma[2 * p:2 * p + 2]
        for j in range(2):
            d_in[p, j] = np.where(diff >= 0, np.exp(lg[j] * np.maximum(diff, 0.0)), 0.0)
        d_k[p] = np.exp(lg[head_k][None, :] * (C - 1.0 - idx)[:, None])
        d_q[p] = np.exp(lg[head_v][None, :] * (idx + 1.0)[:, None])
        d_c[p] = np.broadcast_to(np.exp(lg[head_v] * C)[None, :], (LANES, LANES))
    s_mask = (head_k[:, None] == head_v[None, :]).astype(np.float32)
    return cos_t, sin_t, jnp.asarray(d_in), jnp.asarray(d_k), jnp.asarray(d_q), jnp.asarray(d_c), jnp.asarray(s_mask)


def _rms(x, g):
    return x * lax.rsqrt(jnp.mean(x * x, axis=-1, keepdims=True) + RMS_EPS) * g


def _inproj_kernel(x_ref, g_ref, wm_ref, ws_ref, om_ref, os_ref, *, n_chunk):
    h = _rms(x_ref[...], g_ref[...]).astype(BF16)
    for n0 in range(0, N_MAIN, n_chunk):
        n1 = min(n0 + n_chunk, N_MAIN)
        om_ref[:, n0:n1] = jnp.dot(h, wm_ref[:, n0:n1], preferred_element_type=F32).astype(BF16)
    os_ref[...] = jnp.dot(h, ws_ref[...], preferred_element_type=F32)


def _inproj(x2d, g, wm, ws, tm):
    T = x2d.shape[0]
    return pl.pallas_call(
        functools.partial(_inproj_kernel, n_chunk=512),
        grid=(T // tm,),
        in_specs=[
            pl.BlockSpec((tm, D_MODEL), lambda i: (i, 0)),
            _resident((1, D_MODEL), lambda i: (0, 0)),
            _resident((D_MODEL, N_MAIN), lambda i: (0, 0)),
            _resident((D_MODEL, LANES), lambda i: (0, 0)),
        ],
        out_specs=[
            pl.BlockSpec((tm, N_MAIN), lambda i: (i, 0)),
            pl.BlockSpec((tm, LANES), lambda i: (i, 0)),
        ],
        out_shape=[jax.ShapeDtypeStruct((T, N_MAIN), BF16), jax.ShapeDtypeStruct((T, LANES), F32)],
        compiler_params=_cparams(("parallel",)),
        name="inproj",
    )(x2d, g, wm, ws)


N_SPLIT = 3
AUG_STRIDE = 2 * N_SPLIT


def _split_bf16(x):
    terms, rest = [], x
    for _ in range(N_SPLIT):
        t = rest.astype(BF16)
        rest = rest - t.astype(F32)
        terms.append(t)
    return terms


def _bias_placement():
    mq = np.zeros((N_SPLIT, LANES, LANES), np.float32)
    mk = np.zeros((N_SPLIT, LANES, LANES), np.float32)
    one_q = np.zeros((1, LANES), np.float32)
    one_k = np.zeros((1, LANES), np.float32)
    for h in range(N_FOX):
        base = h * AUG_STRIDE
        for i in range(N_SPLIT):
            mq[i, SMALL_FF + h, base + i] = 1.0
            mk[i, SMALL_FF + h, base + N_SPLIT + i] = -1.0
            one_q[0, base + N_SPLIT + i] = 1.0
            one_k[0, base + i] = 1.0
    return jnp.asarray(mq, BF16), jnp.asarray(mk, BF16), jnp.asarray(one_q), jnp.asarray(one_k)


def _cum_kernel(z_ref, b_ref, tri_ref, mq_ref, mk_ref, oq_ref, ok_ref, qa_ref, ka_ref, edge_ref, *, n_chunks):
    C = RET_CHUNK

    def chunk(c, carry):
        r0 = pl.multiple_of(c * C, C)
        z = z_ref[0, pl.ds(r0, C), :] + b_ref[...]
        lf = jnp.minimum(z, 0.0) - jnp.log1p(jnp.exp(-jnp.abs(z)))
        terms = _split_bf16(lf)
        local = jnp.dot(tri_ref[...], terms[0], preferred_element_type=F32)
        for term in terms[1:]:
            local = local + jnp.dot(tri_ref[...], term, preferred_element_type=F32)
        cs = local + carry
        cs2 = cs * LOG2E
        edge_ref[0, pl.ds(c, 1)] = jnp.concatenate([cs2[0:1], cs2[C - 1:C], jnp.zeros((6, LANES), F32)], axis=0)[None]
        qa, ka = oq_ref[...], ok_ref[...]
        for i, term in enumerate(_split_bf16(cs2)):
            qa = qa + jnp.dot(term, mq_ref[i], preferred_element_type=F32)
            ka = ka + jnp.dot(term, mk_ref[i], preferred_element_type=F32)
        qa_ref[0, pl.ds(r0, C), :] = qa.astype(BF16)
        ka_ref[0, pl.ds(r0, C), :] = ka.astype(BF16)
        return cs[C - 1:C, :]

    unroll = max(u for u in (1, 2, 4) if n_chunks % u == 0)

    def body(i, carry):
        for u in range(unroll):
            carry = chunk(i * unroll + u, carry)
        return carry

    lax.fori_loop(0, n_chunks // unroll, body, jnp.zeros((1, LANES), F32))


def _forget_cumsum(small3, bias_row):
    B, S, _ = small3.shape
    tri = jnp.asarray(np.tril(np.ones((RET_CHUNK, RET_CHUNK), np.float32)), BF16)
    mq, mk, one_q, one_k = _bias_placement()
    const = lambda shape: pl.BlockSpec(shape, lambda b: (0,) * len(shape))
    return pl.pallas_call(
        functools.partial(_cum_kernel, n_chunks=S // RET_CHUNK),
        grid=(B,),
        in_specs=[
            pl.BlockSpec((1, S, LANES), lambda b: (b, 0, 0)),
            const((1, LANES)), const((RET_CHUNK, RET_CHUNK)),
            const((N_SPLIT, LANES, LANES)), const((N_SPLIT, LANES, LANES)), const((1, LANES)), const((1, LANES)),
        ],
        out_specs=[
            pl.BlockSpec((1, S, LANES), lambda b: (b, 0, 0)),
            pl.BlockSpec((1, S, LANES), lambda b: (b, 0, 0)),
            pl.BlockSpec((1, S // RET_CHUNK, 8, LANES), lambda b: (b, 0, 0, 0)),
        ],
        out_shape=[jax.ShapeDtypeStruct((B, S, LANES), BF16), jax.ShapeDtypeStruct((B, S, LANES), BF16),
                   jax.ShapeDtypeStruct((B, S // RET_CHUNK, 8, LANES), F32)],
        compiler_params=_cparams(("parallel",)),
        name="forget_cumsum",
    )(small3, bias_row, tri, mq, mk, one_q, one_k)


def _dot_nt(a, b):
    return lax.dot_general(a, b, (((1,), (1,)), ((), ())), preferred_element_type=F32)


def _ret_kernel(q_ref, k_ref, v_ref, g_ref, cos_ref, sin_ref, din_ref, dk_ref, dq_ref, dc_ref, sm_ref, gain_ref,
                o_ref, state_ref, *, n_chunks, unroll):
    C = RET_BLOCK
    half = HEAD_DIM // 2
    lane = lax.broadcasted_iota(jnp.int32, (C, LANES), 1)
    khead = (lane % HEAD_DIM) // half
    vhead0 = lane < HEAD_DIM
    state_ref[...] = jnp.zeros_like(state_ref)

    def pair_chunk(p, r0):
        rows = pl.ds(r0, C)
        lanes = slice(p * LANES, (p + 1) * LANES)
        cs, sn = cos_ref[rows, :], sin_ref[rows, :]
        q = q_ref[0, rows, lanes].astype(F32)
        k = k_ref[0, rows, lanes].astype(F32)
        q = q * cs + pltpu.roll(q, HEAD_DIM, 1) * sn
        k = k * cs + pltpu.roll(k, HEAD_DIM, 1) * sn
        v = v_ref[0, rows, lanes]
        qb, kb = q.astype(BF16), k.astype(BF16)
        state = state_ref[p]
        cross = jnp.dot(qb, state.astype(BF16), preferred_element_type=F32) * dq_ref[p]
        outs = []
        for j in range(2):
            qm = jnp.where(khead == j, qb, jnp.zeros_like(qb))
            inner = _dot_nt(qm, kb) * din_ref[p, j]
            outs.append(jnp.dot(inner.astype(BF16), v, preferred_element_type=F32))
        out = jnp.where(vhead0, outs[0], outs[1]) + cross
        kd = (k * dk_ref[p]).T.astype(BF16)
        state_ref[p] = state * dc_ref[p] + jnp.dot(kd, v, preferred_element_type=F32) * sm_ref[...]
        inv = 1.0 / HEAD_DIM
        s0 = jnp.sum(jnp.where(vhead0, out, 0.0), axis=-1, keepdims=True)
        s1 = jnp.sum(jnp.where(vhead0, 0.0, out), axis=-1, keepdims=True)
        yc = out - jnp.where(vhead0, s0, s1) * inv
        yc2 = yc * yc
        v0 = jnp.sum(jnp.where(vhead0, yc2, 0.0), axis=-1, keepdims=True)
        v1 = jnp.sum(jnp.where(vhead0, 0.0, yc2), axis=-1, keepdims=True)
        y = yc * lax.rsqrt(jnp.where(vhead0, v0, v1) * inv + RMS_EPS) * gain_ref[:, lanes]
        g = g_ref[0, rows, lanes].astype(F32)
        o_ref[0, rows, lanes] = (y * (g * (1.0 / (1.0 + jnp.exp(-g))))).astype(BF16)

    def body(c, _):
        for u in range(unroll):
            for p in range(N_PAIR):
                pair_chunk(p, pl.multiple_of((c * unroll + u) * C, C))
        return 0

    lax.fori_loop(0, n_chunks // unroll, body, 0)


def _retention(main3, tables, gain_row):
    B, S, _ = main3.shape
    cos_t, sin_t, d_in, d_k, d_q, d_c, s_mask = tables
    C = RET_BLOCK
    n_chunks = S // C
    seq_spec = lambda cb: pl.BlockSpec((1, S, RET_W), lambda b: (b, 0, cb // N_PAIR))
    const = lambda shape: pl.BlockSpec(shape, lambda b: (0,) * len(shape))
    return pl.pallas_call(
        functools.partial(_ret_kernel, n_chunks=n_chunks, unroll=2 if n_chunks % 2 == 0 else 1),
        grid=(B,),
        in_specs=[
            seq_spec(CB_RQ), seq_spec(CB_RK), seq_spec(CB_RV), seq_spec(CB_RG),
            const((S, LANES)), const((S, LANES)),
            const((N_PAIR, 2, C, C)), const((N_PAIR, C, LANES)), const((N_PAIR, C, LANES)),
            const((N_PAIR, LANES, LANES)), const((LANES, LANES)), const((1, RET_W)),
        ],
        out_specs=pl.BlockSpec((1, S, RET_W), lambda b: (b, 0, 0)),
        out_shape=jax.ShapeDtypeStruct((B, S, RET_W), BF16),
        scratch_shapes=[pltpu.VMEM((N_PAIR, LANES, LANES), F32)],
        compiler_params=_cparams(("parallel",)),
        name="retention",
    )(main3, main3, main3, main3, cos_t, sin_t, d_in, d_k, d_q, d_c, s_mask, gain_row)


UNDERFLOW_LOG2 = -170.0


def _head_sq_norm_max(x, havg, head0):
    xf = x.astype(F32)
    mean = jnp.dot((xf * xf).astype(BF16), havg, preferred_element_type=F32)
    bound = mean * (HEAD_DIM * 1.01)
    return jnp.max(jnp.where(head0, bound, 0.0)), jnp.max(jnp.where(head0, 0.0, bound))


def _fox_kernel(cb_ref, q_ref, qa_ref, qs_ref, kh_ref, ka_ref, v_ref, gain_ref, havg_ref, o_ref, k_ref, ft_ref, *, nb, tq, tk, n_tiles):
    p = pl.program_id(1)
    qi = pl.program_id(2)
    lane_q = lax.broadcasted_iota(jnp.int32, (tq, LANES), 1)
    head0 = lane_q < HEAD_DIM

    @pl.when(qi == 0)
    def _():
        head0_s = lax.broadcasted_iota(jnp.int32, kh_ref.shape[1:], 1) < HEAD_DIM
        for bb in range(nb):
            k_ref[bb, :, 0:LANES] = kh_ref[bb]
            k_ref[bb, :, LANES:2 * LANES] = ka_ref[bb]
            b = pl.program_id(0) * nb + bb
            kn = _head_sq_norm_max(kh_ref[bb], havg_ref[...], head0_s)
            for i in range(n_tiles):
                qn = _head_sq_norm_max(qs_ref[bb, i * tq:(i + 1) * tq, :], havg_ref[...], head0)
                for j in range(2):
                    h = 2 * p + j
                    slack = 2.0 * jnp.sqrt(qn[j] * kn[j])
                    cq = cb_ref[((b * N_FOX + h) * 2 + 0) * n_tiles + i]
                    skipped = jnp.int32(0)
                    for t in range(i):
                        ck = cb_ref[((b * N_FOX + h) * 2 + 1) * n_tiles + t]
                        skipped = skipped + (slack + cq - ck < UNDERFLOW_LOG2).astype(jnp.int32)
                    ft_ref[(bb * n_tiles + i) * 2 + j] = skipped

    q_pos = qi * tq + lax.broadcasted_iota(jnp.int32, (tq, tk), 0)
    k_off = lax.broadcasted_iota(jnp.int32, (tq, tk), 1)
    chains = [(bb, j) for bb in range(nb) for j in range(2)]

    qops = {}
    for bb in range(nb):
        q = q_ref[bb]
        qa = qa_ref[bb]
        for j in range(2):
            qm = jnp.where(head0 if j == 0 else jnp.logical_not(head0), q, jnp.zeros_like(q))
            lo = (2 * p + j) * AUG_STRIDE
            own = jnp.logical_and(lane_q >= lo, lane_q < lo + AUG_STRIDE)
            qops[bb, j] = jnp.concatenate([qm, jnp.where(own, qa, jnp.zeros_like(qa))], axis=1)

    def tile(ch, t, n, carry, masked):
        bb, j = ch
        m, acc = carry
        c0 = pl.multiple_of(t * tk, tk)
        s = _dot_nt(qops[ch], k_ref[bb, pl.ds(c0, n * tk), :])
        if masked:
            s = jnp.where(k_off + c0 <= q_pos, s, NEG)
        m_new = jnp.maximum(m, jnp.max(s, axis=-1, keepdims=True))
        pr = jnp.exp2((s - m_new).astype(BF16))
        own = lax.broadcasted_iota(jnp.int32, (n * tk, LANES), 1) < HEAD_DIM
        vj = jnp.where(own if j == 0 else jnp.logical_not(own), v_ref[bb, pl.ds(c0, n * tk), :], jnp.ones((n * tk, LANES), BF16))
        acc = jnp.exp2(m - m_new) * acc + jnp.dot(pr.astype(BF16), vj, preferred_element_type=F32)
        return m_new, acc

    carry = {}
    for ch in chains:
        bb, j = ch
        t0 = ft_ref[(bb * n_tiles + qi) * 2 + j]
        n_act = qi - t0
        c = (jnp.full((tq, 1), NEG, F32), jnp.zeros((tq, LANES), F32))
        c = lax.fori_loop(0, n_act // 2, lambda i, c, ch=ch, t0=t0: tile(ch, t0 + 2 * i, 2, c, False), c)
        carry[ch] = lax.cond(n_act % 2 == 1, lambda c, ch=ch: tile(ch, qi - 1, 1, c, False), lambda c: c, c)
    for ch in chains:
        carry[ch] = tile(ch, qi, 1, carry[ch], True)
    for bb in range(nb):
        a0, a1 = carry[bb, 0][1], carry[bb, 1][1]
        out = jnp.where(head0, a0 * (1.0 / pltpu.roll(a0, HEAD_DIM, 1)), a1 * (1.0 / pltpu.roll(a1, HEAD_DIM, 1)))
        sq = out * out
        sq_hi = sq.astype(BF16)
        sq_lo = (sq - sq_hi.astype(F32)).astype(BF16)
        ms = jnp.dot(sq_hi, havg_ref[...], preferred_element_type=F32) + jnp.dot(sq_lo, havg_ref[...], preferred_element_type=F32)
        o_ref[bb] = (out * lax.rsqrt(ms + RMS_EPS) * gain_ref[...]).astype(BF16)


def _fox(main3, q_aug, k_aug, cum_edges, gain_row, nb, tq, tk):
    B, S, _ = main3.shape
    assert tq == tk, "the kernel handles exactly one diagonal tile per query block"
    head_of = np.arange(LANES) // HEAD_DIM
    havg = (head_of[:, None] == head_of[None, :]).astype(np.float32) / HEAD_DIM
    grid_spec = pltpu.PrefetchScalarGridSpec(
        num_scalar_prefetch=1,
        grid=(B // nb, N_PAIR, S // tq),
        in_specs=[
            pl.BlockSpec((nb, tq, LANES), lambda b, p, i, cb: (b, i, CB_FQ + p)),
            pl.BlockSpec((nb, tq, LANES), lambda b, p, i, cb: (b, i, 0)),
            pl.BlockSpec((nb, S, LANES), lambda b, p, i, cb: (b, 0, CB_FQ + p)),
            pl.BlockSpec((nb, S, LANES), lambda b, p, i, cb: (b, 0, CB_FK + p)),
            pl.BlockSpec((nb, S, LANES), lambda b, p, i, cb: (b, 0, 0)),
            pl.BlockSpec((nb, S, LANES), lambda b, p, i, cb: (b, 0, CB_FV + p)),
            pl.BlockSpec((1, LANES), lambda b, p, i, cb: (0, p)),
            pl.BlockSpec((LANES, LANES), lambda b, p, i, cb: (0, 0)),
        ],
        out_specs=pl.BlockSpec((nb, tq, LANES), lambda b, p, i, cb: (b, i, p)),
        scratch_shapes=[pltpu.VMEM((nb, S, 2 * LANES), BF16), pltpu.SMEM((nb * 2 * (S // tk),), jnp.int32)],
    )
    return pl.pallas_call(
        functools.partial(_fox_kernel, nb=nb, tq=tq, tk=tk, n_tiles=S // tk),
        grid_spec=grid_spec,
        out_shape=jax.ShapeDtypeStruct((B, S, FOX_W), BF16),
        compiler_params=_cparams(("parallel", "parallel", "arbitrary")),
        name="fox_attention",
    )(cum_edges, main3, q_aug, main3, main3, k_aug, main3, gain_row, jnp.asarray(havg, BF16))


def _gelu_tanh(x):
    return 0.5 * x * (1.0 + jnp.tanh(np.sqrt(2.0 / np.pi).astype(np.float32) * (x + 0.044715 * (x * x * x))))


def _cmp_kernel(r_ref, pos_ref, wt_ref, wb_ref, w2_ref, o_ref):
    r = r_ref[0]
    n = r.shape[0]
    top = jnp.dot(r, wt_ref[...], preferred_element_type=F32)
    bot = jnp.dot(r, wb_ref[...], preferred_element_type=F32)
    cpos = jnp.dot(pos_ref[0:8, :], wt_ref[...], preferred_element_type=F32)
    cpos = cpos + jnp.dot(pos_ref[8:16, :], wb_ref[...], preferred_element_type=F32)
    hid = _gelu_tanh(top + pltpu.roll(bot, n - 1, 0) + cpos[0:1, :])
    o_ref[0] = jnp.dot(hid.astype(BF16), w2_ref[...], preferred_element_type=F32)


def _nsa_compress(r, pos2, wt, wb, w2):
    B, n, _ = r.shape
    width = CMP_STRIDE * LANES
    return pl.pallas_call(
        _cmp_kernel,
        grid=(B,),
        in_specs=[
            pl.BlockSpec((1, n, width), lambda b: (b, 0, 0)),
            pl.BlockSpec((16, width), lambda b: (0, 0)),
            pl.BlockSpec((width, 2 * CMP_HIDDEN), lambda b: (0, 0)),
            pl.BlockSpec((width, 2 * CMP_HIDDEN), lambda b: (0, 0)),
            pl.BlockSpec((2 * CMP_HIDDEN, LANES), lambda b: (0, 0)),
        ],
        out_specs=pl.BlockSpec((1, n, LANES), lambda b: (b, 0, 0)),
        out_shape=jax.ShapeDtypeStruct((B, n, LANES), F32),
        compiler_params=_cparams(("parallel",)),
        name="nsa_compress",
    )(r, pos2, wt, wb, w2)


def _softmax_rows(s, mask):
    sm = jnp.where(mask, s, NEG)
    m = jnp.maximum(jnp.max(sm, axis=-1, keepdims=True), 0.1 * NEG)
    e = jnp.exp2(sm - m)
    l = jnp.sum(e, axis=-1, keepdims=True)
    return e / jnp.where(l > 0.0, l, 1.0)


def _nsa_kernel(q_ref, cmp_ref, c2s_ref, sel_ref, noh_ref, win_ref, gl_ref, gain_ref, gsel_ref, havg_ref, o_ref, ksa_ref, *, nb, tq, tk, n_sel,
                top_n):
    H = N_NSA
    qi = pl.program_id(1)
    start = qi * tq

    @pl.when(qi == 0)
    def _():
        key_lanes = lax.broadcasted_iota(jnp.int32, noh_ref.shape, 1) < HEAD_DIM
        for bb in range(nb):
            ksa_ref[bb] = jnp.where(key_lanes, sel_ref[bb], noh_ref[...])

    t_row = start + lax.broadcasted_iota(jnp.int32, (tq, 1), 0)
    t_all = jnp.concatenate([t_row] * H, axis=0)
    zeros64 = jnp.zeros((tq, HEAD_DIM), BF16)
    wlen = WINDOW + tq
    base = pl.multiple_of(jnp.maximum(start - WINDOW, 0), tq)
    n_pad = cmp_ref.shape[1]
    assert CMP_STRIDE == 16 and tq & (tq - 1) == 0
    last_cmp = (t_all - (CMP_LEN - 1)) >> 4
    row_in_block = lax.broadcasted_iota(jnp.int32, (H * tq, wlen), 0) & (tq - 1)
    win_lag = (start - base) - (lax.broadcasted_iota(jnp.int32, (H * tq, wlen), 1) - row_in_block)

    def prologue(bb):
        q = q_ref[bb]
        q_heads = [q[:, h * HEAD_DIM:(h + 1) * HEAD_DIM] for h in range(H)]
        q_pad = jnp.concatenate([jnp.concatenate([qh, zeros64], axis=1) for qh in q_heads], axis=0)

        win = win_ref[bb, pl.ds(base, wlen), :]
        s_w = _dot_nt(q_pad, win)
        s_w = jnp.where(lax.bitcast_convert_type(win_lag, jnp.uint32) < WINDOW, s_w, NEG)
        e_w = jnp.exp2(s_w - jnp.max(s_w, axis=-1, keepdims=True))
        win1 = jnp.where(lax.broadcasted_iota(jnp.int32, (wlen, LANES), 1) < HEAD_DIM, jnp.ones((wlen, LANES), BF16), win)
        o_w = jnp.dot(e_w.astype(BF16), win1, preferred_element_type=F32)

        cmp = cmp_ref[bb]
        cmp_hi = cmp.astype(BF16)
        cmp_lo = (cmp - cmp_hi.astype(F32)).astype(BF16)
        s_c = _dot_nt(q_pad, cmp_hi) + _dot_nt(q_pad, cmp_lo)
        n_id = lax.broadcasted_iota(jnp.int32, (H * tq, n_pad), 1)
        p_c = _softmax_rows(s_c, n_id <= last_cmp)
        o_c = jnp.dot(p_c.astype(BF16), cmp_hi, preferred_element_type=F32)

        p_sum = p_c[0:tq]
        for h in range(1, H):
            p_sum = p_sum + p_c[h * tq:(h + 1) * tq]
        p_hi = p_sum.astype(BF16)
        p_lo = (p_sum - p_hi.astype(F32)).astype(BF16)
        c2s_t = c2s_ref[...]
        imp = _dot_nt(c2s_t, p_hi) + _dot_nt(c2s_t, p_lo)
        t_lane = start + lax.broadcasted_iota(jnp.int32, (HEAD_DIM, tq), 1)
        s_id = lax.broadcasted_iota(jnp.int32, (HEAD_DIM, tq), 0)
        cur = t_lane >> 6
        forced = jnp.logical_or(s_id == 0, jnp.logical_or(s_id == cur, s_id == cur - 1))
        score = jnp.where(forced, FORCED_SCORE, imp)
        score = jnp.where(s_id * SEL_LEN <= t_lane, score, -1.0)
        return q_pad, o_c, o_w, [score[g * SUB:(g + 1) * SUB, :] for g in range(N_GROUP)]

    SUB = 8
    N_GROUP = HEAD_DIM // SUB
    stage = [prologue(bb) for bb in range(nb)]

    n_live = (((start + tq - 1) >> 6) >> 3) + 1
    sub_id = lax.broadcasted_iota(jnp.int32, (SUB, tq), 0)

    def rank_level(level, ranks):
        ranks = list(ranks)
        pairs = [(level, g) for g in range(level + 1)] + [(gc, level) for gc in range(level)]
        for bb in range(nb):
            groups = stage[bb][3]
            for gc, g in pairs:
                for k in range(SUB):
                    c = gc * SUB + k
                    if c >= n_sel:
                        continue
                    row = groups[gc][k:k + 1, :]
                    if g * SUB > c:
                        before = row >= groups[g]
                    elif g * SUB + SUB - 1 < c:
                        before = row > groups[g]
                    else:
                        before = jnp.logical_or(row > groups[g], jnp.logical_and(row == groups[g], sub_id > k))
                    ranks[bb * N_GROUP + g] = ranks[bb * N_GROUP + g] + before.astype(jnp.int32)
        return tuple(ranks)

    ranks = tuple(jnp.zeros((SUB, tq), jnp.int32) for _ in range(nb * N_GROUP))
    for level in range(N_GROUP):
        ranks = lax.cond(level < n_live, functools.partial(rank_level, level), lambda r: r, ranks)

    pro = []
    for bb in range(nb):
        q_pad, o_c, o_w, _ = stage[bb]
        not_sel = jnp.concatenate([jnp.where(r < top_n, 0.0, 1.0) for r in ranks[bb * N_GROUP:(bb + 1) * N_GROUP]], axis=0)
        ns = jnp.concatenate([jnp.zeros((HEAD_DIM, tq), F32), not_sel], axis=0).T.astype(BF16)
        pro.append((q_pad + jnp.concatenate([ns] * H, axis=0), o_c, o_w))

    def sel_tile(bb, c0, size, carry, diagonal):
        m, acc = carry
        s = _dot_nt(pro[bb][0], ksa_ref[bb, pl.ds(c0, size), :])
        if diagonal:
            r_id = lax.broadcasted_iota(jnp.int32, (H * tq, size), 0) & (tq - 1)
            c_id = lax.broadcasted_iota(jnp.int32, (H * tq, size), 1)
            s = jnp.where(c_id <= r_id, s, NEG)
        m_new = jnp.maximum(m, jnp.max(s, axis=-1, keepdims=True))
        pr = jnp.exp2(s - m_new)
        key_lanes = lax.broadcasted_iota(jnp.int32, (size, LANES), 1) < HEAD_DIM
        v1 = jnp.where(key_lanes, jnp.ones((size, LANES), BF16), sel_ref[bb, pl.ds(c0, size), :])
        acc = jnp.exp2(m - m_new) * acc + jnp.dot(pr.astype(BF16), v1, preferred_element_type=F32)
        return m_new, acc

    def all_rows(c0, size, carries, diagonal):
        return tuple(sel_tile(bb, c0, size, carries[bb], diagonal) for bb in range(nb))

    carries = tuple((jnp.full((H * tq, 1), NEG, F32), jnp.zeros((H * tq, LANES), F32)) for _ in range(nb))
    n_full = start // tk
    carries = lax.fori_loop(0, n_full, lambda t, c: all_rows(pl.multiple_of(t * tk, tk), tk, c, False), carries)
    pos = n_full * tk
    size = tk // 2
    while size >= tq:
        take = start - pos >= size
        carries = lax.cond(take, lambda c, pos=pos, size=size: all_rows(pl.multiple_of(pos, tq), size, c, False), lambda c: c, carries)
        pos = pos + jnp.where(take, size, 0)
        size //= 2
    carries = all_rows(pl.multiple_of(start, tq), tq, carries, True)

    first_half = lax.broadcasted_iota(jnp.int32, (tq, LANES), 1) < HEAD_DIM

    def place(acc, normalise):
        cols = []
        for h in range(0, H, 2):
            a0, a1 = acc[h * tq:(h + 1) * tq], acc[(h + 1) * tq:(h + 2) * tq]
            r0 = pltpu.roll(a0, HEAD_DIM, 1)
            if normalise:
                v0 = r0 * (1.0 / a0)
                v1 = a1 * (1.0 / pltpu.roll(a1, HEAD_DIM, 1))
            else:
                v0, v1 = r0, a1
            cols.append(jnp.where(first_half, v0, v1))
        return jnp.concatenate(cols, axis=1)

    for bb in range(nb):
        _, o_c, o_w = pro[bb]
        gates = 1.0 / (1.0 + jnp.exp(-gl_ref[bb]))
        g_hi = gates.astype(BF16)
        g_lo = (gates - g_hi.astype(F32)).astype(BF16)
        mix = None
        for c, y in enumerate((place(o_c, False), place(carries[bb][1], True), place(o_w, True))):
            g = jnp.dot(g_hi, gsel_ref[c], preferred_element_type=F32) + jnp.dot(g_lo, gsel_ref[c], preferred_element_type=F32)
            mix = g * y if mix is None else mix + g * y
        sq = mix * mix
        sq_hi = sq.astype(BF16)
        sq_lo = (sq - sq_hi.astype(F32)).astype(BF16)
        ms = jnp.dot(sq_hi, havg_ref[...], preferred_element_type=F32) + jnp.dot(sq_lo, havg_ref[...], preferred_element_type=F32)
        o_ref[bb] = (mix * lax.rsqrt(ms + RMS_EPS) * gain_ref[...]).astype(BF16)


def _nsa(main3, cmp, c2s, neg_onehot, small3, gain_row, nb, tq, tk):
    B, S, _ = main3.shape
    n_sel = S // SEL_LEN
    n_pad = cmp.shape[1]
    gsel = np.zeros((3, LANES, NSA_W), np.float32)
    for h in range(N_NSA):
        for c in range(3):
            gsel[c, SMALL_GATE + 3 * h + c, h * HEAD_DIM:(h + 1) * HEAD_DIM] = 1.0
    head_of = np.arange(NSA_W) // HEAD_DIM
    havg = (head_of[:, None] == head_of[None, :]).astype(np.float32) / HEAD_DIM
    return pl.pallas_call(
        functools.partial(_nsa_kernel, nb=nb, tq=tq, tk=tk, n_sel=n_sel, top_n=min(TOP_N, n_sel)),
        grid=(B // nb, S // tq),
        in_specs=[
            pl.BlockSpec((nb, tq, NSA_W), lambda b, i: (b, i, CB_NQ // 2)),
            pl.BlockSpec((nb, n_pad, LANES), lambda b, i: (b, 0, 0)),
            pl.BlockSpec((HEAD_DIM, n_pad), lambda b, i: (0, 0)),
            pl.BlockSpec((nb, S, LANES), lambda b, i: (b, 0, CB_SEL)),
            pl.BlockSpec((S, LANES), lambda b, i: (0, 0)),
            pl.BlockSpec((nb, S, LANES), lambda b, i: (b, 0, CB_WIN)),
            pl.BlockSpec((nb, tq, LANES), lambda b, i: (b, i, 0)),
            pl.BlockSpec((1, NSA_W), lambda b, i: (0, 0)),
            pl.BlockSpec((3, LANES, NSA_W), lambda b, i: (0, 0, 0)),
            pl.BlockSpec((NSA_W, NSA_W), lambda b, i: (0, 0)),
        ],
        out_specs=pl.BlockSpec((nb, tq, NSA_W), lambda b, i: (b, i, 0)),
        out_shape=jax.ShapeDtypeStruct((B, S, NSA_W), BF16),
        scratch_shapes=[pltpu.VMEM((nb, S, LANES), BF16)],
        compiler_params=_cparams(("parallel", "arbitrary")),
        name="nsa_attention",
    )(main3, cmp, c2s, main3, neg_onehot, main3, small3, gain_row, jnp.asarray(gsel, BF16), jnp.asarray(havg, BF16))


def _post_kernel(x_ref, yr_ref, yf_ref, yn_ref, wo_ref, g_ref, w1_ref, w2_ref, gf_ref, o_ref, *, ff_chunk, final):
    x = x_ref[...]
    x = x + jnp.dot(yr_ref[...], wo_ref[0:RET_W, :], preferred_element_type=F32)
    x = x + jnp.dot(yf_ref[...], wo_ref[RET_W:RET_W + FOX_W, :], preferred_element_type=F32)
    x = x + jnp.dot(yn_ref[...], wo_ref[RET_W + FOX_W:, :], preferred_element_type=F32)
    h = _rms(x, g_ref[...]).astype(BF16)
    o_ref[...] = x
    for c0 in range(0, D_FF, ff_chunk):
        hid = jnp.maximum(jnp.dot(h, w1_ref[:, c0:c0 + ff_chunk], preferred_element_type=F32), 0.0)
        o_ref[...] += jnp.dot((hid * hid).astype(BF16), w2_ref[c0:c0 + ff_chunk, :], preferred_element_type=F32)
    if final:
        o_ref[...] = _rms(o_ref[...], gf_ref[...])


def _post(x2d, yr, yf, yn, wo, g, w1, w2, gf, tm, final):
    T = x2d.shape[0]
    row = lambda w: pl.BlockSpec((tm, w), lambda i: (i, 0))
    return pl.pallas_call(
        functools.partial(_post_kernel, ff_chunk=512, final=final),
        grid=(T // tm,),
        in_specs=[
            row(D_MODEL), row(RET_W), row(FOX_W), row(NSA_W),
            _resident((D_MODEL, D_MODEL), lambda i: (0, 0)),
            _resident((1, D_MODEL), lambda i: (0, 0)),
            _resident((D_MODEL, D_FF), lambda i: (0, 0)),
            _resident((D_FF, D_MODEL), lambda i: (0, 0)),
            _resident((1, D_MODEL), lambda i: (0, 0)),
        ],
        out_specs=row(D_MODEL),
        out_shape=jax.ShapeDtypeStruct((T, D_MODEL), F32),
        compiler_params=_cparams(("parallel",)),
        name="outproj_mlp",
    )(x2d, yr, yf, yn, wo, g, w1, w2, gf)


def _tile_sizes(batch):
    nb = 4 if batch % 4 == 0 else (2 if batch % 2 == 0 else 1)
    return dict(tm=1024, fox_nb=nb, fox_tq=512, fox_tk=512, nsa_nb=nb, nsa_tq=256, nsa_tk=1024)


def kernel(x, norm_attn, w_in, fox_forget_bias, ret_norm_gain, fox_norm_gain, nsa_norm_gain, nsa_cmp_pos_k, nsa_cmp_pos_v, nsa_cmp_w1_k, nsa_cmp_w2_k, nsa_cmp_w1_v, nsa_cmp_w2_v, w_out, norm_mlp, w_mlp_in, w_mlp_out, norm_final):
    B, S, D = x.shape
    depth = w_in.shape[0]
    ts = _tile_sizes(B)
    assert D == D_MODEL and S % ts["fox_tq"] == 0 and S >= WINDOW + ts["nsa_tq"] and (B * S) % ts["tm"] == 0
    T = B * S

    wm, ws = _in_proj_weights(w_in)
    wo = w_out.astype(BF16)
    w1 = w_mlp_in.astype(BF16)
    w2 = w_mlp_out.astype(BF16)

    def expand_w1(wk, wv):
        L = wk.shape[0]
        wk = wk.astype(BF16).reshape(L, CMP_LEN, HEAD_DIM, CMP_HIDDEN)
        wv = wv.astype(BF16).reshape(L, CMP_LEN, HEAD_DIM, CMP_HIDDEN)
        z = jnp.zeros_like(wk)
        full = jnp.concatenate([jnp.concatenate([wk, z], axis=-1), jnp.concatenate([z, wv], axis=-1)], axis=2)
        full = full.reshape(L, CMP_LEN * LANES, 2 * CMP_HIDDEN)
        return full[:, :CMP_STRIDE * LANES], full[:, CMP_STRIDE * LANES:]

    wt_all, wb_all = expand_w1(nsa_cmp_w1_k, nsa_cmp_w1_v)
    zk = jnp.zeros_like(nsa_cmp_w2_k)
    w2c = jnp.concatenate([jnp.concatenate([nsa_cmp_w2_k, zk], axis=-1),
                           jnp.concatenate([zk, nsa_cmp_w2_v], axis=-1)], axis=1).astype(BF16)
    pos = jnp.concatenate([nsa_cmp_pos_k, nsa_cmp_pos_v], axis=-1)
    pos_tb = pos.reshape(depth, 2, 1, CMP_STRIDE * LANES)
    pos_tb = jnp.broadcast_to(pos_tb, (depth, 2, 8, CMP_STRIDE * LANES)).reshape(depth, 16, CMP_STRIDE * LANES).astype(BF16)

    n_cmp = (S - CMP_LEN) // CMP_STRIDE + 1
    n_sel = S // SEL_LEN
    n_rows = S // CMP_STRIDE
    cs = np.arange(n_rows) * CMP_STRIDE
    ss = np.arange(n_sel) * SEL_LEN
    overlap = np.clip(np.minimum(cs[:, None] + CMP_LEN, ss[None, :] + SEL_LEN) - np.maximum(cs[:, None], ss[None, :]), 0, None)
    overlap[n_cmp:] = 0
    c2s_np = np.zeros((HEAD_DIM, n_rows), np.float32)
    c2s_np[:n_sel] = (overlap / CMP_LEN).T
    c2s = jnp.asarray(c2s_np, dtype=BF16)
    assert n_sel <= HEAD_DIM, "the selected-branch key augmentation has 64 lanes, one per selection block"
    neg_onehot = jnp.asarray(np.where((np.arange(S)[:, None] // SEL_LEN) == np.arange(LANES)[None, :] - HEAD_DIM, NEG, 0.0), dtype=BF16)

    tables = _retention_tables(S)
    fbias = jnp.zeros((depth, 1, LANES), F32).at[:, 0, SMALL_FF:SMALL_FF + N_FOX].set(fox_forget_bias)

    x2d = x.reshape(T, D)
    for l in range(depth):
        main, small = _inproj(x2d, norm_attn[l][None, :], wm[l], ws[l], ts["tm"])
        main3 = main.reshape(B, S, N_MAIN)
        small3 = small.reshape(B, S, LANES)
        fox_qa, fox_ka, edges = _forget_cumsum(small3, fbias[l])
        y_ret = _retention(main3, tables, ret_norm_gain[l][None, :])
        per_tile = ts["fox_tk"] // RET_CHUNK
        cq = edges[:, 0::per_tile, 0, SMALL_FF:SMALL_FF + N_FOX]
        ck = edges[:, per_tile - 1::per_tile, 1, SMALL_FF:SMALL_FF + N_FOX]
        cum_edges = jnp.stack([cq.transpose(0, 2, 1), ck.transpose(0, 2, 1)], axis=2).reshape(-1)
        y_fox = _fox(main3, fox_qa, fox_ka, cum_edges, fox_norm_gain[l][None, :], ts["fox_nb"], ts["fox_tq"], ts["fox_tk"])
        r = main3[:, :, CB_CMP * LANES:(CB_CMP + 1) * LANES].reshape(B, n_rows, CMP_STRIDE * LANES)
        cmp = _nsa_compress(r, pos_tb[l], wt_all[l], wb_all[l], w2c[l])
        y_nsa = _nsa(main3, cmp, c2s, neg_onehot, small3, nsa_norm_gain[l][None, :], ts["nsa_nb"], ts["nsa_tq"], ts["nsa_tk"])
        x2d = _post(x2d, y_ret.reshape(T, RET_W), y_fox.reshape(T, FOX_W), y_nsa.reshape(T, NSA_W), wo[l],
                    norm_mlp[l][None, :], w1[l], w2[l], norm_final[None, :], ts["tm"], final=(l == depth - 1))
    return x2d.reshape(B, S, D)
```

```python
import functools

import numpy as np
import jax
import jax.numpy as jnp
from jax import lax
from jax.experimental import pallas as pl
from jax.experimental.pallas import tpu as pltpu

F32 = jnp.float32
BF16 = jnp.bfloat16

D_MODEL = 1024
HEAD_DIM = 64
N_RET = 6
N_FOX = 6
N_NSA = 4
RET_W = N_RET * HEAD_DIM
FOX_W = N_FOX * HEAD_DIM
NSA_W = N_NSA * HEAD_DIM
D_FF = 4 * D_MODEL
RMS_EPS = 1e-6
RET_CHUNK = 128
RET_BLOCK = 256
ROPE_BASE = 10000.0
CMP_LEN = 32
CMP_STRIDE = 16
CMP_HIDDEN = 4 * HEAD_DIM
SEL_LEN = 64
TOP_N = 16
WINDOW = 512
FORCED_SCORE = 1e4
NEG = -1e30
LOG2E = float(np.log2(np.e))

LANES = 128
N_PAIR = N_RET // 2

CB_RQ, CB_RK, CB_RV, CB_RG = 0, 3, 6, 9
CB_NQ = 12
CB_FQ, CB_FK, CB_FV = 14, 17, 20
CB_CMP, CB_SEL, CB_WIN = 23, 24, 25
N_MAIN = 26 * LANES
SMALL_FF = 0
SMALL_GATE = 8

VMEM_LIMIT = 56 * 1024 * 1024


def _cparams(sem):
    return pltpu.CompilerParams(dimension_semantics=sem, vmem_limit_bytes=VMEM_LIMIT)


def _resident(shape, index_map):
    return pl.BlockSpec(shape, index_map, pipeline_mode=pl.Buffered(1))


def _in_proj_weights(w_in):
    sizes = (RET_W, RET_W, RET_W, RET_W, FOX_W, FOX_W, FOX_W, N_FOX, NSA_W) + (HEAD_DIM,) * 6 + (3 * N_NSA,)
    off = np.concatenate([[0], np.cumsum(sizes)])
    (o_rq, o_rk, o_rv, o_rg, o_fq, o_fk, o_fv, o_ff, o_nq, o_kc, o_vc, o_ks, o_vs, o_kw, o_vw, o_gt) = [int(o) for o in off[:-1]]
    L, D, _ = w_in.shape
    half = HEAD_DIM // 2
    cut = lambda o, n: w_in[:, :, o:o + n]

    def interleave(w):
        return w.reshape(L, D, N_PAIR, 2, 2, half).transpose(0, 1, 2, 4, 3, 5).reshape(L, D, RET_W)

    qk_scale = HEAD_DIM ** -0.5
    sm_scale = qk_scale * LOG2E
    main = jnp.concatenate([
        interleave(cut(o_rq, RET_W)), interleave(cut(o_rk, RET_W)) * qk_scale, cut(o_rv, RET_W), cut(o_rg, RET_W),
        cut(o_nq, NSA_W) * sm_scale, cut(o_fq, FOX_W) * sm_scale, cut(o_fk, FOX_W), cut(o_fv, FOX_W),
        cut(o_kc, 6 * HEAD_DIM),
    ], axis=-1)
    assert main.shape[-1] == N_MAIN and o_vw == o_kc + 5 * HEAD_DIM
    zeros = lambda n: jnp.zeros((L, D, n), w_in.dtype)
    small = jnp.concatenate([
        zeros(SMALL_FF), cut(o_ff, N_FOX), zeros(SMALL_GATE - SMALL_FF - N_FOX),
        cut(o_gt, 3 * N_NSA), zeros(LANES - SMALL_GATE - 3 * N_NSA)], axis=-1)
    return main.astype(BF16), small.astype(BF16)


def _retention_tables(seq):
    half = HEAD_DIM // 2
    inv = 1.0 / (ROPE_BASE ** (jnp.arange(half, dtype=F32) / half))
    ang = jnp.arange(seq, dtype=F32)[:, None] * inv[None, :]
    cos, sin = jnp.cos(ang), jnp.sin(ang)
    cos_t = jnp.concatenate([cos, cos, cos, cos], axis=-1)
    sin_t = jnp.concatenate([-sin, -sin, sin, sin], axis=-1)
    log_gamma = np.log(1.0 - 2.0 ** (-5.0 - np.arange(N_RET, dtype=np.float32))).astype(np.float32)
    C = RET_BLOCK
    idx = np.arange(C, dtype=np.float32)
    diff = idx[:, None] - idx[None, :]
    lane = np.arange(LANES)
    head_k = (lane % HEAD_DIM) // half
    head_v = lane // HEAD_DIM
    d_in = np.zeros((N_PAIR, 2, C, C), np.float32)
    d_k = np.zeros((N_PAIR, C, LANES), np.float32)
    d_q = np.zeros((N_PAIR, C, LANES), np.float32)
    d_c = np.zeros((N_PAIR, LANES, LANES), np.float32)
    for p in range(N_PAIR):
        lg = log_gamma[2 * p:2 * p + 2]
        for j in range(2):
            d_in[p, j] = np.where(diff >= 0, np.exp(lg[j] * np.maximum(diff, 0.0)), 0.0)
        d_k[p] = np.exp(lg[head_k][None, :] * (C - 1.0 - idx)[:, None])
        d_q[p] = np.exp(lg[head_v][None, :] * (idx + 1.0)[:, None])
        d_c[p] = np.broadcast_to(np.exp(lg[head_v] * C)[None, :], (LANES, LANES))
    s_mask = (head_k[:, None] == head_v[None, :]).astype(np.float32)
    return cos_t, sin_t, jnp.asarray(d_in), jnp.asarray(d_k), jnp.asarray(d_q), jnp.asarray(d_c), jnp.asarray(s_mask)


def _rms(x, g):
    return x * lax.rsqrt(jnp.mean(x * x, axis=-1, keepdims=True) + RMS_EPS) * g


def _inproj_kernel(x_ref, g_ref, wm_ref, ws_ref, om_ref, os_ref, *, n_chunk):
    h = _rms(x_ref[...], g_ref[...]).astype(BF16)
    for n0 in range(0, N_MAIN, n_chunk):
        n1 = min(n0 + n_chunk, N_MAIN)
        om_ref[:, n0:n1] = jnp.dot(h, wm_ref[:, n0:n1], preferred_element_type=F32).astype(BF16)
    os_ref[...] = jnp.dot(h, ws_ref[...], preferred_element_type=F32)


def _inproj(x2d, g, wm, ws, tm):
    T = x2d.shape[0]
    return pl.pallas_call(
        functools.partial(_inproj_kernel, n_chunk=512),
        grid=(T // tm,),
        in_specs=[
            pl.BlockSpec((tm, D_MODEL), lambda i: (i, 0)),
            _resident((1, D_MODEL), lambda i: (0, 0)),
            _resident((D_MODEL, N_MAIN), lambda i: (0, 0)),
            _resident((D_MODEL, LANES), lambda i: (0, 0)),
        ],
        out_specs=[
            pl.BlockSpec((tm, N_MAIN), lambda i: (i, 0)),
            pl.BlockSpec((tm, LANES), lambda i: (i, 0)),
        ],
        out_shape=[jax.ShapeDtypeStruct((T, N_MAIN), BF16), jax.ShapeDtypeStruct((T, LANES), F32)],
        compiler_params=_cparams(("parallel",)),
        name="inproj",
    )(x2d, g, wm, ws)


N_SPLIT = 3
AUG_STRIDE = 2 * N_SPLIT


def _split_bf16(x):
    terms, rest = [], x
    for _ in range(N_SPLIT):
        t = rest.astype(BF16)
        rest = rest - t.astype(F32)
        terms.append(t)
    return terms


def _bias_placement():
    mq = np.zeros((N_SPLIT, LANES, LANES), np.float32)
    mk = np.zeros((N_SPLIT, LANES, LANES), np.float32)
    one_q = np.zeros((1, LANES), np.float32)
    one_k = np.zeros((1, LANES), np.float32)
    for h in range(N_FOX):
        base = h * AUG_STRIDE
        for i in range(N_SPLIT):
            mq[i, SMALL_FF + h, base + i] = 1.0
            mk[i, SMALL_FF + h, base + N_SPLIT + i] = -1.0
            one_q[0, base + N_SPLIT + i] = 1.0
            one_k[0, base + i] = 1.0
    return jnp.asarray(mq, BF16), jnp.asarray(mk, BF16), jnp.asarray(one_q), jnp.asarray(one_k)


def _cum_kernel(z_ref, b_ref, tri_ref, mq_ref, mk_ref, oq_ref, ok_ref, qa_ref, ka_ref, edge_ref, *, n_chunks):
    C = RET_CHUNK

    def chunk(c, carry):
        r0 = pl.multiple_of(c * C, C)
        z = z_ref[0, pl.ds(r0, C), :] + b_ref[...]
        lf = jnp.minimum(z, 0.0) - jnp.log1p(jnp.exp(-jnp.abs(z)))
        terms = _split_bf16(lf)
        local = jnp.dot(tri_ref[...], terms[0], preferred_element_type=F32)
        for term in terms[1:]:
            local = local + jnp.dot(tri_ref[...], term, preferred_element_type=F32)
        cs = local + carry
        cs2 = cs * LOG2E
        edge_ref[0, pl.ds(c, 1)] = jnp.concatenate([cs2[0:1], cs2[C - 1:C], jnp.zeros((6, LANES), F32)], axis=0)[None]
        qa, ka = oq_ref[...], ok_ref[...]
        for i, term in enumerate(_split_bf16(cs2)):
            qa = qa + jnp.dot(term, mq_ref[i], preferred_element_type=F32)
            ka = ka + jnp.dot(term, mk_ref[i], preferred_element_type=F32)
        qa_ref[0, pl.ds(r0, C), :] = qa.astype(BF16)
        ka_ref[0, pl.ds(r0, C), :] = ka.astype(BF16)
        return cs[C - 1:C, :]

    unroll = max(u for u in (1, 2, 4) if n_chunks % u == 0)

    def body(i, carry):
        for u in range(unroll):
            carry = chunk(i * unroll + u, carry)
        return carry

    lax.fori_loop(0, n_chunks // unroll, body, jnp.zeros((1, LANES), F32))


def _forget_cumsum(small3, bias_row):
    B, S, _ = small3.shape
    tri = jnp.asarray(np.tril(np.ones((RET_CHUNK, RET_CHUNK), np.float32)), BF16)
    mq, mk, one_q, one_k = _bias_placement()
    const = lambda shape: pl.BlockSpec(shape, lambda b: (0,) * len(shape))
    return pl.pallas_call(
        functools.partial(_cum_kernel, n_chunks=S // RET_CHUNK),
        grid=(B,),
        in_specs=[
            pl.BlockSpec((1, S, LANES), lambda b: (b, 0, 0)),
            const((1, LANES)), const((RET_CHUNK, RET_CHUNK)),
            const((N_SPLIT, LANES, LANES)), const((N_SPLIT, LANES, LANES)), const((1, LANES)), const((1, LANES)),
        ],
        out_specs=[
            pl.BlockSpec((1, S, LANES), lambda b: (b, 0, 0)),
            pl.BlockSpec((1, S, LANES), lambda b: (b, 0, 0)),
            pl.BlockSpec((1, S // RET_CHUNK, 8, LANES), lambda b: (b, 0, 0, 0)),
        ],
        out_shape=[jax.ShapeDtypeStruct((B, S, LANES), BF16), jax.ShapeDtypeStruct((B, S, LANES), BF16),
                   jax.ShapeDtypeStruct((B, S // RET_CHUNK, 8, LANES), F32)],
        compiler_params=_cparams(("parallel",)),
        name="forget_cumsum",
    )(small3, bias_row, tri, mq, mk, one_q, one_k)


def _dot_nt(a, b):
    return lax.dot_general(a, b, (((1,), (1,)), ((), ())), preferred_element_type=F32)


def _ret_kernel(q_ref, k_ref, v_ref, g_ref, cos_ref, sin_ref, din_ref, dk_ref, dq_ref, dc_ref, sm_ref, gain_ref,
                o_ref, state_ref, *, n_chunks, nb):
    C = RET_BLOCK
    half = HEAD_DIM // 2
    lane = lax.broadcasted_iota(jnp.int32, (C, LANES), 1)
    khead = (lane % HEAD_DIM) // half
    vhead0 = lane < HEAD_DIM
    state_ref[...] = jnp.zeros_like(state_ref)

    p = pl.program_id(1)

    def pair_chunk(bb, r0):
        rows = pl.ds(r0, C)
        lanes = slice(0, LANES)
        cs, sn = cos_ref[rows, :], sin_ref[rows, :]
        q = q_ref[bb, rows, lanes].astype(F32)
        k = k_ref[bb, rows, lanes].astype(F32)
        q = q * cs + pltpu.roll(q, HEAD_DIM, 1) * sn
        k = k * cs + pltpu.roll(k, HEAD_DIM, 1) * sn
        v = v_ref[bb, rows, lanes]
        qb, kb = q.astype(BF16), k.astype(BF16)
        state = state_ref[bb]
        cross = jnp.dot(qb, state.astype(BF16), preferred_element_type=F32) * dq_ref[p]
        outs = []
        for j in range(2):
            qm = jnp.where(khead == j, qb, jnp.zeros_like(qb))
            inner = _dot_nt(qm, kb) * din_ref[p, j]
            outs.append(jnp.dot(inner.astype(BF16), v, preferred_element_type=F32))
        out = jnp.where(vhead0, outs[0], outs[1]) + cross
        kd = (k * dk_ref[p]).T.astype(BF16)
        state_ref[bb] = state * dc_ref[p] + jnp.dot(kd, v, preferred_element_type=F32) * sm_ref[...]
        inv = 1.0 / HEAD_DIM
        s0 = jnp.sum(jnp.where(vhead0, out, 0.0), axis=-1, keepdims=True)
        s1 = jnp.sum(jnp.where(vhead0, 0.0, out), axis=-1, keepdims=True)
        yc = out - jnp.where(vhead0, s0, s1) * inv
        yc2 = yc * yc
        v0 = jnp.sum(jnp.where(vhead0, yc2, 0.0), axis=-1, keepdims=True)
        v1 = jnp.sum(jnp.where(vhead0, 0.0, yc2), axis=-1, keepdims=True)
        y = yc * lax.rsqrt(jnp.where(vhead0, v0, v1) * inv + RMS_EPS) * gain_ref[:, lanes]
        g = g_ref[bb, rows, lanes].astype(F32)
        o_ref[bb, rows, lanes] = (y * (g * (1.0 / (1.0 + jnp.exp(-g))))).astype(BF16)

    def body(c, _):
        for bb in range(nb):
            pair_chunk(bb, pl.multiple_of(c * C, C))
        return 0

    lax.fori_loop(0, n_chunks, body, 0)


def _retention(main3, tables, gain_row):
    B, S, _ = main3.shape
    cos_t, sin_t, d_in, d_k, d_q, d_c, s_mask = tables
    C = RET_BLOCK
    n_chunks = S // C
    nb = 4 if B % 4 == 0 else (2 if B % 2 == 0 else 1)
    seq_spec = lambda cb: pl.BlockSpec((nb, S, LANES), lambda b, p: (b, 0, cb + p))
    const = lambda shape: pl.BlockSpec(shape, lambda b, p: (0,) * len(shape))
    return pl.pallas_call(
        functools.partial(_ret_kernel, n_chunks=n_chunks, nb=nb),
        grid=(B // nb, N_PAIR),
        in_specs=[
            seq_spec(CB_RQ), seq_spec(CB_RK), seq_spec(CB_RV), seq_spec(CB_RG),
            const((S, LANES)), const((S, LANES)),
            const((N_PAIR, 2, C, C)), const((N_PAIR, C, LANES)), const((N_PAIR, C, LANES)),
            const((N_PAIR, LANES, LANES)), const((LANES, LANES)), pl.BlockSpec((1, LANES), lambda b, p: (0, p)),
        ],
        out_specs=pl.BlockSpec((nb, S, LANES), lambda b, p: (b, 0, p)),
        out_shape=jax.ShapeDtypeStruct((B, S, RET_W), BF16),
        scratch_shapes=[pltpu.VMEM((nb, LANES, LANES), F32)],
        compiler_params=_cparams(("parallel", "parallel")),
        name="retention",
    )(main3, main3, main3, main3, cos_t, sin_t, d_in, d_k, d_q, d_c, s_mask, gain_row)


UNDERFLOW_LOG2 = -170.0


def _head_sq_norm_max(x, havg, head0):
    xf = x.astype(F32)
    mean = jnp.dot((xf * xf).astype(BF16), havg, preferred_element_type=F32)
    bound = mean * (HEAD_DIM * 1.01)
    return jnp.max(jnp.where(head0, bound, 0.0)), jnp.max(jnp.where(head0, 0.0, bound))


def _fox_kernel(cb_ref, q_ref, qa_ref, qs_ref, kh_ref, ka_ref, v_ref, gain_ref, havg_ref, o_ref, k_ref, ft_ref, *, nb, tq, tk, n_tiles):
    p = pl.program_id(1)
    qi = pl.program_id(2)
    lane_q = lax.broadcasted_iota(jnp.int32, (tq, LANES), 1)
    head0 = lane_q < HEAD_DIM

    @pl.when(qi == 0)
    def _():
        head0_s = lax.broadcasted_iota(jnp.int32, kh_ref.shape[1:], 1) < HEAD_DIM
        for bb in range(nb):
            k_ref[bb, :, 0:LANES] = kh_ref[bb]
            k_ref[bb, :, LANES:2 * LANES] = ka_ref[bb]
            b = pl.program_id(0) * nb + bb
            kn = _head_sq_norm_max(kh_ref[bb], havg_ref[...], head0_s)
            for i in range(n_tiles):
                qn = _head_sq_norm_max(qs_ref[bb, i * tq:(i + 1) * tq, :], havg_ref[...], head0)
                for j in range(2):
                    h = 2 * p + j
                    slack = 2.0 * jnp.sqrt(qn[j] * kn[j])
                    cq = cb_ref[((b * N_FOX + h) * 2 + 0) * n_tiles + i]
                    skipped = jnp.int32(0)
                    for t in range(i):
                        ck = cb_ref[((b * N_FOX + h) * 2 + 1) * n_tiles + t]
                        skipped = skipped + (slack + cq - ck < UNDERFLOW_LOG2).astype(jnp.int32)
                    ft_ref[(bb * n_tiles + i) * 2 + j] = skipped

    q_pos = qi * tq + lax.broadcasted_iota(jnp.int32, (tq, tk), 0)
    k_off = lax.broadcasted_iota(jnp.int32, (tq, tk), 1)
    chains = [(bb, j) for bb in range(nb) for j in range(2)]

    qops = {}
    for bb in range(nb):
        q = q_ref[bb]
        qa = qa_ref[bb]
        for j in range(2):
            qm = jnp.where(head0 if j == 0 else jnp.logical_not(head0), q, jnp.zeros_like(q))
            lo = (2 * p + j) * AUG_STRIDE
            own = jnp.logical_and(lane_q >= lo, lane_q < lo + AUG_STRIDE)
            qops[bb, j] = jnp.concatenate([qm, jnp.where(own, qa, jnp.zeros_like(qa))], axis=1)

    def tile(ch, t, n, carry, masked):
        bb, j = ch
        m, acc = carry
        c0 = pl.multiple_of(t * tk, tk)
        s = _dot_nt(qops[ch], k_ref[bb, pl.ds(c0, n * tk), :])
        if masked:
            s = jnp.where(k_off + c0 <= q_pos, s, NEG)
        m_new = jnp.maximum(m, jnp.max(s, axis=-1, keepdims=True))
        pr = jnp.exp2(s - m_new)
        own = lax.broadcasted_iota(jnp.int32, (n * tk, LANES), 1) < HEAD_DIM
        vj = jnp.where(own if j == 0 else jnp.logical_not(own), v_ref[bb, pl.ds(c0, n * tk), :], jnp.ones((n * tk, LANES), BF16))
        acc = jnp.exp2(m - m_new) * acc + jnp.dot(pr.astype(BF16), vj, preferred_element_type=F32)
        return m_new, acc

    carry = {}
    for ch in chains:
        bb, j = ch
        t0 = ft_ref[(bb * n_tiles + qi) * 2 + j]
        n_act = qi - t0
        c = (jnp.full((tq, 1), NEG, F32), jnp.zeros((tq, LANES), F32))
        c = lax.fori_loop(0, n_act // 2, lambda i, c, ch=ch, t0=t0: tile(ch, t0 + 2 * i, 2, c, False), c)
        carry[ch] = lax.cond(n_act % 2 == 1, lambda c, ch=ch: tile(ch, qi - 1, 1, c, False), lambda c: c, c)
    for ch in chains:
        carry[ch] = tile(ch, qi, 1, carry[ch], True)
    for bb in range(nb):
        a0, a1 = carry[bb, 0][1], carry[bb, 1][1]
        out = jnp.where(head0, a0 * (1.0 / pltpu.roll(a0, HEAD_DIM, 1)), a1 * (1.0 / pltpu.roll(a1, HEAD_DIM, 1)))
        sq = out * out
        sq_hi = sq.astype(BF16)
        sq_lo = (sq - sq_hi.astype(F32)).astype(BF16)
        ms = jnp.dot(sq_hi, havg_ref[...], preferred_element_type=F32) + jnp.dot(sq_lo, havg_ref[...], preferred_element_type=F32)
        o_ref[bb] = (out * lax.rsqrt(ms + RMS_EPS) * gain_ref[...]).astype(BF16)


def _fox(main3, q_aug, k_aug, cum_edges, gain_row, nb, tq, tk):
    B, S, _ = main3.shape
    assert tq == tk, "the kernel handles exactly one diagonal tile per query block"
    head_of = np.arange(LANES) // HEAD_DIM
    havg = (head_of[:, None] == head_of[None, :]).astype(np.float32) / HEAD_DIM
    grid_spec = pltpu.PrefetchScalarGridSpec(
        num_scalar_prefetch=1,
        grid=(B // nb, N_PAIR, S // tq),
        in_specs=[
            pl.BlockSpec((nb, tq, LANES), lambda b, p, i, cb: (b, i, CB_FQ + p)),
            pl.BlockSpec((nb, tq, LANES), lambda b, p, i, cb: (b, i, 0)),
            pl.BlockSpec((nb, S, LANES), lambda b, p, i, cb: (b, 0, CB_FQ + p)),
            pl.BlockSpec((nb, S, LANES), lambda b, p, i, cb: (b, 0, CB_FK + p)),
            pl.BlockSpec((nb, S, LANES), lambda b, p, i, cb: (b, 0, 0)),
            pl.BlockSpec((nb, S, LANES), lambda b, p, i, cb: (b, 0, CB_FV + p)),
            pl.BlockSpec((1, LANES), lambda b, p, i, cb: (0, p)),
            pl.BlockSpec((LANES, LANES), lambda b, p, i, cb: (0, 0)),
        ],
        out_specs=pl.BlockSpec((nb, tq, LANES), lambda b, p, i, cb: (b, i, p)),
        scratch_shapes=[pltpu.VMEM((nb, S, 2 * LANES), BF16), pltpu.SMEM((nb * 2 * (S // tk),), jnp.int32)],
    )
    return pl.pallas_call(
        functools.partial(_fox_kernel, nb=nb, tq=tq, tk=tk, n_tiles=S // tk),
        grid_spec=grid_spec,
        out_shape=jax.ShapeDtypeStruct((B, S, FOX_W), BF16),
        compiler_params=_cparams(("parallel", "parallel", "arbitrary")),
        name="fox_attention",
    )(cum_edges, main3, q_aug, main3, main3, k_aug, main3, gain_row, jnp.asarray(havg, BF16))


def _gelu_tanh(x):
    return 0.5 * x * (1.0 + jnp.tanh(np.sqrt(2.0 / np.pi).astype(np.float32) * (x + 0.044715 * (x * x * x))))


def _cmp_kernel(r_ref, pos_ref, wt_ref, wb_ref, w2_ref, o_ref):
    r = r_ref[0]
    n = r.shape[0]
    top = jnp.dot(r, wt_ref[...], preferred_element_type=F32)
    bot = jnp.dot(r, wb_ref[...], preferred_element_type=F32)
    cpos = jnp.dot(pos_ref[0:8, :], wt_ref[...], preferred_element_type=F32)
    cpos = cpos + jnp.dot(pos_ref[8:16, :], wb_ref[...], preferred_element_type=F32)
    hid = _gelu_tanh(top + pltpu.roll(bot, n - 1, 0) + cpos[0:1, :])
    o_ref[0] = jnp.dot(hid.astype(BF16), w2_ref[...], preferred_element_type=F32)


def _nsa_compress(r, pos2, wt, wb, w2):
    B, n, _ = r.shape
    width = CMP_STRIDE * LANES
    return pl.pallas_call(
        _cmp_kernel,
        grid=(B,),
        in_specs=[
            pl.BlockSpec((1, n, width), lambda b: (b, 0, 0)),
            pl.BlockSpec((16, width), lambda b: (0, 0)),
            pl.BlockSpec((width, 2 * CMP_HIDDEN), lambda b: (0, 0)),
            pl.BlockSpec((width, 2 * CMP_HIDDEN), lambda b: (0, 0)),
            pl.BlockSpec((2 * CMP_HIDDEN, LANES), lambda b: (0, 0)),
        ],
        out_specs=pl.BlockSpec((1, n, LANES), lambda b: (b, 0, 0)),
        out_shape=jax.ShapeDtypeStruct((B, n, LANES), F32),
        compiler_params=_cparams(("parallel",)),
        name="nsa_compress",
    )(r, pos2, wt, wb, w2)


def _softmax_rows(s, mask):
    sm = jnp.where(mask, s, NEG)
    m = jnp.maximum(jnp.max(sm, axis=-1, keepdims=True), 0.1 * NEG)
    e = jnp.exp2(sm - m)
    l = jnp.sum(e, axis=-1, keepdims=True)
    return e / jnp.where(l > 0.0, l, 1.0)


def _nsa_kernel(q_ref, cmp_ref, c2s_ref, sel_ref, noh_ref, win_ref, gl_ref, gain_ref, gsel_ref, havg_ref, o_ref, ksa_ref, *, nb, tq, tk, n_sel,
                top_n):
    H = N_NSA
    qi = pl.program_id(1)
    start = qi * tq

    @pl.when(qi == 0)
    def _():
        key_lanes = lax.broadcasted_iota(jnp.int32, noh_ref.shape, 1) < HEAD_DIM
        for bb in range(nb):
            ksa_ref[bb] = jnp.where(key_lanes, sel_ref[bb], noh_ref[...])

    t_row = start + lax.broadcasted_iota(jnp.int32, (tq, 1), 0)
    t_all = jnp.concatenate([t_row] * H, axis=0)
    zeros64 = jnp.zeros((tq, HEAD_DIM), BF16)
    wlen = WINDOW + tq
    base = pl.multiple_of(jnp.maximum(start - WINDOW, 0), tq)
    n_pad = cmp_ref.shape[1]
    assert CMP_STRIDE == 16 and tq & (tq - 1) == 0
    last_cmp = (t_all - (CMP_LEN - 1)) >> 4
    row_in_block = lax.broadcasted_iota(jnp.int32, (H * tq, wlen), 0) & (tq - 1)
    win_lag = (start - base) - (lax.broadcasted_iota(jnp.int32, (H * tq, wlen), 1) - row_in_block)

    def prologue(bb):
        q = q_ref[bb]
        q_heads = [q[:, h * HEAD_DIM:(h + 1) * HEAD_DIM] for h in range(H)]
        q_pad = jnp.concatenate([jnp.concatenate([qh, zeros64], axis=1) for qh in q_heads], axis=0)

        win = win_ref[bb, pl.ds(base, wlen), :]
        s_w = _dot_nt(q_pad, win)
        s_w = jnp.where(lax.bitcast_convert_type(win_lag, jnp.uint32) < WINDOW, s_w, NEG)
        e_w = jnp.exp2(s_w - jnp.max(s_w, axis=-1, keepdims=True))
        win1 = jnp.where(lax.broadcasted_iota(jnp.int32, (wlen, LANES), 1) < HEAD_DIM, jnp.ones((wlen, LANES), BF16), win)
        o_w = jnp.dot(e_w.astype(BF16), win1, preferred_element_type=F32)

        cmp = cmp_ref[bb]
        cmp_hi = cmp.astype(BF16)
        cmp_lo = (cmp - cmp_hi.astype(F32)).astype(BF16)
        s_c = _dot_nt(q_pad, cmp_hi) + _dot_nt(q_pad, cmp_lo)
        n_id = lax.broadcasted_iota(jnp.int32, (H * tq, n_pad), 1)
        p_c = _softmax_rows(s_c, n_id <= last_cmp)
        o_c = jnp.dot(p_c.astype(BF16), cmp_hi, preferred_element_type=F32)

        p_sum = p_c[0:tq]
        for h in range(1, H):
            p_sum = p_sum + p_c[h * tq:(h + 1) * tq]
        p_hi = p_sum.astype(BF16)
        p_lo = (p_sum - p_hi.astype(F32)).astype(BF16)
        c2s_t = c2s_ref[...]
        imp = _dot_nt(c2s_t, p_hi) + _dot_nt(c2s_t, p_lo)
        t_lane = start + lax.broadcasted_iota(jnp.int32, (HEAD_DIM, tq), 1)
        s_id = lax.broadcasted_iota(jnp.int32, (HEAD_DIM, tq), 0)
        cur = t_lane >> 6
        forced = jnp.logical_or(s_id == 0, jnp.logical_or(s_id == cur, s_id == cur - 1))
        score = jnp.where(forced, FORCED_SCORE, imp)
        score = jnp.where(s_id * SEL_LEN <= t_lane, score, -1.0)
        return q_pad, o_c, o_w, [score[g * SUB:(g + 1) * SUB, :] for g in range(N_GROUP)]

    SUB = 8
    N_GROUP = HEAD_DIM // SUB
    stage = [prologue(bb) for bb in range(nb)]

    n_live = (((start + tq - 1) >> 6) >> 3) + 1
    sub_id = lax.broadcasted_iota(jnp.int32, (SUB, tq), 0)

    def rank_level(level, ranks):
        ranks = list(ranks)
        pairs = [(level, g) for g in range(level + 1)] + [(gc, level) for gc in range(level)]
        for bb in range(nb):
            groups = stage[bb][3]
            for gc, g in pairs:
                for k in range(SUB):
                    c = gc * SUB + k
                    if c >= n_sel:
                        continue
                    row = groups[gc][k:k + 1, :]
                    if g * SUB > c:
                        before = row >= groups[g]
                    elif g * SUB + SUB - 1 < c:
                        before = row > groups[g]
                    else:
                        before = jnp.logical_or(row > groups[g], jnp.logical_and(row == groups[g], sub_id > k))
                    ranks[bb * N_GROUP + g] = ranks[bb * N_GROUP + g] + before.astype(jnp.int32)
        return tuple(ranks)

    ranks = tuple(jnp.zeros((SUB, tq), jnp.int32) for _ in range(nb * N_GROUP))
    for level in range(N_GROUP):
        ranks = lax.cond(level < n_live, functools.partial(rank_level, level), lambda r: r, ranks)

    pro = []
    for bb in range(nb):
        q_pad, o_c, o_w, _ = stage[bb]
        not_sel = jnp.concatenate([jnp.where(r < top_n, 0.0, 1.0) for r in ranks[bb * N_GROUP:(bb + 1) * N_GROUP]], axis=0)
        ns = jnp.concatenate([jnp.zeros((HEAD_DIM, tq), F32), not_sel], axis=0).T.astype(BF16)
        pro.append((q_pad + jnp.concatenate([ns] * H, axis=0), o_c, o_w))

    def sel_tile(bb, c0, size, carry, diagonal):
        m, acc = carry
        s = _dot_nt(pro[bb][0], ksa_ref[bb, pl.ds(c0, size), :])
        if diagonal:
            r_id = lax.broadcasted_iota(jnp.int32, (H * tq, size), 0) & (tq - 1)
            c_id = lax.broadcasted_iota(jnp.int32, (H * tq, size), 1)
            s = jnp.where(c_id <= r_id, s, NEG)
        m_new = jnp.maximum(m, jnp.max(s, axis=-1, keepdims=True))
        pr = jnp.exp2(s - m_new)
        key_lanes = lax.broadcasted_iota(jnp.int32, (size, LANES), 1) < HEAD_DIM
        v1 = jnp.where(key_lanes, jnp.ones((size, LANES), BF16), sel_ref[bb, pl.ds(c0, size), :])
        acc = jnp.exp2(m - m_new) * acc + jnp.dot(pr.astype(BF16), v1, preferred_element_type=F32)
        return m_new, acc

    def all_rows(c0, size, carries, diagonal):
        return tuple(sel_tile(bb, c0, size, carries[bb], diagonal) for bb in range(nb))

    carries = tuple((jnp.full((H * tq, 1), NEG, F32), jnp.zeros((H * tq, LANES), F32)) for _ in range(nb))
    n_full = start // tk
    carries = lax.fori_loop(0, n_full, lambda t, c: all_rows(pl.multiple_of(t * tk, tk), tk, c, False), carries)
    pos = n_full * tk
    size = tk // 2
    while size >= tq:
        take = start - pos >= size
        carries = lax.cond(take, lambda c, pos=pos, size=size: all_rows(pl.multiple_of(pos, tq), size, c, False), lambda c: c, carries)
        pos = pos + jnp.where(take, size, 0)
        size //= 2
    carries = all_rows(pl.multiple_of(start, tq), tq, carries, True)

    first_half = lax.broadcasted_iota(jnp.int32, (tq, LANES), 1) < HEAD_DIM

    def place(acc, normalise):
        cols = []
        for h in range(0, H, 2):
            a0, a1 = acc[h * tq:(h + 1) * tq], acc[(h + 1) * tq:(h + 2) * tq]
            r0 = pltpu.roll(a0, HEAD_DIM, 1)
            if normalise:
                v0 = r0 * (1.0 / a0)
                v1 = a1 * (1.0 / pltpu.roll(a1, HEAD_DIM, 1))
            else:
                v0, v1 = r0, a1
            cols.append(jnp.where(first_half, v0, v1))
        return jnp.concatenate(cols, axis=1)

    for bb in range(nb):
        _, o_c, o_w = pro[bb]
        gates = 1.0 / (1.0 + jnp.exp(-gl_ref[bb]))
        g_hi = gates.astype(BF16)
        g_lo = (gates - g_hi.astype(F32)).astype(BF16)
        mix = None
        for c, y in enumerate((place(o_c, False), place(carries[bb][1], True), place(o_w, True))):
            g = jnp.dot(g_hi, gsel_ref[c], preferred_element_type=F32) + jnp.dot(g_lo, gsel_ref[c], preferred_element_type=F32)
            mix = g * y if mix is None else mix + g * y
        sq = mix * mix
        sq_hi = sq.astype(BF16)
        sq_lo = (sq - sq_hi.astype(F32)).astype(BF16)
        ms = jnp.dot(sq_hi, havg_ref[...], preferred_element_type=F32) + jnp.dot(sq_lo, havg_ref[...], preferred_element_type=F32)
        o_ref[bb] = (mix * lax.rsqrt(ms + RMS_EPS) * gain_ref[...]).astype(BF16)


def _nsa(main3, cmp, c2s, neg_onehot, small3, gain_row, nb, tq, tk):
    B, S, _ = main3.shape
    n_sel = S // SEL_LEN
    n_pad = cmp.shape[1]
    gsel = np.zeros((3, LANES, NSA_W), np.float32)
    for h in range(N_NSA):
        for c in range(3):
            gsel[c, SMALL_GATE + 3 * h + c, h * HEAD_DIM:(h + 1) * HEAD_DIM] = 1.0
    head_of = np.arange(NSA_W) // HEAD_DIM
    havg = (head_of[:, None] == head_of[None, :]).astype(np.float32) / HEAD_DIM
    return pl.pallas_call(
        functools.partial(_nsa_kernel, nb=nb, tq=tq, tk=tk, n_sel=n_sel, top_n=min(TOP_N, n_sel)),
        grid=(B // nb, S // tq),
        in_specs=[
            pl.BlockSpec((nb, tq, NSA_W), lambda b, i: (b, i, CB_NQ // 2)),
            pl.BlockSpec((nb, n_pad, LANES), lambda b, i: (b, 0, 0)),
            pl.BlockSpec((HEAD_DIM, n_pad), lambda b, i: (0, 0)),
            pl.BlockSpec((nb, S, LANES), lambda b, i: (b, 0, CB_SEL)),
            pl.BlockSpec((S, LANES), lambda b, i: (0, 0)),
            pl.BlockSpec((nb, S, LANES), lambda b, i: (b, 0, CB_WIN)),
            pl.BlockSpec((nb, tq, LANES), lambda b, i: (b, i, 0)),
            pl.BlockSpec((1, NSA_W), lambda b, i: (0, 0)),
            pl.BlockSpec((3, LANES, NSA_W), lambda b, i: (0, 0, 0)),
            pl.BlockSpec((NSA_W, NSA_W), lambda b, i: (0, 0)),
        ],
        out_specs=pl.BlockSpec((nb, tq, NSA_W), lambda b, i: (b, i, 0)),
        out_shape=jax.ShapeDtypeStruct((B, S, NSA_W), BF16),
        scratch_shapes=[pltpu.VMEM((nb, S, LANES), BF16)],
        compiler_params=_cparams(("parallel", "arbitrary")),
        name="nsa_attention",
    )(main3, cmp, c2s, main3, neg_onehot, main3, small3, gain_row, jnp.asarray(gsel, BF16), jnp.asarray(havg, BF16))


def _post_kernel(x_ref, yr_ref, yf_ref, yn_ref, wo_ref, g_ref, w1_ref, w2_ref, gf_ref, o_ref, *, ff_chunk, final):
    x = x_ref[...]
    x = x + jnp.dot(yr_ref[...], wo_ref[0:RET_W, :], preferred_element_type=F32)
    x = x + jnp.dot(yf_ref[...], wo_ref[RET_W:RET_W + FOX_W, :], preferred_element_type=F32)
    x = x + jnp.dot(yn_ref[...], wo_ref[RET_W + FOX_W:, :], preferred_element_type=F32)
    h = _rms(x, g_ref[...]).astype(BF16)
    o_ref[...] = x
    for c0 in range(0, D_FF, ff_chunk):
        hid = jnp.maximum(jnp.dot(h, w1_ref[:, c0:c0 + ff_chunk], preferred_element_type=F32), 0.0)
        o_ref[...] += jnp.dot((hid * hid).astype(BF16), w2_ref[c0:c0 + ff_chunk, :], preferred_element_type=F32)
    if final:
        o_ref[...] = _rms(o_ref[...], gf_ref[...])


def _post(x2d, yr, yf, yn, wo, g, w1, w2, gf, tm, final):
    T = x2d.shape[0]
    row = lambda w: pl.BlockSpec((tm, w), lambda i: (i, 0))
    return pl.pallas_call(
        functools.partial(_post_kernel, ff_chunk=512, final=final),
        grid=(T // tm,),
        in_specs=[
            row(D_MODEL), row(RET_W), row(FOX_W), row(NSA_W),
            _resident((D_MODEL, D_MODEL), lambda i: (0, 0)),
            _resident((1, D_MODEL), lambda i: (0, 0)),
            _resident((D_MODEL, D_FF), lambda i: (0, 0)),
            _resident((D_FF, D_MODEL), lambda i: (0, 0)),
            _resident((1, D_MODEL), lambda i: (0, 0)),
        ],
        out_specs=row(D_MODEL),
        out_shape=jax.ShapeDtypeStruct((T, D_MODEL), F32),
        compiler_params=_cparams(("parallel",)),
        name="outproj_mlp",
    )(x2d, yr, yf, yn, wo, g, w1, w2, gf)


def _tile_sizes(batch):
    nb = 4 if batch % 4 == 0 else (2 if batch % 2 == 0 else 1)
    return dict(tm=1024, fox_nb=nb, fox_tq=512, fox_tk=512, nsa_nb=nb, nsa_tq=256, nsa_tk=1024)


def kernel(x, norm_attn, w_in, fox_forget_bias, ret_norm_gain, fox_norm_gain, nsa_norm_gain, nsa_cmp_pos_k, nsa_cmp_pos_v, nsa_cmp_w1_k, nsa_cmp_w2_k, nsa_cmp_w1_v, nsa_cmp_w2_v, w_out, norm_mlp, w_mlp_in, w_mlp_out, norm_final):
    B, S, D = x.shape
    depth = w_in.shape[0]
    ts = _tile_sizes(B)
    assert D == D_MODEL and S % ts["fox_tq"] == 0 and S >= WINDOW + ts["nsa_tq"] and (B * S) % ts["tm"] == 0
    T = B * S

    wm, ws = _in_proj_weights(w_in)
    wo = w_out.astype(BF16)
    w1 = w_mlp_in.astype(BF16)
    w2 = w_mlp_out.astype(BF16)

    def expand_w1(wk, wv):
        L = wk.shape[0]
        wk = wk.astype(BF16).reshape(L, CMP_LEN, HEAD_DIM, CMP_HIDDEN)
        wv = wv.astype(BF16).reshape(L, CMP_LEN, HEAD_DIM, CMP_HIDDEN)
        z = jnp.zeros_like(wk)
        full = jnp.concatenate([jnp.concatenate([wk, z], axis=-1), jnp.concatenate([z, wv], axis=-1)], axis=2)
        full = full.reshape(L, CMP_LEN * LANES, 2 * CMP_HIDDEN)
        return full[:, :CMP_STRIDE * LANES], full[:, CMP_STRIDE * LANES:]

    wt_all, wb_all = expand_w1(nsa_cmp_w1_k, nsa_cmp_w1_v)
    zk = jnp.zeros_like(nsa_cmp_w2_k)
    w2c = jnp.concatenate([jnp.concatenate([nsa_cmp_w2_k, zk], axis=-1),
                           jnp.concatenate([zk, nsa_cmp_w2_v], axis=-1)], axis=1).astype(BF16)
    pos = jnp.concatenate([nsa_cmp_pos_k, nsa_cmp_pos_v], axis=-1)
    pos_tb = pos.reshape(depth, 2, 1, CMP_STRIDE * LANES)
    pos_tb = jnp.broadcast_to(pos_tb, (depth, 2, 8, CMP_STRIDE * LANES)).reshape(depth, 16, CMP_STRIDE * LANES).astype(BF16)

    n_cmp = (S - CMP_LEN) // CMP_STRIDE + 1
    n_sel = S // SEL_LEN
    n_rows = S // CMP_STRIDE
    cs = np.arange(n_rows) * CMP_STRIDE
    ss = np.arange(n_sel) * SEL_LEN
    overlap = np.clip(np.minimum(cs[:, None] + CMP_LEN, ss[None, :] + SEL_LEN) - np.maximum(cs[:, None], ss[None, :]), 0, None)
    overlap[n_cmp:] = 0
    c2s_np = np.zeros((HEAD_DIM, n_rows), np.float32)
    c2s_np[:n_sel] = (overlap / CMP_LEN).T
    c2s = jnp.asarray(c2s_np, dtype=BF16)
    assert n_sel <= HEAD_DIM, "the selected-branch key augmentation has 64 lanes, one per selection block"
    neg_onehot = jnp.asarray(np.where((np.arange(S)[:, None] // SEL_LEN) == np.arange(LANES)[None, :] - HEAD_DIM, NEG, 0.0), dtype=BF16)

    tables = _retention_tables(S)
    fbias = jnp.zeros((depth, 1, LANES), F32).at[:, 0, SMALL_FF:SMALL_FF + N_FOX].set(fox_forget_bias)

    x2d = x.reshape(T, D)
    for l in range(depth):
        main, small = _inproj(x2d, norm_attn[l][None, :], wm[l], ws[l], ts["tm"])
        main3 = main.reshape(B, S, N_MAIN)
        small3 = small.reshape(B, S, LANES)
        fox_qa, fox_ka, edges = _forget_cumsum(small3, fbias[l])
        y_ret = _retention(main3, tables, ret_norm_gain[l][None, :])
        per_tile = ts["fox_tk"] // RET_CHUNK
        cq = edges[:, 0::per_tile, 0, SMALL_FF:SMALL_FF + N_FOX]
        ck = edges[:, per_tile - 1::per_tile, 1, SMALL_FF:SMALL_FF + N_FOX]
        cum_edges = jnp.stack([cq.transpose(0, 2, 1), ck.transpose(0, 2, 1)], axis=2).reshape(-1)
        y_fox = _fox(main3, fox_qa, fox_ka, cum_edges, fox_norm_gain[l][None, :], ts["fox_nb"], ts["fox_tq"], ts["fox_tk"])
        r = main3[:, :, CB_CMP * LANES:(CB_CMP + 1) * LANES].reshape(B, n_rows, CMP_STRIDE * LANES)
        cmp = _nsa_compress(r, pos_tb[l], wt_all[l], wb_all[l], w2c[l])
        y_nsa = _nsa(main3, cmp, c2s, neg_onehot, small3, nsa_norm_gain[l][None, :], ts["nsa_nb"], ts["nsa_tq"], ts["nsa_tk"])
        x2d = _post(x2d, y_ret.reshape(T, RET_W), y_fox.reshape(T, FOX_W), y_nsa.reshape(T, NSA_W), wo[l],
                    norm_mlp[l][None, :], w1[l], w2[l], norm_final[None, :], ts["tm"], final=(l == depth - 1))
    return x2d.reshape(B, S, D)
```
